```python
import math
import jax, jax.numpy as jnp
from jax import lax
import numpy as np

D_MODEL = 2048
BATCH = 8
SEQ = 4096
DEPTH = 4

A_WINDOWS = (128, 512, 2048)
A_DILATIONS = (1, 4, 16)
A_GROUPS = len(A_WINDOWS)
A_HEADS = 8
HEAD_DIM = 128
A_WIDTH = A_HEADS * HEAD_DIM
ATTN_BLOCK = 128
B_WIDTH = D_MODEL - A_WIDTH
B_CONV = 3
C_WIDTH = D_MODEL // 2
C_CONV = 31
D_WIDTH = D_MODEL - C_WIDTH
D_WINDOWS = (2, 4, 8, 16)
D_GROUP = D_WIDTH // len(D_WINDOWS)
D_FF = ((8 * D_MODEL // 3 + 255) // 256) * 256
EPS = 1e-6
N_EVEN = (DEPTH + 1) // 2
N_ODD = DEPTH // 2
EVEN_IN = 3 * A_GROUPS * A_WIDTH + 3 * B_WIDTH
ODD_IN = 2 * C_WIDTH + D_WIDTH

kernel_name = "hybrid_dilated_attn_shortconv_conformer_pool"


def _rmsnorm(x, g):
    xf = x.astype(jnp.float32)
    y = xf * lax.rsqrt(jnp.mean(xf * xf, axis=-1, keepdims=True) + EPS)
    return (y * g.astype(jnp.float32)).astype(x.dtype)


def _layernorm(x, g, b):
    xf = x.astype(jnp.float32)
    mu = jnp.mean(xf, axis=-1, keepdims=True)
    var = jnp.mean(jnp.square(xf - mu), axis=-1, keepdims=True)
    y = (xf - mu) * lax.rsqrt(var + EPS)
    return (y * g.astype(jnp.float32) + b.astype(jnp.float32)).astype(x.dtype)


def _swiglu(xn, w1, w3, w2):
    return (jax.nn.silu(xn @ w1) * (xn @ w3)) @ w2


def _causal_dwconv(u, w):
    K, C = w.shape
    return lax.conv_general_dilated(u, w[:, None, :].astype(u.dtype), window_strides=(1,),
                                    padding=[(K - 1, 0)],
                                    dimension_numbers=('NWC', 'WIO', 'NWC'),
                                    feature_group_count=C)


def _alibi_slopes(n):
    return jnp.asarray(2.0 ** (-8.0 * np.arange(1, n + 1) / n), dtype=jnp.float32)


def _dilated_attention(q, k, v, dilation, window, slopes):
    Bn, S, H, Dh = q.shape
    d = dilation
    L = S // d
    Q = ATTN_BLOCK
    w_sub = window // d
    nblk = -(-L // Q)
    Lp = nblk * Q

    def blocks(t):
        t = t.reshape(Bn, L, d, H, Dh).transpose(0, 2, 1, 3, 4)
        t = jnp.pad(t, ((0, 0), (0, 0), (0, Lp - L), (0, 0), (0, 0)))
        return t.reshape(Bn, d, nblk, Q, H, Dh)

    def with_prev(t):
        prev = jnp.pad(t, ((0, 0), (0, 0), (1, 0), (0, 0), (0, 0), (0, 0)))[:, :, :-1]
        return jnp.concatenate([prev, t], axis=3)

    qb = blocks(q)
    kk = with_prev(blocks(k))
    vv = with_prev(blocks(v))
    s = jnp.einsum('brnqhd,brnkhd->brnhqk', qb, kk,
                   preferred_element_type=jnp.float32) * (Dh ** -0.5)
    qi = jnp.arange(Q)[:, None]
    ci = jnp.arange(2 * Q)[None, :]
    dist = qi + Q - ci
    key_pos = jnp.arange(nblk)[:, None, None] * Q - Q + ci[None]
    valid = (dist >= 0) & (dist <= w_sub) & (key_pos >= 0)
    bias = -slopes[:, None, None] * (dist * d).astype(jnp.float32)
    s = jnp.where(valid[:, None], s + bias, -jnp.inf)
    lse = jax.nn.logsumexp(s, axis=-1)
    p = jnp.exp(s - lse[..., None])
    o = jnp.einsum('brnhqk,brnkhd->brnqhd', p.astype(v.dtype), vv,
                   preferred_element_type=jnp.float32)

    def unblock(t):
        t = t.reshape((Bn, d, Lp) + t.shape[4:])[:, :, :L]
        t = jnp.swapaxes(t, 1, 2)
        return t.reshape((Bn, S) + t.shape[3:])

    return unblock(o), unblock(jnp.swapaxes(lse, 3, 4))


def _even_mixer(xn, w_in, q_gain, k_gain, conv_w, w_out):
    Bn, S, _ = xn.shape
    h = xn @ w_in
    n_qkv = 3 * A_GROUPS * A_WIDTH
    qkv = h[..., :n_qkv].reshape(Bn, S, 3, A_GROUPS, A_HEADS, HEAD_DIM)
    q = _rmsnorm(qkv[:, :, 0], q_gain)
    k = _rmsnorm(qkv[:, :, 1], k_gain)
    v = qkv[:, :, 2]
    slopes = _alibi_slopes(A_HEADS)
    outs, lses = [], []
    for g in range(A_GROUPS):
        o, l = _dilated_attention(q[:, :, g], k[:, :, g], v[:, :, g],
                                  A_DILATIONS[g], A_WINDOWS[g], slopes)
        outs.append(o)
        lses.append(l)
    alpha = jax.nn.softmax(jnp.stack(lses), axis=0)
    y_a = jnp.einsum('gbsh,gbshd->bshd', alpha, jnp.stack(outs))
    y_a = y_a.reshape(Bn, S, A_WIDTH).astype(xn.dtype)
    b_gate, c_gate, xt = jnp.split(h[..., n_qkv:], 3, axis=-1)
    y_b = b_gate * _causal_dwconv(c_gate * xt, conv_w)
    return jnp.concatenate([y_a, y_b], axis=-1) @ w_out


def _odd_mixer(xn, w_in, conv_w, conv_b, ln_g, ln_b, pool_w, pool_scale, w_out):
    Bn, S, _ = xn.shape
    h = xn @ w_in
    u = h[..., :C_WIDTH] * jax.nn.sigmoid(h[..., C_WIDTH:2 * C_WIDTH])
    u = _causal_dwconv(u, conv_w) + conv_b
    u = jax.nn.silu(_layernorm(u, ln_g, ln_b))
    z = h[..., 2 * C_WIDTH:].reshape(Bn, S, len(D_WINDOWS), D_GROUP)
    zf = z.astype(jnp.float32)
    cs = jnp.cumsum(zf, axis=1)
    t1 = jnp.arange(1, S + 1, dtype=jnp.float32)
    pooled = []
    for g, kw in enumerate(D_WINDOWS):
        c = cs[:, :, g]
        lo = jnp.pad(c, ((0, 0), (kw, 0), (0, 0)))[:, :S]
        pooled.append((c - lo) / jnp.minimum(t1, float(kw))[None, :, None])
    pooled = jnp.stack(pooled, axis=2) - zf
    y_d = jnp.einsum('bsgc,gce->bsge', pooled.astype(xn.dtype), pool_w)
    y_d = y_d.reshape(Bn, S, D_WIDTH) * pool_scale
    return jnp.concatenate([u, y_d], axis=-1) @ w_out


def _fwd_setup_inputs(seed: int = 0) -> dict:
    key = jax.random.key(seed)
    ks = jax.random.split(key, 20)
    f32 = jnp.float32

    def nrm(k, shape, scale):
        return jax.random.normal(k, shape, f32) * scale

    return {
        "x": nrm(ks[0], (BATCH, SEQ, D_MODEL), 1.0),
        "norm_g": 1.0 + nrm(ks[1], (DEPTH, 3, D_MODEL), 0.02),
        "ffn_w1": nrm(ks[2], (DEPTH, 2, D_MODEL, D_FF), D_MODEL ** -0.5),
        "ffn_w3": nrm(ks[3], (DEPTH, 2, D_MODEL, D_FF), D_MODEL ** -0.5),
        "ffn_w2": nrm(ks[4], (DEPTH, 2, D_FF, D_MODEL), D_FF ** -0.5),
        "ev_w_in": nrm(ks[5], (N_EVEN, D_MODEL, EVEN_IN), D_MODEL ** -0.5),
        "ev_q_gain": 1.0 + nrm(ks[6], (N_EVEN, HEAD_DIM), 0.02),
        "ev_k_gain": 1.0 + nrm(ks[7], (N_EVEN, HEAD_DIM), 0.02),
        "ev_conv_w": nrm(ks[8], (N_EVEN, B_CONV, B_WIDTH), B_CONV ** -0.5),
        "ev_w_out": nrm(ks[9], (N_EVEN, D_MODEL, D_MODEL), D_MODEL ** -0.5),
        "od_w_in": nrm(ks[10], (N_ODD, D_MODEL, ODD_IN), D_MODEL ** -0.5),
        "od_conv_w": nrm(ks[11], (N_ODD, C_CONV, C_WIDTH), C_CONV ** -0.5),
        "od_conv_b": nrm(ks[12], (N_ODD, C_WIDTH), 0.02),
        "od_ln_g": 1.0 + nrm(ks[13], (N_ODD, C_WIDTH), 0.02),
        "od_ln_b": nrm(ks[14], (N_ODD, C_WIDTH), 0.02),
        "od_pool_w": nrm(ks[15], (N_ODD, len(D_WINDOWS), D_GROUP, D_GROUP), D_GROUP ** -0.5),
        "od_pool_scale": 1.0 + nrm(ks[16], (N_ODD, D_WIDTH), 0.02),
        "od_w_out": nrm(ks[17], (N_ODD, D_MODEL, D_MODEL), D_MODEL ** -0.5),
    }


def _fwd_reference(x, norm_g, ffn_w1, ffn_w3, ffn_w2, ev_w_in, ev_q_gain, ev_k_gain, ev_conv_w,
              ev_w_out, od_w_in, od_conv_w, od_conv_b, od_ln_g, od_ln_b, od_pool_w,
              od_pool_scale, od_w_out):
    for layer in range(DEPTH):
        x = x + 0.5 * _swiglu(_rmsnorm(x, norm_g[layer, 0]),
                              ffn_w1[layer, 0], ffn_w3[layer, 0], ffn_w2[layer, 0])
        xn = _rmsnorm(x, norm_g[layer, 1])
        if layer % 2 == 0:
            i = layer // 2
            x = x + _even_mixer(xn, ev_w_in[i], ev_q_gain[i], ev_k_gain[i],
                                ev_conv_w[i], ev_w_out[i])
        else:
            i = layer // 2
            x = x + _odd_mixer(xn, od_w_in[i], od_conv_w[i], od_conv_b[i], od_ln_g[i],
                               od_ln_b[i], od_pool_w[i], od_pool_scale[i], od_w_out[i])
        x = x + 0.5 * _swiglu(_rmsnorm(x, norm_g[layer, 2]),
                              ffn_w1[layer, 1], ffn_w3[layer, 1], ffn_w2[layer, 1])
    return x


import jax as _jax
import jax.numpy as _jnp

TWIN_FORMAT = 'train_step'
FWD_PARAMS = ['x', 'norm_g', 'ffn_w1', 'ffn_w3', 'ffn_w2', 'ev_w_in', 'ev_q_gain', 'ev_k_gain', 'ev_conv_w', 'ev_w_out', 'od_w_in', 'od_conv_w', 'od_conv_b', 'od_ln_g', 'od_ln_b', 'od_pool_w', 'od_pool_scale', 'od_w_out']
TWIN_WEIGHTS = ['norm_g', 'ffn_w1', 'ffn_w3', 'ffn_w2', 'ev_w_in', 'ev_q_gain', 'ev_k_gain', 'ev_conv_w', 'ev_w_out', 'od_w_in', 'od_conv_w', 'od_conv_b', 'od_ln_g', 'od_ln_b', 'od_pool_w', 'od_pool_scale', 'od_w_out']
TWIN_DIFF_INPUT = 'x'
TWIN_INPUTS = ['x', 'norm_g', 'ffn_w1', 'ffn_w3', 'ffn_w2', 'ev_w_in', 'ev_q_gain', 'ev_k_gain', 'ev_conv_w', 'ev_w_out', 'od_w_in', 'od_conv_w', 'od_conv_b', 'od_ln_g', 'od_ln_b', 'od_pool_w', 'od_pool_scale', 'od_w_out', 'loss_target', 'm_norm_g', 'm_ffn_w1', 'm_ffn_w3', 'm_ffn_w2', 'm_ev_w_in', 'm_ev_q_gain', 'm_ev_k_gain', 'm_ev_conv_w', 'm_ev_w_out', 'm_od_w_in', 'm_od_conv_w', 'm_od_conv_b', 'm_od_ln_g', 'm_od_ln_b', 'm_od_pool_w', 'm_od_pool_scale', 'm_od_w_out', 'v_norm_g', 'v_ffn_w1', 'v_ffn_w3', 'v_ffn_w2', 'v_ev_w_in', 'v_ev_q_gain', 'v_ev_k_gain', 'v_ev_conv_w', 'v_ev_w_out', 'v_od_w_in', 'v_od_conv_w', 'v_od_conv_b', 'v_od_ln_g', 'v_od_ln_b', 'v_od_pool_w', 'v_od_pool_scale', 'v_od_w_out']
TWIN_OUTPUTS = ['loss', 'grad_x', 'grad_norm_g', 'grad_ffn_w1', 'grad_ffn_w3', 'grad_ffn_w2', 'grad_ev_w_in', 'grad_ev_q_gain', 'grad_ev_k_gain', 'grad_ev_conv_w', 'grad_ev_w_out', 'grad_od_w_in', 'grad_od_conv_w', 'grad_od_conv_b', 'grad_od_ln_g', 'grad_od_ln_b', 'grad_od_pool_w', 'grad_od_pool_scale', 'grad_od_w_out', 'delta_norm_g', 'delta_ffn_w1', 'delta_ffn_w3', 'delta_ffn_w2', 'delta_ev_w_in', 'delta_ev_q_gain', 'delta_ev_k_gain', 'delta_ev_conv_w', 'delta_ev_w_out', 'delta_od_w_in', 'delta_od_conv_w', 'delta_od_conv_b', 'delta_od_ln_g', 'delta_od_ln_b', 'delta_od_pool_w', 'delta_od_pool_scale', 'delta_od_w_out', 'new_m_norm_g', 'new_m_ffn_w1', 'new_m_ffn_w3', 'new_m_ffn_w2', 'new_m_ev_w_in', 'new_m_ev_q_gain', 'new_m_ev_k_gain', 'new_m_ev_conv_w', 'new_m_ev_w_out', 'new_m_od_w_in', 'new_m_od_conv_w', 'new_m_od_conv_b', 'new_m_od_ln_g', 'new_m_od_ln_b', 'new_m_od_pool_w', 'new_m_od_pool_scale', 'new_m_od_w_out', 'new_v_norm_g', 'new_v_ffn_w1', 'new_v_ffn_w3', 'new_v_ffn_w2', 'new_v_ev_w_in', 'new_v_ev_q_gain', 'new_v_ev_k_gain', 'new_v_ev_conv_w', 'new_v_ev_w_out', 'new_v_od_w_in', 'new_v_od_conv_w', 'new_v_od_conv_b', 'new_v_od_ln_g', 'new_v_od_ln_b', 'new_v_od_pool_w', 'new_v_od_pool_scale', 'new_v_od_w_out']
TWIN_LEAF_KINDS = {'loss': 'loss', 'grad_x': 'grad_x', 'grad_norm_g': 'grad_w', 'grad_ffn_w1': 'grad_w', 'grad_ffn_w3': 'grad_w', 'grad_ffn_w2': 'grad_w', 'grad_ev_w_in': 'grad_w', 'grad_ev_q_gain': 'grad_w', 'grad_ev_k_gain': 'grad_w', 'grad_ev_conv_w': 'grad_w', 'grad_ev_w_out': 'grad_w', 'grad_od_w_in': 'grad_w', 'grad_od_conv_w': 'grad_w', 'grad_od_conv_b': 'grad_w', 'grad_od_ln_g': 'grad_w', 'grad_od_ln_b': 'grad_w', 'grad_od_pool_w': 'grad_w', 'grad_od_pool_scale': 'grad_w', 'grad_od_w_out': 'grad_w', 'delta_norm_g': 'delta_w', 'delta_ffn_w1': 'delta_w', 'delta_ffn_w3': 'delta_w', 'delta_ffn_w2': 'delta_w', 'delta_ev_w_in': 'delta_w', 'delta_ev_q_gain': 'delta_w', 'delta_ev_k_gain': 'delta_w', 'delta_ev_conv_w': 'delta_w', 'delta_ev_w_out': 'delta_w', 'delta_od_w_in': 'delta_w', 'delta_od_conv_w': 'delta_w', 'delta_od_conv_b': 'delta_w', 'delta_od_ln_g': 'delta_w', 'delta_od_ln_b': 'delta_w', 'delta_od_pool_w': 'delta_w', 'delta_od_pool_scale': 'delta_w', 'delta_od_w_out': 'delta_w', 'new_m_norm_g': 'new_m', 'new_m_ffn_w1': 'new_m', 'new_m_ffn_w3': 'new_m', 'new_m_ffn_w2': 'new_m', 'new_m_ev_w_in': 'new_m', 'new_m_ev_q_gain': 'new_m', 'new_m_ev_k_gain': 'new_m', 'new_m_ev_conv_w': 'new_m', 'new_m_ev_w_out': 'new_m', 'new_m_od_w_in': 'new_m', 'new_m_od_conv_w': 'new_m', 'new_m_od_conv_b': 'new_m', 'new_m_od_ln_g': 'new_m', 'new_m_od_ln_b': 'new_m', 'new_m_od_pool_w': 'new_m', 'new_m_od_pool_scale': 'new_m', 'new_m_od_w_out': 'new_m', 'new_v_norm_g': 'new_v', 'new_v_ffn_w1': 'new_v', 'new_v_ffn_w3': 'new_v', 'new_v_ffn_w2': 'new_v', 'new_v_ev_w_in': 'new_v', 'new_v_ev_q_gain': 'new_v', 'new_v_ev_k_gain': 'new_v', 'new_v_ev_conv_w': 'new_v', 'new_v_ev_w_out': 'new_v', 'new_v_od_w_in': 'new_v', 'new_v_od_conv_w': 'new_v', 'new_v_od_conv_b': 'new_v', 'new_v_od_ln_g': 'new_v', 'new_v_od_ln_b': 'new_v', 'new_v_od_pool_w': 'new_v', 'new_v_od_pool_scale': 'new_v', 'new_v_od_w_out': 'new_v'}


def _forward(args):
    return _fwd_reference(*[args[k] for k in FWD_PARAMS])


def _output_shape():
    out = _jax.eval_shape(lambda: _forward(_fwd_setup_inputs(0)))
    return out.shape, out.dtype

N_MICROBATCH = 1
ADAM_LR = 0.001
ADAM_B1 = 0.9
ADAM_B2 = 0.999
ADAM_EPS = 1e-08
ADAM_WD = 0.01
ADAM_STEP = 10
PER_EXAMPLE_BATCH_AXIS = {'x': 0, 'loss_target': 0}
SHARED_INPUTS = []
_WEIGHT_DTYPES = {'norm_g': _jnp.float32, 'ffn_w1': _jnp.float32, 'ffn_w3': _jnp.float32, 'ffn_w2': _jnp.float32, 'ev_w_in': _jnp.float32, 'ev_q_gain': _jnp.float32, 'ev_k_gain': _jnp.float32, 'ev_conv_w': _jnp.float32, 'ev_w_out': _jnp.float32, 'od_w_in': _jnp.float32, 'od_conv_w': _jnp.float32, 'od_conv_b': _jnp.float32, 'od_ln_g': _jnp.float32, 'od_ln_b': _jnp.float32, 'od_pool_w': _jnp.float32, 'od_pool_scale': _jnp.float32, 'od_w_out': _jnp.float32}
MOMENT_SCALE = {'norm_g': 1.036668e+01, 'ffn_w1': 9.024104e-02, 'ffn_w3': 9.084056e-02, 'ffn_w2': 1.500045e-01, 'ev_w_in': 2.980856e-01, 'ev_q_gain': 3.943329e+00, 'ev_k_gain': 3.962311e+00, 'ev_conv_w': 8.898800e+00, 'ev_w_out': 3.966928e-01, 'od_w_in': 3.440726e-01, 'od_conv_w': 2.620790e-01, 'od_conv_b': 3.274148e+00, 'od_ln_g': 6.940007e+00, 'od_ln_b': 4.691315e+00, 'od_pool_w': 8.081566e-01, 'od_pool_scale': 1.265287e+01, 'od_w_out': 5.769928e-01}


def _to_microbatches(a, axis):
    t = _jnp.moveaxis(a, axis, 0)
    t = t.reshape((N_MICROBATCH, t.shape[0] // N_MICROBATCH) + t.shape[1:])
    return _jnp.moveaxis(t, 1, axis + 1)


def setup_inputs(seed: int = 0) -> dict:
    inp = _fwd_setup_inputs(seed)
    key = _jax.random.fold_in(_jax.random.key(seed), 7919)
    shape, _ = _output_shape()
    out = dict(inp)
    out["loss_target"] = _jax.random.normal(_jax.random.fold_in(key, 0), shape, _jnp.float32)
    for i, name in enumerate(TWIN_WEIGHTS):
        w = inp[name].astype(_jnp.float32)
        if MOMENT_SCALE is None:
            s = _jnp.sqrt(_jnp.mean(_jnp.square(w)) + 1e-30)
        else:
            s = MOMENT_SCALE[name]
        km, kv = _jax.random.split(_jax.random.fold_in(key, i + 1))
        out[name] = w
        out["m_" + name] = s * _jax.random.normal(km, w.shape, _jnp.float32)
        out["v_" + name] = (s * s) * _jax.random.uniform(kv, w.shape, _jnp.float32, 0.5, 1.5)
    if N_MICROBATCH > 1:
        for name, axis in PER_EXAMPLE_BATCH_AXIS.items():
            out[name] = _to_microbatches(out[name], axis)
    return {'x': out['x'], 'norm_g': out['norm_g'], 'ffn_w1': out['ffn_w1'], 'ffn_w3': out['ffn_w3'], 'ffn_w2': out['ffn_w2'], 'ev_w_in': out['ev_w_in'], 'ev_q_gain': out['ev_q_gain'], 'ev_k_gain': out['ev_k_gain'], 'ev_conv_w': out['ev_conv_w'], 'ev_w_out': out['ev_w_out'], 'od_w_in': out['od_w_in'], 'od_conv_w': out['od_conv_w'], 'od_conv_b': out['od_conv_b'], 'od_ln_g': out['od_ln_g'], 'od_ln_b': out['od_ln_b'], 'od_pool_w': out['od_pool_w'], 'od_pool_scale': out['od_pool_scale'], 'od_w_out': out['od_w_out'], 'loss_target': out['loss_target'], 'm_norm_g': out['m_norm_g'], 'm_ffn_w1': out['m_ffn_w1'], 'm_ffn_w3': out['m_ffn_w3'], 'm_ffn_w2': out['m_ffn_w2'], 'm_ev_w_in': out['m_ev_w_in'], 'm_ev_q_gain': out['m_ev_q_gain'], 'm_ev_k_gain': out['m_ev_k_gain'], 'm_ev_conv_w': out['m_ev_conv_w'], 'm_ev_w_out': out['m_ev_w_out'], 'm_od_w_in': out['m_od_w_in'], 'm_od_conv_w': out['m_od_conv_w'], 'm_od_conv_b': out['m_od_conv_b'], 'm_od_ln_g': out['m_od_ln_g'], 'm_od_ln_b': out['m_od_ln_b'], 'm_od_pool_w': out['m_od_pool_w'], 'm_od_pool_scale': out['m_od_pool_scale'], 'm_od_w_out': out['m_od_w_out'], 'v_norm_g': out['v_norm_g'], 'v_ffn_w1': out['v_ffn_w1'], 'v_ffn_w3': out['v_ffn_w3'], 'v_ffn_w2': out['v_ffn_w2'], 'v_ev_w_in': out['v_ev_w_in'], 'v_ev_q_gain': out['v_ev_q_gain'], 'v_ev_k_gain': out['v_ev_k_gain'], 'v_ev_conv_w': out['v_ev_conv_w'], 'v_ev_w_out': out['v_ev_w_out'], 'v_od_w_in': out['v_od_w_in'], 'v_od_conv_w': out['v_od_conv_w'], 'v_od_conv_b': out['v_od_conv_b'], 'v_od_ln_g': out['v_od_ln_g'], 'v_od_ln_b': out['v_od_ln_b'], 'v_od_pool_w': out['v_od_pool_w'], 'v_od_pool_scale': out['v_od_pool_scale'], 'v_od_w_out': out['v_od_w_out']}


def _loss(weights, diff, rest, loss_target):
    with _jax.named_scope("forward"):
        args = {**rest, TWIN_DIFF_INPUT: diff, **{k: w.astype(_WEIGHT_DTYPES[k]) for k, w in weights.items()}}
        y = _forward(args)
    with _jax.named_scope("loss_head"):
        err = _jnp.square(y.astype(_jnp.float32) - loss_target)
        return 0.5 * _jnp.sum(_jnp.mean(err, axis=-1)) if err.ndim else 0.5 * err


def _adamw(w, g, m, v):
    m = ADAM_B1 * m + (1.0 - ADAM_B1) * g
    v = ADAM_B2 * v + (1.0 - ADAM_B2) * _jnp.square(g)
    m_hat = m / (1.0 - ADAM_B1 ** ADAM_STEP)
    v_hat = v / (1.0 - ADAM_B2 ** ADAM_STEP)
    delta = -ADAM_LR * (m_hat / (_jnp.sqrt(v_hat) + ADAM_EPS) + ADAM_WD * w)
    return delta, m, v


def reference(x, norm_g, ffn_w1, ffn_w3, ffn_w2, ev_w_in, ev_q_gain, ev_k_gain, ev_conv_w, ev_w_out, od_w_in, od_conv_w, od_conv_b, od_ln_g, od_ln_b, od_pool_w, od_pool_scale, od_w_out, loss_target, m_norm_g, m_ffn_w1, m_ffn_w3, m_ffn_w2, m_ev_w_in, m_ev_q_gain, m_ev_k_gain, m_ev_conv_w, m_ev_w_out, m_od_w_in, m_od_conv_w, m_od_conv_b, m_od_ln_g, m_od_ln_b, m_od_pool_w, m_od_pool_scale, m_od_w_out, v_norm_g, v_ffn_w1, v_ffn_w3, v_ffn_w2, v_ev_w_in, v_ev_q_gain, v_ev_k_gain, v_ev_conv_w, v_ev_w_out, v_od_w_in, v_od_conv_w, v_od_conv_b, v_od_ln_g, v_od_ln_b, v_od_pool_w, v_od_pool_scale, v_od_w_out):
    given = dict(x=x, norm_g=norm_g, ffn_w1=ffn_w1, ffn_w3=ffn_w3, ffn_w2=ffn_w2, ev_w_in=ev_w_in, ev_q_gain=ev_q_gain, ev_k_gain=ev_k_gain, ev_conv_w=ev_conv_w, ev_w_out=ev_w_out, od_w_in=od_w_in, od_conv_w=od_conv_w, od_conv_b=od_conv_b, od_ln_g=od_ln_g, od_ln_b=od_ln_b, od_pool_w=od_pool_w, od_pool_scale=od_pool_scale, od_w_out=od_w_out, loss_target=loss_target, m_norm_g=m_norm_g, m_ffn_w1=m_ffn_w1, m_ffn_w3=m_ffn_w3, m_ffn_w2=m_ffn_w2, m_ev_w_in=m_ev_w_in, m_ev_q_gain=m_ev_q_gain, m_ev_k_gain=m_ev_k_gain, m_ev_conv_w=m_ev_conv_w, m_ev_w_out=m_ev_w_out, m_od_w_in=m_od_w_in, m_od_conv_w=m_od_conv_w, m_od_conv_b=m_od_conv_b, m_od_ln_g=m_od_ln_g, m_od_ln_b=m_od_ln_b, m_od_pool_w=m_od_pool_w, m_od_pool_scale=m_od_pool_scale, m_od_w_out=m_od_w_out, v_norm_g=v_norm_g, v_ffn_w1=v_ffn_w1, v_ffn_w3=v_ffn_w3, v_ffn_w2=v_ffn_w2, v_ev_w_in=v_ev_w_in, v_ev_q_gain=v_ev_q_gain, v_ev_k_gain=v_ev_k_gain, v_ev_conv_w=v_ev_conv_w, v_ev_w_out=v_ev_w_out, v_od_w_in=v_od_w_in, v_od_conv_w=v_od_conv_w, v_od_conv_b=v_od_conv_b, v_od_ln_g=v_od_ln_g, v_od_ln_b=v_od_ln_b, v_od_pool_w=v_od_pool_w, v_od_pool_scale=v_od_pool_scale, v_od_w_out=v_od_w_out)
    weights = {n: given[n] for n in TWIN_WEIGHTS}
    shared = {n: given[n] for n in SHARED_INPUTS}
    per_example = {n: given[n] for n in ['x']}
    grad_fn = _jax.value_and_grad(_loss, argnums=(0, 1))

    def one_microbatch(ex, loss_target):
        ex = dict(ex)
        diff = ex.pop(TWIN_DIFF_INPUT)
        return grad_fn(weights, diff, {**shared, **ex}, loss_target)

    if N_MICROBATCH == 1:
        loss, (grad_w, grad_x) = one_microbatch(per_example, given["loss_target"])
    else:
        def body(carry, xs):
            loss_sum, grad_sum = carry
            l_k, (gw_k, gx_k) = one_microbatch(xs[0], xs[1])
            with _jax.named_scope("update"):
                return (loss_sum + l_k, _jax.tree.map(_jnp.add, grad_sum, gw_k)), gx_k

        init = (_jnp.zeros((), _jnp.float32), _jax.tree.map(_jnp.zeros_like, weights))
        (loss, grad_w), grad_x = _jax.lax.scan(body, init, (per_example, given["loss_target"]))
    with _jax.named_scope("update"):
        delta_w, new_m, new_v = {}, {}, {}
        for n in TWIN_WEIGHTS:
            delta_w[n], new_m[n], new_v[n] = _adamw(weights[n], grad_w[n], given["m_" + n], given["v_" + n])
    return (loss, grad_x, *[grad_w[n] for n in TWIN_WEIGHTS], *[delta_w[n] for n in TWIN_WEIGHTS],
            *[new_m[n] for n in TWIN_WEIGHTS], *[new_v[n] for n in TWIN_WEIGHTS])
```

```python
import functools

import jax
import jax.numpy as jnp
from jax import lax
from jax.experimental import pallas as pl
from jax.experimental.pallas import tpu as pltpu

F32 = jnp.float32
BF16 = jnp.bfloat16
MESH = pl.DeviceIdType.MESH

EPS = 1e-6
HEADS = 8
HEAD_DIM = 128
A_WIDTH = HEADS * HEAD_DIM
A_WINDOWS = (128, 512, 2048)
A_DILATIONS = (1, 4, 16)
ATTN_BLOCK = 128
B_CONV = 3
C_CONV = 31
D_WINDOWS = (2, 4, 8, 16)
HALO = 32
N_CHIPS = 4
ADAM_LR = 0.001
ADAM_B1 = 0.9
ADAM_B2 = 0.999
ADAM_EPS = 1e-08
ADAM_WD = 0.01
ADAM_STEP = 10
VMEM_LIMIT_BYTES = 56 * 1024 * 1024
NEG_BIG = -1e30


def _params(sem, **kw):
    return pltpu.CompilerParams(dimension_semantics=sem, vmem_limit_bytes=VMEM_LIMIT_BYTES, **kw)


def _sigmoid(x):
    return 1.0 / (1.0 + jnp.exp(-x))


def _matmul(pairs, *, m, n, k, tm, tn, tk, ta=False, tb=False, out_dtype=F32, res=None,
            alpha=1.0, name):
    nk = k // tk
    npairs = len(pairs)
    dn = (((0 if ta else 1,), (1 if tb else 0,)), ((), ()))

    def body(*refs):
        ab = refs[:2 * npairs]
        pos = 2 * npairs
        res_ref = None
        if res is not None:
            res_ref = refs[pos]
            pos += 1
        o_ref = refs[pos]
        acc_ref = refs[pos + 1] if nk > 1 else None

        def dots():
            tot = None
            for p in range(npairs):
                d = lax.dot_general(ab[2 * p][...], ab[2 * p + 1][...], dn, preferred_element_type=F32)
                tot = d if tot is None else tot + d
            return tot

        def finish(acc):
            r = acc * alpha if alpha != 1.0 else acc
            if res_ref is not None:
                r = res_ref[...].astype(F32) + r
            o_ref[...] = r.astype(o_ref.dtype)

        if nk == 1:
            finish(dots())
        else:
            kk = pl.program_id(2)

            @pl.when(kk == 0)
            def _():
                acc_ref[...] = dots()

            @pl.when(kk > 0)
            def _():
                acc_ref[...] += dots()

            @pl.when(kk == nk - 1)
            def _():
                finish(acc_ref[...])

    a_spec = pl.BlockSpec((tk, tm), lambda i, j, kk: (kk, i)) if ta else pl.BlockSpec((tm, tk), lambda i, j, kk: (i, kk))
    b_spec = pl.BlockSpec((tn, tk), lambda i, j, kk: (j, kk)) if tb else pl.BlockSpec((tk, tn), lambda i, j, kk: (kk, j))
    o_spec = pl.BlockSpec((tm, tn), lambda i, j, kk: (i, j))
    in_specs = [a_spec, b_spec] * npairs
    args = [t for p in pairs for t in p]
    if res is not None:
        in_specs.append(o_spec)
        args.append(res)
    return pl.pallas_call(
        body, name=name,
        out_shape=jax.ShapeDtypeStruct((m, n), out_dtype),
        grid=(m // tm, n // tn, nk),
        in_specs=in_specs, out_specs=o_spec,
        scratch_shapes=[pltpu.VMEM((tm, tn), F32)] if nk > 1 else [],
        compiler_params=_params(("parallel", "parallel", "arbitrary")),
    )(*args)


def _ffn_up(xn, w1, w3, *, tm, tn):
    t, d = xn.shape
    f = w1.shape[1]

    def body(x_ref, w1_ref, w3_ref, a_ref, b_ref, h_ref):
        x = x_ref[...]
        a = jnp.dot(x, w1_ref[...], preferred_element_type=F32)
        b = jnp.dot(x, w3_ref[...], preferred_element_type=F32)
        a_ref[...] = a.astype(BF16)
        b_ref[...] = b.astype(BF16)
        h_ref[...] = (a * _sigmoid(a) * b).astype(BF16)

    x_spec = pl.BlockSpec((tm, d), lambda i, j: (i, 0))
    w_spec = pl.BlockSpec((d, tn), lambda i, j: (0, j))
    o_spec = pl.BlockSpec((tm, tn), lambda i, j: (i, j))
    shp = jax.ShapeDtypeStruct((t, f), BF16)
    return pl.pallas_call(
        body, name="ffn_up", out_shape=(shp, shp, shp), grid=(t // tm, f // tn),
        in_specs=[x_spec, w_spec, w_spec], out_specs=(o_spec, o_spec, o_spec),
        compiler_params=_params(("parallel", "parallel")),
    )(xn, w1, w3)


def _ffn_dh(dyb, w2, a, b, *, tm, tn):
    t, d = dyb.shape
    f = w2.shape[0]
    dn = (((1,), (1,)), ((), ()))

    def body(dy_ref, w2_ref, a_ref, b_ref, da_ref, db_ref):
        dh = 0.5 * lax.dot_general(dy_ref[...], w2_ref[...], dn, preferred_element_type=F32)
        av = a_ref[...].astype(F32)
        bv = b_ref[...].astype(F32)
        sig = _sigmoid(av)
        da_ref[...] = (dh * bv * (sig * (1.0 + av * (1.0 - sig)))).astype(BF16)
        db_ref[...] = (dh * (av * sig)).astype(BF16)

    dy_spec = pl.BlockSpec((tm, d), lambda i, j: (i, 0))
    w_spec = pl.BlockSpec((tn, d), lambda i, j: (j, 0))
    o_spec = pl.BlockSpec((tm, tn), lambda i, j: (i, j))
    shp = jax.ShapeDtypeStruct((t, f), BF16)
    return pl.pallas_call(
        body, name="ffn_dh", out_shape=(shp, shp), grid=(t // tm, f // tn),
        in_specs=[dy_spec, w_spec, o_spec, o_spec], out_specs=(o_spec, o_spec),
        compiler_params=_params(("parallel", "parallel")),
    )(dyb, w2, a, b)


def _rmsnorm_fwd(x, g, *, tr=256):
    t, d = x.shape

    def body(x_ref, g_ref, o_ref):
        xv = x_ref[...]
        y = xv * lax.rsqrt(jnp.mean(xv * xv, axis=-1, keepdims=True) + EPS)
        o_ref[...] = (y * g_ref[...]).astype(BF16)

    return pl.pallas_call(
        body, name="rmsnorm_fwd", out_shape=jax.ShapeDtypeStruct((t, d), BF16), grid=(t // tr,),
        in_specs=[pl.BlockSpec((tr, d), lambda i: (i, 0)), pl.BlockSpec((1, d), lambda i: (0, 0))],
        out_specs=pl.BlockSpec((tr, d), lambda i: (i, 0)),
        compiler_params=_params(("parallel",)),
    )(x, g.reshape(1, d))


def _rmsnorm_bwd(dy, x, g, dres, *, tr=256):
    t, d = x.shape

    def body(dy_ref, x_ref, g_ref, dres_ref, dx_ref, dxb_ref, dg_ref):
        xv = x_ref[...]
        dyv = dy_ref[...].astype(F32)
        r = lax.rsqrt(jnp.mean(xv * xv, axis=-1, keepdims=True) + EPS)
        xhat = xv * r
        dxhat = dyv * g_ref[...]
        c = jnp.mean(dxhat * xhat, axis=-1, keepdims=True)
        dx = dres_ref[...] + r * (dxhat - xhat * c)
        dx_ref[...] = dx
        dxb_ref[...] = dx.astype(BF16)
        part = jnp.sum(dyv * xhat, axis=0, keepdims=True)

        @pl.when(pl.program_id(0) == 0)
        def _():
            dg_ref[...] = part

        @pl.when(pl.program_id(0) > 0)
        def _():
            dg_ref[...] += part

    row = pl.BlockSpec((tr, d), lambda i: (i, 0))
    vec = pl.BlockSpec((1, d), lambda i: (0, 0))
    dx, dxb, dg = pl.pallas_call(
        body, name="rmsnorm_bwd",
        out_shape=(jax.ShapeDtypeStruct((t, d), F32), jax.ShapeDtypeStruct((t, d), BF16),
                   jax.ShapeDtypeStruct((1, d), F32)),
        grid=(t // tr,), in_specs=[row, row, vec, row], out_specs=(row, row, vec),
        compiler_params=_params(("arbitrary",)),
    )(dy, x, g.reshape(1, d), dres)
    return dx, dxb, dg.reshape(d)


def _loss_head(y, target, *, tr=256):
    t, d = y.shape

    def body(y_ref, t_ref, dy_ref, dyb_ref, s_ref):
        err = y_ref[...] - t_ref[...]
        dy = err * (1.0 / d)
        dy_ref[...] = dy
        dyb_ref[...] = dy.astype(BF16)
        part = jnp.full((1, 128), jnp.sum(err * err), F32)

        @pl.when(pl.program_id(0) == 0)
        def _():
            s_ref[...] = part

        @pl.when(pl.program_id(0) > 0)
        def _():
            s_ref[...] += part

    row = pl.BlockSpec((tr, d), lambda i: (i, 0))
    return pl.pallas_call(
        body, name="loss_head",
        out_shape=(jax.ShapeDtypeStruct((t, d), F32), jax.ShapeDtypeStruct((t, d), BF16),
                   jax.ShapeDtypeStruct((1, 128), F32)),
        grid=(t // tr,), in_specs=[row, row],
        out_specs=(row, row, pl.BlockSpec((1, 128), lambda i: (0, 0))),
        compiler_params=_params(("arbitrary",)),
    )(y, target)


def _adamw(w, g, m, v, *, tr):
    rows, cols = w.shape

    def body(w_ref, g_ref, m_ref, v_ref, d_ref, nm_ref, nv_ref):
        gv = g_ref[...]
        nm = ADAM_B1 * m_ref[...] + (1.0 - ADAM_B1) * gv
        nv = ADAM_B2 * v_ref[...] + (1.0 - ADAM_B2) * jnp.square(gv)
        m_hat = nm / (1.0 - ADAM_B1 ** ADAM_STEP)
        v_hat = nv / (1.0 - ADAM_B2 ** ADAM_STEP)
        d_ref[...] = -ADAM_LR * (m_hat / (jnp.sqrt(v_hat) + ADAM_EPS) + ADAM_WD * w_ref[...])
        nm_ref[...] = nm
        nv_ref[...] = nv

    spec = pl.BlockSpec((tr, cols), lambda i: (i, 0))
    shp = jax.ShapeDtypeStruct((rows, cols), F32)
    return pl.pallas_call(
        body, name="adamw", out_shape=(shp, shp, shp), grid=(rows // tr,),
        in_specs=[spec] * 4, out_specs=(spec, spec, spec),
        compiler_params=_params(("parallel",)),
    )(w, g, m, v)


def _cast_bf16(w, *, tr):
    rows, cols = w.shape

    def body(w_ref, o_ref):
        o_ref[...] = w_ref[...].astype(BF16)

    spec = pl.BlockSpec((tr, cols), lambda i: (i, 0))
    return pl.pallas_call(
        body, name="cast_bf16", out_shape=jax.ShapeDtypeStruct((rows, cols), BF16), grid=(rows // tr,),
        in_specs=[spec], out_specs=spec, compiler_params=_params(("parallel",)),
    )(w)


def _headnorm(xf, g):
    r = lax.rsqrt(jnp.mean(xf * xf, axis=-1, keepdims=True) + EPS)
    xhat = xf * r
    return xhat * g, xhat, r


def _headnorm_bwd(dn, xhat, r, g):
    dxhat = dn * g
    return r * (dxhat - xhat * jnp.mean(dxhat * xhat, axis=-1, keepdims=True))


_NT = (((1,), (1,)), ((), ()))
_TN = (((0,), (0,)), ((), ()))


def _attn_masks(n, nb):
    qi = lax.broadcasted_iota(jnp.int32, (ATTN_BLOCK, ATTN_BLOCK), 0)
    ci = lax.broadcasted_iota(jnp.int32, (ATTN_BLOCK, ATTN_BLOCK), 1)
    d_prev = qi + ATTN_BLOCK - ci
    d_cur = qi - ci
    return d_prev, d_cur, (ci >= qi), (ci <= qi)


def _attn_fwd(qv, kv, vv, qg, kg, *, dil):
    l, w = qv.shape
    nb = l // ATTN_BLOCK
    scale = HEAD_DIM ** -0.5

    def body(q_ref, kp_ref, kc_ref, vp_ref, vc_ref, qg_ref, kg_ref, o_ref, lse_ref):
        n = pl.program_id(1)
        d_prev, d_cur, ok_prev, ok_cur = _attn_masks(n, nb)
        ok_prev = ok_prev & (n > 0)
        b_prev = d_prev.astype(F32) * float(dil)
        b_cur = d_cur.astype(F32) * float(dil)
        for h in range(HEADS):
            sl = slice(h * HEAD_DIM, (h + 1) * HEAD_DIM)
            slope = 2.0 ** (-8.0 * (h + 1) / HEADS)
            q = _headnorm(q_ref[:, sl].astype(F32), qg_ref[...])[0].astype(BF16)
            kp = _headnorm(kp_ref[:, sl].astype(F32), kg_ref[...])[0].astype(BF16)
            kc = _headnorm(kc_ref[:, sl].astype(F32), kg_ref[...])[0].astype(BF16)
            s1 = lax.dot_general(q, kp, _NT, preferred_element_type=F32) * scale
            s2 = lax.dot_general(q, kc, _NT, preferred_element_type=F32) * scale
            s1 = jnp.where(ok_prev, s1 - slope * b_prev, NEG_BIG)
            s2 = jnp.where(ok_cur, s2 - slope * b_cur, NEG_BIG)
            m = jnp.maximum(jnp.max(s1, axis=-1, keepdims=True), jnp.max(s2, axis=-1, keepdims=True))
            p1 = jnp.exp(s1 - m)
            p2 = jnp.exp(s2 - m)
            den = jnp.sum(p1, axis=-1, keepdims=True) + jnp.sum(p2, axis=-1, keepdims=True)
            inv = 1.0 / den
            o = jnp.dot((p1 * inv).astype(BF16), vp_ref[:, sl], preferred_element_type=F32)
            o = o + jnp.dot((p2 * inv).astype(BF16), vc_ref[:, sl], preferred_element_type=F32)
            o_ref[:, sl] = o
            lse_ref[:, sl] = jnp.broadcast_to(m + jnp.log(den), (ATTN_BLOCK, HEAD_DIM))

    cur = pl.BlockSpec((ATTN_BLOCK, A_WIDTH), lambda r, n: (n, r))
    prev = pl.BlockSpec((ATTN_BLOCK, A_WIDTH), lambda r, n: (jnp.maximum(n - 1, 0), r))
    vec = pl.BlockSpec((1, HEAD_DIM), lambda r, n: (0, 0))
    shp = jax.ShapeDtypeStruct((l, w), F32)
    return pl.pallas_call(
        body, name="attn_fwd_d%d" % dil, out_shape=(shp, shp), grid=(dil, nb),
        in_specs=[cur, prev, cur, prev, cur, vec, vec], out_specs=(cur, cur),
        compiler_params=_params(("parallel", "parallel")),
    )(qv, kv, kv, vv, vv, qg.reshape(1, HEAD_DIM), kg.reshape(1, HEAD_DIM))


def _attn_combine(outs, lses, *, tr=256):
    t, w = outs[0].shape

    def body(o0, o1, o2, l0, l1, l2, y_ref, lse_ref):
        a0, a1, a2 = l0[...], l1[...], l2[...]
        m = jnp.maximum(jnp.maximum(a0, a1), a2)
        e0, e1, e2 = jnp.exp(a0 - m), jnp.exp(a1 - m), jnp.exp(a2 - m)
        s = e0 + e1 + e2
        inv = 1.0 / s
        y_ref[...] = ((e0 * inv) * o0[...] + (e1 * inv) * o1[...] + (e2 * inv) * o2[...]).astype(BF16)
        lse_ref[...] = m + jnp.log(s)

    row = pl.BlockSpec((tr, w), lambda i: (i, 0))
    return pl.pallas_call(
        body, name="attn_combine",
        out_shape=(jax.ShapeDtypeStruct((t, w), BF16), jax.ShapeDtypeStruct((t, w), F32)),
        grid=(t // tr,), in_specs=[row] * 6, out_specs=(row, row),
        compiler_params=_params(("parallel",)),
    )(*outs, *lses)


def _attn_delta(dy, y, *, tr=256):
    t, w = y.shape

    def body(dy_ref, y_ref, o_ref):
        for h in range(HEADS):
            sl = slice(h * HEAD_DIM, (h + 1) * HEAD_DIM)
            dlt = jnp.sum(dy_ref[:, sl] * y_ref[:, sl].astype(F32), axis=-1, keepdims=True)
            o_ref[:, sl] = jnp.broadcast_to(dlt, (tr, HEAD_DIM))

    row = pl.BlockSpec((tr, w), lambda i: (i, 0))
    return pl.pallas_call(
        body, name="attn_delta", out_shape=jax.ShapeDtypeStruct((t, w), F32), grid=(t // tr,),
        in_specs=[row, row], out_specs=row, compiler_params=_params(("parallel",)),
    )(dy, y)


def _attn_bwd(qv, kv, vv, dyv, lsev, dltv, qg, kg, *, dil):
    l, w = qv.shape
    nb = l // ATTN_BLOCK
    scale = HEAD_DIM ** -0.5

    def body(qc_ref, qn_ref, kp_ref, kc_ref, vp_ref, vc_ref, dyc_ref, dyn_ref, lc_ref, ln_ref,
             dc_ref, dn_ref, qg_ref, kg_ref, dq_ref, dk_ref, dv_ref, dqg_ref, dkg_ref):
        n = pl.program_id(1)
        first = (pl.program_id(0) == 0) & (n == 0)
        d_prev, d_cur, ok_prev, ok_cur = _attn_masks(n, nb)
        ok_t1 = ok_prev & (n > 0)
        ok_t3 = ok_prev & (n < nb - 1)
        b_prev = d_prev.astype(F32) * float(dil)
        b_cur = d_cur.astype(F32) * float(dil)
        qgv, kgv = qg_ref[...], kg_ref[...]
        dqg = jnp.zeros((1, HEAD_DIM), F32)
        dkg = jnp.zeros((1, HEAD_DIM), F32)
        for h in range(HEADS):
            sl = slice(h * HEAD_DIM, (h + 1) * HEAD_DIM)
            slope = 2.0 ** (-8.0 * (h + 1) / HEADS)
            qc, qc_hat, qc_r = _headnorm(qc_ref[:, sl].astype(F32), qgv)
            qn = _headnorm(qn_ref[:, sl].astype(F32), qgv)[0].astype(BF16)
            kp = _headnorm(kp_ref[:, sl].astype(F32), kgv)[0].astype(BF16)
            kc, kc_hat, kc_r = _headnorm(kc_ref[:, sl].astype(F32), kgv)
            qc = qc.astype(BF16)
            kc = kc.astype(BF16)
            vp, vc = vp_ref[:, sl], vc_ref[:, sl]
            dyc, dyn = dyc_ref[:, sl].astype(BF16), dyn_ref[:, sl].astype(BF16)

            def tile(q, k, v, dy, lse, dlt, ok, bias):
                s = lax.dot_general(q, k, _NT, preferred_element_type=F32) * scale
                p = jnp.where(ok, jnp.exp(jnp.where(ok, s - slope * bias, NEG_BIG) - lse), 0.0)
                dp = lax.dot_general(dy, v, _NT, preferred_element_type=F32)
                return p.astype(BF16), (p * (dp - dlt)).astype(BF16)

            p1, ds1 = tile(qc, kp, vp, dyc, lc_ref[:, sl], dc_ref[:, sl], ok_t1, b_prev)
            p2, ds2 = tile(qc, kc, vc, dyc, lc_ref[:, sl], dc_ref[:, sl], ok_cur, b_cur)
            p3, ds3 = tile(qn, kc, vc, dyn, ln_ref[:, sl], dn_ref[:, sl], ok_t3, b_prev)
            dqn = scale * (jnp.dot(ds1, kp, preferred_element_type=F32) + jnp.dot(ds2, kc, preferred_element_type=F32))
            dkn = scale * (lax.dot_general(ds2, qc, _TN, preferred_element_type=F32)
                           + lax.dot_general(ds3, qn, _TN, preferred_element_type=F32))
            dv = (lax.dot_general(p2, dyc, _TN, preferred_element_type=F32)
                  + lax.dot_general(p3, dyn, _TN, preferred_element_type=F32))
            dqg = dqg + jnp.sum(dqn * qc_hat, axis=0, keepdims=True)
            dkg = dkg + jnp.sum(dkn * kc_hat, axis=0, keepdims=True)
            dq_ref[:, sl] = _headnorm_bwd(dqn, qc_hat, qc_r, qgv).astype(BF16)
            dk_ref[:, sl] = _headnorm_bwd(dkn, kc_hat, kc_r, kgv).astype(BF16)
            dv_ref[:, sl] = dv.astype(BF16)

        @pl.when(first)
        def _():
            dqg_ref[...] = dqg
            dkg_ref[...] = dkg

        @pl.when(jnp.logical_not(first))
        def _():
            dqg_ref[...] += dqg
            dkg_ref[...] += dkg

    blk = (ATTN_BLOCK, A_WIDTH)
    cur = pl.BlockSpec(blk, lambda r, n: (n, r))
    prev = pl.BlockSpec(blk, lambda r, n: (jnp.maximum(n - 1, 0), r))
    nxt = pl.BlockSpec(blk, lambda r, n: (jnp.minimum(n + 1, nb - 1), r))
    vec = pl.BlockSpec((1, HEAD_DIM), lambda r, n: (0, 0))
    shp = jax.ShapeDtypeStruct((l, w), BF16)
    gshp = jax.ShapeDtypeStruct((1, HEAD_DIM), F32)
    return pl.pallas_call(
        body, name="attn_bwd_d%d" % dil, out_shape=(shp, shp, shp, gshp, gshp), grid=(dil, nb),
        in_specs=[cur, nxt, prev, cur, prev, cur, cur, nxt, cur, nxt, cur, nxt, vec, vec],
        out_specs=(cur, cur, cur, vec, vec),
        compiler_params=_params(("arbitrary", "arbitrary")),
    )(qv, qv, kv, kv, vv, vv, dyv, dyv, lsev, lsev, dltv, dltv,
      qg.reshape(1, HEAD_DIM), kg.reshape(1, HEAD_DIM))


def _prev_halo(tr, tc, col0):
    return pl.BlockSpec((HALO, tc), lambda j, i: (jnp.maximum(i * (tr // HALO) - 1, 0), col0 + j))


def _next_halo(tr, tc, col0, rows):
    last = rows // HALO - 1
    return pl.BlockSpec((HALO, tc), lambda j, i: (jnp.minimum((i + 1) * (tr // HALO), last), col0 + j))


def _cur_block(tr, tc, col0):
    return pl.BlockSpec((tr, tc), lambda j, i: (i, col0 + j))


def _gateconv_fwd(h, conv_w, *, col0, tr=512, tc=256):
    t = h.shape[0]
    width = conv_w.shape[1]
    nc = width // tc
    c0 = col0 // tc

    def body(bg_ref, cg_ref, xt_ref, cgh_ref, xth_ref, w_ref, y_ref, pad_ref):
        i = pl.program_id(1)
        halo = cgh_ref[...].astype(F32) * xth_ref[...].astype(F32)
        pad_ref[0:HALO, :] = jnp.where(i > 0, halo, 0.0)
        pad_ref[HALO:HALO + tr, :] = cg_ref[...].astype(F32) * xt_ref[...].astype(F32)
        conv = None
        for j in range(B_CONV):
            term = w_ref[j:j + 1, :] * pad_ref[HALO - (B_CONV - 1) + j:HALO - (B_CONV - 1) + j + tr, :]
            conv = term if conv is None else conv + term
        y_ref[...] = (bg_ref[...].astype(F32) * conv).astype(BF16)

    return pl.pallas_call(
        body, name="gateconv_fwd", out_shape=jax.ShapeDtypeStruct((t, width), BF16), grid=(nc, t // tr),
        in_specs=[_cur_block(tr, tc, c0), _cur_block(tr, tc, c0 + nc), _cur_block(tr, tc, c0 + 2 * nc),
                  _prev_halo(tr, tc, c0 + nc), _prev_halo(tr, tc, c0 + 2 * nc),
                  pl.BlockSpec((8, tc), lambda j, i: (0, j))],
        out_specs=_cur_block(tr, tc, 0),
        scratch_shapes=[pltpu.VMEM((HALO + tr, tc), F32)],
        compiler_params=_params(("parallel", "arbitrary")),
    )(h, h, h, h, h, _pad_rows(conv_w, 8))


def _pad_rows(w, rows):
    return jnp.pad(w, ((0, rows - w.shape[0]), (0, 0)))


def _gateconv_bwd(h, dy, conv_w, *, col0, dcol0, tr=512, tc=256):
    t = h.shape[0]
    width = conv_w.shape[1]
    nc = width // tc
    c0 = col0 // tc
    dc0 = dcol0 // tc
    nt = t // tr

    def body(bg_ref, cg_ref, xt_ref, cgh_ref, xth_ref, bgn_ref, dy_ref, dyn_ref, w_ref,
             dbg_ref, dcg_ref, dxt_ref, dw_ref, pad_ref, padd_ref):
        i = pl.program_id(1)
        cg = cg_ref[...].astype(F32)
        xt = xt_ref[...].astype(F32)
        bg = bg_ref[...].astype(F32)
        dyv = dy_ref[...]
        halo = cgh_ref[...].astype(F32) * xth_ref[...].astype(F32)
        pad_ref[0:HALO, :] = jnp.where(i > 0, halo, 0.0)
        pad_ref[HALO:HALO + tr, :] = cg * xt
        dconv = dyv * bg
        padd_ref[0:tr, :] = dconv
        padd_ref[tr:tr + HALO, :] = jnp.where(i < nt - 1, dyn_ref[...] * bgn_ref[...].astype(F32), 0.0)
        conv = None
        du = None
        dws = []
        for j in range(B_CONV):
            off = HALO - (B_CONV - 1) + j
            shifted = pad_ref[off:off + tr, :]
            term = w_ref[j:j + 1, :] * shifted
            conv = term if conv is None else conv + term
            dws.append(jnp.sum(dconv * shifted, axis=0, keepdims=True))
            back = w_ref[j:j + 1, :] * padd_ref[B_CONV - 1 - j:B_CONV - 1 - j + tr, :]
            du = back if du is None else du + back
        dbg_ref[...] = (dyv * conv).astype(BF16)
        dcg_ref[...] = (du * xt).astype(BF16)
        dxt_ref[...] = (du * cg).astype(BF16)
        dw = _stack_rows(dws, 8, tc)

        @pl.when(i == 0)
        def _():
            dw_ref[...] = dw

        @pl.when(i > 0)
        def _():
            dw_ref[...] += dw

    oshp = jax.ShapeDtypeStruct((t, width), BF16)
    return pl.pallas_call(
        body, name="gateconv_bwd",
        out_shape=(oshp, oshp, oshp, jax.ShapeDtypeStruct((8, width), F32)), grid=(nc, nt),
        in_specs=[_cur_block(tr, tc, c0), _cur_block(tr, tc, c0 + nc), _cur_block(tr, tc, c0 + 2 * nc),
                  _prev_halo(tr, tc, c0 + nc), _prev_halo(tr, tc, c0 + 2 * nc),
                  _next_halo(tr, tc, c0, t), _cur_block(tr, tc, dc0), _next_halo(tr, tc, dc0, t),
                  pl.BlockSpec((8, tc), lambda j, i: (0, j))],
        out_specs=(_cur_block(tr, tc, 0), _cur_block(tr, tc, 0), _cur_block(tr, tc, 0),
                   pl.BlockSpec((8, tc), lambda j, i: (0, j))),
        scratch_shapes=[pltpu.VMEM((HALO + tr, tc), F32), pltpu.VMEM((tr + HALO, tc), F32)],
        compiler_params=_params(("parallel", "arbitrary")),
    )(h, h, h, h, h, h, dy, dy, _pad_rows(conv_w, 8))


def _stack_rows(rows, n, width):
    idx = lax.broadcasted_iota(jnp.int32, (n, width), 0)
    out = jnp.zeros((n, width), F32)
    for j, r in enumerate(rows):
        out = jnp.where(idx == j, r, out)
    return out


CONV_ROWS = 64


def _glu_conv_fwd(hod, conv_w, conv_b, *, tr=512, tc=256):
    t = hod.shape[0]
    width = conv_w.shape[1]
    nc = width // tc

    def body(val_ref, gate_ref, valh_ref, gateh_ref, w_ref, b_ref, u1_ref, pad_ref):
        i = pl.program_id(1)
        halo = valh_ref[...].astype(F32) * _sigmoid(gateh_ref[...].astype(F32))
        pad_ref[0:HALO, :] = jnp.where(i > 0, halo, 0.0)
        pad_ref[HALO:HALO + tr, :] = val_ref[...].astype(F32) * _sigmoid(gate_ref[...].astype(F32))
        for c in range(tr // CONV_ROWS):
            base = HALO + c * CONV_ROWS - (C_CONV - 1)
            acc = None
            for j in range(C_CONV):
                term = w_ref[j:j + 1, :] * pad_ref[base + j:base + j + CONV_ROWS, :]
                acc = term if acc is None else acc + term
            u1_ref[c * CONV_ROWS:(c + 1) * CONV_ROWS, :] = acc + b_ref[...]

    return pl.pallas_call(
        body, name="glu_conv_fwd", out_shape=jax.ShapeDtypeStruct((t, width), F32), grid=(nc, t // tr),
        in_specs=[_cur_block(tr, tc, 0), _cur_block(tr, tc, nc), _prev_halo(tr, tc, 0), _prev_halo(tr, tc, nc),
                  pl.BlockSpec((32, tc), lambda j, i: (0, j)), pl.BlockSpec((1, tc), lambda j, i: (0, j))],
        out_specs=_cur_block(tr, tc, 0),
        scratch_shapes=[pltpu.VMEM((HALO + tr, tc), F32)],
        compiler_params=_params(("parallel", "arbitrary")),
    )(hod, hod, hod, hod, _pad_rows(conv_w, 32), conv_b.reshape(1, width))


def _ln_silu_fwd(u1, g, b, *, tr=256):
    t, width = u1.shape

    def body(u_ref, g_ref, b_ref, o_ref):
        uv = u_ref[...]
        mu = jnp.mean(uv, axis=-1, keepdims=True)
        var = jnp.mean(jnp.square(uv - mu), axis=-1, keepdims=True)
        u2 = ((uv - mu) * lax.rsqrt(var + EPS)) * g_ref[...] + b_ref[...]
        o_ref[...] = (u2 * _sigmoid(u2)).astype(BF16)

    row = pl.BlockSpec((tr, width), lambda i: (i, 0))
    vec = pl.BlockSpec((1, width), lambda i: (0, 0))
    return pl.pallas_call(
        body, name="ln_silu_fwd", out_shape=jax.ShapeDtypeStruct((t, width), BF16), grid=(t // tr,),
        in_specs=[row, vec, vec], out_specs=row, compiler_params=_params(("parallel",)),
    )(u1, g.reshape(1, width), b.reshape(1, width))


def _ln_silu_bwd(du, u1, g, b, *, col0, tr=256):
    t, width = u1.shape

    def body(du_ref, u_ref, g_ref, b_ref, du1_ref, dg_ref, db_ref, dcb_ref):
        uv = u_ref[...]
        mu = jnp.mean(uv, axis=-1, keepdims=True)
        var = jnp.mean(jnp.square(uv - mu), axis=-1, keepdims=True)
        rstd = lax.rsqrt(var + EPS)
        xh = (uv - mu) * rstd
        u2 = xh * g_ref[...] + b_ref[...]
        sig = _sigmoid(u2)
        du2 = du_ref[...] * (sig * (1.0 + u2 * (1.0 - sig)))
        dxh = du2 * g_ref[...]
        du1 = rstd * (dxh - jnp.mean(dxh, axis=-1, keepdims=True)
                      - xh * jnp.mean(dxh * xh, axis=-1, keepdims=True))
        du1_ref[...] = du1
        parts = (jnp.sum(du2 * xh, axis=0, keepdims=True), jnp.sum(du2, axis=0, keepdims=True),
                 jnp.sum(du1, axis=0, keepdims=True))

        @pl.when(pl.program_id(0) == 0)
        def _():
            dg_ref[...], db_ref[...], dcb_ref[...] = parts

        @pl.when(pl.program_id(0) > 0)
        def _():
            dg_ref[...] += parts[0]
            db_ref[...] += parts[1]
            dcb_ref[...] += parts[2]

    row = pl.BlockSpec((tr, width), lambda i: (i, 0))
    vec = pl.BlockSpec((1, width), lambda i: (0, 0))
    vshp = jax.ShapeDtypeStruct((1, width), F32)
    return pl.pallas_call(
        body, name="ln_silu_bwd", out_shape=(jax.ShapeDtypeStruct((t, width), F32), vshp, vshp, vshp),
        grid=(t // tr,),
        in_specs=[pl.BlockSpec((tr, width), lambda i: (i, col0 // width)), row, vec, vec],
        out_specs=(row, vec, vec, vec), compiler_params=_params(("arbitrary",)),
    )(du, u1, g.reshape(1, width), b.reshape(1, width))


def _glu_conv_bwd(hod, du1, conv_w, *, tr=512, tc=256):
    t = hod.shape[0]
    width = conv_w.shape[1]
    nc = width // tc
    nt = t // tr

    def body(val_ref, gate_ref, valh_ref, gateh_ref, du_ref, dun_ref, w_ref,
             dval_ref, dgate_ref, dw_ref, pad_ref, padd_ref, du0_ref):
        i = pl.program_id(1)
        val = val_ref[...].astype(F32)
        sig = _sigmoid(gate_ref[...].astype(F32))
        halo = valh_ref[...].astype(F32) * _sigmoid(gateh_ref[...].astype(F32))
        pad_ref[0:HALO, :] = jnp.where(i > 0, halo, 0.0)
        pad_ref[HALO:HALO + tr, :] = val * sig
        padd_ref[0:tr, :] = du_ref[...]
        padd_ref[tr:tr + HALO, :] = jnp.where(i < nt - 1, dun_ref[...], 0.0)
        dws = [jnp.zeros((1, tc), F32)] * C_CONV
        for c in range(tr // CONV_ROWS):
            r0 = c * CONV_ROWS
            duc = padd_ref[r0:r0 + CONV_ROWS, :]
            acc = None
            for j in range(C_CONV):
                back = w_ref[j:j + 1, :] * padd_ref[r0 + C_CONV - 1 - j:r0 + C_CONV - 1 - j + CONV_ROWS, :]
                acc = back if acc is None else acc + back
                off = HALO + r0 - (C_CONV - 1) + j
                dws[j] = dws[j] + jnp.sum(duc * pad_ref[off:off + CONV_ROWS, :], axis=0, keepdims=True)
            du0_ref[r0:r0 + CONV_ROWS, :] = acc
        du0 = du0_ref[...]
        dval_ref[...] = (du0 * sig).astype(BF16)
        dgate_ref[...] = (du0 * val * sig * (1.0 - sig)).astype(BF16)
        dw = _stack_rows(dws, 32, tc)

        @pl.when(i == 0)
        def _():
            dw_ref[...] = dw

        @pl.when(i > 0)
        def _():
            dw_ref[...] += dw

    oshp = jax.ShapeDtypeStruct((t, width), BF16)
    wspec = pl.BlockSpec((32, tc), lambda j, i: (0, j))
    return pl.pallas_call(
        body, name="glu_conv_bwd", out_shape=(oshp, oshp, jax.ShapeDtypeStruct((32, width), F32)), grid=(nc, nt),
        in_specs=[_cur_block(tr, tc, 0), _cur_block(tr, tc, nc), _prev_halo(tr, tc, 0), _prev_halo(tr, tc, nc),
                  _cur_block(tr, tc, 0), _next_halo(tr, tc, 0, t), wspec],
        out_specs=(_cur_block(tr, tc, 0), _cur_block(tr, tc, 0), wspec),
        scratch_shapes=[pltpu.VMEM((HALO + tr, tc), F32), pltpu.VMEM((tr + HALO, tc), F32),
                        pltpu.VMEM((tr, tc), F32)],
        compiler_params=_params(("parallel", "arbitrary")),
    )(hod, hod, hod, hod, du1, du1, _pad_rows(conv_w, 32))


def _pooled(pad_ref, g, kw, tr, i):
    gw = pad_ref.shape[1] // len(D_WINDOWS)
    cols = slice(g * gw, (g + 1) * gw)
    tot = None
    for j in range(kw):
        sh = pad_ref[HALO - j:HALO - j + tr, cols]
        tot = sh if tot is None else tot + sh
    return tot / _window_count(tr, gw, kw, i * tr) - pad_ref[HALO:HALO + tr, cols]


def _window_count(rows, width, kw, row0):
    t1 = (lax.broadcasted_iota(jnp.int32, (rows, width), 0) + (row0 + 1)).astype(F32)
    return jnp.minimum(t1, float(kw))


def _pool_fwd(hod, pool_w, pool_scale, *, tr=256):
    t = hod.shape[0]
    width = pool_scale.shape[0]
    ng = len(D_WINDOWS)
    gw = width // ng

    def body(z_ref, zh_ref, w_ref, s_ref, y_ref, pad_ref):
        i = pl.program_id(1)
        pad_ref[0:HALO, :] = jnp.where(i > 0, zh_ref[...].astype(F32), 0.0)
        pad_ref[HALO:HALO + tr, :] = z_ref[...].astype(F32)
        for g, kw in enumerate(D_WINDOWS):
            cols = slice(g * gw, (g + 1) * gw)
            pre = jnp.dot(_pooled(pad_ref, g, kw, tr, i).astype(BF16), w_ref[g], preferred_element_type=F32)
            y_ref[:, cols] = (pre * s_ref[:, cols]).astype(BF16)

    return pl.pallas_call(
        body, name="pool_fwd", out_shape=jax.ShapeDtypeStruct((t, width), BF16), grid=(1, t // tr),
        in_specs=[_cur_block(tr, width, 2), _prev_halo(tr, width, 2),
                  pl.BlockSpec((ng, gw, gw), lambda j, i: (0, 0, 0)), pl.BlockSpec((1, width), lambda j, i: (0, 0))],
        out_specs=_cur_block(tr, width, 0),
        scratch_shapes=[pltpu.VMEM((HALO + tr, width), F32)],
        compiler_params=_params(("parallel", "arbitrary")),
    )(hod, hod, pool_w, pool_scale.reshape(1, width))


def _pool_bwd(hod, dy, pool_w, pool_scale, *, dcol0, tr=256):
    t = hod.shape[0]
    width = pool_scale.shape[0]
    ng = len(D_WINDOWS)
    gw = width // ng
    nt = t // tr

    def body(z_ref, zh_ref, dy_ref, dyn_ref, w_ref, s_ref, dz_ref, dw_ref, ds_ref, pad_ref, pade_ref):
        i = pl.program_id(1)
        pad_ref[0:HALO, :] = jnp.where(i > 0, zh_ref[...].astype(F32), 0.0)
        pad_ref[HALO:HALO + tr, :] = z_ref[...].astype(F32)
        dws = []
        dss = []
        for g, kw in enumerate(D_WINDOWS):
            cols = slice(g * gw, (g + 1) * gw)
            wg = w_ref[g]
            dyc = dy_ref[:, cols]
            dpre = (dyc * s_ref[:, cols]).astype(BF16)
            dpre_n = (dyn_ref[:, cols] * s_ref[:, cols]).astype(BF16)
            dpl = lax.dot_general(dpre, wg, _NT, preferred_element_type=F32)
            dpl_n = lax.dot_general(dpre_n, wg, _NT, preferred_element_type=F32)
            pade_ref[0:tr, cols] = dpl / _window_count(tr, gw, kw, i * tr)
            pade_ref[tr:tr + HALO, cols] = jnp.where(i < nt - 1, dpl_n / _window_count(HALO, gw, kw, (i + 1) * tr), 0.0)
            tot = None
            for j in range(kw):
                sh = pade_ref[j:j + tr, cols]
                tot = sh if tot is None else tot + sh
            dz_ref[:, cols] = (tot - dpl).astype(BF16)
            pooled = _pooled(pad_ref, g, kw, tr, i).astype(BF16)
            pre = jnp.dot(pooled, wg, preferred_element_type=F32)
            dss.append(jnp.sum(dyc * pre, axis=0, keepdims=True))
            dws.append(lax.dot_general(pooled, dpre, _TN, preferred_element_type=F32))

        @pl.when(i == 0)
        def _():
            for g in range(ng):
                dw_ref[g] = dws[g]
                ds_ref[:, g * gw:(g + 1) * gw] = dss[g]

        @pl.when(i > 0)
        def _():
            for g in range(ng):
                dw_ref[g] += dws[g]
                ds_ref[:, g * gw:(g + 1) * gw] += dss[g]

    dc = dcol0 // width
    wspec = pl.BlockSpec((ng, gw, gw), lambda j, i: (0, 0, 0))
    vspec = pl.BlockSpec((1, width), lambda j, i: (0, 0))
    return pl.pallas_call(
        body, name="pool_bwd",
        out_shape=(jax.ShapeDtypeStruct((t, width), BF16), jax.ShapeDtypeStruct((ng, gw, gw), F32),
                   jax.ShapeDtypeStruct((1, width), F32)),
        grid=(1, nt),
        in_specs=[_cur_block(tr, width, 2), _prev_halo(tr, width, 2), _cur_block(tr, width, dc),
                  _next_halo(tr, width, dc, t), wspec, vspec],
        out_specs=(_cur_block(tr, width, 0), wspec, vspec),
        scratch_shapes=[pltpu.VMEM((HALO + tr, width), F32), pltpu.VMEM((tr + HALO, width), F32)],
        compiler_params=_params(("arbitrary", "arbitrary")),
    )(hod, hod, dy, dy, pool_w, pool_scale.reshape(1, width))


TM = 512
TN = 512


def _ffn_fwd(x, g, w1, w3, w2):
    t, d = x.shape
    f = w1.shape[1]
    xn = _rmsnorm_fwd(x, g)
    a, b, h = _ffn_up(xn, w1, w3, tm=TM, tn=TN)
    y = _matmul([(h, w2)], m=t, n=d, k=f, tm=TM, tn=TN, tk=f, res=x, alpha=0.5, name="ffn_down")
    return y, (x, xn, a, b, h)


def _ffn_bwd(dx, dxb, saved, g, w1, w3, w2):
    x, xn, a, b, h = saved
    t, d = x.shape
    f = w1.shape[1]
    da, db = _ffn_dh(dxb, w2, a, b, tm=TM, tn=TN)
    dw2 = _matmul([(h, dxb)], ta=True, m=f, n=d, k=t, tm=TM, tn=TN, tk=t, alpha=0.5, out_dtype=BF16, name="ffn_dw2")
    dw1 = _matmul([(xn, da)], ta=True, m=d, n=f, k=t, tm=TM, tn=TN, tk=t, out_dtype=BF16, name="ffn_dw1")
    dw3 = _matmul([(xn, db)], ta=True, m=d, n=f, k=t, tm=TM, tn=TN, tk=t, out_dtype=BF16, name="ffn_dw3")
    dxn = _matmul([(da, w1), (db, w3)], tb=True, m=t, n=d, k=f, tm=TM, tn=TN, tk=f // 2, name="ffn_dxn")
    dx, dxb, dg = _rmsnorm_bwd(dxn, x, g, dx)
    return dx, dxb, dg, dw1, dw3, dw2


def _mix_out_fwd(x, ycat, w_out):
    t, d = x.shape
    return _matmul([(ycat, w_out)], m=t, n=d, k=d, tm=TM, tn=TN, tk=d, res=x, name="mix_out")


def _mix_out_bwd(dxb, ycat, w_out):
    t, d = dxb.shape
    dycat = _matmul([(dxb, w_out)], tb=True, m=t, n=d, k=d, tm=TM, tn=TN, tk=d, name="mix_dy")
    dw_out = _matmul([(ycat, dxb)], ta=True, m=d, n=d, k=t, tm=TM, tn=TN, tk=t, out_dtype=BF16, name="mix_dw_out")
    return dycat, dw_out


def _mix_in_bwd(dh, xn, w_in, x, g, dx, *, tk):
    t, d = x.shape
    n_in = w_in.shape[1]
    dxn = _matmul([(dh, w_in)], tb=True, m=t, n=d, k=n_in, tm=TM, tn=TN, tk=tk, name="mix_dxn")
    dw_in = _matmul([(xn, dh)], ta=True, m=d, n=n_in, k=t, tm=TM, tn=TN, tk=t, out_dtype=BF16, name="mix_dw_in")
    dx, dxb, dg = _rmsnorm_bwd(dxn, x, g, dx)
    return dx, dxb, dg, dw_in


def _group_view(a, col0, dil):
    t = a.shape[0]
    return a[:, col0:col0 + A_WIDTH].reshape(t // dil, dil * A_WIDTH)


def _even_fwd(x, g, w_in, qg, kg, conv_w, w_out):
    t, d = x.shape
    n_in = w_in.shape[1]
    nq = len(A_DILATIONS) * A_WIDTH
    xn = _rmsnorm_fwd(x, g)
    h = _matmul([(xn, w_in)], m=t, n=n_in, k=d, tm=TM, tn=TN, tk=d, out_dtype=BF16, name="ev_in")
    outs, lses = [], []
    for gi, dil in enumerate(A_DILATIONS):
        o, l = _attn_fwd(_group_view(h, gi * A_WIDTH, dil), _group_view(h, nq + gi * A_WIDTH, dil),
                         _group_view(h, 2 * nq + gi * A_WIDTH, dil), qg, kg, dil=dil)
        outs.append(o.reshape(t, A_WIDTH))
        lses.append(l.reshape(t, A_WIDTH))
    ya, lse = _attn_combine(outs, lses)
    yb = _gateconv_fwd(h, conv_w, col0=3 * nq)
    ycat = jnp.concatenate([ya, yb], axis=1)
    return _mix_out_fwd(x, ycat, w_out), (x, xn, h, ya, lse, ycat)


def _even_bwd(dx, dxb, saved, g, w_in, qg, kg, conv_w, w_out):
    x, xn, h, ya, lse, ycat = saved
    t, d = x.shape
    nq = len(A_DILATIONS) * A_WIDTH
    dycat, dw_out = _mix_out_bwd(dxb, ycat, w_out)
    dlt = _attn_delta(dycat, ya)
    dqs, dks, dvs = [], [], []
    dqg = jnp.zeros((HEAD_DIM,), F32)
    dkg = jnp.zeros((HEAD_DIM,), F32)
    for gi, dil in enumerate(A_DILATIONS):
        dq, dk, dv, dqg_i, dkg_i = _attn_bwd(
            _group_view(h, gi * A_WIDTH, dil), _group_view(h, nq + gi * A_WIDTH, dil),
            _group_view(h, 2 * nq + gi * A_WIDTH, dil), _group_view(dycat, 0, dil),
            _group_view(lse, 0, dil), _group_view(dlt, 0, dil), qg, kg, dil=dil)
        dqs.append(dq.reshape(t, A_WIDTH))
        dks.append(dk.reshape(t, A_WIDTH))
        dvs.append(dv.reshape(t, A_WIDTH))
        dqg = dqg + dqg_i.reshape(HEAD_DIM)
        dkg = dkg + dkg_i.reshape(HEAD_DIM)
    dbg, dcg, dxt, dcw = _gateconv_bwd(h, dycat, conv_w, col0=3 * nq, dcol0=A_WIDTH)
    dh = jnp.concatenate(dqs + dks + dvs + [dbg, dcg, dxt], axis=1)
    dx, dxb, dg, dw_in = _mix_in_bwd(dh, xn, w_in, x, g, dx, tk=2048)
    return dx, dxb, dg, dw_in, dqg, dkg, dcw[:B_CONV], dw_out


def _odd_fwd(x, g, w_in, conv_w, conv_b, ln_g, ln_b, pool_w, pool_scale, w_out):
    t, d = x.shape
    n_in = w_in.shape[1]
    xn = _rmsnorm_fwd(x, g)
    hod = _matmul([(xn, w_in)], m=t, n=n_in, k=d, tm=TM, tn=TN, tk=d, out_dtype=BF16, name="od_in")
    u1 = _glu_conv_fwd(hod, conv_w, conv_b)
    u = _ln_silu_fwd(u1, ln_g, ln_b)
    yd = _pool_fwd(hod, pool_w.astype(BF16), pool_scale)
    ycat = jnp.concatenate([u, yd], axis=1)
    return _mix_out_fwd(x, ycat, w_out), (x, xn, hod, u1, ycat)


def _odd_bwd(dx, dxb, saved, g, w_in, conv_w, conv_b, ln_g, ln_b, pool_w, pool_scale, w_out):
    x, xn, hod, u1, ycat = saved
    width = conv_w.shape[1]
    dycat, dw_out = _mix_out_bwd(dxb, ycat, w_out)
    du1, dlg, dlb, dcb = _ln_silu_bwd(dycat, u1, ln_g, ln_b, col0=0)
    dval, dgate, dcw = _glu_conv_bwd(hod, du1, conv_w)
    dz, dpw, dps = _pool_bwd(hod, dycat, pool_w.astype(BF16), pool_scale, dcol0=width)
    dh = jnp.concatenate([dval, dgate, dz], axis=1)
    dx, dxb, dg, dw_in = _mix_in_bwd(dh, xn, w_in, x, g, dx, tk=dh.shape[1])
    return (dx, dxb, dg, dw_in, dcw[:C_CONV], dcb.reshape(width), dlg.reshape(width), dlb.reshape(width),
            dpw, dps.reshape(width), dw_out)


def _local_step(x, target, wts):
    depth = wts["norm_g"].shape[0]
    saved = []
    for layer in range(depth):
        i = layer // 2
        x, s0 = _ffn_fwd(x, wts["norm_g"][layer, 0], wts["ffn_w1"][2 * layer], wts["ffn_w3"][2 * layer],
                         wts["ffn_w2"][2 * layer])
        if layer % 2 == 0:
            x, s1 = _even_fwd(x, wts["norm_g"][layer, 1], wts["ev_w_in"][i], wts["ev_q_gain"][i],
                              wts["ev_k_gain"][i], wts["ev_conv_w"][i], wts["ev_w_out"][i])
        else:
            x, s1 = _odd_fwd(x, wts["norm_g"][layer, 1], wts["od_w_in"][i], wts["od_conv_w"][i],
                             wts["od_conv_b"][i], wts["od_ln_g"][i], wts["od_ln_b"][i], wts["od_pool_w"][i],
                             wts["od_pool_scale"][i], wts["od_w_out"][i])
        x, s2 = _ffn_fwd(x, wts["norm_g"][layer, 2], wts["ffn_w1"][2 * layer + 1], wts["ffn_w3"][2 * layer + 1],
                         wts["ffn_w2"][2 * layer + 1])
        saved.append((s0, s1, s2))
    dx, dxb, sq = _loss_head(x, target)

    n_even, n_odd = (depth + 1) // 2, depth // 2
    gr = {k: [None] * (2 * depth) for k in ("ffn_w1", "ffn_w3", "ffn_w2")}
    for k in ("ev_w_in", "ev_q_gain", "ev_k_gain", "ev_conv_w", "ev_w_out"):
        gr[k] = [None] * n_even
    for k in ("od_w_in", "od_conv_w", "od_conv_b", "od_ln_g", "od_ln_b", "od_pool_w", "od_pool_scale", "od_w_out"):
        gr[k] = [None] * n_odd
    dnorm = [[None] * 3 for _ in range(depth)]
    for layer in reversed(range(depth)):
        i = layer // 2
        s0, s1, s2 = saved[layer]
        j = 2 * layer + 1
        dx, dxb, dnorm[layer][2], gr["ffn_w1"][j], gr["ffn_w3"][j], gr["ffn_w2"][j] = _ffn_bwd(
            dx, dxb, s2, wts["norm_g"][layer, 2], wts["ffn_w1"][j], wts["ffn_w3"][j], wts["ffn_w2"][j])
        if layer % 2 == 0:
            (dx, dxb, dnorm[layer][1], gr["ev_w_in"][i], gr["ev_q_gain"][i], gr["ev_k_gain"][i],
             gr["ev_conv_w"][i], gr["ev_w_out"][i]) = _even_bwd(
                dx, dxb, s1, wts["norm_g"][layer, 1], wts["ev_w_in"][i], wts["ev_q_gain"][i],
                wts["ev_k_gain"][i], wts["ev_conv_w"][i], wts["ev_w_out"][i])
        else:
            (dx, dxb, dnorm[layer][1], gr["od_w_in"][i], gr["od_conv_w"][i], gr["od_conv_b"][i], gr["od_ln_g"][i],
             gr["od_ln_b"][i], gr["od_pool_w"][i], gr["od_pool_scale"][i], gr["od_w_out"][i]) = _odd_bwd(
                dx, dxb, s1, wts["norm_g"][layer, 1], wts["od_w_in"][i], wts["od_conv_w"][i],
                wts["od_conv_b"][i], wts["od_ln_g"][i], wts["od_ln_b"][i], wts["od_pool_w"][i],
                wts["od_pool_scale"][i], wts["od_w_out"][i])
        j = 2 * layer
        dx, dxb, dnorm[layer][0], gr["ffn_w1"][j], gr["ffn_w3"][j], gr["ffn_w2"][j] = _ffn_bwd(
            dx, dxb, s0, wts["norm_g"][layer, 0], wts["ffn_w1"][j], wts["ffn_w3"][j], wts["ffn_w2"][j])
    gr["norm_g"] = jnp.stack([jnp.stack(r) for r in dnorm])
    return sq, dx, gr


HBM_SPEC = pl.BlockSpec(memory_space=pltpu.HBM)


def _place():
    x, y, c = lax.axis_index("x"), lax.axis_index("y"), lax.axis_index("c")
    chips = [(1 - x, y), (x, 1 - y), (1 - x, 1 - y)]
    return x, y, c, chips


def _chip_index(x, y):
    return 2 * x + y


def _ds(start, size, align):
    if isinstance(start, int):
        return pl.ds(start, size)
    return pl.ds(pl.multiple_of(start, align), size)


def _half(ref, axis, h):
    r, c = ref.shape[-2:]
    if axis == 1:
        return ref.at[_ds(h * (r // 2), r // 2, 16), :]
    return ref.at[:, _ds(h * (c // 2), c // 2, 128)]


def _chunk(ref, axis, j, n=N_CHIPS):
    r, c = ref.shape[-2:]
    if axis == 1:
        return ref.at[:, _ds(j * (c // n), c // n, 128)]
    return ref.at[_ds(j * (r // n), r // n, 16), :]


def _remote(src, dst, send_sem, recv_sem, device):
    return pltpu.make_async_remote_copy(src_ref=src, dst_ref=dst, send_sem=send_sem, recv_sem=recv_sem,
                                        device_id=device, device_id_type=MESH)


def _ag_matrix(shard, axis):
    r, c = shard.shape
    full = (r, N_CHIPS * c) if axis == 1 else (N_CHIPS * r, c)

    def body(s_ref, o_ref, send_sems, recv_sems, local_sem):
        x, y, cc, chips = _place()
        me = _chip_index(x, y)
        own = pltpu.make_async_copy(s_ref, _chunk(o_ref, axis, me), local_sem)
        own.start()

        def slot(chip, h):
            return _half(_chunk(o_ref, axis, chip), axis, h)

        first = [_remote(_half(s_ref, axis, cc), slot(me, cc), send_sems.at[k], recv_sems.at[k], (*chip, cc))
                 for k, chip in enumerate(chips)]
        for cp in first:
            cp.start()
        passed = []
        for k, chip in enumerate(chips):
            got = slot(_chip_index(*chip), cc)
            _remote(got, got, send_sems.at[k], recv_sems.at[k], (x, y, cc)).wait_recv()
            fwd = _remote(got, got, send_sems.at[3 + k], recv_sems.at[3 + k], (x, y, 1 - cc))
            fwd.start()
            passed.append(fwd)
        for k, chip in enumerate(chips):
            got = slot(_chip_index(*chip), 1 - cc)
            _remote(got, got, send_sems.at[3 + k], recv_sems.at[3 + k], (x, y, cc)).wait_recv()
        for cp in first + passed:
            cp.wait_send()
        own.wait()

    return pl.pallas_call(
        body, name="ag_matrix", out_shape=jax.ShapeDtypeStruct(full, shard.dtype),
        in_specs=[HBM_SPEC], out_specs=HBM_SPEC,
        scratch_shapes=[pltpu.SemaphoreType.DMA((6,)), pltpu.SemaphoreType.DMA((6,)), pltpu.SemaphoreType.DMA],
    )(shard)


def _rs_pair(g, axis):
    r, c = g.shape
    hshape = (r // 2, c) if axis == 1 else (r, c // 2)

    def body(g_ref, o_ref, send_sem, recv_sem):
        x, y, cc, _ = _place()
        cp = _remote(_half(g_ref, axis, 1 - cc), o_ref, send_sem, recv_sem, (x, y, 1 - cc))
        cp.start()
        cp.wait()

    return pl.pallas_call(
        body, name="rs_pair", out_shape=jax.ShapeDtypeStruct(hshape, g.dtype),
        in_specs=[HBM_SPEC], out_specs=HBM_SPEC,
        scratch_shapes=[pltpu.SemaphoreType.DMA, pltpu.SemaphoreType.DMA],
    )(g)


def _add_pair(g, got, cc, axis, *, tr, tc):
    r, c = g.shape
    hr, hc = got.shape
    if axis == 1:
        g_map = lambda i, j, s: (s[0] * (hr // tr) + i, j)
    else:
        g_map = lambda i, j, s: (i, s[0] * (hc // tc) + j)

    def body(s_ref, g_ref, r_ref, o_ref):
        o_ref[...] = (g_ref[...].astype(F32) + r_ref[...].astype(F32)).astype(BF16)

    blk = pl.BlockSpec((tr, tc), lambda i, j, s: (i, j))
    return pl.pallas_call(
        body, name="add_pair", out_shape=jax.ShapeDtypeStruct((hr, hc), BF16),
        grid_spec=pltpu.PrefetchScalarGridSpec(
            num_scalar_prefetch=1, grid=(hr // tr, hc // tc),
            in_specs=[pl.BlockSpec((tr, tc), g_map), blk], out_specs=blk),
        compiler_params=_params(("parallel", "parallel")),
    )(cc, g, got)


def _rs_chips(p, axis):
    r, c = p.shape
    piece = (r, c // N_CHIPS) if axis == 1 else (r // N_CHIPS, c)

    def body(p_ref, o_ref, send_sems, recv_sems):
        x, y, cc, chips = _place()
        cps = [_remote(_chunk(p_ref, axis, _chip_index(*chip)), o_ref.at[k], send_sems.at[k], recv_sems.at[k],
                       (*chip, cc)) for k, chip in enumerate(chips)]
        for cp in cps:
            cp.start()
        for cp in cps:
            cp.wait()

    return pl.pallas_call(
        body, name="rs_chips", out_shape=jax.ShapeDtypeStruct((3,) + piece, p.dtype),
        in_specs=[HBM_SPEC], out_specs=HBM_SPEC,
        scratch_shapes=[pltpu.SemaphoreType.DMA((3,)), pltpu.SemaphoreType.DMA((3,))],
    )(p)


def _add_chips(p, got, chip, axis, *, tr, tc):
    _, pr, pc = got.shape
    if axis == 1:
        p_map = lambda i, j, s: (i, s[0] * (pc // tc) + j)
    else:
        p_map = lambda i, j, s: (s[0] * (pr // tr) + i, j)

    def body(s_ref, p_ref, r_ref, o_ref):
        acc = p_ref[...].astype(F32)
        for k in range(3):
            acc = acc + r_ref[k].astype(F32)
        o_ref[...] = acc

    blk = pl.BlockSpec((tr, tc), lambda i, j, s: (i, j))
    return pl.pallas_call(
        body, name="add_chips", out_shape=jax.ShapeDtypeStruct((pr, pc), F32),
        grid_spec=pltpu.PrefetchScalarGridSpec(
            num_scalar_prefetch=1, grid=(pr // tr, pc // tc),
            in_specs=[pl.BlockSpec((tr, tc), p_map), pl.BlockSpec((3, tr, tc), lambda i, j, s: (0, i, j))],
            out_specs=blk),
        compiler_params=_params(("parallel", "parallel")),
    )(chip, p, got)


def _rs_final(halves, axis):
    n = len(halves)
    hr, hc = halves[0].shape
    shard = (2 * hr, hc) if axis == 1 else (hr, 2 * hc)

    def body(*refs):
        s_refs, o_ref = refs[:n], refs[n]
        send_sems, recv_sems, local_sems = refs[n + 1:]
        x, y, cc, _ = _place()
        cps = []
        for i in range(n):
            mine = _half(o_ref.at[i], axis, cc)
            loc = pltpu.make_async_copy(s_refs[i], mine, local_sems.at[i])
            loc.start()
            cp = _remote(s_refs[i], mine, send_sems.at[i], recv_sems.at[i], (x, y, 1 - cc))
            cp.start()
            cps.append((loc, cp))
        for i, (loc, cp) in enumerate(cps):
            other = _half(o_ref.at[i], axis, 1 - cc)
            cp.wait_send()
            _remote(s_refs[i], other, send_sems.at[i], recv_sems.at[i], (x, y, cc)).wait_recv()
            loc.wait()

    return pl.pallas_call(
        body, name="rs_final", out_shape=jax.ShapeDtypeStruct((n,) + shard, F32),
        in_specs=[HBM_SPEC] * n, out_specs=HBM_SPEC,
        scratch_shapes=[pltpu.SemaphoreType.DMA((n,)), pltpu.SemaphoreType.DMA((n,)), pltpu.SemaphoreType.DMA((n,))],
    )(*halves)


def _ag_small(packed):
    rows, cols = packed.shape

    def body(s_ref, o_ref, send_sems, recv_sems, local_sem):
        x, y, cc, chips = _place()
        me = _chip_index(x, y)
        own = pltpu.make_async_copy(s_ref, o_ref.at[me], local_sem)
        own.start()
        cps = [_remote(s_ref, o_ref.at[me], send_sems.at[k], recv_sems.at[k], (*chip, cc))
               for k, chip in enumerate(chips)]
        for cp in cps:
            cp.start()
        for k, chip in enumerate(chips):
            cps[k].wait_send()
            got = o_ref.at[_chip_index(*chip)]
            _remote(got, got, send_sems.at[k], recv_sems.at[k], (x, y, cc)).wait_recv()
        own.wait()

    return pl.pallas_call(
        body, name="ag_small", out_shape=jax.ShapeDtypeStruct((N_CHIPS, rows, cols), packed.dtype),
        in_specs=[HBM_SPEC], out_specs=HBM_SPEC,
        scratch_shapes=[pltpu.SemaphoreType.DMA((3,)), pltpu.SemaphoreType.DMA((3,)), pltpu.SemaphoreType.DMA],
    )(packed)


def _rs_small(packed):
    _, rows, cols = packed.shape
    rels = [(bx, by, bc) for bx in (0, 1) for by in (0, 1) for bc in (0, 1)][1:]

    def body(s_ref, o_ref, send_sems, recv_sems, local_sem):
        x, y, cc, _ = _place()
        me = 4 * x + 2 * y + cc
        own = pltpu.make_async_copy(s_ref.at[_chip_index(x, y)], o_ref.at[me], local_sem)
        own.start()
        peers = [(jnp.bitwise_xor(x, bx), jnp.bitwise_xor(y, by), jnp.bitwise_xor(cc, bc)) for bx, by, bc in rels]
        cps = [_remote(s_ref.at[_chip_index(px, py)], o_ref.at[me], send_sems.at[k], recv_sems.at[k], (px, py, pc))
               for k, (px, py, pc) in enumerate(peers)]
        for cp in cps:
            cp.start()
        for k, (px, py, pc) in enumerate(peers):
            cps[k].wait_send()
            got = o_ref.at[4 * px + 2 * py + pc]
            _remote(got, got, send_sems.at[k], recv_sems.at[k], (x, y, cc)).wait_recv()
        own.wait()

    return pl.pallas_call(
        body, name="rs_small", out_shape=jax.ShapeDtypeStruct((2 * N_CHIPS, rows, cols), packed.dtype),
        in_specs=[HBM_SPEC], out_specs=HBM_SPEC,
        scratch_shapes=[pltpu.SemaphoreType.DMA((7,)), pltpu.SemaphoreType.DMA((7,)), pltpu.SemaphoreType.DMA],
    )(packed)


def _sum_slots(slots, *, tr=8):
    n, rows, cols = slots.shape

    def body(s_ref, o_ref):
        acc = s_ref[0]
        for k in range(1, n):
            acc = acc + s_ref[k]
        o_ref[...] = acc

    return pl.pallas_call(
        body, name="sum_slots", out_shape=jax.ShapeDtypeStruct((rows, cols), F32), grid=(rows // tr,),
        in_specs=[pl.BlockSpec((n, tr, cols), lambda i: (0, i, 0))], out_specs=pl.BlockSpec((tr, cols), lambda i: (i, 0)),
        compiler_params=_params(("parallel",)),
    )(slots)


MATRIX_AXIS = {"ffn_w1": 1, "ffn_w3": 1, "ffn_w2": 0, "ev_w_in": 1, "ev_w_out": 0, "od_w_in": 1, "od_w_out": 0}
SMALL_SHARDED = ("norm_g", "ev_conv_w", "od_conv_w", "od_conv_b", "od_ln_g", "od_ln_b", "od_pool_scale")
SMALL_REPLICATED = ("ev_q_gain", "ev_k_gain")
PACK_COLS = 1024
PACK_ROW_ALIGN = 8


def _pack(parts):
    flat = jnp.concatenate([p.reshape(-1).astype(F32) for p in parts])
    per = PACK_COLS * PACK_ROW_ALIGN
    total = -(-flat.shape[0] // per) * per
    return jnp.pad(flat, (0, total - flat.shape[0])).reshape(total // PACK_COLS, PACK_COLS)


def _unpack(packed, shapes):
    flat = packed.reshape(-1)
    out, pos = [], 0
    for shp in shapes:
        size = 1
        for s in shp:
            size *= s
        out.append(flat[pos:pos + size].reshape(shp))
        pos += size
    return out


def _row_tile(rows, cols, itemsize=4, budget=1 << 20):
    tr = 8
    while rows % (2 * tr) == 0 and 2 * tr * cols * itemsize <= budget:
        tr *= 2
    return tr


def _reduce_scatter_matrix(g, axis, cc, chip):
    got = _rs_pair(g, axis)
    hr, hc = got.shape
    pr, pc = (hr, hc // N_CHIPS) if axis == 1 else (hr // N_CHIPS, hc)
    tc = pc
    tr = 128 if pr % 128 == 0 else pr
    while tr > 8 and tr * tc * 4 > (1 << 20):
        tr //= 2
    p = _add_pair(g, got, cc, axis, tr=tr, tc=tc)
    arrived = _rs_chips(p, axis)
    return _add_chips(p, arrived, chip, axis, tr=tr, tc=tc)


def kernel(x, norm_g, ffn_w1, ffn_w3, ffn_w2, ev_w_in, ev_q_gain, ev_k_gain, ev_conv_w, ev_w_out, od_w_in, od_conv_w, od_conv_b, od_ln_g, od_ln_b, od_pool_w, od_pool_scale, od_w_out, loss_target, m_norm_g, m_ffn_w1, m_ffn_w3, m_ffn_w2, m_ev_w_in, m_ev_q_gain, m_ev_k_gain, m_ev_conv_w, m_ev_w_out, m_od_w_in, m_od_conv_w, m_od_conv_b, m_od_ln_g, m_od_ln_b, m_od_pool_w, m_od_pool_scale, m_od_w_out, v_norm_g, v_ffn_w1, v_ffn_w3, v_ffn_w2, v_ev_w_in, v_ev_q_gain, v_ev_k_gain, v_ev_conv_w, v_ev_w_out, v_od_w_in, v_od_conv_w, v_od_conv_b, v_od_ln_g, v_od_ln_b, v_od_pool_w, v_od_pool_scale, v_od_w_out):
    names = ["norm_g", "ffn_w1", "ffn_w3", "ffn_w2", "ev_w_in", "ev_q_gain", "ev_k_gain", "ev_conv_w", "ev_w_out",
             "od_w_in", "od_conv_w", "od_conv_b", "od_ln_g", "od_ln_b", "od_pool_w", "od_pool_scale", "od_w_out"]
    w = dict(zip(names, (norm_g, ffn_w1, ffn_w3, ffn_w2, ev_w_in, ev_q_gain, ev_k_gain, ev_conv_w, ev_w_out,
                         od_w_in, od_conv_w, od_conv_b, od_ln_g, od_ln_b, od_pool_w, od_pool_scale, od_w_out)))
    m = dict(zip(names, (m_norm_g, m_ffn_w1, m_ffn_w3, m_ffn_w2, m_ev_w_in, m_ev_q_gain, m_ev_k_gain, m_ev_conv_w,
                         m_ev_w_out, m_od_w_in, m_od_conv_w, m_od_conv_b, m_od_ln_g, m_od_ln_b, m_od_pool_w,
                         m_od_pool_scale, m_od_w_out)))
    v = dict(zip(names, (v_norm_g, v_ffn_w1, v_ffn_w3, v_ffn_w2, v_ev_w_in, v_ev_q_gain, v_ev_k_gain, v_ev_conv_w,
                         v_ev_w_out, v_od_w_in, v_od_conv_w, v_od_conv_b, v_od_ln_g, v_od_ln_b, v_od_pool_w,
                         v_od_pool_scale, v_od_w_out)))
    cx, cy, cc = lax.axis_index("x"), lax.axis_index("y"), lax.axis_index("c")
    cc_arr = jnp.reshape(cc, (1,)).astype(jnp.int32)
    chip_arr = jnp.reshape(_chip_index(cx, cy), (1,)).astype(jnp.int32)

    wts = {}
    for name, axis in MATRIX_AXIS.items():
        shard = w[name]
        r, c = shard.shape[-2:]
        stacked = _cast_bf16(shard.reshape(-1, c), tr=_row_tile(shard.size // c, c)).reshape(-1, r, c)
        wts[name] = [_ag_matrix(stacked[i], axis) for i in range(stacked.shape[0])]
    small_names = SMALL_SHARDED + ("od_pool_w",)
    gathered = _ag_small(_pack([w[k] for k in small_names]))
    per_chip = [_unpack(gathered[j], [w[k].shape for k in small_names]) for j in range(N_CHIPS)]
    for idx, k in enumerate(small_names):
        ax = 2 if k == "od_pool_w" else w[k].ndim - 1
        wts[k] = jnp.concatenate([per_chip[j][idx] for j in range(N_CHIPS)], axis=ax)
    for k in SMALL_REPLICATED:
        wts[k] = w[k]

    sq, dx, gr = _local_step(x[0], loss_target[0], wts)

    grads = {}
    for name, axis in MATRIX_AXIS.items():
        halves = [_reduce_scatter_matrix(g, axis, cc_arr, chip_arr) for g in gr[name]]
        grads[name] = _rs_final(halves, axis).reshape(w[name].shape)
    small_full = {k: (gr[k] if k == "norm_g" else jnp.stack(gr[k])) for k in small_names + SMALL_REPLICATED}
    chunks = []
    for j in range(N_CHIPS):
        parts = []
        for k in small_names:
            ax = 2 if k == "od_pool_w" else w[k].ndim - 1
            size = w[k].shape[ax]
            parts.append(lax.slice_in_dim(small_full[k], j * size, (j + 1) * size, axis=ax))
        parts += [small_full[k] for k in SMALL_REPLICATED] + [sq[0, :1]]
        chunks.append(_pack(parts))
    summed = _sum_slots(_rs_small(jnp.stack(chunks)))
    pack_names = small_names + SMALL_REPLICATED
    unpacked = _unpack(summed, [w[k].shape for k in pack_names] + [(1,)])
    for k, g in zip(pack_names, unpacked):
        grads[k] = g
    loss = (0.5 / x.shape[-1]) * unpacked[-1][0]

    delta, new_m, new_v = {}, {}, {}
    for name in MATRIX_AXIS:
        shp = w[name].shape
        cols = shp[-1]
        rows = w[name].size // cols
        tr = _row_tile(rows, cols, budget=1 << 20)
        d_, m_, v_ = _adamw(w[name].reshape(rows, cols), grads[name].reshape(rows, cols),
                            m[name].reshape(rows, cols), v[name].reshape(rows, cols), tr=tr)
        delta[name], new_m[name], new_v[name] = d_.reshape(shp), m_.reshape(shp), v_.reshape(shp)
    pw, pg, pm, pv = (_pack([t[k] for k in pack_names]) for t in (w, grads, m, v))
    d_, m_, v_ = _adamw(pw, pg, pm, pv, tr=PACK_ROW_ALIGN)
    shapes = [w[k].shape for k in pack_names]
    for store, packed in ((delta, d_), (new_m, m_), (new_v, v_)):
        for k, a in zip(pack_names, _unpack(packed, shapes)):
            store[k] = a

    return (loss, dx[None], *[grads[k] for k in names], *[delta[k] for k in names],
            *[new_m[k] for k in names], *[new_v[k] for k in names])
```

```python
import jax
import jax.numpy as jnp
from jax import lax
from jax.experimental import pallas as pl
from jax.experimental.pallas import tpu as pltpu

F32 = jnp.float32
BF16 = jnp.bfloat16
MESH = pl.DeviceIdType.MESH

EPS = 1e-6
HEADS = 8
HEAD_DIM = 128
A_WIDTH = HEADS * HEAD_DIM
A_WINDOWS = (128, 512, 2048)
A_DILATIONS = (1, 4, 16)
ATTN_BLOCK = 128
B_CONV = 3
C_CONV = 31
D_WINDOWS = (2, 4, 8, 16)
HALO = 32
N_CHIPS = 4
ADAM_LR = 0.001
ADAM_B1 = 0.9
ADAM_B2 = 0.999
ADAM_EPS = 1e-08
ADAM_WD = 0.01
ADAM_STEP = 10
VMEM_LIMIT_BYTES = 56 * 1024 * 1024
NEG_BIG = -1e30


def _params(sem, **kw):
    return pltpu.CompilerParams(dimension_semantics=sem, vmem_limit_bytes=VMEM_LIMIT_BYTES, **kw)


def _sigmoid(x):
    return 1.0 / (1.0 + jnp.exp(-x))


def _matmul(pairs, *, m, n, k, tm, tn, tk, ta=False, tb=False, out_dtype=F32, res=None,
            alpha=1.0, name):
    nk = k // tk
    npairs = len(pairs)
    dn = (((0 if ta else 1,), (1 if tb else 0,)), ((), ()))

    def body(*refs):
        ab = refs[:2 * npairs]
        pos = 2 * npairs
        res_ref = None
        if res is not None:
            res_ref = refs[pos]
            pos += 1
        o_ref = refs[pos]
        acc_ref = refs[pos + 1] if nk > 1 else None

        def dots():
            tot = None
            for p in range(npairs):
                d = lax.dot_general(ab[2 * p][...], ab[2 * p + 1][...], dn, preferred_element_type=F32)
                tot = d if tot is None else tot + d
            return tot

        def finish(acc):
            r = acc * alpha if alpha != 1.0 else acc
            if res_ref is not None:
                r = res_ref[...].astype(F32) + r
            o_ref[...] = r.astype(o_ref.dtype)

        if nk == 1:
            finish(dots())
        else:
            kk = pl.program_id(2)

            @pl.when(kk == 0)
            def _():
                acc_ref[...] = dots()

            @pl.when(kk > 0)
            def _():
                acc_ref[...] += dots()

            @pl.when(kk == nk - 1)
            def _():
                finish(acc_ref[...])

    a_spec = pl.BlockSpec((tk, tm), lambda i, j, kk: (kk, i)) if ta else pl.BlockSpec((tm, tk), lambda i, j, kk: (i, kk))
    b_spec = pl.BlockSpec((tn, tk), lambda i, j, kk: (j, kk)) if tb else pl.BlockSpec((tk, tn), lambda i, j, kk: (kk, j))
    o_spec = pl.BlockSpec((tm, tn), lambda i, j, kk: (i, j))
    in_specs = [a_spec, b_spec] * npairs
    args = [t for p in pairs for t in p]
    if res is not None:
        in_specs.append(o_spec)
        args.append(res)
    return pl.pallas_call(
        body, name=name,
        out_shape=jax.ShapeDtypeStruct((m, n), out_dtype),
        grid=(m // tm, n // tn, nk),
        in_specs=in_specs, out_specs=o_spec,
        scratch_shapes=[pltpu.VMEM((tm, tn), F32)] if nk > 1 else [],
        compiler_params=_params(("parallel", "parallel", "arbitrary")),
    )(*args)


def _ffn_up(xn, w1, w3, *, tm, tn):
    t, d = xn.shape
    f = w1.shape[1]

    def body(x_ref, w1_ref, w3_ref, a_ref, b_ref, h_ref):
        x = x_ref[...]
        a = jnp.dot(x, w1_ref[...], preferred_element_type=F32)
        b = jnp.dot(x, w3_ref[...], preferred_element_type=F32)
        a_ref[...] = a.astype(BF16)
        b_ref[...] = b.astype(BF16)
        h_ref[...] = (a * _sigmoid(a) * b).astype(BF16)

    x_spec = pl.BlockSpec((tm, d), lambda i, j: (i, 0))
    w_spec = pl.BlockSpec((d, tn), lambda i, j: (0, j))
    o_spec = pl.BlockSpec((tm, tn), lambda i, j: (i, j))
    shp = jax.ShapeDtypeStruct((t, f), BF16)
    return pl.pallas_call(
        body, name="ffn_up", out_shape=(shp, shp, shp), grid=(t // tm, f // tn),
        in_specs=[x_spec, w_spec, w_spec], out_specs=(o_spec, o_spec, o_spec),
        compiler_params=_params(("parallel", "parallel")),
    )(xn, w1, w3)


def _ffn_dh(dyb, w2, a, b, *, tm, tn):
    t, d = dyb.shape
    f = w2.shape[0]
    dn = (((1,), (1,)), ((), ()))

    def body(dy_ref, w2_ref, a_ref, b_ref, da_ref, db_ref):
        dh = 0.5 * lax.dot_general(dy_ref[...], w2_ref[...], dn, preferred_element_type=F32)
        av = a_ref[...].astype(F32)
        bv = b_ref[...].astype(F32)
        sig = _sigmoid(av)
        da_ref[...] = (dh * bv * (sig * (1.0 + av * (1.0 - sig)))).astype(BF16)
        db_ref[...] = (dh * (av * sig)).astype(BF16)

    dy_spec = pl.BlockSpec((tm, d), lambda i, j: (i, 0))
    w_spec = pl.BlockSpec((tn, d), lambda i, j: (j, 0))
    o_spec = pl.BlockSpec((tm, tn), lambda i, j: (i, j))
    shp = jax.ShapeDtypeStruct((t, f), BF16)
    return pl.pallas_call(
        body, name="ffn_dh", out_shape=(shp, shp), grid=(t // tm, f // tn),
        in_specs=[dy_spec, w_spec, o_spec, o_spec], out_specs=(o_spec, o_spec),
        compiler_params=_params(("parallel", "parallel")),
    )(dyb, w2, a, b)


def _rmsnorm_fwd(x, g, *, tr=256):
    t, d = x.shape

    def body(x_ref, g_ref, o_ref):
        xv = x_ref[...]
        y = xv * lax.rsqrt(jnp.mean(xv * xv, axis=-1, keepdims=True) + EPS)
        o_ref[...] = (y * g_ref[...]).astype(BF16)

    return pl.pallas_call(
        body, name="rmsnorm_fwd", out_shape=jax.ShapeDtypeStruct((t, d), BF16), grid=(t // tr,),
        in_specs=[pl.BlockSpec((tr, d), lambda i: (i, 0)), pl.BlockSpec((1, d), lambda i: (0, 0))],
        out_specs=pl.BlockSpec((tr, d), lambda i: (i, 0)),
        compiler_params=_params(("parallel",)),
    )(x, g.reshape(1, d))


def _rmsnorm_bwd(dy, x, g, dres, *, tr=256):
    t, d = x.shape

    def body(dy_ref, x_ref, g_ref, dres_ref, dx_ref, dxb_ref, dg_ref):
        xv = x_ref[...]
        dyv = dy_ref[...].astype(F32)
        r = lax.rsqrt(jnp.mean(xv * xv, axis=-1, keepdims=True) + EPS)
        xhat = xv * r
        dxhat = dyv * g_ref[...]
        c = jnp.mean(dxhat * xhat, axis=-1, keepdims=True)
        dx = dres_ref[...] + r * (dxhat - xhat * c)
        dx_ref[...] = dx
        dxb_ref[...] = dx.astype(BF16)
        part = jnp.sum(dyv * xhat, axis=0, keepdims=True)

        @pl.when(pl.program_id(0) == 0)
        def _():
            dg_ref[...] = part

        @pl.when(pl.program_id(0) > 0)
        def _():
            dg_ref[...] += part

    row = pl.BlockSpec((tr, d), lambda i: (i, 0))
    vec = pl.BlockSpec((1, d), lambda i: (0, 0))
    dx, dxb, dg = pl.pallas_call(
        body, name="rmsnorm_bwd",
        out_shape=(jax.ShapeDtypeStruct((t, d), F32), jax.ShapeDtypeStruct((t, d), BF16),
                   jax.ShapeDtypeStruct((1, d), F32)),
        grid=(t // tr,), in_specs=[row, row, vec, row], out_specs=(row, row, vec),
        compiler_params=_params(("arbitrary",)),
    )(dy, x, g.reshape(1, d), dres)
    return dx, dxb, dg.reshape(d)


def _loss_head(y, target, *, tr=256):
    t, d = y.shape

    def body(y_ref, t_ref, dy_ref, dyb_ref, s_ref):
        err = y_ref[...] - t_ref[...]
        dy = err * (1.0 / d)
        dy_ref[...] = dy
        dyb_ref[...] = dy.astype(BF16)
        part = jnp.full((1, 128), jnp.sum(err * err), F32)

        @pl.when(pl.program_id(0) == 0)
        def _():
            s_ref[...] = part

        @pl.when(pl.program_id(0) > 0)
        def _():
            s_ref[...] += part

    row = pl.BlockSpec((tr, d), lambda i: (i, 0))
    return pl.pallas_call(
        body, name="loss_head",
        out_shape=(jax.ShapeDtypeStruct((t, d), F32), jax.ShapeDtypeStruct((t, d), BF16),
                   jax.ShapeDtypeStruct((1, 128), F32)),
        grid=(t // tr,), in_specs=[row, row],
        out_specs=(row, row, pl.BlockSpec((1, 128), lambda i: (0, 0))),
        compiler_params=_params(("arbitrary",)),
    )(y, target)


def _adamw(w, g, m, v, *, tr):
    rows, cols = w.shape

    def body(w_ref, g_ref, m_ref, v_ref, d_ref, nm_ref, nv_ref):
        gv = g_ref[...]
        nm = ADAM_B1 * m_ref[...] + (1.0 - ADAM_B1) * gv
        nv = ADAM_B2 * v_ref[...] + (1.0 - ADAM_B2) * jnp.square(gv)
        m_hat = nm / (1.0 - ADAM_B1 ** ADAM_STEP)
        v_hat = nv / (1.0 - ADAM_B2 ** ADAM_STEP)
        d_ref[...] = -ADAM_LR * (m_hat / (jnp.sqrt(v_hat) + ADAM_EPS) + ADAM_WD * w_ref[...])
        nm_ref[...] = nm
        nv_ref[...] = nv

    spec = pl.BlockSpec((tr, cols), lambda i: (i, 0))
    shp = jax.ShapeDtypeStruct((rows, cols), F32)
    return pl.pallas_call(
        body, name="adamw", out_shape=(shp, shp, shp), grid=(rows // tr,),
        in_specs=[spec] * 4, out_specs=(spec, spec, spec),
        compiler_params=_params(("parallel",)),
    )(w, g, m, v)


def _headnorm(xf, g):
    r = lax.rsqrt(jnp.mean(xf * xf, axis=-1, keepdims=True) + EPS)
    xhat = xf * r
    return xhat * g, xhat, r


def _headnorm_bwd(dn, xhat, r, g):
    dxhat = dn * g
    return r * (dxhat - xhat * jnp.mean(dxhat * xhat, axis=-1, keepdims=True))


_NT = (((1,), (1,)), ((), ()))
_TN = (((0,), (0,)), ((), ()))


def _attn_masks(n, nb):
    qi = lax.broadcasted_iota(jnp.int32, (ATTN_BLOCK, ATTN_BLOCK), 0)
    ci = lax.broadcasted_iota(jnp.int32, (ATTN_BLOCK, ATTN_BLOCK), 1)
    d_prev = qi + ATTN_BLOCK - ci
    d_cur = qi - ci
    return d_prev, d_cur, (ci >= qi), (ci <= qi)


def _attn_fwd(qv, kv, vv, qg, kg, *, dil):
    l, w = qv.shape
    nb = l // ATTN_BLOCK
    scale = HEAD_DIM ** -0.5

    def body(q_ref, kp_ref, kc_ref, vp_ref, vc_ref, qg_ref, kg_ref, o_ref, lse_ref):
        n = pl.program_id(1)
        d_prev, d_cur, ok_prev, ok_cur = _attn_masks(n, nb)
        ok_prev = ok_prev & (n > 0)
        b_prev = d_prev.astype(F32) * float(dil)
        b_cur = d_cur.astype(F32) * float(dil)
        for h in range(HEADS):
            sl = slice(h * HEAD_DIM, (h + 1) * HEAD_DIM)
            slope = 2.0 ** (-8.0 * (h + 1) / HEADS)
            q = _headnorm(q_ref[:, sl].astype(F32), qg_ref[...])[0].astype(BF16)
            kp = _headnorm(kp_ref[:, sl].astype(F32), kg_ref[...])[0].astype(BF16)
            kc = _headnorm(kc_ref[:, sl].astype(F32), kg_ref[...])[0].astype(BF16)
            s1 = lax.dot_general(q, kp, _NT, preferred_element_type=F32) * scale
            s2 = lax.dot_general(q, kc, _NT, preferred_element_type=F32) * scale
            s1 = jnp.where(ok_prev, s1 - slope * b_prev, NEG_BIG)
            s2 = jnp.where(ok_cur, s2 - slope * b_cur, NEG_BIG)
            m = jnp.maximum(jnp.max(s1, axis=-1, keepdims=True), jnp.max(s2, axis=-1, keepdims=True))
            p1 = jnp.exp(s1 - m)
            p2 = jnp.exp(s2 - m)
            den = jnp.sum(p1, axis=-1, keepdims=True) + jnp.sum(p2, axis=-1, keepdims=True)
            inv = 1.0 / den
            o = jnp.dot((p1 * inv).astype(BF16), vp_ref[:, sl], preferred_element_type=F32)
            o = o + jnp.dot((p2 * inv).astype(BF16), vc_ref[:, sl], preferred_element_type=F32)
            o_ref[:, sl] = o
            lse_ref[:, sl] = jnp.broadcast_to(m + jnp.log(den), (ATTN_BLOCK, HEAD_DIM))

    cur = pl.BlockSpec((ATTN_BLOCK, A_WIDTH), lambda r, n: (n, r))
    prev = pl.BlockSpec((ATTN_BLOCK, A_WIDTH), lambda r, n: (jnp.maximum(n - 1, 0), r))
    vec = pl.BlockSpec((1, HEAD_DIM), lambda r, n: (0, 0))
    shp = jax.ShapeDtypeStruct((l, w), F32)
    return pl.pallas_call(
        body, name="attn_fwd_d%d" % dil, out_shape=(shp, shp), grid=(dil, nb),
        in_specs=[cur, prev, cur, prev, cur, vec, vec], out_specs=(cur, cur),
        compiler_params=_params(("parallel", "parallel")),
    )(qv, kv, kv, vv, vv, qg.reshape(1, HEAD_DIM), kg.reshape(1, HEAD_DIM))


def _attn_combine(outs, lses, *, tr=256):
    t, w = outs[0].shape

    def body(o0, o1, o2, l0, l1, l2, y_ref, lse_ref):
        a0, a1, a2 = l0[...], l1[...], l2[...]
        m = jnp.maximum(jnp.maximum(a0, a1), a2)
        e0, e1, e2 = jnp.exp(a0 - m), jnp.exp(a1 - m), jnp.exp(a2 - m)
        s = e0 + e1 + e2
        inv = 1.0 / s
        y_ref[...] = ((e0 * inv) * o0[...] + (e1 * inv) * o1[...] + (e2 * inv) * o2[...]).astype(BF16)
        lse_ref[...] = m + jnp.log(s)

    row = pl.BlockSpec((tr, w), lambda i: (i, 0))
    return pl.pallas_call(
        body, name="attn_combine",
        out_shape=(jax.ShapeDtypeStruct((t, w), BF16), jax.ShapeDtypeStruct((t, w), F32)),
        grid=(t // tr,), in_specs=[row] * 6, out_specs=(row, row),
        compiler_params=_params(("parallel",)),
    )(*outs, *lses)


def _attn_delta(dy, y, *, tr=256):
    t, w = y.shape

    def body(dy_ref, y_ref, o_ref):
        for h in range(HEADS):
            sl = slice(h * HEAD_DIM, (h + 1) * HEAD_DIM)
            dlt = jnp.sum(dy_ref[:, sl] * y_ref[:, sl].astype(F32), axis=-1, keepdims=True)
            o_ref[:, sl] = jnp.broadcast_to(dlt, (tr, HEAD_DIM))

    row = pl.BlockSpec((tr, w), lambda i: (i, 0))
    return pl.pallas_call(
        body, name="attn_delta", out_shape=jax.ShapeDtypeStruct((t, w), F32), grid=(t // tr,),
        in_specs=[row, row], out_specs=row, compiler_params=_params(("parallel",)),
    )(dy, y)


def _attn_bwd(qv, kv, vv, dyv, lsev, dltv, qg, kg, *, dil):
    l, w = qv.shape
    nb = l // ATTN_BLOCK
    scale = HEAD_DIM ** -0.5

    def body(qc_ref, qn_ref, kp_ref, kc_ref, vp_ref, vc_ref, dyc_ref, dyn_ref, lc_ref, ln_ref,
             dc_ref, dn_ref, qg_ref, kg_ref, dq_ref, dk_ref, dv_ref, dqg_ref, dkg_ref):
        n = pl.program_id(1)
        first = (pl.program_id(0) == 0) & (n == 0)
        d_prev, d_cur, ok_prev, ok_cur = _attn_masks(n, nb)
        ok_t1 = ok_prev & (n > 0)
        ok_t3 = ok_prev & (n < nb - 1)
        b_prev = d_prev.astype(F32) * float(dil)
        b_cur = d_cur.astype(F32) * float(dil)
        qgv, kgv = qg_ref[...], kg_ref[...]
        dqg = jnp.zeros((1, HEAD_DIM), F32)
        dkg = jnp.zeros((1, HEAD_DIM), F32)
        for h in range(HEADS):
            sl = slice(h * HEAD_DIM, (h + 1) * HEAD_DIM)
            slope = 2.0 ** (-8.0 * (h + 1) / HEADS)
            qc, qc_hat, qc_r = _headnorm(qc_ref[:, sl].astype(F32), qgv)
            qn = _headnorm(qn_ref[:, sl].astype(F32), qgv)[0].astype(BF16)
            kp = _headnorm(kp_ref[:, sl].astype(F32), kgv)[0].astype(BF16)
            kc, kc_hat, kc_r = _headnorm(kc_ref[:, sl].astype(F32), kgv)
            qc = qc.astype(BF16)
            kc = kc.astype(BF16)
            vp, vc = vp_ref[:, sl], vc_ref[:, sl]
            dyc, dyn = dyc_ref[:, sl].astype(BF16), dyn_ref[:, sl].astype(BF16)

            def tile(q, k, v, dy, lse, dlt, ok, bias):
                s = lax.dot_general(q, k, _NT, preferred_element_type=F32) * scale
                p = jnp.where(ok, jnp.exp(jnp.where(ok, s - slope * bias, NEG_BIG) - lse), 0.0)
                dp = lax.dot_general(dy, v, _NT, preferred_element_type=F32)
                return p.astype(BF16), (p * (dp - dlt)).astype(BF16)

            p1, ds1 = tile(qc, kp, vp, dyc, lc_ref[:, sl], dc_ref[:, sl], ok_t1, b_prev)
            p2, ds2 = tile(qc, kc, vc, dyc, lc_ref[:, sl], dc_ref[:, sl], ok_cur, b_cur)
            p3, ds3 = tile(qn, kc, vc, dyn, ln_ref[:, sl], dn_ref[:, sl], ok_t3, b_prev)
            dqn = scale * (jnp.dot(ds1, kp, preferred_element_type=F32) + jnp.dot(ds2, kc, preferred_element_type=F32))
            dkn = scale * (lax.dot_general(ds2, qc, _TN, preferred_element_type=F32)
                           + lax.dot_general(ds3, qn, _TN, preferred_element_type=F32))
            dv = (lax.dot_general(p2, dyc, _TN, preferred_element_type=F32)
                  + lax.dot_general(p3, dyn, _TN, preferred_element_type=F32))
            dqg = dqg + jnp.sum(dqn * qc_hat, axis=0, keepdims=True)
            dkg = dkg + jnp.sum(dkn * kc_hat, axis=0, keepdims=True)
            dq_ref[:, sl] = _headnorm_bwd(dqn, qc_hat, qc_r, qgv).astype(BF16)
            dk_ref[:, sl] = _headnorm_bwd(dkn, kc_hat, kc_r, kgv).astype(BF16)
            dv_ref[:, sl] = dv.astype(BF16)

        @pl.when(first)
        def _():
            dqg_ref[...] = dqg
            dkg_ref[...] = dkg

        @pl.when(jnp.logical_not(first))
        def _():
            dqg_ref[...] += dqg
            dkg_ref[...] += dkg

    blk = (ATTN_BLOCK, A_WIDTH)
    cur = pl.BlockSpec(blk, lambda r, n: (n, r))
    prev = pl.BlockSpec(blk, lambda r, n: (jnp.maximum(n - 1, 0), r))
    nxt = pl.BlockSpec(blk, lambda r, n: (jnp.minimum(n + 1, nb - 1), r))
    vec = pl.BlockSpec((1, HEAD_DIM), lambda r, n: (0, 0))
    shp = jax.ShapeDtypeStruct((l, w), BF16)
    gshp = jax.ShapeDtypeStruct((1, HEAD_DIM), F32)
    return pl.pallas_call(
        body, name="attn_bwd_d%d" % dil, out_shape=(shp, shp, shp, gshp, gshp), grid=(dil, nb),
        in_specs=[cur, nxt, prev, cur, prev, cur, cur, nxt, cur, nxt, cur, nxt, vec, vec],
        out_specs=(cur, cur, cur, vec, vec),
        compiler_params=_params(("arbitrary", "arbitrary")),
    )(qv, qv, kv, kv, vv, vv, dyv, dyv, lsev, lsev, dltv, dltv,
      qg.reshape(1, HEAD_DIM), kg.reshape(1, HEAD_DIM))


def _prev_halo(tr, tc, col0):
    return pl.BlockSpec((HALO, tc), lambda j, i: (jnp.maximum(i * (tr // HALO) - 1, 0), col0 + j))


def _next_halo(tr, tc, col0, rows):
    last = rows // HALO - 1
    return pl.BlockSpec((HALO, tc), lambda j, i: (jnp.minimum((i + 1) * (tr // HALO), last), col0 + j))


def _cur_block(tr, tc, col0):
    return pl.BlockSpec((tr, tc), lambda j, i: (i, col0 + j))


def _gateconv_fwd(h, conv_w, *, col0, tr=512, tc=256):
    t = h.shape[0]
    width = conv_w.shape[1]
    nc = width // tc
    c0 = col0 // tc

    def body(bg_ref, cg_ref, xt_ref, cgh_ref, xth_ref, w_ref, y_ref, pad_ref):
        i = pl.program_id(1)
        halo = cgh_ref[...].astype(F32) * xth_ref[...].astype(F32)
        pad_ref[0:HALO, :] = jnp.where(i > 0, halo, 0.0)
        pad_ref[HALO:HALO + tr, :] = cg_ref[...].astype(F32) * xt_ref[...].astype(F32)
        conv = None
        for j in range(B_CONV):
            term = w_ref[j:j + 1, :] * pad_ref[HALO - (B_CONV - 1) + j:HALO - (B_CONV - 1) + j + tr, :]
            conv = term if conv is None else conv + term
        y_ref[...] = (bg_ref[...].astype(F32) * conv).astype(BF16)

    return pl.pallas_call(
        body, name="gateconv_fwd", out_shape=jax.ShapeDtypeStruct((t, width), BF16), grid=(nc, t // tr),
        in_specs=[_cur_block(tr, tc, c0), _cur_block(tr, tc, c0 + nc), _cur_block(tr, tc, c0 + 2 * nc),
                  _prev_halo(tr, tc, c0 + nc), _prev_halo(tr, tc, c0 + 2 * nc),
                  pl.BlockSpec((8, tc), lambda j, i: (0, j))],
        out_specs=_cur_block(tr, tc, 0),
        scratch_shapes=[pltpu.VMEM((HALO + tr, tc), F32)],
        compiler_params=_params(("parallel", "arbitrary")),
    )(h, h, h, h, h, _pad_rows(conv_w, 8))


def _pad_rows(w, rows):
    return jnp.pad(w, ((0, rows - w.shape[0]), (0, 0)))


def _gateconv_bwd(h, dy, conv_w, *, col0, dcol0, tr=512, tc=256):
    t = h.shape[0]
    width = conv_w.shape[1]
    nc = width // tc
    c0 = col0 // tc
    dc0 = dcol0 // tc
    nt = t // tr

    def body(bg_ref, cg_ref, xt_ref, cgh_ref, xth_ref, bgn_ref, dy_ref, dyn_ref, w_ref,
             dbg_ref, dcg_ref, dxt_ref, dw_ref, pad_ref, padd_ref):
        i = pl.program_id(1)
        cg = cg_ref[...].astype(F32)
        xt = xt_ref[...].astype(F32)
        bg = bg_ref[...].astype(F32)
        dyv = dy_ref[...]
        halo = cgh_ref[...].astype(F32) * xth_ref[...].astype(F32)
        pad_ref[0:HALO, :] = jnp.where(i > 0, halo, 0.0)
        pad_ref[HALO:HALO + tr, :] = cg * xt
        dconv = dyv * bg
        padd_ref[0:tr, :] = dconv
        padd_ref[tr:tr + HALO, :] = jnp.where(i < nt - 1, dyn_ref[...] * bgn_ref[...].astype(F32), 0.0)
        conv = None
        du = None
        dws = []
        for j in range(B_CONV):
            off = HALO - (B_CONV - 1) + j
            shifted = pad_ref[off:off + tr, :]
            term = w_ref[j:j + 1, :] * shifted
            conv = term if conv is None else conv + term
            dws.append(jnp.sum(dconv * shifted, axis=0, keepdims=True))
            back = w_ref[j:j + 1, :] * padd_ref[B_CONV - 1 - j:B_CONV - 1 - j + tr, :]
            du = back if du is None else du + back
        dbg_ref[...] = (dyv * conv).astype(BF16)
        dcg_ref[...] = (du * xt).astype(BF16)
        dxt_ref[...] = (du * cg).astype(BF16)
        dw = _stack_rows(dws, 8, tc)

        @pl.when(i == 0)
        def _():
            dw_ref[...] = dw

        @pl.when(i > 0)
        def _():
            dw_ref[...] += dw

    oshp = jax.ShapeDtypeStruct((t, width), BF16)
    return pl.pallas_call(
        body, name="gateconv_bwd",
        out_shape=(oshp, oshp, oshp, jax.ShapeDtypeStruct((8, width), F32)), grid=(nc, nt),
        in_specs=[_cur_block(tr, tc, c0), _cur_block(tr, tc, c0 + nc), _cur_block(tr, tc, c0 + 2 * nc),
                  _prev_halo(tr, tc, c0 + nc), _prev_halo(tr, tc, c0 + 2 * nc),
                  _next_halo(tr, tc, c0, t), _cur_block(tr, tc, dc0), _next_halo(tr, tc, dc0, t),
                  pl.BlockSpec((8, tc), lambda j, i: (0, j))],
        out_specs=(_cur_block(tr, tc, 0), _cur_block(tr, tc, 0), _cur_block(tr, tc, 0),
                   pl.BlockSpec((8, tc), lambda j, i: (0, j))),
        scratch_shapes=[pltpu.VMEM((HALO + tr, tc), F32), pltpu.VMEM((tr + HALO, tc), F32)],
        compiler_params=_params(("parallel", "arbitrary")),
    )(h, h, h, h, h, h, dy, dy, _pad_rows(conv_w, 8))


def _stack_rows(rows, n, width):
    idx = lax.broadcasted_iota(jnp.int32, (n, width), 0)
    out = jnp.zeros((n, width), F32)
    for j, r in enumerate(rows):
        out = jnp.where(idx == j, r, out)
    return out


CONV_ROWS = 64


def _glu_conv_fwd(hod, conv_w, conv_b, *, tr=512, tc=256):
    t = hod.shape[0]
    width = conv_w.shape[1]
    nc = width // tc

    def body(val_ref, gate_ref, valh_ref, gateh_ref, w_ref, b_ref, u1_ref, pad_ref):
        i = pl.program_id(1)
        halo = valh_ref[...].astype(F32) * _sigmoid(gateh_ref[...].astype(F32))
        pad_ref[0:HALO, :] = jnp.where(i > 0, halo, 0.0)
        pad_ref[HALO:HALO + tr, :] = val_ref[...].astype(F32) * _sigmoid(gate_ref[...].astype(F32))
        for c in range(tr // CONV_ROWS):
            base = HALO + c * CONV_ROWS - (C_CONV - 1)
            acc = None
            for j in range(C_CONV):
                term = w_ref[j:j + 1, :] * pad_ref[base + j:base + j + CONV_ROWS, :]
                acc = term if acc is None else acc + term
            u1_ref[c * CONV_ROWS:(c + 1) * CONV_ROWS, :] = acc + b_ref[...]

    return pl.pallas_call(
        body, name="glu_conv_fwd", out_shape=jax.ShapeDtypeStruct((t, width), F32), grid=(nc, t // tr),
        in_specs=[_cur_block(tr, tc, 0), _cur_block(tr, tc, nc), _prev_halo(tr, tc, 0), _prev_halo(tr, tc, nc),
                  pl.BlockSpec((32, tc), lambda j, i: (0, j)), pl.BlockSpec((1, tc), lambda j, i: (0, j))],
        out_specs=_cur_block(tr, tc, 0),
        scratch_shapes=[pltpu.VMEM((HALO + tr, tc), F32)],
        compiler_params=_params(("parallel", "arbitrary")),
    )(hod, hod, hod, hod, _pad_rows(conv_w, 32), conv_b.reshape(1, width))


def _ln_silu_fwd(u1, g, b, *, tr=256):
    t, width = u1.shape

    def body(u_ref, g_ref, b_ref, o_ref):
        uv = u_ref[...]
        mu = jnp.mean(uv, axis=-1, keepdims=True)
        var = jnp.mean(jnp.square(uv - mu), axis=-1, keepdims=True)
        u2 = ((uv - mu) * lax.rsqrt(var + EPS)) * g_ref[...] + b_ref[...]
        o_ref[...] = (u2 * _sigmoid(u2)).astype(BF16)

    row = pl.BlockSpec((tr, width), lambda i: (i, 0))
    vec = pl.BlockSpec((1, width), lambda i: (0, 0))
    return pl.pallas_call(
        body, name="ln_silu_fwd", out_shape=jax.ShapeDtypeStruct((t, width), BF16), grid=(t // tr,),
        in_specs=[row, vec, vec], out_specs=row, compiler_params=_params(("parallel",)),
    )(u1, g.reshape(1, width), b.reshape(1, width))


def _ln_silu_bwd(du, u1, g, b, *, col0, tr=256):
    t, width = u1.shape

    def body(du_ref, u_ref, g_ref, b_ref, du1_ref, dg_ref, db_ref, dcb_ref):
        uv = u_ref[...]
        mu = jnp.mean(uv, axis=-1, keepdims=True)
        var = jnp.mean(jnp.square(uv - mu), axis=-1, keepdims=True)
        rstd = lax.rsqrt(var + EPS)
        xh = (uv - mu) * rstd
        u2 = xh * g_ref[...] + b_ref[...]
        sig = _sigmoid(u2)
        du2 = du_ref[...] * (sig * (1.0 + u2 * (1.0 - sig)))
        dxh = du2 * g_ref[...]
        du1 = rstd * (dxh - jnp.mean(dxh, axis=-1, keepdims=True)
                      - xh * jnp.mean(dxh * xh, axis=-1, keepdims=True))
        du1_ref[...] = du1
        parts = (jnp.sum(du2 * xh, axis=0, keepdims=True), jnp.sum(du2, axis=0, keepdims=True),
                 jnp.sum(du1, axis=0, keepdims=True))

        @pl.when(pl.program_id(0) == 0)
        def _():
            dg_ref[...], db_ref[...], dcb_ref[...] = parts

        @pl.when(pl.program_id(0) > 0)
        def _():
            dg_ref[...] += parts[0]
            db_ref[...] += parts[1]
            dcb_ref[...] += parts[2]

    row = pl.BlockSpec((tr, width), lambda i: (i, 0))
    vec = pl.BlockSpec((1, width), lambda i: (0, 0))
    vshp = jax.ShapeDtypeStruct((1, width), F32)
    return pl.pallas_call(
        body, name="ln_silu_bwd", out_shape=(jax.ShapeDtypeStruct((t, width), F32), vshp, vshp, vshp),
        grid=(t // tr,),
        in_specs=[pl.BlockSpec((tr, width), lambda i: (i, col0 // width)), row, vec, vec],
        out_specs=(row, vec, vec, vec), compiler_params=_params(("arbitrary",)),
    )(du, u1, g.reshape(1, width), b.reshape(1, width))


def _glu_conv_bwd(hod, du1, conv_w, *, tr=512, tc=256):
    t = hod.shape[0]
    width = conv_w.shape[1]
    nc = width // tc
    nt = t // tr

    def body(val_ref, gate_ref, valh_ref, gateh_ref, du_ref, dun_ref, w_ref,
             dval_ref, dgate_ref, dw_ref, pad_ref, padd_ref, du0_ref):
        i = pl.program_id(1)
        val = val_ref[...].astype(F32)
        sig = _sigmoid(gate_ref[...].astype(F32))
        halo = valh_ref[...].astype(F32) * _sigmoid(gateh_ref[...].astype(F32))
        pad_ref[0:HALO, :] = jnp.where(i > 0, halo, 0.0)
        pad_ref[HALO:HALO + tr, :] = val * sig
        padd_ref[0:tr, :] = du_ref[...]
        padd_ref[tr:tr + HALO, :] = jnp.where(i < nt - 1, dun_ref[...], 0.0)
        dws = [jnp.zeros((1, tc), F32)] * C_CONV
        for c in range(tr // CONV_ROWS):
            r0 = c * CONV_ROWS
            duc = padd_ref[r0:r0 + CONV_ROWS, :]
            acc = None
            for j in range(C_CONV):
                back = w_ref[j:j + 1, :] * padd_ref[r0 + C_CONV - 1 - j:r0 + C_CONV - 1 - j + CONV_ROWS, :]
                acc = back if acc is None else acc + back
                off = HALO + r0 - (C_CONV - 1) + j
                dws[j] = dws[j] + jnp.sum(duc * pad_ref[off:off + CONV_ROWS, :], axis=0, keepdims=True)
            du0_ref[r0:r0 + CONV_ROWS, :] = acc
        du0 = du0_ref[...]
        dval_ref[...] = (du0 * sig).astype(BF16)
        dgate_ref[...] = (du0 * val * sig * (1.0 - sig)).astype(BF16)
        dw = _stack_rows(dws, 32, tc)

        @pl.when(i == 0)
        def _():
            dw_ref[...] = dw

        @pl.when(i > 0)
        def _():
            dw_ref[...] += dw

    oshp = jax.ShapeDtypeStruct((t, width), BF16)
    wspec = pl.BlockSpec((32, tc), lambda j, i: (0, j))
    return pl.pallas_call(
        body, name="glu_conv_bwd", out_shape=(oshp, oshp, jax.ShapeDtypeStruct((32, width), F32)), grid=(nc, nt),
        in_specs=[_cur_block(tr, tc, 0), _cur_block(tr, tc, nc), _prev_halo(tr, tc, 0), _prev_halo(tr, tc, nc),
                  _cur_block(tr, tc, 0), _next_halo(tr, tc, 0, t), wspec],
        out_specs=(_cur_block(tr, tc, 0), _cur_block(tr, tc, 0), wspec),
        scratch_shapes=[pltpu.VMEM((HALO + tr, tc), F32), pltpu.VMEM((tr + HALO, tc), F32),
                        pltpu.VMEM((tr, tc), F32)],
        compiler_params=_params(("parallel", "arbitrary")),
    )(hod, hod, hod, hod, du1, du1, _pad_rows(conv_w, 32))


def _pooled(pad_ref, g, kw, tr, i):
    gw = pad_ref.shape[1] // len(D_WINDOWS)
    cols = slice(g * gw, (g + 1) * gw)
    tot = None
    for j in range(kw):
        sh = pad_ref[HALO - j:HALO - j + tr, cols]
        tot = sh if tot is None else tot + sh
    return tot / _window_count(tr, gw, kw, i * tr) - pad_ref[HALO:HALO + tr, cols]


def _window_count(rows, width, kw, row0):
    t1 = (lax.broadcasted_iota(jnp.int32, (rows, width), 0) + (row0 + 1)).astype(F32)
    return jnp.minimum(t1, float(kw))


def _pool_fwd(hod, pool_w, pool_scale, *, tr=256):
    t = hod.shape[0]
    width = pool_scale.shape[0]
    ng = len(D_WINDOWS)
    gw = width // ng

    def body(z_ref, zh_ref, w_ref, s_ref, y_ref, pad_ref):
        i = pl.program_id(1)
        pad_ref[0:HALO, :] = jnp.where(i > 0, zh_ref[...].astype(F32), 0.0)
        pad_ref[HALO:HALO + tr, :] = z_ref[...].astype(F32)
        for g, kw in enumerate(D_WINDOWS):
            cols = slice(g * gw, (g + 1) * gw)
            pre = jnp.dot(_pooled(pad_ref, g, kw, tr, i).astype(BF16), w_ref[g], preferred_element_type=F32)
            y_ref[:, cols] = (pre * s_ref[:, cols]).astype(BF16)

    return pl.pallas_call(
        body, name="pool_fwd", out_shape=jax.ShapeDtypeStruct((t, width), BF16), grid=(1, t // tr),
        in_specs=[_cur_block(tr, width, 2), _prev_halo(tr, width, 2),
                  pl.BlockSpec((ng, gw, gw), lambda j, i: (0, 0, 0)), pl.BlockSpec((1, width), lambda j, i: (0, 0))],
        out_specs=_cur_block(tr, width, 0),
        scratch_shapes=[pltpu.VMEM((HALO + tr, width), F32)],
        compiler_params=_params(("parallel", "arbitrary")),
    )(hod, hod, pool_w, pool_scale.reshape(1, width))


def _pool_bwd(hod, dy, pool_w, pool_scale, *, dcol0, tr=256):
    t = hod.shape[0]
    width = pool_scale.shape[0]
    ng = len(D_WINDOWS)
    gw = width // ng
    nt = t // tr

    def body(z_ref, zh_ref, dy_ref, dyn_ref, w_ref, s_ref, dz_ref, dw_ref, ds_ref, pad_ref, pade_ref):
        i = pl.program_id(1)
        pad_ref[0:HALO, :] = jnp.where(i > 0, zh_ref[...].astype(F32), 0.0)
        pad_ref[HALO:HALO + tr, :] = z_ref[...].astype(F32)
        dws = []
        dss = []
        for g, kw in enumerate(D_WINDOWS):
            cols = slice(g * gw, (g + 1) * gw)
            wg = w_ref[g]
            dyc = dy_ref[:, cols]
            dpre = (dyc * s_ref[:, cols]).astype(BF16)
            dpre_n = (dyn_ref[:, cols] * s_ref[:, cols]).astype(BF16)
            dpl = lax.dot_general(dpre, wg, _NT, preferred_element_type=F32)
            dpl_n = lax.dot_general(dpre_n, wg, _NT, preferred_element_type=F32)
            pade_ref[0:tr, cols] = dpl / _window_count(tr, gw, kw, i * tr)
            pade_ref[tr:tr + HALO, cols] = jnp.where(i < nt - 1, dpl_n / _window_count(HALO, gw, kw, (i + 1) * tr), 0.0)
            tot = None
            for j in range(kw):
                sh = pade_ref[j:j + tr, cols]
                tot = sh if tot is None else tot + sh
            dz_ref[:, cols] = (tot - dpl).astype(BF16)
            pooled = _pooled(pad_ref, g, kw, tr, i).astype(BF16)
            pre = jnp.dot(pooled, wg, preferred_element_type=F32)
            dss.append(jnp.sum(dyc * pre, axis=0, keepdims=True))
            dws.append(lax.dot_general(pooled, dpre, _TN, preferred_element_type=F32))

        @pl.when(i == 0)
        def _():
            for g in range(ng):
                dw_ref[g] = dws[g]
                ds_ref[:, g * gw:(g + 1) * gw] = dss[g]

        @pl.when(i > 0)
        def _():
            for g in range(ng):
                dw_ref[g] += dws[g]
                ds_ref[:, g * gw:(g + 1) * gw] += dss[g]

    dc = dcol0 // width
    wspec = pl.BlockSpec((ng, gw, gw), lambda j, i: (0, 0, 0))
    vspec = pl.BlockSpec((1, width), lambda j, i: (0, 0))
    return pl.pallas_call(
        body, name="pool_bwd",
        out_shape=(jax.ShapeDtypeStruct((t, width), BF16), jax.ShapeDtypeStruct((ng, gw, gw), F32),
                   jax.ShapeDtypeStruct((1, width), F32)),
        grid=(1, nt),
        in_specs=[_cur_block(tr, width, 2), _prev_halo(tr, width, 2), _cur_block(tr, width, dc),
                  _next_halo(tr, width, dc, t), wspec, vspec],
        out_specs=(_cur_block(tr, width, 0), wspec, vspec),
        scratch_shapes=[pltpu.VMEM((HALO + tr, width), F32), pltpu.VMEM((tr + HALO, width), F32)],
        compiler_params=_params(("arbitrary", "arbitrary")),
    )(hod, hod, dy, dy, pool_w, pool_scale.reshape(1, width))


TM = 512
TN = 512


def _ffn_fwd(x, g, w1, w3, w2):
    t, d = x.shape
    f = w1.shape[1]
    xn = _rmsnorm_fwd(x, g)
    a, b, h = _ffn_up(xn, w1, w3, tm=TM, tn=TN)
    y = _matmul([(h, w2)], m=t, n=d, k=f, tm=TM, tn=TN, tk=f, res=x, alpha=0.5, name="ffn_down")
    return y, (x, xn, a, b, h)


def _ffn_bwd(dx, dxb, saved, g, w1, w3, w2):
    x, xn, a, b, h = saved
    t, d = x.shape
    f = w1.shape[1]
    da, db = _ffn_dh(dxb, w2, a, b, tm=TM, tn=TN)
    dw2 = _matmul([(h, dxb)], ta=True, m=f, n=d, k=t, tm=TM, tn=TN, tk=t, alpha=0.5, out_dtype=BF16, name="ffn_dw2")
    dw1 = _matmul([(xn, da)], ta=True, m=d, n=f, k=t, tm=TM, tn=TN, tk=t, out_dtype=BF16, name="ffn_dw1")
    dw3 = _matmul([(xn, db)], ta=True, m=d, n=f, k=t, tm=TM, tn=TN, tk=t, out_dtype=BF16, name="ffn_dw3")
    dxn = _matmul([(da, w1), (db, w3)], tb=True, m=t, n=d, k=f, tm=TM, tn=TN, tk=f // 2, name="ffn_dxn")
    dx, dxb, dg = _rmsnorm_bwd(dxn, x, g, dx)
    return dx, dxb, dg, dw1, dw3, dw2


def _mix_out_fwd(x, ycat, w_out):
    t, d = x.shape
    return _matmul([(ycat, w_out)], m=t, n=d, k=d, tm=TM, tn=TN, tk=d, res=x, name="mix_out")


def _mix_out_bwd(dxb, ycat, w_out):
    t, d = dxb.shape
    dycat = _matmul([(dxb, w_out)], tb=True, m=t, n=d, k=d, tm=TM, tn=TN, tk=d, name="mix_dy")
    dw_out = _matmul([(ycat, dxb)], ta=True, m=d, n=d, k=t, tm=TM, tn=TN, tk=t, out_dtype=BF16, name="mix_dw_out")
    return dycat, dw_out


def _mix_in_bwd(dh, xn, w_in, x, g, dx, *, tk):
    t, d = x.shape
    n_in = w_in.shape[1]
    dxn = _matmul([(dh, w_in)], tb=True, m=t, n=d, k=n_in, tm=TM, tn=TN, tk=tk, name="mix_dxn")
    dw_in = _matmul([(xn, dh)], ta=True, m=d, n=n_in, k=t, tm=TM, tn=TN, tk=t, out_dtype=BF16, name="mix_dw_in")
    dx, dxb, dg = _rmsnorm_bwd(dxn, x, g, dx)
    return dx, dxb, dg, dw_in


def _group_view(a, col0, dil):
    t = a.shape[0]
    return a[:, col0:col0 + A_WIDTH].reshape(t // dil, dil * A_WIDTH)


def _even_fwd(x, g, w_in, qg, kg, conv_w, w_out):
    t, d = x.shape
    n_in = w_in.shape[1]
    nq = len(A_DILATIONS) * A_WIDTH
    xn = _rmsnorm_fwd(x, g)
    h = _matmul([(xn, w_in)], m=t, n=n_in, k=d, tm=TM, tn=TN, tk=d, out_dtype=BF16, name="ev_in")
    outs, lses = [], []
    for gi, dil in enumerate(A_DILATIONS):
        o, l = _attn_fwd(_group_view(h, gi * A_WIDTH, dil), _group_view(h, nq + gi * A_WIDTH, dil),
                         _group_view(h, 2 * nq + gi * A_WIDTH, dil), qg, kg, dil=dil)
        outs.append(o.reshape(t, A_WIDTH))
        lses.append(l.reshape(t, A_WIDTH))
    ya, lse = _attn_combine(outs, lses)
    yb = _gateconv_fwd(h, conv_w, col0=3 * nq)
    ycat = jnp.concatenate([ya, yb], axis=1)
    return _mix_out_fwd(x, ycat, w_out), (x, xn, h, ya, lse, ycat)


def _even_bwd(dx, dxb, saved, g, w_in, qg, kg, conv_w, w_out):
    x, xn, h, ya, lse, ycat = saved
    t, d = x.shape
    nq = len(A_DILATIONS) * A_WIDTH
    dycat, dw_out = _mix_out_bwd(dxb, ycat, w_out)
    dlt = _attn_delta(dycat, ya)
    dqs, dks, dvs = [], [], []
    dqg = jnp.zeros((HEAD_DIM,), F32)
    dkg = jnp.zeros((HEAD_DIM,), F32)
    for gi, dil in enumerate(A_DILATIONS):
        dq, dk, dv, dqg_i, dkg_i = _attn_bwd(
            _group_view(h, gi * A_WIDTH, dil), _group_view(h, nq + gi * A_WIDTH, dil),
            _group_view(h, 2 * nq + gi * A_WIDTH, dil), _group_view(dycat, 0, dil),
            _group_view(lse, 0, dil), _group_view(dlt, 0, dil), qg, kg, dil=dil)
        dqs.append(dq.reshape(t, A_WIDTH))
        dks.append(dk.reshape(t, A_WIDTH))
        dvs.append(dv.reshape(t, A_WIDTH))
        dqg = dqg + dqg_i.reshape(HEAD_DIM)
        dkg = dkg + dkg_i.reshape(HEAD_DIM)
    dbg, dcg, dxt, dcw = _gateconv_bwd(h, dycat, conv_w, col0=3 * nq, dcol0=A_WIDTH)
    dh = jnp.concatenate(dqs + dks + dvs + [dbg, dcg, dxt], axis=1)
    dx, dxb, dg, dw_in = _mix_in_bwd(dh, xn, w_in, x, g, dx, tk=2048)
    return dx, dxb, dg, dw_in, dqg, dkg, dcw[:B_CONV], dw_out


def _odd_fwd(x, g, w_in, conv_w, conv_b, ln_g, ln_b, pool_w, pool_scale, w_out):
    t, d = x.shape
    n_in = w_in.shape[1]
    xn = _rmsnorm_fwd(x, g)
    hod = _matmul([(xn, w_in)], m=t, n=n_in, k=d, tm=TM, tn=TN, tk=d, out_dtype=BF16, name="od_in")
    u1 = _glu_conv_fwd(hod, conv_w, conv_b)
    u = _ln_silu_fwd(u1, ln_g, ln_b)
    yd = _pool_fwd(hod, pool_w.astype(BF16), pool_scale)
    ycat = jnp.concatenate([u, yd], axis=1)
    return _mix_out_fwd(x, ycat, w_out), (x, xn, hod, u1, ycat)


def _odd_bwd(dx, dxb, saved, g, w_in, conv_w, conv_b, ln_g, ln_b, pool_w, pool_scale, w_out):
    x, xn, hod, u1, ycat = saved
    width = conv_w.shape[1]
    dycat, dw_out = _mix_out_bwd(dxb, ycat, w_out)
    du1, dlg, dlb, dcb = _ln_silu_bwd(dycat, u1, ln_g, ln_b, col0=0)
    dval, dgate, dcw = _glu_conv_bwd(hod, du1, conv_w)
    dz, dpw, dps = _pool_bwd(hod, dycat, pool_w.astype(BF16), pool_scale, dcol0=width)
    dh = jnp.concatenate([dval, dgate, dz], axis=1)
    dx, dxb, dg, dw_in = _mix_in_bwd(dh, xn, w_in, x, g, dx, tk=dh.shape[1])
    return (dx, dxb, dg, dw_in, dcw[:C_CONV], dcb.reshape(width), dlg.reshape(width), dlb.reshape(width),
            dpw, dps.reshape(width), dw_out)


def _sublayer_matrices(s):
    layer, slot = divmod(s, 3)
    if slot == 1:
        kind = "ev" if layer % 2 == 0 else "od"
        return [(kind + "_w_in", layer // 2), (kind + "_w_out", layer // 2)]
    j = 2 * layer + slot // 2
    return [("ffn_w1", j), ("ffn_w3", j), ("ffn_w2", j)]


def _local_step(x, target, wts, fetch=None, emit=None):
    depth = wts["norm_g"].shape[0]
    if fetch is None:
        fetch = lambda s, after: ([wts[name][idx] for name, idx in _sublayer_matrices(s)], 0.0)
    gr = {}
    if emit is None:
        def emit(s, mats):
            for (name, idx), g in zip(_sublayer_matrices(s), mats):
                gr.setdefault(name, {})[idx] = g
            return 0.0

    def gain(layer, slot, tok):
        return wts["norm_g"][layer, slot] + tok

    saved = []
    for layer in range(depth):
        i = layer // 2
        s = 3 * layer
        m0, tok = fetch(s, x)
        x, s0 = _ffn_fwd(x, gain(layer, 0, tok), *m0)
        m1, tok = fetch(s + 1, x)
        if layer % 2 == 0:
            x, s1 = _even_fwd(x, gain(layer, 1, tok), m1[0], wts["ev_q_gain"][i],
                              wts["ev_k_gain"][i], wts["ev_conv_w"][i], m1[1])
        else:
            x, s1 = _odd_fwd(x, gain(layer, 1, tok), m1[0], wts["od_conv_w"][i],
                             wts["od_conv_b"][i], wts["od_ln_g"][i], wts["od_ln_b"][i], wts["od_pool_w"][i],
                             wts["od_pool_scale"][i], m1[1])
        m2, tok = fetch(s + 2, x)
        x, s2 = _ffn_fwd(x, gain(layer, 2, tok), *m2)
        saved.append(((s0, m0), (s1, m1), (s2, m2)))
    dx, dxb, sq = _loss_head(x, target)

    n_even, n_odd = (depth + 1) // 2, depth // 2
    for k in ("ev_q_gain", "ev_k_gain", "ev_conv_w"):
        gr[k] = [None] * n_even
    for k in ("od_conv_w", "od_conv_b", "od_ln_g", "od_ln_b", "od_pool_w", "od_pool_scale"):
        gr[k] = [None] * n_odd
    dnorm = [[None] * 3 for _ in range(depth)]
    tok = 0.0
    for layer in reversed(range(depth)):
        i = layer // 2
        s = 3 * layer
        (s0, m0), (s1, m1), (s2, m2) = saved[layer]
        dx, dxb, dnorm[layer][2], dw1, dw3, dw2 = _ffn_bwd(dx, dxb, s2, gain(layer, 2, tok), *m2)
        tok = emit(s + 2, [dw1, dw3, dw2])
        if layer % 2 == 0:
            (dx, dxb, dnorm[layer][1], dw_in, gr["ev_q_gain"][i], gr["ev_k_gain"][i],
             gr["ev_conv_w"][i], dw_out) = _even_bwd(
                dx, dxb, s1, gain(layer, 1, tok), m1[0], wts["ev_q_gain"][i],
                wts["ev_k_gain"][i], wts["ev_conv_w"][i], m1[1])
        else:
            (dx, dxb, dnorm[layer][1], dw_in, gr["od_conv_w"][i], gr["od_conv_b"][i], gr["od_ln_g"][i],
             gr["od_ln_b"][i], gr["od_pool_w"][i], gr["od_pool_scale"][i], dw_out) = _odd_bwd(
                dx, dxb, s1, gain(layer, 1, tok), m1[0], wts["od_conv_w"][i],
                wts["od_conv_b"][i], wts["od_ln_g"][i], wts["od_ln_b"][i], wts["od_pool_w"][i],
                wts["od_pool_scale"][i], m1[1])
        tok = emit(s + 1, [dw_in, dw_out])
        dx, dxb, dnorm[layer][0], dw1, dw3, dw2 = _ffn_bwd(dx, dxb, s0, gain(layer, 0, tok), *m0)
        tok = emit(s, [dw1, dw3, dw2])
    gr["norm_g"] = jnp.stack([jnp.stack(r) for r in dnorm])
    for name in list(gr):
        if isinstance(gr[name], dict):
            gr[name] = [gr[name][idx] for idx in sorted(gr[name])]
    return sq, dx, gr


HBM_SPEC = pl.BlockSpec(memory_space=pltpu.HBM)
SEM_SPEC = pl.BlockSpec(memory_space=pltpu.SEMAPHORE)
ANY_SPEC = pl.BlockSpec(memory_space=pl.ANY)
EFFECT = pltpu.SideEffectType.DATAFLOW_SIDE_EFFECTING


def _place():
    x, y, c = lax.axis_index("x"), lax.axis_index("y"), lax.axis_index("c")
    chips = [(1 - x, y), (x, 1 - y), (1 - x, 1 - y)]
    return x, y, c, chips


def _chip_index(x, y):
    return 2 * x + y


def _ds(start, size, align):
    if isinstance(start, int):
        return pl.ds(start, size)
    return pl.ds(pl.multiple_of(start, align), size)


def _half(ref, axis, h):
    r, c = ref.shape[-2:]
    if axis == 1:
        return ref.at[_ds(h * (r // 2), r // 2, 16), :]
    return ref.at[:, _ds(h * (c // 2), c // 2, 128)]


def _chunk(ref, axis, j, n=N_CHIPS):
    r, c = ref.shape[-2:]
    if axis == 1:
        return ref.at[:, _ds(j * (c // n), c // n, 128)]
    return ref.at[_ds(j * (r // n), r // n, 16), :]


def _remote(src, dst, send_sem, recv_sem, device):
    return pltpu.make_async_remote_copy(src_ref=src, dst_ref=dst, send_sem=send_sem, recv_sem=recv_sem,
                                        device_id=device, device_id_type=MESH)


def _hbm(a):
    return pltpu.with_memory_space_constraint(a, pltpu.HBM)


def _cast_into(stacked, idx, chip, axis):
    _, r, c = stacked.shape
    full = (r, N_CHIPS * c) if axis == 1 else (N_CHIPS * r, c)
    tr = 128
    while tr > 16 and tr * c * 4 > (1 << 20):
        tr //= 2
    if axis == 1:
        o_map = lambda i, s: (i, s[0])
    else:
        o_map = lambda i, s: (s[0] * (r // tr) + i, 0)

    def body(s_ref, w_ref, o_ref):
        o_ref[...] = w_ref[...].astype(BF16)

    return pl.pallas_call(
        body, name="cast_into", out_shape=jax.ShapeDtypeStruct(full, BF16),
        grid_spec=pltpu.PrefetchScalarGridSpec(
            num_scalar_prefetch=1, grid=(r // tr,),
            in_specs=[pl.BlockSpec((None, tr, c), lambda i, s: (idx, i, 0))],
            out_specs=pl.BlockSpec((tr, c), o_map)),
        compiler_params=_params(("parallel",)),
    )(chip, stacked)


def _own_piece(ref, axis, me, cc):
    return _half(_chunk(ref, axis, me), axis, cc)


def _ag_start(fulls, axes):
    n = len(fulls)

    def body(*refs):
        ins = refs[:n]
        send, recv = refs[n:4 * n], refs[4 * n:7 * n]
        token = refs[8 * n]
        x, y, cc, chips = _place()
        me = _chip_index(x, y)
        for i in range(n):
            piece = _own_piece(ins[i], axes[i], me, cc)
            for k, chip in enumerate(chips):
                _remote(piece, piece, send[3 * i + k], recv[3 * i + k], (*chip, cc)).start()
        token[...] = jnp.zeros_like(token)

    sem = pltpu.SemaphoreType.DMA(())
    outs = pl.pallas_call(
        body, name="ag_start_%d" % n,
        out_shape=tuple([sem] * (6 * n) + [pltpu.HBM(f.shape, f.dtype) for f in fulls]
                        + [jax.ShapeDtypeStruct((8, 128), F32)]),
        in_specs=[HBM_SPEC] * n,
        out_specs=tuple([SEM_SPEC] * (6 * n) + [HBM_SPEC] * n + [pl.BlockSpec(memory_space=pltpu.VMEM)]),
        input_output_aliases={i: 6 * n + i for i in range(n)},
        compiler_params=pltpu.CompilerParams(has_side_effects=EFFECT),
    )(*[_hbm(f) for f in fulls])
    return outs[:3 * n], outs[3 * n:6 * n], outs[6 * n:7 * n], outs[7 * n][0, 0]


def _ag_wait(send, recv, fulls, axes, after):
    n = len(fulls)

    def body(*refs):
        ins = refs[:n]
        send_s, recv_s = refs[n:4 * n], refs[4 * n:7 * n]
        x, y, cc, chips = _place()
        me = _chip_index(x, y)
        for i in range(n):
            mine = _own_piece(ins[i], axes[i], me, cc)
            for k, chip in enumerate(chips):
                got = _own_piece(ins[i], axes[i], _chip_index(*chip), cc)
                cp = _remote(mine, got, send_s[3 * i + k], recv_s[3 * i + k], (*chip, cc))
                cp.wait_send()
                cp.wait_recv()

    return pl.pallas_call(
        body, name="ag_wait_%d" % n,
        out_shape=tuple(pltpu.HBM(f.shape, f.dtype) for f in fulls),
        in_specs=[HBM_SPEC] * n + [SEM_SPEC] * (6 * n) + [ANY_SPEC],
        out_specs=tuple([HBM_SPEC] * n),
        input_output_aliases={i: i for i in range(n)},
        compiler_params=pltpu.CompilerParams(has_side_effects=EFFECT),
    )(*fulls, *send, *recv, after)


def _ag_forward(fulls, axes):
    n = len(fulls)

    def body(*refs):
        ins = refs[:n]
        send_sems, recv_sems = refs[2 * n], refs[2 * n + 1]
        x, y, cc, chips = _place()
        cps = []
        for i in range(n):
            for k, chip in enumerate(chips):
                got = _own_piece(ins[i], axes[i], _chip_index(*chip), cc)
                cp = _remote(got, got, send_sems.at[3 * i + k], recv_sems.at[3 * i + k], (x, y, 1 - cc))
                cp.start()
                cps.append(cp)
        for i in range(n):
            for k, chip in enumerate(chips):
                other = _own_piece(ins[i], axes[i], _chip_index(*chip), 1 - cc)
                cps[3 * i + k].wait_send()
                _remote(other, other, send_sems.at[3 * i + k], recv_sems.at[3 * i + k], (x, y, cc)).wait_recv()

    return pl.pallas_call(
        body, name="ag_forward_%d" % n,
        out_shape=tuple(jax.ShapeDtypeStruct(f.shape, f.dtype) for f in fulls),
        in_specs=[HBM_SPEC] * n, out_specs=tuple([HBM_SPEC] * n),
        input_output_aliases={i: i for i in range(n)},
        scratch_shapes=[pltpu.SemaphoreType.DMA((3 * n,)), pltpu.SemaphoreType.DMA((3 * n,))],
    )(*fulls)


def _rs_pair(g, axis):
    r, c = g.shape
    hshape = (r // 2, c) if axis == 1 else (r, c // 2)

    def body(g_ref, o_ref, send_sem, recv_sem):
        x, y, cc, _ = _place()
        cp = _remote(_half(g_ref, axis, 1 - cc), o_ref, send_sem, recv_sem, (x, y, 1 - cc))
        cp.start()
        cp.wait()

    return pl.pallas_call(
        body, name="rs_pair", out_shape=jax.ShapeDtypeStruct(hshape, g.dtype),
        in_specs=[HBM_SPEC], out_specs=HBM_SPEC,
        scratch_shapes=[pltpu.SemaphoreType.DMA, pltpu.SemaphoreType.DMA],
    )(g)


def _add_pair(g, got, cc, axis, *, tr, tc):
    r, c = g.shape
    hr, hc = got.shape
    if axis == 1:
        g_map = lambda i, j, s: (s[0] * (hr // tr) + i, j)
    else:
        g_map = lambda i, j, s: (i, s[0] * (hc // tc) + j)

    def body(s_ref, g_ref, r_ref, o_ref):
        o_ref[...] = (g_ref[...].astype(F32) + r_ref[...].astype(F32)).astype(BF16)

    blk = pl.BlockSpec((tr, tc), lambda i, j, s: (i, j))
    return pl.pallas_call(
        body, name="add_pair", out_shape=jax.ShapeDtypeStruct((hr, hc), BF16),
        grid_spec=pltpu.PrefetchScalarGridSpec(
            num_scalar_prefetch=1, grid=(hr // tr, hc // tc),
            in_specs=[pl.BlockSpec((tr, tc), g_map), blk], out_specs=blk),
        compiler_params=_params(("parallel", "parallel")),
    )(cc, g, got)


def _piece_shape(p, axis):
    r, c = p.shape
    return (r, c // N_CHIPS) if axis == 1 else (r // N_CHIPS, c)


def _rs_chips_start(ps, axes):
    n = len(ps)
    lands = [lax.empty((3,) + _piece_shape(p, ax), p.dtype) for p, ax in zip(ps, axes)]

    def body(*refs):
        p_refs, land_refs = refs[:n], refs[n:2 * n]
        send, recv = refs[2 * n:5 * n], refs[5 * n:8 * n]
        token = refs[10 * n]
        x, y, cc, chips = _place()
        for i in range(n):
            for k, chip in enumerate(chips):
                _remote(_chunk(p_refs[i], axes[i], _chip_index(*chip)), land_refs[i].at[k],
                        send[3 * i + k], recv[3 * i + k], (*chip, cc)).start()
        token[...] = jnp.zeros_like(token)

    sem = pltpu.SemaphoreType.DMA(())
    outs = pl.pallas_call(
        body, name="rs_start_%d" % n,
        out_shape=tuple([sem] * (6 * n) + [pltpu.HBM(a.shape, a.dtype) for a in list(ps) + lands]
                        + [jax.ShapeDtypeStruct((8, 128), F32)]),
        in_specs=[HBM_SPEC] * (2 * n),
        out_specs=tuple([SEM_SPEC] * (6 * n) + [HBM_SPEC] * (2 * n) + [pl.BlockSpec(memory_space=pltpu.VMEM)]),
        input_output_aliases={i: 6 * n + i for i in range(2 * n)},
        compiler_params=pltpu.CompilerParams(has_side_effects=EFFECT),
    )(*[_hbm(a) for a in list(ps) + lands])
    return outs[:3 * n], outs[3 * n:6 * n], outs[6 * n:7 * n], outs[7 * n:8 * n], outs[8 * n][0, 0]


def _rs_chips_wait(send, recv, ps, lands, axes, after):
    n = len(ps)

    def body(*refs):
        p_refs, land_refs = refs[:n], refs[n:2 * n]
        send_s, recv_s = refs[2 * n:5 * n], refs[5 * n:8 * n]
        x, y, cc, chips = _place()
        for i in range(n):
            for k, chip in enumerate(chips):
                cp = _remote(_chunk(p_refs[i], axes[i], _chip_index(*chip)), land_refs[i].at[k],
                             send_s[3 * i + k], recv_s[3 * i + k], (*chip, cc))
                cp.wait_send()
                cp.wait_recv()

    outs = pl.pallas_call(
        body, name="rs_wait_%d" % n,
        out_shape=tuple(pltpu.HBM(a.shape, a.dtype) for a in list(ps) + list(lands)),
        in_specs=[HBM_SPEC] * (2 * n) + [SEM_SPEC] * (6 * n) + [ANY_SPEC],
        out_specs=tuple([HBM_SPEC] * (2 * n)),
        input_output_aliases={i: i for i in range(2 * n)},
        compiler_params=pltpu.CompilerParams(has_side_effects=EFFECT),
    )(*ps, *lands, *send, *recv, after)
    return outs[:n], outs[n:]


def _add_chips(p, got, chip, core, axis, *, idx, count, into, tr, tc):
    _, pr, pc = got.shape
    shard = (2 * pr, pc) if axis == 1 else (pr, 2 * pc)
    if axis == 1:
        p_map = lambda i, j, sc, so: (i, sc[0] * (pc // tc) + j)
        o_map = lambda i, j, sc, so: (idx, so[0] * (pr // tr) + i, j)
    else:
        p_map = lambda i, j, sc, so: (sc[0] * (pr // tr) + i, j)
        o_map = lambda i, j, sc, so: (idx, i, so[0] * (pc // tc) + j)

    def body(sc_ref, so_ref, p_ref, r_ref, *rest):
        o_ref = rest[-1]
        acc = p_ref[...].astype(F32)
        for k in range(3):
            acc = acc + r_ref[k].astype(F32)
        o_ref[...] = acc

    in_specs = [pl.BlockSpec((tr, tc), p_map), pl.BlockSpec((3, tr, tc), lambda i, j, sc, so: (0, i, j))]
    args = [chip, core, p, got]
    aliases = {}
    if into is not None:
        in_specs.append(ANY_SPEC)
        args.append(into)
        aliases = {4: 0}
    return pl.pallas_call(
        body, name="add_chips", out_shape=jax.ShapeDtypeStruct((count,) + shard, F32),
        grid_spec=pltpu.PrefetchScalarGridSpec(
            num_scalar_prefetch=2, grid=(pr // tr, pc // tc), in_specs=in_specs,
            out_specs=pl.BlockSpec((None, tr, tc), o_map)),
        input_output_aliases=aliases,
        compiler_params=_params(("parallel", "parallel")),
    )(*args)


def _rs_final(stacked, axis):
    n = stacked.shape[0]

    def body(s_ref, o_ref, send_sems, recv_sems):
        x, y, cc, _ = _place()
        cps = []
        for i in range(n):
            mine = _half(s_ref.at[i], axis, cc)
            cp = _remote(mine, mine, send_sems.at[i], recv_sems.at[i], (x, y, 1 - cc))
            cp.start()
            cps.append(cp)
        for i, cp in enumerate(cps):
            other = _half(s_ref.at[i], axis, 1 - cc)
            cp.wait_send()
            _remote(other, other, send_sems.at[i], recv_sems.at[i], (x, y, cc)).wait_recv()

    return pl.pallas_call(
        body, name="rs_final", out_shape=jax.ShapeDtypeStruct(stacked.shape, stacked.dtype),
        in_specs=[HBM_SPEC], out_specs=HBM_SPEC, input_output_aliases={0: 0},
        scratch_shapes=[pltpu.SemaphoreType.DMA((n,)), pltpu.SemaphoreType.DMA((n,))],
    )(stacked)


def _ag_small(packed):
    rows, cols = packed.shape

    def body(s_ref, o_ref, send_sems, recv_sems, local_sem):
        x, y, cc, chips = _place()
        me = _chip_index(x, y)
        own = pltpu.make_async_copy(s_ref, o_ref.at[me], local_sem)
        own.start()
        cps = [_remote(s_ref, o_ref.at[me], send_sems.at[k], recv_sems.at[k], (*chip, cc))
               for k, chip in enumerate(chips)]
        for cp in cps:
            cp.start()
        for k, chip in enumerate(chips):
            cps[k].wait_send()
            got = o_ref.at[_chip_index(*chip)]
            _remote(got, got, send_sems.at[k], recv_sems.at[k], (x, y, cc)).wait_recv()
        own.wait()

    return pl.pallas_call(
        body, name="ag_small", out_shape=jax.ShapeDtypeStruct((N_CHIPS, rows, cols), packed.dtype),
        in_specs=[HBM_SPEC], out_specs=HBM_SPEC,
        scratch_shapes=[pltpu.SemaphoreType.DMA((3,)), pltpu.SemaphoreType.DMA((3,)), pltpu.SemaphoreType.DMA],
    )(packed)


def _rs_small(packed):
    _, rows, cols = packed.shape
    rels = [(bx, by, bc) for bx in (0, 1) for by in (0, 1) for bc in (0, 1)][1:]

    def body(s_ref, o_ref, send_sems, recv_sems, local_sem):
        x, y, cc, _ = _place()
        me = 4 * x + 2 * y + cc
        own = pltpu.make_async_copy(s_ref.at[_chip_index(x, y)], o_ref.at[me], local_sem)
        own.start()
        peers = [(jnp.bitwise_xor(x, bx), jnp.bitwise_xor(y, by), jnp.bitwise_xor(cc, bc)) for bx, by, bc in rels]
        cps = [_remote(s_ref.at[_chip_index(px, py)], o_ref.at[me], send_sems.at[k], recv_sems.at[k], (px, py, pc))
               for k, (px, py, pc) in enumerate(peers)]
        for cp in cps:
            cp.start()
        for k, (px, py, pc) in enumerate(peers):
            cps[k].wait_send()
            got = o_ref.at[4 * px + 2 * py + pc]
            _remote(got, got, send_sems.at[k], recv_sems.at[k], (x, y, cc)).wait_recv()
        own.wait()

    return pl.pallas_call(
        body, name="rs_small", out_shape=jax.ShapeDtypeStruct((2 * N_CHIPS, rows, cols), packed.dtype),
        in_specs=[HBM_SPEC], out_specs=HBM_SPEC,
        scratch_shapes=[pltpu.SemaphoreType.DMA((7,)), pltpu.SemaphoreType.DMA((7,)), pltpu.SemaphoreType.DMA],
    )(packed)


def _sum_slots(slots, *, tr=8):
    n, rows, cols = slots.shape

    def body(s_ref, o_ref):
        acc = s_ref[0]
        for k in range(1, n):
            acc = acc + s_ref[k]
        o_ref[...] = acc

    return pl.pallas_call(
        body, name="sum_slots", out_shape=jax.ShapeDtypeStruct((rows, cols), F32), grid=(rows // tr,),
        in_specs=[pl.BlockSpec((n, tr, cols), lambda i: (0, i, 0))], out_specs=pl.BlockSpec((tr, cols), lambda i: (i, 0)),
        compiler_params=_params(("parallel",)),
    )(slots)


MATRIX_AXIS = {"ffn_w1": 1, "ffn_w3": 1, "ffn_w2": 0, "ev_w_in": 1, "ev_w_out": 0, "od_w_in": 1, "od_w_out": 0}
SMALL_SHARDED = ("norm_g", "ev_conv_w", "od_conv_w", "od_conv_b", "od_ln_g", "od_ln_b", "od_pool_scale")
SMALL_REPLICATED = ("ev_q_gain", "ev_k_gain")
PACK_COLS = 1024
PACK_ROW_ALIGN = 8


def _pack(parts):
    flat = jnp.concatenate([p.reshape(-1).astype(F32) for p in parts])
    per = PACK_COLS * PACK_ROW_ALIGN
    total = -(-flat.shape[0] // per) * per
    return jnp.pad(flat, (0, total - flat.shape[0])).reshape(total // PACK_COLS, PACK_COLS)


def _unpack(packed, shapes):
    flat = packed.reshape(-1)
    out, pos = [], 0
    for shp in shapes:
        size = 1
        for s in shp:
            size *= s
        out.append(flat[pos:pos + size].reshape(shp))
        pos += size
    return out


def _row_tile(rows, cols, itemsize=4, budget=1 << 20):
    tr = 8
    while rows % (2 * tr) == 0 and 2 * tr * cols * itemsize <= budget:
        tr *= 2
    return tr


def _piece_tiles(hr, hc, axis):
    pr, pc = (hr, hc // N_CHIPS) if axis == 1 else (hr // N_CHIPS, hc)
    tr = 128
    while tr > 16 and tr * pc * 4 > (1 << 20):
        tr //= 2
    return tr, pc


AG_LOOKAHEAD = 2


def kernel(x, norm_g, ffn_w1, ffn_w3, ffn_w2, ev_w_in, ev_q_gain, ev_k_gain, ev_conv_w, ev_w_out, od_w_in, od_conv_w, od_conv_b, od_ln_g, od_ln_b, od_pool_w, od_pool_scale, od_w_out, loss_target, m_norm_g, m_ffn_w1, m_ffn_w3, m_ffn_w2, m_ev_w_in, m_ev_q_gain, m_ev_k_gain, m_ev_conv_w, m_ev_w_out, m_od_w_in, m_od_conv_w, m_od_conv_b, m_od_ln_g, m_od_ln_b, m_od_pool_w, m_od_pool_scale, m_od_w_out, v_norm_g, v_ffn_w1, v_ffn_w3, v_ffn_w2, v_ev_w_in, v_ev_q_gain, v_ev_k_gain, v_ev_conv_w, v_ev_w_out, v_od_w_in, v_od_conv_w, v_od_conv_b, v_od_ln_g, v_od_ln_b, v_od_pool_w, v_od_pool_scale, v_od_w_out):
    names = ["norm_g", "ffn_w1", "ffn_w3", "ffn_w2", "ev_w_in", "ev_q_gain", "ev_k_gain", "ev_conv_w", "ev_w_out",
             "od_w_in", "od_conv_w", "od_conv_b", "od_ln_g", "od_ln_b", "od_pool_w", "od_pool_scale", "od_w_out"]
    w = dict(zip(names, (norm_g, ffn_w1, ffn_w3, ffn_w2, ev_w_in, ev_q_gain, ev_k_gain, ev_conv_w, ev_w_out,
                         od_w_in, od_conv_w, od_conv_b, od_ln_g, od_ln_b, od_pool_w, od_pool_scale, od_w_out)))
    m = dict(zip(names, (m_norm_g, m_ffn_w1, m_ffn_w3, m_ffn_w2, m_ev_w_in, m_ev_q_gain, m_ev_k_gain, m_ev_conv_w,
                         m_ev_w_out, m_od_w_in, m_od_conv_w, m_od_conv_b, m_od_ln_g, m_od_ln_b, m_od_pool_w,
                         m_od_pool_scale, m_od_w_out)))
    v = dict(zip(names, (v_norm_g, v_ffn_w1, v_ffn_w3, v_ffn_w2, v_ev_w_in, v_ev_q_gain, v_ev_k_gain, v_ev_conv_w,
                         v_ev_w_out, v_od_w_in, v_od_conv_w, v_od_conv_b, v_od_ln_g, v_od_ln_b, v_od_pool_w,
                         v_od_pool_scale, v_od_w_out)))
    cx, cy, cc = lax.axis_index("x"), lax.axis_index("y"), lax.axis_index("c")
    cc_arr = jnp.reshape(cc, (1,)).astype(jnp.int32)
    chip_arr = jnp.reshape(_chip_index(cx, cy), (1,)).astype(jnp.int32)
    n_sub = 3 * norm_g.shape[0]
    stacked = {name: w[name].reshape((-1,) + w[name].shape[-2:]) for name in MATRIX_AXIS}

    def axes_of(s):
        return [MATRIX_AXIS[name] for name, _ in _sublayer_matrices(s)]

    ag_inflight = {}

    def ag_start(s, after):
        fulls = []
        for name, idx in _sublayer_matrices(s):
            src, _ = lax.optimization_barrier((stacked[name], after))
            fulls.append(_cast_into(src, idx, chip_arr, MATRIX_AXIS[name]))
        send, recv, fulls, tok = _ag_start(fulls, axes_of(s))
        ag_inflight[s] = (send, recv, fulls)
        return tok

    x0 = x[0]
    first_tok = sum(ag_start(s, x0) for s in range(min(AG_LOOKAHEAD, n_sub)))

    def fetch(s, after):
        tok = first_tok if s == 0 else 0.0
        if s + AG_LOOKAHEAD < n_sub:
            tok = tok + ag_start(s + AG_LOOKAHEAD, after)
        send, recv, fulls = ag_inflight.pop(s)
        fulls = _ag_wait(send, recv, fulls, axes_of(s), after)
        return list(_ag_forward(fulls, axes_of(s))), tok

    rs_inflight = {}

    def emit(s, mats):
        axes = axes_of(s)
        ps = []
        for g, ax in zip(mats, axes):
            got = _rs_pair(g, ax)
            tr, tc = _piece_tiles(*got.shape, ax)
            ps.append(_add_pair(g, got, cc_arr, ax, tr=tr, tc=tc))
        send, recv, ps, lands, tok = _rs_chips_start(ps, axes)
        rs_inflight[s] = (send, recv, ps, lands)
        return tok

    wts = {}
    small_names = SMALL_SHARDED + ("od_pool_w",)
    gathered = _ag_small(_pack([w[k] for k in small_names]))
    per_chip = [_unpack(gathered[j], [w[k].shape for k in small_names]) for j in range(N_CHIPS)]
    for idx, k in enumerate(small_names):
        ax = 2 if k == "od_pool_w" else w[k].ndim - 1
        wts[k] = jnp.concatenate([per_chip[j][idx] for j in range(N_CHIPS)], axis=ax)
    for k in SMALL_REPLICATED:
        wts[k] = w[k]

    sq, dx, gr = _local_step(x0, loss_target[0], wts, fetch, emit)

    grads = {}
    reduced = {name: None for name in MATRIX_AXIS}
    for s in reversed(range(n_sub)):
        send, recv, ps, lands = rs_inflight.pop(s)
        axes = axes_of(s)
        ps, lands = _rs_chips_wait(send, recv, ps, lands, axes, dx)
        for (name, idx), p, land, ax in zip(_sublayer_matrices(s), ps, lands, axes):
            tr, tc = _piece_tiles(*p.shape, ax)
            reduced[name] = _add_chips(p, land, chip_arr, cc_arr, ax, idx=idx, count=stacked[name].shape[0],
                                       into=reduced[name], tr=tr, tc=tc)
    for name, axis in MATRIX_AXIS.items():
        grads[name] = _rs_final(reduced[name], axis).reshape(w[name].shape)
    small_full = {k: (gr[k] if k == "norm_g" else jnp.stack(gr[k])) for k in small_names + SMALL_REPLICATED}
    chunks = []
    for j in range(N_CHIPS):
        parts = []
        for k in small_names:
            ax = 2 if k == "od_pool_w" else w[k].ndim - 1
            size = w[k].shape[ax]
            parts.append(lax.slice_in_dim(small_full[k], j * size, (j + 1) * size, axis=ax))
        parts += [small_full[k] for k in SMALL_REPLICATED] + [sq[0, :1]]
        chunks.append(_pack(parts))
    summed = _sum_slots(_rs_small(jnp.stack(chunks)))
    pack_names = small_names + SMALL_REPLICATED
    unpacked = _unpack(summed, [w[k].shape for k in pack_names] + [(1,)])
    for k, g in zip(pack_names, unpacked):
        grads[k] = g
    loss = (0.5 / x.shape[-1]) * unpacked[-1][0]

    delta, new_m, new_v = {}, {}, {}
    for name in MATRIX_AXIS:
        shp = w[name].shape
        cols = shp[-1]
        rows = w[name].size // cols
        tr = _row_tile(rows, cols, budget=1 << 20)
        d_, m_, v_ = _adamw(w[name].reshape(rows, cols), grads[name].reshape(rows, cols),
                            m[name].reshape(rows, cols), v[name].reshape(rows, cols), tr=tr)
        delta[name], new_m[name], new_v[name] = d_.reshape(shp), m_.reshape(shp), v_.reshape(shp)
    pw, pg, pm, pv = (_pack([t[k] for k in pack_names]) for t in (w, grads, m, v))
    d_, m_, v_ = _adamw(pw, pg, pm, pv, tr=PACK_ROW_ALIGN)
    shapes = [w[k].shape for k in pack_names]
    for store, packed in ((delta, d_), (new_m, m_), (new_v, v_)):
        for k, a in zip(pack_names, _unpack(packed, shapes)):
            store[k] = a

    return (loss, dx[None], *[grads[k] for k in names], *[delta[k] for k in names],
            *[new_m[k] for k in names], *[new_v[k] for k in names])
```

```python
import jax
import jax.numpy as jnp
from jax import lax
from jax.experimental import pallas as pl
from jax.experimental.pallas import tpu as pltpu

F32 = jnp.float32
BF16 = jnp.bfloat16
MESH = pl.DeviceIdType.MESH

EPS = 1e-6
HEADS = 8
HEAD_DIM = 128
A_WIDTH = HEADS * HEAD_DIM
A_WINDOWS = (128, 512, 2048)
A_DILATIONS = (1, 4, 16)
ATTN_BLOCK = 128
B_CONV = 3
C_CONV = 31
D_WINDOWS = (2, 4, 8, 16)
HALO = 32
N_CHIPS = 4
ADAM_LR = 0.001
ADAM_B1 = 0.9
ADAM_B2 = 0.999
ADAM_EPS = 1e-08
ADAM_WD = 0.01
ADAM_STEP = 10
VMEM_LIMIT_BYTES = 56 * 1024 * 1024
NEG_BIG = -1e30


def _params(sem, **kw):
    return pltpu.CompilerParams(dimension_semantics=sem, vmem_limit_bytes=VMEM_LIMIT_BYTES, **kw)


def _sigmoid(x):
    return 1.0 / (1.0 + jnp.exp(-x))


EPILOGUE_COLS = 256


def _matmul(pairs, *, m, n, k, tm, tn, tk, ta=False, tb=False, out_dtype=F32, res=None,
            alpha=1.0, name, j_outer=False, half=None):
    nk = k // tk
    npairs = len(pairs)
    dn = (((0 if ta else 1,), (1 if tb else 0,)), ((), ()))

    def ij(p, q):
        return (q, p) if j_outer else (p, q)

    def shift(s, operand, blocks):
        if half is None or half[2] != operand:
            return 0
        h = s[0][0]
        return (h if half[1] else 1 - h) * blocks

    def a_map(p, q, kk, *s):
        i = ij(p, q)[0] + shift(s, "a", m // tm)
        return (kk, i) if ta else (i, kk)

    def b_map(p, q, kk, *s):
        j = ij(p, q)[1] + shift(s, "b", n // tn)
        return (j, kk) if tb else (kk, j)

    def o_map(p, q, kk, *s):
        return ij(p, q)

    def body(*refs):
        if half is not None:
            refs = refs[1:]
        ab = refs[:2 * npairs]
        pos = 2 * npairs
        res_ref = None
        if res is not None:
            res_ref = refs[pos]
            pos += 1
        o_ref = refs[pos]
        acc_ref = refs[pos + 1] if nk > 1 else None

        def dots():
            tot = None
            for p in range(npairs):
                d = lax.dot_general(ab[2 * p][...], ab[2 * p + 1][...], dn, preferred_element_type=F32)
                tot = d if tot is None else tot + d
            return tot

        def finish(acc):
            r = acc * alpha if alpha != 1.0 else acc
            if res_ref is not None:
                r = res_ref[...].astype(F32) + r
            o_ref[...] = r.astype(o_ref.dtype)

        if nk == 1:
            finish(dots())
        else:
            kk = pl.program_id(2)

            @pl.when(kk == 0)
            def _():
                acc_ref[...] = dots()

            @pl.when(kk > 0)
            def _():
                acc_ref[...] += dots()

            @pl.when(kk == nk - 1)
            def _():
                finish(acc_ref[...])

    a_spec = pl.BlockSpec((tk, tm) if ta else (tm, tk), a_map)
    b_spec = pl.BlockSpec((tn, tk) if tb else (tk, tn), b_map)
    o_spec = pl.BlockSpec((tm, tn), o_map)
    in_specs = [a_spec, b_spec] * npairs
    args = [t for p in pairs for t in p]
    if res is not None:
        in_specs.append(o_spec)
        args.append(res)
    grid = ij(m // tm, n // tn) + (nk,)
    scratch = [pltpu.VMEM((tm, tn), F32)] if nk > 1 else []
    if half is None:
        kwargs = dict(grid=grid, in_specs=in_specs, out_specs=o_spec, scratch_shapes=scratch)
    else:
        args = [half[0]] + args
        kwargs = dict(grid_spec=pltpu.PrefetchScalarGridSpec(
            num_scalar_prefetch=1, grid=grid, in_specs=in_specs, out_specs=o_spec, scratch_shapes=scratch))
    return pl.pallas_call(
        body, name=name, out_shape=jax.ShapeDtypeStruct((m, n), out_dtype),
        compiler_params=_params(("parallel", "parallel", "arbitrary")), **kwargs,
    )(*args)


def _ffn_up(xn, w1, w3, *, tm, tn):
    t, d = xn.shape
    f = w1.shape[1]

    def body(x_ref, w1_ref, w3_ref, a_ref, b_ref, h_ref):
        x = x_ref[...]
        for c in range(tn // EPILOGUE_COLS):
            cols = slice(c * EPILOGUE_COLS, (c + 1) * EPILOGUE_COLS)
            a = jnp.dot(x, w1_ref[:, cols], preferred_element_type=F32)
            b = jnp.dot(x, w3_ref[:, cols], preferred_element_type=F32)
            a_ref[:, cols] = a.astype(BF16)
            b_ref[:, cols] = b.astype(BF16)
            h_ref[:, cols] = (a * _sigmoid(a) * b).astype(BF16)

    x_spec = pl.BlockSpec((tm, d), lambda i, j: (i, 0))
    w_spec = pl.BlockSpec((d, tn), lambda i, j: (0, j))
    o_spec = pl.BlockSpec((tm, tn), lambda i, j: (i, j))
    shp = jax.ShapeDtypeStruct((t, f), BF16)
    return pl.pallas_call(
        body, name="ffn_up", out_shape=(shp, shp, shp), grid=(t // tm, f // tn),
        in_specs=[x_spec, w_spec, w_spec], out_specs=(o_spec, o_spec, o_spec),
        compiler_params=_params(("parallel", "parallel")),
    )(xn, w1, w3)


def _ffn_dh(dyb, w2, a, b, *, tm, tn):
    t, d = dyb.shape
    f = w2.shape[0]
    dn = (((1,), (1,)), ((), ()))

    def body(dy_ref, w2_ref, a_ref, b_ref, da_ref, db_ref):
        dy = dy_ref[...]
        for c in range(tn // EPILOGUE_COLS):
            cols = slice(c * EPILOGUE_COLS, (c + 1) * EPILOGUE_COLS)
            dh = 0.5 * lax.dot_general(dy, w2_ref[cols, :], dn, preferred_element_type=F32)
            av = a_ref[:, cols].astype(F32)
            bv = b_ref[:, cols].astype(F32)
            sig = _sigmoid(av)
            da_ref[:, cols] = (dh * bv * (sig * (1.0 + av * (1.0 - sig)))).astype(BF16)
            db_ref[:, cols] = (dh * (av * sig)).astype(BF16)

    dy_spec = pl.BlockSpec((tm, d), lambda i, j: (i, 0))
    w_spec = pl.BlockSpec((tn, d), lambda i, j: (j, 0))
    o_spec = pl.BlockSpec((tm, tn), lambda i, j: (i, j))
    shp = jax.ShapeDtypeStruct((t, f), BF16)
    return pl.pallas_call(
        body, name="ffn_dh", out_shape=(shp, shp), grid=(t // tm, f // tn),
        in_specs=[dy_spec, w_spec, o_spec, o_spec], out_specs=(o_spec, o_spec),
        compiler_params=_params(("parallel", "parallel")),
    )(dyb, w2, a, b)


def _rmsnorm_fwd(x, g, *, tr=256):
    t, d = x.shape

    def body(x_ref, g_ref, o_ref):
        xv = x_ref[...]
        y = xv * lax.rsqrt(jnp.mean(xv * xv, axis=-1, keepdims=True) + EPS)
        o_ref[...] = (y * g_ref[...]).astype(BF16)

    return pl.pallas_call(
        body, name="rmsnorm_fwd", out_shape=jax.ShapeDtypeStruct((t, d), BF16), grid=(t // tr,),
        in_specs=[pl.BlockSpec((tr, d), lambda i: (i, 0)), pl.BlockSpec((1, d), lambda i: (0, 0))],
        out_specs=pl.BlockSpec((tr, d), lambda i: (i, 0)),
        compiler_params=_params(("parallel",)),
    )(x, g.reshape(1, d))


def _rmsnorm_bwd(dy, x, g, dres, *, tr=256):
    t, d = x.shape

    def body(dy_ref, x_ref, g_ref, dres_ref, dx_ref, dxb_ref, dg_ref):
        xv = x_ref[...]
        dyv = dy_ref[...].astype(F32)
        r = lax.rsqrt(jnp.mean(xv * xv, axis=-1, keepdims=True) + EPS)
        xhat = xv * r
        dxhat = dyv * g_ref[...]
        c = jnp.mean(dxhat * xhat, axis=-1, keepdims=True)
        dx = dres_ref[...] + r * (dxhat - xhat * c)
        dx_ref[...] = dx
        dxb_ref[...] = dx.astype(BF16)
        part = jnp.sum(dyv * xhat, axis=0, keepdims=True)

        @pl.when(pl.program_id(0) == 0)
        def _():
            dg_ref[...] = part

        @pl.when(pl.program_id(0) > 0)
        def _():
            dg_ref[...] += part

    row = pl.BlockSpec((tr, d), lambda i: (i, 0))
    vec = pl.BlockSpec((1, d), lambda i: (0, 0))
    dx, dxb, dg = pl.pallas_call(
        body, name="rmsnorm_bwd",
        out_shape=(jax.ShapeDtypeStruct((t, d), F32), jax.ShapeDtypeStruct((t, d), BF16),
                   jax.ShapeDtypeStruct((1, d), F32)),
        grid=(t // tr,), in_specs=[row, row, vec, row], out_specs=(row, row, vec),
        compiler_params=_params(("arbitrary",)),
    )(dy, x, g.reshape(1, d), dres)
    return dx, dxb, dg.reshape(d)


def _loss_head(y, target, *, tr=256):
    t, d = y.shape

    def body(y_ref, t_ref, dy_ref, dyb_ref, s_ref):
        err = y_ref[...] - t_ref[...]
        dy = err * (1.0 / d)
        dy_ref[...] = dy
        dyb_ref[...] = dy.astype(BF16)
        part = jnp.full((1, 128), jnp.sum(err * err), F32)

        @pl.when(pl.program_id(0) == 0)
        def _():
            s_ref[...] = part

        @pl.when(pl.program_id(0) > 0)
        def _():
            s_ref[...] += part

    row = pl.BlockSpec((tr, d), lambda i: (i, 0))
    return pl.pallas_call(
        body, name="loss_head",
        out_shape=(jax.ShapeDtypeStruct((t, d), F32), jax.ShapeDtypeStruct((t, d), BF16),
                   jax.ShapeDtypeStruct((1, 128), F32)),
        grid=(t // tr,), in_specs=[row, row],
        out_specs=(row, row, pl.BlockSpec((1, 128), lambda i: (0, 0))),
        compiler_params=_params(("arbitrary",)),
    )(y, target)


def _adamw(w, g, m, v, *, tr):
    rows, cols = w.shape

    def body(w_ref, g_ref, m_ref, v_ref, d_ref, nm_ref, nv_ref):
        gv = g_ref[...]
        nm = ADAM_B1 * m_ref[...] + (1.0 - ADAM_B1) * gv
        nv = ADAM_B2 * v_ref[...] + (1.0 - ADAM_B2) * jnp.square(gv)
        m_hat = nm / (1.0 - ADAM_B1 ** ADAM_STEP)
        v_hat = nv / (1.0 - ADAM_B2 ** ADAM_STEP)
        d_ref[...] = -ADAM_LR * (m_hat / (jnp.sqrt(v_hat) + ADAM_EPS) + ADAM_WD * w_ref[...])
        nm_ref[...] = nm
        nv_ref[...] = nv

    spec = pl.BlockSpec((tr, cols), lambda i: (i, 0))
    shp = jax.ShapeDtypeStruct((rows, cols), F32)
    return pl.pallas_call(
        body, name="adamw", out_shape=(shp, shp, shp), grid=(rows // tr,),
        in_specs=[spec] * 4, out_specs=(spec, spec, spec),
        compiler_params=_params(("parallel",)),
    )(w, g, m, v)


def _headnorm(xf, g):
    r = lax.rsqrt(jnp.mean(xf * xf, axis=-1, keepdims=True) + EPS)
    xhat = xf * r
    return xhat * g, xhat, r


def _headnorm_bwd(dn, xhat, r, g):
    dxhat = dn * g
    return r * (dxhat - xhat * jnp.mean(dxhat * xhat, axis=-1, keepdims=True))


_NT = (((1,), (1,)), ((), ()))
_TN = (((0,), (0,)), ((), ()))


def _attn_masks(n, nb):
    qi = lax.broadcasted_iota(jnp.int32, (ATTN_BLOCK, ATTN_BLOCK), 0)
    ci = lax.broadcasted_iota(jnp.int32, (ATTN_BLOCK, ATTN_BLOCK), 1)
    d_prev = qi + ATTN_BLOCK - ci
    d_cur = qi - ci
    return d_prev, d_cur, (ci >= qi), (ci <= qi)


def _attn_fwd(qv, kv, vv, qg, kg, *, dil):
    l, w = qv.shape
    nb = l // ATTN_BLOCK
    scale = HEAD_DIM ** -0.5

    def body(q_ref, kp_ref, kc_ref, vp_ref, vc_ref, qg_ref, kg_ref, o_ref, lse_ref):
        n = pl.program_id(1)
        d_prev, d_cur, ok_prev, ok_cur = _attn_masks(n, nb)
        ok_prev = ok_prev & (n > 0)
        b_prev = d_prev.astype(F32) * float(dil)
        b_cur = d_cur.astype(F32) * float(dil)
        for h in range(HEADS):
            sl = slice(h * HEAD_DIM, (h + 1) * HEAD_DIM)
            slope = 2.0 ** (-8.0 * (h + 1) / HEADS)
            q = _headnorm(q_ref[:, sl].astype(F32), qg_ref[...])[0].astype(BF16)
            kp = _headnorm(kp_ref[:, sl].astype(F32), kg_ref[...])[0].astype(BF16)
            kc = _headnorm(kc_ref[:, sl].astype(F32), kg_ref[...])[0].astype(BF16)
            s1 = lax.dot_general(q, kp, _NT, preferred_element_type=F32) * scale
            s2 = lax.dot_general(q, kc, _NT, preferred_element_type=F32) * scale
            s1 = jnp.where(ok_prev, s1 - slope * b_prev, NEG_BIG)
            s2 = jnp.where(ok_cur, s2 - slope * b_cur, NEG_BIG)
            m = jnp.maximum(jnp.max(s1, axis=-1, keepdims=True), jnp.max(s2, axis=-1, keepdims=True))
            p1 = jnp.exp(s1 - m)
            p2 = jnp.exp(s2 - m)
            den = jnp.sum(p1, axis=-1, keepdims=True) + jnp.sum(p2, axis=-1, keepdims=True)
            inv = 1.0 / den
            o = jnp.dot((p1 * inv).astype(BF16), vp_ref[:, sl], preferred_element_type=F32)
            o = o + jnp.dot((p2 * inv).astype(BF16), vc_ref[:, sl], preferred_element_type=F32)
            o_ref[:, sl] = o
            lse_ref[:, sl] = jnp.broadcast_to(m + jnp.log(den), (ATTN_BLOCK, HEAD_DIM))

    cur = pl.BlockSpec((ATTN_BLOCK, A_WIDTH), lambda r, n: (n, r))
    prev = pl.BlockSpec((ATTN_BLOCK, A_WIDTH), lambda r, n: (jnp.maximum(n - 1, 0), r))
    vec = pl.BlockSpec((1, HEAD_DIM), lambda r, n: (0, 0))
    shp = jax.ShapeDtypeStruct((l, w), F32)
    return pl.pallas_call(
        body, name="attn_fwd_d%d" % dil, out_shape=(shp, shp), grid=(dil, nb),
        in_specs=[cur, prev, cur, prev, cur, vec, vec], out_specs=(cur, cur),
        compiler_params=_params(("parallel", "parallel")),
    )(qv, kv, kv, vv, vv, qg.reshape(1, HEAD_DIM), kg.reshape(1, HEAD_DIM))


def _attn_combine(outs, lses, *, tr=256):
    t, w = outs[0].shape

    def body(o0, o1, o2, l0, l1, l2, y_ref, lse_ref):
        a0, a1, a2 = l0[...], l1[...], l2[...]
        m = jnp.maximum(jnp.maximum(a0, a1), a2)
        e0, e1, e2 = jnp.exp(a0 - m), jnp.exp(a1 - m), jnp.exp(a2 - m)
        s = e0 + e1 + e2
        inv = 1.0 / s
        y_ref[...] = ((e0 * inv) * o0[...] + (e1 * inv) * o1[...] + (e2 * inv) * o2[...]).astype(BF16)
        lse_ref[...] = m + jnp.log(s)

    row = pl.BlockSpec((tr, w), lambda i: (i, 0))
    return pl.pallas_call(
        body, name="attn_combine",
        out_shape=(jax.ShapeDtypeStruct((t, w), BF16), jax.ShapeDtypeStruct((t, w), F32)),
        grid=(t // tr,), in_specs=[row] * 6, out_specs=(row, row),
        compiler_params=_params(("parallel",)),
    )(*outs, *lses)


def _attn_delta(dy, y, *, tr=256):
    t, w = y.shape

    def body(dy_ref, y_ref, o_ref):
        for h in range(HEADS):
            sl = slice(h * HEAD_DIM, (h + 1) * HEAD_DIM)
            dlt = jnp.sum(dy_ref[:, sl] * y_ref[:, sl].astype(F32), axis=-1, keepdims=True)
            o_ref[:, sl] = jnp.broadcast_to(dlt, (tr, HEAD_DIM))

    row = pl.BlockSpec((tr, w), lambda i: (i, 0))
    return pl.pallas_call(
        body, name="attn_delta", out_shape=jax.ShapeDtypeStruct((t, w), F32), grid=(t // tr,),
        in_specs=[row, row], out_specs=row, compiler_params=_params(("parallel",)),
    )(dy, y)


def _attn_bwd(qv, kv, vv, dyv, lsev, dltv, qg, kg, *, dil):
    l, w = qv.shape
    nb = l // ATTN_BLOCK
    scale = HEAD_DIM ** -0.5

    def body(qc_ref, qn_ref, kp_ref, kc_ref, vp_ref, vc_ref, dyc_ref, dyn_ref, lc_ref, ln_ref,
             dc_ref, dn_ref, qg_ref, kg_ref, dq_ref, dk_ref, dv_ref, dqg_ref, dkg_ref):
        n = pl.program_id(1)
        first = (pl.program_id(0) == 0) & (n == 0)
        d_prev, d_cur, ok_prev, ok_cur = _attn_masks(n, nb)
        ok_t1 = ok_prev & (n > 0)
        ok_t3 = ok_prev & (n < nb - 1)
        b_prev = d_prev.astype(F32) * float(dil)
        b_cur = d_cur.astype(F32) * float(dil)
        qgv, kgv = qg_ref[...], kg_ref[...]
        dqg = jnp.zeros((1, HEAD_DIM), F32)
        dkg = jnp.zeros((1, HEAD_DIM), F32)
        for h in range(HEADS):
            sl = slice(h * HEAD_DIM, (h + 1) * HEAD_DIM)
            slope = 2.0 ** (-8.0 * (h + 1) / HEADS)
            qc, qc_hat, qc_r = _headnorm(qc_ref[:, sl].astype(F32), qgv)
            qn = _headnorm(qn_ref[:, sl].astype(F32), qgv)[0].astype(BF16)
            kp = _headnorm(kp_ref[:, sl].astype(F32), kgv)[0].astype(BF16)
            kc, kc_hat, kc_r = _headnorm(kc_ref[:, sl].astype(F32), kgv)
            qc = qc.astype(BF16)
            kc = kc.astype(BF16)
            vp, vc = vp_ref[:, sl], vc_ref[:, sl]
            dyc, dyn = dyc_ref[:, sl].astype(BF16), dyn_ref[:, sl].astype(BF16)

            def tile(q, k, v, dy, lse, dlt, ok, bias):
                s = lax.dot_general(q, k, _NT, preferred_element_type=F32) * scale
                p = jnp.where(ok, jnp.exp(jnp.where(ok, s - slope * bias, NEG_BIG) - lse), 0.0)
                dp = lax.dot_general(dy, v, _NT, preferred_element_type=F32)
                return p.astype(BF16), (p * (dp - dlt)).astype(BF16)

            p1, ds1 = tile(qc, kp, vp, dyc, lc_ref[:, sl], dc_ref[:, sl], ok_t1, b_prev)
            p2, ds2 = tile(qc, kc, vc, dyc, lc_ref[:, sl], dc_ref[:, sl], ok_cur, b_cur)
            p3, ds3 = tile(qn, kc, vc, dyn, ln_ref[:, sl], dn_ref[:, sl], ok_t3, b_prev)
            dqn = scale * (jnp.dot(ds1, kp, preferred_element_type=F32) + jnp.dot(ds2, kc, preferred_element_type=F32))
            dkn = scale * (lax.dot_general(ds2, qc, _TN, preferred_element_type=F32)
                           + lax.dot_general(ds3, qn, _TN, preferred_element_type=F32))
            dv = (lax.dot_general(p2, dyc, _TN, preferred_element_type=F32)
                  + lax.dot_general(p3, dyn, _TN, preferred_element_type=F32))
            dqg = dqg + jnp.sum(dqn * qc_hat, axis=0, keepdims=True)
            dkg = dkg + jnp.sum(dkn * kc_hat, axis=0, keepdims=True)
            dq_ref[:, sl] = _headnorm_bwd(dqn, qc_hat, qc_r, qgv).astype(BF16)
            dk_ref[:, sl] = _headnorm_bwd(dkn, kc_hat, kc_r, kgv).astype(BF16)
            dv_ref[:, sl] = dv.astype(BF16)

        @pl.when(first)
        def _():
            dqg_ref[...] = dqg
            dkg_ref[...] = dkg

        @pl.when(jnp.logical_not(first))
        def _():
            dqg_ref[...] += dqg
            dkg_ref[...] += dkg

    blk = (ATTN_BLOCK, A_WIDTH)
    cur = pl.BlockSpec(blk, lambda r, n: (n, r))
    prev = pl.BlockSpec(blk, lambda r, n: (jnp.maximum(n - 1, 0), r))
    nxt = pl.BlockSpec(blk, lambda r, n: (jnp.minimum(n + 1, nb - 1), r))
    vec = pl.BlockSpec((1, HEAD_DIM), lambda r, n: (0, 0))
    shp = jax.ShapeDtypeStruct((l, w), BF16)
    gshp = jax.ShapeDtypeStruct((1, HEAD_DIM), F32)
    return pl.pallas_call(
        body, name="attn_bwd_d%d" % dil, out_shape=(shp, shp, shp, gshp, gshp), grid=(dil, nb),
        in_specs=[cur, nxt, prev, cur, prev, cur, cur, nxt, cur, nxt, cur, nxt, vec, vec],
        out_specs=(cur, cur, cur, vec, vec),
        compiler_params=_params(("arbitrary", "arbitrary")),
    )(qv, qv, kv, kv, vv, vv, dyv, dyv, lsev, lsev, dltv, dltv,
      qg.reshape(1, HEAD_DIM), kg.reshape(1, HEAD_DIM))


def _prev_halo(tr, tc, col0):
    return pl.BlockSpec((HALO, tc), lambda j, i: (jnp.maximum(i * (tr // HALO) - 1, 0), col0 + j))


def _next_halo(tr, tc, col0, rows):
    last = rows // HALO - 1
    return pl.BlockSpec((HALO, tc), lambda j, i: (jnp.minimum((i + 1) * (tr // HALO), last), col0 + j))


def _cur_block(tr, tc, col0):
    return pl.BlockSpec((tr, tc), lambda j, i: (i, col0 + j))


def _gateconv_fwd(h, conv_w, *, col0, tr=512, tc=256):
    t = h.shape[0]
    width = conv_w.shape[1]
    nc = width // tc
    c0 = col0 // tc

    def body(bg_ref, cg_ref, xt_ref, cgh_ref, xth_ref, w_ref, y_ref, pad_ref):
        i = pl.program_id(1)
        halo = cgh_ref[...].astype(F32) * xth_ref[...].astype(F32)
        pad_ref[0:HALO, :] = jnp.where(i > 0, halo, 0.0)
        pad_ref[HALO:HALO + tr, :] = cg_ref[...].astype(F32) * xt_ref[...].astype(F32)
        conv = None
        for j in range(B_CONV):
            term = w_ref[j:j + 1, :] * pad_ref[HALO - (B_CONV - 1) + j:HALO - (B_CONV - 1) + j + tr, :]
            conv = term if conv is None else conv + term
        y_ref[...] = (bg_ref[...].astype(F32) * conv).astype(BF16)

    return pl.pallas_call(
        body, name="gateconv_fwd", out_shape=jax.ShapeDtypeStruct((t, width), BF16), grid=(nc, t // tr),
        in_specs=[_cur_block(tr, tc, c0), _cur_block(tr, tc, c0 + nc), _cur_block(tr, tc, c0 + 2 * nc),
                  _prev_halo(tr, tc, c0 + nc), _prev_halo(tr, tc, c0 + 2 * nc),
                  pl.BlockSpec((8, tc), lambda j, i: (0, j))],
        out_specs=_cur_block(tr, tc, 0),
        scratch_shapes=[pltpu.VMEM((HALO + tr, tc), F32)],
        compiler_params=_params(("parallel", "arbitrary")),
    )(h, h, h, h, h, _pad_rows(conv_w, 8))


def _pad_rows(w, rows):
    return jnp.pad(w, ((0, rows - w.shape[0]), (0, 0)))


def _gateconv_bwd(h, dy, conv_w, *, col0, dcol0, tr=512, tc=256):
    t = h.shape[0]
    width = conv_w.shape[1]
    nc = width // tc
    c0 = col0 // tc
    dc0 = dcol0 // tc
    nt = t // tr

    def body(bg_ref, cg_ref, xt_ref, cgh_ref, xth_ref, bgn_ref, dy_ref, dyn_ref, w_ref,
             dbg_ref, dcg_ref, dxt_ref, dw_ref, pad_ref, padd_ref):
        i = pl.program_id(1)
        cg = cg_ref[...].astype(F32)
        xt = xt_ref[...].astype(F32)
        bg = bg_ref[...].astype(F32)
        dyv = dy_ref[...]
        halo = cgh_ref[...].astype(F32) * xth_ref[...].astype(F32)
        pad_ref[0:HALO, :] = jnp.where(i > 0, halo, 0.0)
        pad_ref[HALO:HALO + tr, :] = cg * xt
        dconv = dyv * bg
        padd_ref[0:tr, :] = dconv
        padd_ref[tr:tr + HALO, :] = jnp.where(i < nt - 1, dyn_ref[...] * bgn_ref[...].astype(F32), 0.0)
        conv = None
        du = None
        dws = []
        for j in range(B_CONV):
            off = HALO - (B_CONV - 1) + j
            shifted = pad_ref[off:off + tr, :]
            term = w_ref[j:j + 1, :] * shifted
            conv = term if conv is None else conv + term
            dws.append(jnp.sum(dconv * shifted, axis=0, keepdims=True))
            back = w_ref[j:j + 1, :] * padd_ref[B_CONV - 1 - j:B_CONV - 1 - j + tr, :]
            du = back if du is None else du + back
        dbg_ref[...] = (dyv * conv).astype(BF16)
        dcg_ref[...] = (du * xt).astype(BF16)
        dxt_ref[...] = (du * cg).astype(BF16)
        dw = _stack_rows(dws, 8, tc)

        @pl.when(i == 0)
        def _():
            dw_ref[...] = dw

        @pl.when(i > 0)
        def _():
            dw_ref[...] += dw

    oshp = jax.ShapeDtypeStruct((t, width), BF16)
    return pl.pallas_call(
        body, name="gateconv_bwd",
        out_shape=(oshp, oshp, oshp, jax.ShapeDtypeStruct((8, width), F32)), grid=(nc, nt),
        in_specs=[_cur_block(tr, tc, c0), _cur_block(tr, tc, c0 + nc), _cur_block(tr, tc, c0 + 2 * nc),
                  _prev_halo(tr, tc, c0 + nc), _prev_halo(tr, tc, c0 + 2 * nc),
                  _next_halo(tr, tc, c0, t), _cur_block(tr, tc, dc0), _next_halo(tr, tc, dc0, t),
                  pl.BlockSpec((8, tc), lambda j, i: (0, j))],
        out_specs=(_cur_block(tr, tc, 0), _cur_block(tr, tc, 0), _cur_block(tr, tc, 0),
                   pl.BlockSpec((8, tc), lambda j, i: (0, j))),
        scratch_shapes=[pltpu.VMEM((HALO + tr, tc), F32), pltpu.VMEM((tr + HALO, tc), F32)],
        compiler_params=_params(("parallel", "arbitrary")),
    )(h, h, h, h, h, h, dy, dy, _pad_rows(conv_w, 8))


def _stack_rows(rows, n, width):
    idx = lax.broadcasted_iota(jnp.int32, (n, width), 0)
    out = jnp.zeros((n, width), F32)
    for j, r in enumerate(rows):
        out = jnp.where(idx == j, r, out)
    return out


CONV_ROWS = 64


def _glu_conv_fwd(hod, conv_w, conv_b, *, tr=512, tc=256):
    t = hod.shape[0]
    width = conv_w.shape[1]
    nc = width // tc

    def body(val_ref, gate_ref, valh_ref, gateh_ref, w_ref, b_ref, u1_ref, pad_ref):
        i = pl.program_id(1)
        halo = valh_ref[...].astype(F32) * _sigmoid(gateh_ref[...].astype(F32))
        pad_ref[0:HALO, :] = jnp.where(i > 0, halo, 0.0)
        pad_ref[HALO:HALO + tr, :] = val_ref[...].astype(F32) * _sigmoid(gate_ref[...].astype(F32))
        for c in range(tr // CONV_ROWS):
            base = HALO + c * CONV_ROWS - (C_CONV - 1)
            acc = None
            for j in range(C_CONV):
                term = w_ref[j:j + 1, :] * pad_ref[base + j:base + j + CONV_ROWS, :]
                acc = term if acc is None else acc + term
            u1_ref[c * CONV_ROWS:(c + 1) * CONV_ROWS, :] = acc + b_ref[...]

    return pl.pallas_call(
        body, name="glu_conv_fwd", out_shape=jax.ShapeDtypeStruct((t, width), F32), grid=(nc, t // tr),
        in_specs=[_cur_block(tr, tc, 0), _cur_block(tr, tc, nc), _prev_halo(tr, tc, 0), _prev_halo(tr, tc, nc),
                  pl.BlockSpec((32, tc), lambda j, i: (0, j)), pl.BlockSpec((1, tc), lambda j, i: (0, j))],
        out_specs=_cur_block(tr, tc, 0),
        scratch_shapes=[pltpu.VMEM((HALO + tr, tc), F32)],
        compiler_params=_params(("parallel", "arbitrary")),
    )(hod, hod, hod, hod, _pad_rows(conv_w, 32), conv_b.reshape(1, width))


def _ln_silu_fwd(u1, g, b, *, tr=256):
    t, width = u1.shape

    def body(u_ref, g_ref, b_ref, o_ref):
        uv = u_ref[...]
        mu = jnp.mean(uv, axis=-1, keepdims=True)
        var = jnp.mean(jnp.square(uv - mu), axis=-1, keepdims=True)
        u2 = ((uv - mu) * lax.rsqrt(var + EPS)) * g_ref[...] + b_ref[...]
        o_ref[...] = (u2 * _sigmoid(u2)).astype(BF16)

    row = pl.BlockSpec((tr, width), lambda i: (i, 0))
    vec = pl.BlockSpec((1, width), lambda i: (0, 0))
    return pl.pallas_call(
        body, name="ln_silu_fwd", out_shape=jax.ShapeDtypeStruct((t, width), BF16), grid=(t // tr,),
        in_specs=[row, vec, vec], out_specs=row, compiler_params=_params(("parallel",)),
    )(u1, g.reshape(1, width), b.reshape(1, width))


def _ln_silu_bwd(du, u1, g, b, *, col0, tr=256):
    t, width = u1.shape

    def body(du_ref, u_ref, g_ref, b_ref, du1_ref, dg_ref, db_ref, dcb_ref):
        uv = u_ref[...]
        mu = jnp.mean(uv, axis=-1, keepdims=True)
        var = jnp.mean(jnp.square(uv - mu), axis=-1, keepdims=True)
        rstd = lax.rsqrt(var + EPS)
        xh = (uv - mu) * rstd
        u2 = xh * g_ref[...] + b_ref[...]
        sig = _sigmoid(u2)
        du2 = du_ref[...] * (sig * (1.0 + u2 * (1.0 - sig)))
        dxh = du2 * g_ref[...]
        du1 = rstd * (dxh - jnp.mean(dxh, axis=-1, keepdims=True)
                      - xh * jnp.mean(dxh * xh, axis=-1, keepdims=True))
        du1_ref[...] = du1
        parts = (jnp.sum(du2 * xh, axis=0, keepdims=True), jnp.sum(du2, axis=0, keepdims=True),
                 jnp.sum(du1, axis=0, keepdims=True))

        @pl.when(pl.program_id(0) == 0)
        def _():
            dg_ref[...], db_ref[...], dcb_ref[...] = parts

        @pl.when(pl.program_id(0) > 0)
        def _():
            dg_ref[...] += parts[0]
            db_ref[...] += parts[1]
            dcb_ref[...] += parts[2]

    row = pl.BlockSpec((tr, width), lambda i: (i, 0))
    vec = pl.BlockSpec((1, width), lambda i: (0, 0))
    vshp = jax.ShapeDtypeStruct((1, width), F32)
    return pl.pallas_call(
        body, name="ln_silu_bwd", out_shape=(jax.ShapeDtypeStruct((t, width), F32), vshp, vshp, vshp),
        grid=(t // tr,),
        in_specs=[pl.BlockSpec((tr, width), lambda i: (i, col0 // width)), row, vec, vec],
        out_specs=(row, vec, vec, vec), compiler_params=_params(("arbitrary",)),
    )(du, u1, g.reshape(1, width), b.reshape(1, width))


def _glu_conv_bwd(hod, du1, conv_w, *, tr=512, tc=256):
    t = hod.shape[0]
    width = conv_w.shape[1]
    nc = width // tc
    nt = t // tr

    def body(val_ref, gate_ref, valh_ref, gateh_ref, du_ref, dun_ref, w_ref,
             dval_ref, dgate_ref, dw_ref, pad_ref, padd_ref, du0_ref):
        i = pl.program_id(1)
        val = val_ref[...].astype(F32)
        sig = _sigmoid(gate_ref[...].astype(F32))
        halo = valh_ref[...].astype(F32) * _sigmoid(gateh_ref[...].astype(F32))
        pad_ref[0:HALO, :] = jnp.where(i > 0, halo, 0.0)
        pad_ref[HALO:HALO + tr, :] = val * sig
        padd_ref[0:tr, :] = du_ref[...]
        padd_ref[tr:tr + HALO, :] = jnp.where(i < nt - 1, dun_ref[...], 0.0)
        dws = [jnp.zeros((1, tc), F32)] * C_CONV
        for c in range(tr // CONV_ROWS):
            r0 = c * CONV_ROWS
            duc = padd_ref[r0:r0 + CONV_ROWS, :]
            acc = None
            for j in range(C_CONV):
                back = w_ref[j:j + 1, :] * padd_ref[r0 + C_CONV - 1 - j:r0 + C_CONV - 1 - j + CONV_ROWS, :]
                acc = back if acc is None else acc + back
                off = HALO + r0 - (C_CONV - 1) + j
                dws[j] = dws[j] + jnp.sum(duc * pad_ref[off:off + CONV_ROWS, :], axis=0, keepdims=True)
            du0_ref[r0:r0 + CONV_ROWS, :] = acc
        du0 = du0_ref[...]
        dval_ref[...] = (du0 * sig).astype(BF16)
        dgate_ref[...] = (du0 * val * sig * (1.0 - sig)).astype(BF16)
        dw = _stack_rows(dws, 32, tc)

        @pl.when(i == 0)
        def _():
            dw_ref[...] = dw

        @pl.when(i > 0)
        def _():
            dw_ref[...] += dw

    oshp = jax.ShapeDtypeStruct((t, width), BF16)
    wspec = pl.BlockSpec((32, tc), lambda j, i: (0, j))
    return pl.pallas_call(
        body, name="glu_conv_bwd", out_shape=(oshp, oshp, jax.ShapeDtypeStruct((32, width), F32)), grid=(nc, nt),
        in_specs=[_cur_block(tr, tc, 0), _cur_block(tr, tc, nc), _prev_halo(tr, tc, 0), _prev_halo(tr, tc, nc),
                  _cur_block(tr, tc, 0), _next_halo(tr, tc, 0, t), wspec],
        out_specs=(_cur_block(tr, tc, 0), _cur_block(tr, tc, 0), wspec),
        scratch_shapes=[pltpu.VMEM((HALO + tr, tc), F32), pltpu.VMEM((tr + HALO, tc), F32),
                        pltpu.VMEM((tr, tc), F32)],
        compiler_params=_params(("parallel", "arbitrary")),
    )(hod, hod, hod, hod, du1, du1, _pad_rows(conv_w, 32))


def _pooled(pad_ref, g, kw, tr, i):
    gw = pad_ref.shape[1] // len(D_WINDOWS)
    cols = slice(g * gw, (g + 1) * gw)
    tot = None
    for j in range(kw):
        sh = pad_ref[HALO - j:HALO - j + tr, cols]
        tot = sh if tot is None else tot + sh
    return tot / _window_count(tr, gw, kw, i * tr) - pad_ref[HALO:HALO + tr, cols]


def _window_count(rows, width, kw, row0):
    t1 = (lax.broadcasted_iota(jnp.int32, (rows, width), 0) + (row0 + 1)).astype(F32)
    return jnp.minimum(t1, float(kw))


def _pool_fwd(hod, pool_w, pool_scale, *, tr=256):
    t = hod.shape[0]
    width = pool_scale.shape[0]
    ng = len(D_WINDOWS)
    gw = width // ng

    def body(z_ref, zh_ref, w_ref, s_ref, y_ref, pad_ref):
        i = pl.program_id(1)
        pad_ref[0:HALO, :] = jnp.where(i > 0, zh_ref[...].astype(F32), 0.0)
        pad_ref[HALO:HALO + tr, :] = z_ref[...].astype(F32)
        for g, kw in enumerate(D_WINDOWS):
            cols = slice(g * gw, (g + 1) * gw)
            pre = jnp.dot(_pooled(pad_ref, g, kw, tr, i).astype(BF16), w_ref[g], preferred_element_type=F32)
            y_ref[:, cols] = (pre * s_ref[:, cols]).astype(BF16)

    return pl.pallas_call(
        body, name="pool_fwd", out_shape=jax.ShapeDtypeStruct((t, width), BF16), grid=(1, t // tr),
        in_specs=[_cur_block(tr, width, 2), _prev_halo(tr, width, 2),
                  pl.BlockSpec((ng, gw, gw), lambda j, i: (0, 0, 0)), pl.BlockSpec((1, width), lambda j, i: (0, 0))],
        out_specs=_cur_block(tr, width, 0),
        scratch_shapes=[pltpu.VMEM((HALO + tr, width), F32)],
        compiler_params=_params(("parallel", "arbitrary")),
    )(hod, hod, pool_w, pool_scale.reshape(1, width))


def _pool_bwd(hod, dy, pool_w, pool_scale, *, dcol0, tr=256):
    t = hod.shape[0]
    width = pool_scale.shape[0]
    ng = len(D_WINDOWS)
    gw = width // ng
    nt = t // tr

    def body(z_ref, zh_ref, dy_ref, dyn_ref, w_ref, s_ref, dz_ref, dw_ref, ds_ref, pad_ref, pade_ref):
        i = pl.program_id(1)
        pad_ref[0:HALO, :] = jnp.where(i > 0, zh_ref[...].astype(F32), 0.0)
        pad_ref[HALO:HALO + tr, :] = z_ref[...].astype(F32)
        dws = []
        dss = []
        for g, kw in enumerate(D_WINDOWS):
            cols = slice(g * gw, (g + 1) * gw)
            wg = w_ref[g]
            dyc = dy_ref[:, cols]
            dpre = (dyc * s_ref[:, cols]).astype(BF16)
            dpre_n = (dyn_ref[:, cols] * s_ref[:, cols]).astype(BF16)
            dpl = lax.dot_general(dpre, wg, _NT, preferred_element_type=F32)
            dpl_n = lax.dot_general(dpre_n, wg, _NT, preferred_element_type=F32)
            pade_ref[0:tr, cols] = dpl / _window_count(tr, gw, kw, i * tr)
            pade_ref[tr:tr + HALO, cols] = jnp.where(i < nt - 1, dpl_n / _window_count(HALO, gw, kw, (i + 1) * tr), 0.0)
            tot = None
            for j in range(kw):
                sh = pade_ref[j:j + tr, cols]
                tot = sh if tot is None else tot + sh
            dz_ref[:, cols] = (tot - dpl).astype(BF16)
            pooled = _pooled(pad_ref, g, kw, tr, i).astype(BF16)
            pre = jnp.dot(pooled, wg, preferred_element_type=F32)
            dss.append(jnp.sum(dyc * pre, axis=0, keepdims=True))
            dws.append(lax.dot_general(pooled, dpre, _TN, preferred_element_type=F32))

        @pl.when(i == 0)
        def _():
            for g in range(ng):
                dw_ref[g] = dws[g]
                ds_ref[:, g * gw:(g + 1) * gw] = dss[g]

        @pl.when(i > 0)
        def _():
            for g in range(ng):
                dw_ref[g] += dws[g]
                ds_ref[:, g * gw:(g + 1) * gw] += dss[g]

    dc = dcol0 // width
    wspec = pl.BlockSpec((ng, gw, gw), lambda j, i: (0, 0, 0))
    vspec = pl.BlockSpec((1, width), lambda j, i: (0, 0))
    return pl.pallas_call(
        body, name="pool_bwd",
        out_shape=(jax.ShapeDtypeStruct((t, width), BF16), jax.ShapeDtypeStruct((ng, gw, gw), F32),
                   jax.ShapeDtypeStruct((1, width), F32)),
        grid=(1, nt),
        in_specs=[_cur_block(tr, width, 2), _prev_halo(tr, width, 2), _cur_block(tr, width, dc),
                  _next_halo(tr, width, dc, t), wspec, vspec],
        out_specs=(_cur_block(tr, width, 0), wspec, vspec),
        scratch_shapes=[pltpu.VMEM((HALO + tr, width), F32), pltpu.VMEM((tr + HALO, width), F32)],
        compiler_params=_params(("arbitrary", "arbitrary")),
    )(hod, hod, dy, dy, pool_w, pool_scale.reshape(1, width))


TM = 1024
TN = 512
TK_ACC = 512


def _dw_full(a, b, *, m, n, alpha, axis, name):
    t = a.shape[0]
    tm = TM if m % TM == 0 else TN
    return _matmul([(a, b)], ta=True, m=m, n=n, k=t, tm=tm, tn=TN, tk=t, alpha=alpha, out_dtype=BF16, name=name)


def _ffn_fwd(x, g, w1, w3, w2):
    t, d = x.shape
    f = w1.shape[1]
    xn = _rmsnorm_fwd(x, g)
    a, b, h = _ffn_up(xn, w1, w3, tm=TM, tn=TN)
    y = _matmul([(h, w2)], m=t, n=d, k=f, tm=TM, tn=TN, tk=f, res=x, alpha=0.5, name="ffn_down")
    return y, (x, xn, a, b, h)


def _ffn_bwd(dx, dxb, saved, g, w1, w3, w2, dw):
    x, xn, a, b, h = saved
    t, d = x.shape
    f = w1.shape[1]
    da, db = _ffn_dh(dxb, w2, a, b, tm=TM, tn=TN)
    dw2 = dw(h, dxb, m=f, n=d, alpha=0.5, axis=0, name="ffn_dw2")
    dw1 = dw(xn, da, m=d, n=f, alpha=1.0, axis=1, name="ffn_dw1")
    dw3 = dw(xn, db, m=d, n=f, alpha=1.0, axis=1, name="ffn_dw3")
    dxn = _matmul([(da, w1), (db, w3)], tb=True, m=t, n=d, k=f, tm=TM, tn=d, tk=TK_ACC, name="ffn_dxn")
    dx, dxb, dg = _rmsnorm_bwd(dxn, x, g, dx)
    return dx, dxb, dg, dw1, dw3, dw2


def _mix_out_fwd(x, ycat, w_out):
    t, d = x.shape
    return _matmul([(ycat, w_out)], m=t, n=d, k=d, tm=TM, tn=TN, tk=d, res=x, name="mix_out")


def _mix_out_bwd(dxb, ycat, w_out, dw):
    t, d = dxb.shape
    dycat = _matmul([(dxb, w_out)], tb=True, m=t, n=d, k=d, tm=TM, tn=TN, tk=d, name="mix_dy")
    dw_out = dw(ycat, dxb, m=d, n=d, alpha=1.0, axis=0, name="mix_dw_out")
    return dycat, dw_out


def _mix_in_bwd(dh, xn, w_in, x, g, dx, dw):
    t, d = x.shape
    n_in = w_in.shape[1]
    dxn = _matmul([(dh, w_in)], tb=True, m=t, n=d, k=n_in, tm=TM, tn=d, tk=TK_ACC, name="mix_dxn")
    dw_in = dw(xn, dh, m=d, n=n_in, alpha=1.0, axis=1, name="mix_dw_in")
    dx, dxb, dg = _rmsnorm_bwd(dxn, x, g, dx)
    return dx, dxb, dg, dw_in


def _group_view(a, col0, dil):
    t = a.shape[0]
    return a[:, col0:col0 + A_WIDTH].reshape(t // dil, dil * A_WIDTH)


def _even_fwd(x, g, w_in, qg, kg, conv_w, w_out):
    t, d = x.shape
    n_in = w_in.shape[1]
    nq = len(A_DILATIONS) * A_WIDTH
    xn = _rmsnorm_fwd(x, g)
    h = _matmul([(xn, w_in)], m=t, n=n_in, k=d, tm=TM, tn=TN, tk=d, out_dtype=BF16, name="ev_in")
    outs, lses = [], []
    for gi, dil in enumerate(A_DILATIONS):
        o, l = _attn_fwd(_group_view(h, gi * A_WIDTH, dil), _group_view(h, nq + gi * A_WIDTH, dil),
                         _group_view(h, 2 * nq + gi * A_WIDTH, dil), qg, kg, dil=dil)
        outs.append(o.reshape(t, A_WIDTH))
        lses.append(l.reshape(t, A_WIDTH))
    ya, lse = _attn_combine(outs, lses)
    yb = _gateconv_fwd(h, conv_w, col0=3 * nq)
    ycat = jnp.concatenate([ya, yb], axis=1)
    return _mix_out_fwd(x, ycat, w_out), (x, xn, h, ya, lse, ycat)


def _even_bwd(dx, dxb, saved, g, w_in, qg, kg, conv_w, w_out, dw):
    x, xn, h, ya, lse, ycat = saved
    t, d = x.shape
    nq = len(A_DILATIONS) * A_WIDTH
    dycat, dw_out = _mix_out_bwd(dxb, ycat, w_out, dw)
    dlt = _attn_delta(dycat, ya)
    dqs, dks, dvs = [], [], []
    dqg = jnp.zeros((HEAD_DIM,), F32)
    dkg = jnp.zeros((HEAD_DIM,), F32)
    for gi, dil in enumerate(A_DILATIONS):
        dq, dk, dv, dqg_i, dkg_i = _attn_bwd(
            _group_view(h, gi * A_WIDTH, dil), _group_view(h, nq + gi * A_WIDTH, dil),
            _group_view(h, 2 * nq + gi * A_WIDTH, dil), _group_view(dycat, 0, dil),
            _group_view(lse, 0, dil), _group_view(dlt, 0, dil), qg, kg, dil=dil)
        dqs.append(dq.reshape(t, A_WIDTH))
        dks.append(dk.reshape(t, A_WIDTH))
        dvs.append(dv.reshape(t, A_WIDTH))
        dqg = dqg + dqg_i.reshape(HEAD_DIM)
        dkg = dkg + dkg_i.reshape(HEAD_DIM)
    dbg, dcg, dxt, dcw = _gateconv_bwd(h, dycat, conv_w, col0=3 * nq, dcol0=A_WIDTH)
    dh = jnp.concatenate(dqs + dks + dvs + [dbg, dcg, dxt], axis=1)
    dx, dxb, dg, dw_in = _mix_in_bwd(dh, xn, w_in, x, g, dx, dw)
    return dx, dxb, dg, dw_in, dqg, dkg, dcw[:B_CONV], dw_out


def _odd_fwd(x, g, w_in, conv_w, conv_b, ln_g, ln_b, pool_w, pool_scale, w_out):
    t, d = x.shape
    n_in = w_in.shape[1]
    xn = _rmsnorm_fwd(x, g)
    hod = _matmul([(xn, w_in)], m=t, n=n_in, k=d, tm=TM, tn=TN, tk=d, out_dtype=BF16, name="od_in")
    u1 = _glu_conv_fwd(hod, conv_w, conv_b)
    u = _ln_silu_fwd(u1, ln_g, ln_b)
    yd = _pool_fwd(hod, pool_w.astype(BF16), pool_scale)
    ycat = jnp.concatenate([u, yd], axis=1)
    return _mix_out_fwd(x, ycat, w_out), (x, xn, hod, u1, ycat)


def _odd_bwd(dx, dxb, saved, g, w_in, conv_w, conv_b, ln_g, ln_b, pool_w, pool_scale, w_out, dw):
    x, xn, hod, u1, ycat = saved
    width = conv_w.shape[1]
    dycat, dw_out = _mix_out_bwd(dxb, ycat, w_out, dw)
    du1, dlg, dlb, dcb = _ln_silu_bwd(dycat, u1, ln_g, ln_b, col0=0)
    dval, dgate, dcw = _glu_conv_bwd(hod, du1, conv_w)
    dz, dpw, dps = _pool_bwd(hod, dycat, pool_w.astype(BF16), pool_scale, dcol0=width)
    dh = jnp.concatenate([dval, dgate, dz], axis=1)
    dx, dxb, dg, dw_in = _mix_in_bwd(dh, xn, w_in, x, g, dx, dw)
    return (dx, dxb, dg, dw_in, dcw[:C_CONV], dcb.reshape(width), dlg.reshape(width), dlb.reshape(width),
            dpw, dps.reshape(width), dw_out)


def _sublayer_matrices(s):
    layer, slot = divmod(s, 3)
    if slot == 1:
        kind = "ev" if layer % 2 == 0 else "od"
        return [(kind + "_w_in", layer // 2), (kind + "_w_out", layer // 2)]
    j = 2 * layer + slot // 2
    return [("ffn_w1", j), ("ffn_w3", j), ("ffn_w2", j)]


def _local_step(x, target, wts, fetch=None, emit=None, dw=_dw_full):
    depth = wts["norm_g"].shape[0]
    if fetch is None:
        fetch = lambda s, after: ([wts[name][idx] for name, idx in _sublayer_matrices(s)], 0.0)
    gr = {}
    if emit is None:
        def emit(s, mats):
            for (name, idx), g in zip(_sublayer_matrices(s), mats):
                gr.setdefault(name, {})[idx] = g
            return 0.0

    def gain(layer, slot, tok):
        return wts["norm_g"][layer, slot] + tok

    saved = []
    for layer in range(depth):
        i = layer // 2
        s = 3 * layer
        m0, tok = fetch(s, x)
        x, s0 = _ffn_fwd(x, gain(layer, 0, tok), *m0)
        m1, tok = fetch(s + 1, x)
        if layer % 2 == 0:
            x, s1 = _even_fwd(x, gain(layer, 1, tok), m1[0], wts["ev_q_gain"][i],
                              wts["ev_k_gain"][i], wts["ev_conv_w"][i], m1[1])
        else:
            x, s1 = _odd_fwd(x, gain(layer, 1, tok), m1[0], wts["od_conv_w"][i],
                             wts["od_conv_b"][i], wts["od_ln_g"][i], wts["od_ln_b"][i], wts["od_pool_w"][i],
                             wts["od_pool_scale"][i], m1[1])
        m2, tok = fetch(s + 2, x)
        x, s2 = _ffn_fwd(x, gain(layer, 2, tok), *m2)
        saved.append(((s0, m0), (s1, m1), (s2, m2)))
    dx, dxb, sq = _loss_head(x, target)

    n_even, n_odd = (depth + 1) // 2, depth // 2
    for k in ("ev_q_gain", "ev_k_gain", "ev_conv_w"):
        gr[k] = [None] * n_even
    for k in ("od_conv_w", "od_conv_b", "od_ln_g", "od_ln_b", "od_pool_w", "od_pool_scale"):
        gr[k] = [None] * n_odd
    dnorm = [[None] * 3 for _ in range(depth)]
    tok = 0.0
    for layer in reversed(range(depth)):
        i = layer // 2
        s = 3 * layer
        (s0, m0), (s1, m1), (s2, m2) = saved[layer]
        dx, dxb, dnorm[layer][2], dw1, dw3, dw2 = _ffn_bwd(dx, dxb, s2, gain(layer, 2, tok), *m2, dw)
        tok = emit(s + 2, [dw1, dw3, dw2])
        if layer % 2 == 0:
            (dx, dxb, dnorm[layer][1], dw_in, gr["ev_q_gain"][i], gr["ev_k_gain"][i],
             gr["ev_conv_w"][i], dw_out) = _even_bwd(
                dx, dxb, s1, gain(layer, 1, tok), m1[0], wts["ev_q_gain"][i],
                wts["ev_k_gain"][i], wts["ev_conv_w"][i], m1[1], dw)
        else:
            (dx, dxb, dnorm[layer][1], dw_in, gr["od_conv_w"][i], gr["od_conv_b"][i], gr["od_ln_g"][i],
             gr["od_ln_b"][i], gr["od_pool_w"][i], gr["od_pool_scale"][i], dw_out) = _odd_bwd(
                dx, dxb, s1, gain(layer, 1, tok), m1[0], wts["od_conv_w"][i],
                wts["od_conv_b"][i], wts["od_ln_g"][i], wts["od_ln_b"][i], wts["od_pool_w"][i],
                wts["od_pool_scale"][i], m1[1], dw)
        tok = emit(s + 1, [dw_in, dw_out])
        dx, dxb, dnorm[layer][0], dw1, dw3, dw2 = _ffn_bwd(dx, dxb, s0, gain(layer, 0, tok), *m0, dw)
        tok = emit(s, [dw1, dw3, dw2])
    gr["norm_g"] = jnp.stack([jnp.stack(r) for r in dnorm])
    for name in list(gr):
        if isinstance(gr[name], dict):
            gr[name] = [gr[name][idx] for idx in sorted(gr[name])]
    return sq, lax.optimization_barrier((dx, jnp.asarray(tok, F32)))[0], gr


HBM_SPEC = pl.BlockSpec(memory_space=pltpu.HBM)
SEM_SPEC = pl.BlockSpec(memory_space=pltpu.SEMAPHORE)
ANY_SPEC = pl.BlockSpec(memory_space=pl.ANY)
EFFECT = pltpu.SideEffectType.DATAFLOW_SIDE_EFFECTING


def _place():
    x, y, c = lax.axis_index("x"), lax.axis_index("y"), lax.axis_index("c")
    chips = [(1 - x, y), (x, 1 - y), (1 - x, 1 - y)]
    return x, y, c, chips


def _chip_index(x, y):
    return 2 * x + y


def _ds(start, size, align):
    if isinstance(start, int):
        return pl.ds(start, size)
    return pl.ds(pl.multiple_of(start, align), size)


def _half(ref, axis, h):
    r, c = ref.shape[-2:]
    if axis == 1:
        return ref.at[_ds(h * (r // 2), r // 2, 16), :]
    return ref.at[:, _ds(h * (c // 2), c // 2, 128)]


def _chunk(ref, axis, j, n=N_CHIPS):
    r, c = ref.shape[-2:]
    if axis == 1:
        return ref.at[:, _ds(j * (c // n), c // n, 128)]
    return ref.at[_ds(j * (r // n), r // n, 16), :]


def _remote(src, dst, send_sem, recv_sem, device):
    return pltpu.make_async_remote_copy(src_ref=src, dst_ref=dst, send_sem=send_sem, recv_sem=recv_sem,
                                        device_id=device, device_id_type=MESH)


def _hbm(a):
    return pltpu.with_memory_space_constraint(a, pltpu.HBM)


def _cast_into(stacked, idx, chip, axis):
    _, r, c = stacked.shape
    full = (r, N_CHIPS * c) if axis == 1 else (N_CHIPS * r, c)
    tr = 128
    while tr > 16 and tr * c * 4 > (1 << 20):
        tr //= 2
    if axis == 1:
        o_map = lambda i, s: (i, s[0])
    else:
        o_map = lambda i, s: (s[0] * (r // tr) + i, 0)

    def body(s_ref, w_ref, o_ref):
        o_ref[...] = w_ref[...].astype(BF16)

    return pl.pallas_call(
        body, name="cast_into", out_shape=jax.ShapeDtypeStruct(full, BF16),
        grid_spec=pltpu.PrefetchScalarGridSpec(
            num_scalar_prefetch=1, grid=(r // tr,),
            in_specs=[pl.BlockSpec((None, tr, c), lambda i, s: (idx, i, 0))],
            out_specs=pl.BlockSpec((tr, c), o_map)),
        compiler_params=_params(("parallel",)),
    )(chip, stacked)


def _own_piece(ref, axis, me, cc):
    return _half(_chunk(ref, axis, me), axis, cc)


def _ag_start(fulls, axes):
    n = len(fulls)

    def body(*refs):
        ins = refs[:n]
        send, recv = refs[n:4 * n], refs[4 * n:7 * n]
        token = refs[8 * n]
        x, y, cc, chips = _place()
        me = _chip_index(x, y)
        for i in range(n):
            piece = _own_piece(ins[i], axes[i], me, cc)
            for k, chip in enumerate(chips):
                _remote(piece, piece, send[3 * i + k], recv[3 * i + k], (*chip, cc)).start()
        token[...] = jnp.zeros_like(token)

    sem = pltpu.SemaphoreType.DMA(())
    outs = pl.pallas_call(
        body, name="ag_start_%d" % n,
        out_shape=tuple([sem] * (6 * n) + [pltpu.HBM(f.shape, f.dtype) for f in fulls]
                        + [jax.ShapeDtypeStruct((8, 128), F32)]),
        in_specs=[HBM_SPEC] * n,
        out_specs=tuple([SEM_SPEC] * (6 * n) + [HBM_SPEC] * n + [pl.BlockSpec(memory_space=pltpu.VMEM)]),
        input_output_aliases={i: 6 * n + i for i in range(n)},
        compiler_params=pltpu.CompilerParams(has_side_effects=EFFECT),
    )(*[_hbm(f) for f in fulls])
    return outs[:3 * n], outs[3 * n:6 * n], outs[6 * n:7 * n], outs[7 * n][0, 0]


def _ag_wait(send, recv, fulls, axes, after):
    n = len(fulls)

    def body(*refs):
        ins = refs[:n]
        send_s, recv_s = refs[n:4 * n], refs[4 * n:7 * n]
        x, y, cc, chips = _place()
        me = _chip_index(x, y)
        for i in range(n):
            mine = _own_piece(ins[i], axes[i], me, cc)
            for k, chip in enumerate(chips):
                got = _own_piece(ins[i], axes[i], _chip_index(*chip), cc)
                cp = _remote(mine, got, send_s[3 * i + k], recv_s[3 * i + k], (*chip, cc))
                cp.wait_send()
                cp.wait_recv()

    return pl.pallas_call(
        body, name="ag_wait_%d" % n,
        out_shape=tuple(pltpu.HBM(f.shape, f.dtype) for f in fulls),
        in_specs=[HBM_SPEC] * n + [SEM_SPEC] * (6 * n) + [ANY_SPEC],
        out_specs=tuple([HBM_SPEC] * n),
        input_output_aliases={i: i for i in range(n)},
        compiler_params=pltpu.CompilerParams(has_side_effects=EFFECT),
    )(*fulls, *send, *recv, after)


def _ag_forward(fulls, axes):
    n = len(fulls)

    def body(*refs):
        ins = refs[:n]
        send_sems, recv_sems = refs[2 * n], refs[2 * n + 1]
        x, y, cc, chips = _place()
        cps = []
        for i in range(n):
            for k, chip in enumerate(chips):
                got = _own_piece(ins[i], axes[i], _chip_index(*chip), cc)
                cp = _remote(got, got, send_sems.at[3 * i + k], recv_sems.at[3 * i + k], (x, y, 1 - cc))
                cp.start()
                cps.append(cp)
        for i in range(n):
            for k, chip in enumerate(chips):
                other = _own_piece(ins[i], axes[i], _chip_index(*chip), 1 - cc)
                cps[3 * i + k].wait_send()
                _remote(other, other, send_sems.at[3 * i + k], recv_sems.at[3 * i + k], (x, y, cc)).wait_recv()

    return pl.pallas_call(
        body, name="ag_forward_%d" % n,
        out_shape=tuple(jax.ShapeDtypeStruct(f.shape, f.dtype) for f in fulls),
        in_specs=[HBM_SPEC] * n, out_specs=tuple([HBM_SPEC] * n),
        input_output_aliases={i: i for i in range(n)},
        scratch_shapes=[pltpu.SemaphoreType.DMA((3 * n,)), pltpu.SemaphoreType.DMA((3 * n,))],
    )(*fulls)


def _rs_pair(g):
    def body(g_ref, o_ref, send_sem, recv_sem):
        x, y, cc, _ = _place()
        cp = _remote(g_ref, o_ref, send_sem, recv_sem, (x, y, 1 - cc))
        cp.start()
        cp.wait()

    return pl.pallas_call(
        body, name="rs_pair", out_shape=jax.ShapeDtypeStruct(g.shape, g.dtype),
        in_specs=[HBM_SPEC], out_specs=HBM_SPEC,
        scratch_shapes=[pltpu.SemaphoreType.DMA, pltpu.SemaphoreType.DMA],
    )(g)


def _dw_pair(core):
    def dw(a, b, *, m, n, alpha, axis, name):
        t = a.shape[0]
        if axis == 1:
            dims = dict(m=m // 2, n=n, tm=min(TM, m // 2), tn=TN)
            operand = "a"
        else:
            dims = dict(m=m, n=n // 2, tm=TN, tn=n // 2)
            operand = "b"
        common = dict(ta=True, k=t, tk=t, alpha=alpha, out_dtype=BF16, **dims)
        sent = _matmul([(a, b)], half=(core, False, operand), name=name + "_sib", **common)
        return _matmul([(a, b)], half=(core, True, operand), res=_rs_pair(sent), name=name, **common)

    return dw


def _piece_shape(p, axis):
    r, c = p.shape
    return (r, c // N_CHIPS) if axis == 1 else (r // N_CHIPS, c)


def _rs_chips_start(ps, axes):
    n = len(ps)
    lands = [lax.empty((3,) + _piece_shape(p, ax), p.dtype) for p, ax in zip(ps, axes)]

    def body(*refs):
        p_refs, land_refs = refs[:n], refs[n:2 * n]
        send, recv = refs[2 * n:5 * n], refs[5 * n:8 * n]
        token = refs[10 * n]
        x, y, cc, chips = _place()
        for i in range(n):
            for k, chip in enumerate(chips):
                _remote(_chunk(p_refs[i], axes[i], _chip_index(*chip)), land_refs[i].at[k],
                        send[3 * i + k], recv[3 * i + k], (*chip, cc)).start()
        token[...] = jnp.zeros_like(token)

    sem = pltpu.SemaphoreType.DMA(())
    outs = pl.pallas_call(
        body, name="rs_start_%d" % n,
        out_shape=tuple([sem] * (6 * n) + [pltpu.HBM(a.shape, a.dtype) for a in list(ps) + lands]
                        + [jax.ShapeDtypeStruct((8, 128), F32)]),
        in_specs=[HBM_SPEC] * (2 * n),
        out_specs=tuple([SEM_SPEC] * (6 * n) + [HBM_SPEC] * (2 * n) + [pl.BlockSpec(memory_space=pltpu.VMEM)]),
        input_output_aliases={i: 6 * n + i for i in range(2 * n)},
        compiler_params=pltpu.CompilerParams(has_side_effects=EFFECT),
    )(*[_hbm(a) for a in list(ps) + lands])
    return outs[:3 * n], outs[3 * n:6 * n], outs[6 * n:7 * n], outs[7 * n:8 * n], outs[8 * n][0, 0]


def _rs_chips_wait(send, recv, ps, lands, axes, after):
    n = len(ps)

    def body(*refs):
        p_refs, land_refs = refs[:n], refs[n:2 * n]
        send_s, recv_s = refs[2 * n:5 * n], refs[5 * n:8 * n]
        x, y, cc, chips = _place()
        for i in range(n):
            for k, chip in enumerate(chips):
                cp = _remote(_chunk(p_refs[i], axes[i], _chip_index(*chip)), land_refs[i].at[k],
                             send_s[3 * i + k], recv_s[3 * i + k], (*chip, cc))
                cp.wait_send()
                cp.wait_recv()

    outs = pl.pallas_call(
        body, name="rs_wait_%d" % n,
        out_shape=tuple(pltpu.HBM(a.shape, a.dtype) for a in list(ps) + list(lands)),
        in_specs=[HBM_SPEC] * (2 * n) + [SEM_SPEC] * (6 * n) + [ANY_SPEC],
        out_specs=tuple([HBM_SPEC] * (2 * n)),
        input_output_aliases={i: i for i in range(2 * n)},
        compiler_params=pltpu.CompilerParams(has_side_effects=EFFECT),
    )(*ps, *lands, *send, *recv, after)
    return outs[:n], outs[n:]


def _add_chips(p, got, chip, core, axis, *, idx, count, into, tr, tc):
    _, pr, pc = got.shape
    shard = (2 * pr, pc) if axis == 1 else (pr, 2 * pc)
    if axis == 1:
        p_map = lambda i, j, sc, so: (i, sc[0] * (pc // tc) + j)
        o_map = lambda i, j, sc, so: (idx, so[0] * (pr // tr) + i, j)
    else:
        p_map = lambda i, j, sc, so: (sc[0] * (pr // tr) + i, j)
        o_map = lambda i, j, sc, so: (idx, i, so[0] * (pc // tc) + j)

    def body(sc_ref, so_ref, p_ref, r_ref, *rest):
        o_ref = rest[-1]
        acc = p_ref[...].astype(F32)
        for k in range(3):
            acc = acc + r_ref[k].astype(F32)
        o_ref[...] = acc

    in_specs = [pl.BlockSpec((tr, tc), p_map), pl.BlockSpec((3, tr, tc), lambda i, j, sc, so: (0, i, j))]
    args = [chip, core, p, got]
    aliases = {}
    if into is not None:
        in_specs.append(ANY_SPEC)
        args.append(into)
        aliases = {4: 0}
    return pl.pallas_call(
        body, name="add_chips", out_shape=jax.ShapeDtypeStruct((count,) + shard, F32),
        grid_spec=pltpu.PrefetchScalarGridSpec(
            num_scalar_prefetch=2, grid=(pr // tr, pc // tc), in_specs=in_specs,
            out_specs=pl.BlockSpec((None, tr, tc), o_map)),
        input_output_aliases=aliases,
        compiler_params=_params(("parallel", "parallel")),
    )(*args)


def _rs_final(stacked, axis):
    n = stacked.shape[0]

    def body(s_ref, o_ref, send_sems, recv_sems):
        x, y, cc, _ = _place()
        cps = []
        for i in range(n):
            mine = _half(s_ref.at[i], axis, cc)
            cp = _remote(mine, mine, send_sems.at[i], recv_sems.at[i], (x, y, 1 - cc))
            cp.start()
            cps.append(cp)
        for i, cp in enumerate(cps):
            other = _half(s_ref.at[i], axis, 1 - cc)
            cp.wait_send()
            _remote(other, other, send_sems.at[i], recv_sems.at[i], (x, y, cc)).wait_recv()

    return pl.pallas_call(
        body, name="rs_final", out_shape=jax.ShapeDtypeStruct(stacked.shape, stacked.dtype),
        in_specs=[HBM_SPEC], out_specs=HBM_SPEC, input_output_aliases={0: 0},
        scratch_shapes=[pltpu.SemaphoreType.DMA((n,)), pltpu.SemaphoreType.DMA((n,))],
    )(stacked)


def _ag_small(packed):
    rows, cols = packed.shape

    def body(s_ref, o_ref, send_sems, recv_sems, local_sem):
        x, y, cc, chips = _place()
        me = _chip_index(x, y)
        own = pltpu.make_async_copy(s_ref, o_ref.at[me], local_sem)
        own.start()
        cps = [_remote(s_ref, o_ref.at[me], send_sems.at[k], recv_sems.at[k], (*chip, cc))
               for k, chip in enumerate(chips)]
        for cp in cps:
            cp.start()
        for k, chip in enumerate(chips):
            cps[k].wait_send()
            got = o_ref.at[_chip_index(*chip)]
            _remote(got, got, send_sems.at[k], recv_sems.at[k], (x, y, cc)).wait_recv()
        own.wait()

    return pl.pallas_call(
        body, name="ag_small", out_shape=jax.ShapeDtypeStruct((N_CHIPS, rows, cols), packed.dtype),
        in_specs=[HBM_SPEC], out_specs=HBM_SPEC,
        scratch_shapes=[pltpu.SemaphoreType.DMA((3,)), pltpu.SemaphoreType.DMA((3,)), pltpu.SemaphoreType.DMA],
    )(packed)


def _rs_small(packed):
    _, rows, cols = packed.shape
    rels = [(bx, by, bc) for bx in (0, 1) for by in (0, 1) for bc in (0, 1)][1:]

    def body(s_ref, o_ref, send_sems, recv_sems, local_sem):
        x, y, cc, _ = _place()
        me = 4 * x + 2 * y + cc
        own = pltpu.make_async_copy(s_ref.at[_chip_index(x, y)], o_ref.at[me], local_sem)
        own.start()
        peers = [(jnp.bitwise_xor(x, bx), jnp.bitwise_xor(y, by), jnp.bitwise_xor(cc, bc)) for bx, by, bc in rels]
        cps = [_remote(s_ref.at[_chip_index(px, py)], o_ref.at[me], send_sems.at[k], recv_sems.at[k], (px, py, pc))
               for k, (px, py, pc) in enumerate(peers)]
        for cp in cps:
            cp.start()
        for k, (px, py, pc) in enumerate(peers):
            cps[k].wait_send()
            got = o_ref.at[4 * px + 2 * py + pc]
            _remote(got, got, send_sems.at[k], recv_sems.at[k], (x, y, cc)).wait_recv()
        own.wait()

    return pl.pallas_call(
        body, name="rs_small", out_shape=jax.ShapeDtypeStruct((2 * N_CHIPS, rows, cols), packed.dtype),
        in_specs=[HBM_SPEC], out_specs=HBM_SPEC,
        scratch_shapes=[pltpu.SemaphoreType.DMA((7,)), pltpu.SemaphoreType.DMA((7,)), pltpu.SemaphoreType.DMA],
    )(packed)


def _sum_slots(slots, *, tr=8):
    n, rows, cols = slots.shape

    def body(s_ref, o_ref):
        acc = s_ref[0]
        for k in range(1, n):
            acc = acc + s_ref[k]
        o_ref[...] = acc

    return pl.pallas_call(
        body, name="sum_slots", out_shape=jax.ShapeDtypeStruct((rows, cols), F32), grid=(rows // tr,),
        in_specs=[pl.BlockSpec((n, tr, cols), lambda i: (0, i, 0))], out_specs=pl.BlockSpec((tr, cols), lambda i: (i, 0)),
        compiler_params=_params(("parallel",)),
    )(slots)


MATRIX_AXIS = {"ffn_w1": 1, "ffn_w3": 1, "ffn_w2": 0, "ev_w_in": 1, "ev_w_out": 0, "od_w_in": 1, "od_w_out": 0}
SMALL_SHARDED = ("norm_g", "ev_conv_w", "od_conv_w", "od_conv_b", "od_ln_g", "od_ln_b", "od_pool_scale")
SMALL_REPLICATED = ("ev_q_gain", "ev_k_gain")
PACK_COLS = 1024
PACK_ROW_ALIGN = 8


def _pack(parts):
    flat = jnp.concatenate([p.reshape(-1).astype(F32) for p in parts])
    per = PACK_COLS * PACK_ROW_ALIGN
    total = -(-flat.shape[0] // per) * per
    return jnp.pad(flat, (0, total - flat.shape[0])).reshape(total // PACK_COLS, PACK_COLS)


def _unpack(packed, shapes):
    flat = packed.reshape(-1)
    out, pos = [], 0
    for shp in shapes:
        size = 1
        for s in shp:
            size *= s
        out.append(flat[pos:pos + size].reshape(shp))
        pos += size
    return out


def _row_tile(rows, cols, itemsize=4, budget=1 << 20):
    tr = 8
    while rows % (2 * tr) == 0 and 2 * tr * cols * itemsize <= budget:
        tr *= 2
    return tr


def _piece_tiles(hr, hc, axis):
    pr, pc = (hr, hc // N_CHIPS) if axis == 1 else (hr // N_CHIPS, hc)
    tr = 128
    while tr > 16 and tr * pc * 4 > (1 << 20):
        tr //= 2
    return tr, pc


AG_LOOKAHEAD = 2


def kernel(x, norm_g, ffn_w1, ffn_w3, ffn_w2, ev_w_in, ev_q_gain, ev_k_gain, ev_conv_w, ev_w_out, od_w_in, od_conv_w, od_conv_b, od_ln_g, od_ln_b, od_pool_w, od_pool_scale, od_w_out, loss_target, m_norm_g, m_ffn_w1, m_ffn_w3, m_ffn_w2, m_ev_w_in, m_ev_q_gain, m_ev_k_gain, m_ev_conv_w, m_ev_w_out, m_od_w_in, m_od_conv_w, m_od_conv_b, m_od_ln_g, m_od_ln_b, m_od_pool_w, m_od_pool_scale, m_od_w_out, v_norm_g, v_ffn_w1, v_ffn_w3, v_ffn_w2, v_ev_w_in, v_ev_q_gain, v_ev_k_gain, v_ev_conv_w, v_ev_w_out, v_od_w_in, v_od_conv_w, v_od_conv_b, v_od_ln_g, v_od_ln_b, v_od_pool_w, v_od_pool_scale, v_od_w_out):
    names = ["norm_g", "ffn_w1", "ffn_w3", "ffn_w2", "ev_w_in", "ev_q_gain", "ev_k_gain", "ev_conv_w", "ev_w_out",
             "od_w_in", "od_conv_w", "od_conv_b", "od_ln_g", "od_ln_b", "od_pool_w", "od_pool_scale", "od_w_out"]
    w = dict(zip(names, (norm_g, ffn_w1, ffn_w3, ffn_w2, ev_w_in, ev_q_gain, ev_k_gain, ev_conv_w, ev_w_out,
                         od_w_in, od_conv_w, od_conv_b, od_ln_g, od_ln_b, od_pool_w, od_pool_scale, od_w_out)))
    m = dict(zip(names, (m_norm_g, m_ffn_w1, m_ffn_w3, m_ffn_w2, m_ev_w_in, m_ev_q_gain, m_ev_k_gain, m_ev_conv_w,
                         m_ev_w_out, m_od_w_in, m_od_conv_w, m_od_conv_b, m_od_ln_g, m_od_ln_b, m_od_pool_w,
                         m_od_pool_scale, m_od_w_out)))
    v = dict(zip(names, (v_norm_g, v_ffn_w1, v_ffn_w3, v_ffn_w2, v_ev_w_in, v_ev_q_gain, v_ev_k_gain, v_ev_conv_w,
                         v_ev_w_out, v_od_w_in, v_od_conv_w, v_od_conv_b, v_od_ln_g, v_od_ln_b, v_od_pool_w,
                         v_od_pool_scale, v_od_w_out)))
    cx, cy, cc = lax.axis_index("x"), lax.axis_index("y"), lax.axis_index("c")
    cc_arr = jnp.reshape(cc, (1,)).astype(jnp.int32)
    chip_arr = jnp.reshape(_chip_index(cx, cy), (1,)).astype(jnp.int32)
    n_sub = 3 * norm_g.shape[0]
    stacked = {name: w[name].reshape((-1,) + w[name].shape[-2:]) for name in MATRIX_AXIS}

    def axes_of(s):
        return [MATRIX_AXIS[name] for name, _ in _sublayer_matrices(s)]

    ag_inflight = {}

    def ag_start(s, after):
        fulls = []
        for name, idx in _sublayer_matrices(s):
            src, _ = lax.optimization_barrier((stacked[name], after))
            fulls.append(_cast_into(src, idx, chip_arr, MATRIX_AXIS[name]))
        send, recv, fulls, tok = _ag_start(fulls, axes_of(s))
        ag_inflight[s] = (send, recv, fulls)
        return tok

    x0 = x[0]
    first_tok = sum(ag_start(s, x0) for s in range(min(AG_LOOKAHEAD, n_sub)))

    def fetch(s, after):
        tok = first_tok if s == 0 else 0.0
        if s + AG_LOOKAHEAD < n_sub:
            tok = tok + ag_start(s + AG_LOOKAHEAD, after)
        send, recv, fulls = ag_inflight.pop(s)
        fulls = _ag_wait(send, recv, fulls, axes_of(s), after)
        return list(_ag_forward(fulls, axes_of(s))), tok

    rs_inflight = {}

    def emit(s, ps):
        send, recv, ps, lands, tok = _rs_chips_start(ps, axes_of(s))
        rs_inflight[s] = (send, recv, ps, lands)
        return tok

    wts = {}
    small_names = SMALL_SHARDED + ("od_pool_w",)
    gathered = _ag_small(_pack([w[k] for k in small_names]))
    per_chip = [_unpack(gathered[j], [w[k].shape for k in small_names]) for j in range(N_CHIPS)]
    for idx, k in enumerate(small_names):
        ax = 2 if k == "od_pool_w" else w[k].ndim - 1
        wts[k] = jnp.concatenate([per_chip[j][idx] for j in range(N_CHIPS)], axis=ax)
    for k in SMALL_REPLICATED:
        wts[k] = w[k]

    sq, dx, gr = _local_step(x0, loss_target[0], wts, fetch, emit, _dw_pair(cc_arr))

    grads = {}
    reduced = {name: None for name in MATRIX_AXIS}
    for s in reversed(range(n_sub)):
        send, recv, ps, lands = rs_inflight.pop(s)
        axes = axes_of(s)
        ps, lands = _rs_chips_wait(send, recv, ps, lands, axes, dx)
        for (name, idx), p, land, ax in zip(_sublayer_matrices(s), ps, lands, axes):
            tr, tc = _piece_tiles(*p.shape, ax)
            reduced[name] = _add_chips(p, land, chip_arr, cc_arr, ax, idx=idx, count=stacked[name].shape[0],
                                       into=reduced[name], tr=tr, tc=tc)
    for name, axis in MATRIX_AXIS.items():
        grads[name] = _rs_final(reduced[name], axis).reshape(w[name].shape)
    small_full = {k: (gr[k] if k == "norm_g" else jnp.stack(gr[k])) for k in small_names + SMALL_REPLICATED}
    chunks = []
    for j in range(N_CHIPS):
        parts = []
        for k in small_names:
            ax = 2 if k == "od_pool_w" else w[k].ndim - 1
            size = w[k].shape[ax]
            parts.append(lax.slice_in_dim(small_full[k], j * size, (j + 1) * size, axis=ax))
        parts += [small_full[k] for k in SMALL_REPLICATED] + [sq[0, :1]]
        chunks.append(_pack(parts))
    summed = _sum_slots(_rs_small(jnp.stack(chunks)))
    pack_names = small_names + SMALL_REPLICATED
    unpacked = _unpack(summed, [w[k].shape for k in pack_names] + [(1,)])
    for k, g in zip(pack_names, unpacked):
        grads[k] = g
    loss = (0.5 / x.shape[-1]) * unpacked[-1][0]

    delta, new_m, new_v = {}, {}, {}
    for name in MATRIX_AXIS:
        shp = w[name].shape
        cols = shp[-1]
        rows = w[name].size // cols
        tr = _row_tile(rows, cols, budget=1 << 20)
        d_, m_, v_ = _adamw(w[name].reshape(rows, cols), grads[name].reshape(rows, cols),
                            m[name].reshape(rows, cols), v[name].reshape(rows, cols), tr=tr)
        delta[name], new_m[name], new_v[name] = d_.reshape(shp), m_.reshape(shp), v_.reshape(shp)
    pw, pg, pm, pv = (_pack([t[k] for k in pack_names]) for t in (w, grads, m, v))
    d_, m_, v_ = _adamw(pw, pg, pm, pv, tr=PACK_ROW_ALIGN)
    shapes = [w[k].shape for k in pack_names]
    for store, packed in ((delta, d_), (new_m, m_), (new_v, v_)):
        for k, a in zip(pack_names, _unpack(packed, shapes)):
            store[k] = a

    return (loss, dx[None], *[grads[k] for k in names], *[delta[k] for k in names],
            *[new_m[k] for k in names], *[new_v[k] for k in names])
```

```python
import jax
import jax.numpy as jnp
from jax import lax
from jax.experimental import pallas as pl
from jax.experimental.pallas import tpu as pltpu

F32 = jnp.float32
BF16 = jnp.bfloat16
MESH = pl.DeviceIdType.MESH

EPS = 1e-6
HEADS = 8
HEAD_DIM = 128
A_WIDTH = HEADS * HEAD_DIM
A_WINDOWS = (128, 512, 2048)
A_DILATIONS = (1, 4, 16)
ATTN_BLOCK = 128
B_CONV = 3
C_CONV = 31
D_WINDOWS = (2, 4, 8, 16)
HALO = 32
N_CHIPS = 4
ADAM_LR = 0.001
ADAM_B1 = 0.9
ADAM_B2 = 0.999
ADAM_EPS = 1e-08
ADAM_WD = 0.01
ADAM_STEP = 10
VMEM_LIMIT_BYTES = 56 * 1024 * 1024
NEG_BIG = -1e30


def _params(sem, **kw):
    return pltpu.CompilerParams(dimension_semantics=sem, vmem_limit_bytes=VMEM_LIMIT_BYTES, **kw)


def _sigmoid(x):
    return 1.0 / (1.0 + jnp.exp(-x))


EPILOGUE_COLS = 256


def _matmul(pairs, *, m, n, k, tm, tn, tk, ta=False, tb=False, out_dtype=F32, res=None,
            alpha=1.0, name, j_outer=False, half=None, tok=None):
    nk = k // tk
    npairs = len(pairs)
    dn = (((0 if ta else 1,), (1 if tb else 0,)), ((), ()))

    def ij(p, q):
        return (q, p) if j_outer else (p, q)

    def shift(s, operand, blocks):
        if half is None or half[2] != operand:
            return 0
        h = s[0][0]
        return (h if half[1] else 1 - h) * blocks

    def a_map(p, q, kk, *s):
        i = ij(p, q)[0] + shift(s, "a", m // tm)
        return (kk, i) if ta else (i, kk)

    def b_map(p, q, kk, *s):
        j = ij(p, q)[1] + shift(s, "b", n // tn)
        return (j, kk) if tb else (kk, j)

    def o_map(p, q, kk, *s):
        return ij(p, q)

    def body(*refs):
        if half is not None:
            refs = refs[1:]
        ab = refs[:2 * npairs]
        pos = 2 * npairs
        res_ref = tok_ref = None
        if res is not None:
            res_ref = refs[pos]
            pos += 1
        if tok is not None:
            tok_ref = refs[pos]
            pos += 1
        o_ref = refs[pos]
        acc_ref = refs[pos + 1] if nk > 1 else None

        def dots():
            tot = None
            for p in range(npairs):
                d = lax.dot_general(ab[2 * p][...], ab[2 * p + 1][...], dn, preferred_element_type=F32)
                tot = d if tot is None else tot + d
            return tot

        def finish(acc):
            r = acc * alpha if alpha != 1.0 else acc
            if res_ref is not None:
                r = res_ref[...].astype(F32) + r
            if tok_ref is not None:
                r = r + tok_ref[0:1, 0:1]
            o_ref[...] = r.astype(o_ref.dtype)

        if nk == 1:
            finish(dots())
        else:
            kk = pl.program_id(2)

            @pl.when(kk == 0)
            def _():
                acc_ref[...] = dots()

            @pl.when(kk > 0)
            def _():
                acc_ref[...] += dots()

            @pl.when(kk == nk - 1)
            def _():
                finish(acc_ref[...])

    a_spec = pl.BlockSpec((tk, tm) if ta else (tm, tk), a_map)
    b_spec = pl.BlockSpec((tn, tk) if tb else (tk, tn), b_map)
    o_spec = pl.BlockSpec((tm, tn), o_map)
    in_specs = [a_spec, b_spec] * npairs
    args = [t for p in pairs for t in p]
    if res is not None:
        in_specs.append(o_spec)
        args.append(res)
    if tok is not None:
        in_specs.append(pl.BlockSpec((8, 128), lambda p, q, kk, *s: (0, 0)))
        args.append(jnp.full((8, 128), tok, F32))
    grid = ij(m // tm, n // tn) + (nk,)
    scratch = [pltpu.VMEM((tm, tn), F32)] if nk > 1 else []
    if half is None:
        kwargs = dict(grid=grid, in_specs=in_specs, out_specs=o_spec, scratch_shapes=scratch)
    else:
        args = [half[0]] + args
        kwargs = dict(grid_spec=pltpu.PrefetchScalarGridSpec(
            num_scalar_prefetch=1, grid=grid, in_specs=in_specs, out_specs=o_spec, scratch_shapes=scratch))
    return pl.pallas_call(
        body, name=name, out_shape=jax.ShapeDtypeStruct((m, n), out_dtype),
        compiler_params=_params(("parallel", "parallel", "arbitrary")), **kwargs,
    )(*args)


def _ffn_up(xn, w1, w3, *, tm, tn):
    t, d = xn.shape
    f = w1.shape[1]

    def body(x_ref, w1_ref, w3_ref, a_ref, b_ref, h_ref):
        x = x_ref[...]
        for c in range(tn // EPILOGUE_COLS):
            cols = slice(c * EPILOGUE_COLS, (c + 1) * EPILOGUE_COLS)
            a = jnp.dot(x, w1_ref[:, cols], preferred_element_type=F32)
            b = jnp.dot(x, w3_ref[:, cols], preferred_element_type=F32)
            a_ref[:, cols] = a.astype(BF16)
            b_ref[:, cols] = b.astype(BF16)
            h_ref[:, cols] = (a * _sigmoid(a) * b).astype(BF16)

    x_spec = pl.BlockSpec((tm, d), lambda i, j: (i, 0))
    w_spec = pl.BlockSpec((d, tn), lambda i, j: (0, j))
    o_spec = pl.BlockSpec((tm, tn), lambda i, j: (i, j))
    shp = jax.ShapeDtypeStruct((t, f), BF16)
    return pl.pallas_call(
        body, name="ffn_up", out_shape=(shp, shp, shp), grid=(t // tm, f // tn),
        in_specs=[x_spec, w_spec, w_spec], out_specs=(o_spec, o_spec, o_spec),
        compiler_params=_params(("parallel", "parallel")),
    )(xn, w1, w3)


def _ffn_dh(dyb, w2, a, b, *, tm, tn):
    t, d = dyb.shape
    f = w2.shape[0]
    dn = (((1,), (1,)), ((), ()))

    def body(dy_ref, w2_ref, a_ref, b_ref, da_ref, db_ref):
        dy = dy_ref[...]
        for c in range(tn // EPILOGUE_COLS):
            cols = slice(c * EPILOGUE_COLS, (c + 1) * EPILOGUE_COLS)
            dh = 0.5 * lax.dot_general(dy, w2_ref[cols, :], dn, preferred_element_type=F32)
            av = a_ref[:, cols].astype(F32)
            bv = b_ref[:, cols].astype(F32)
            sig = _sigmoid(av)
            da_ref[:, cols] = (dh * bv * (sig * (1.0 + av * (1.0 - sig)))).astype(BF16)
            db_ref[:, cols] = (dh * (av * sig)).astype(BF16)

    dy_spec = pl.BlockSpec((tm, d), lambda i, j: (i, 0))
    w_spec = pl.BlockSpec((tn, d), lambda i, j: (j, 0))
    o_spec = pl.BlockSpec((tm, tn), lambda i, j: (i, j))
    shp = jax.ShapeDtypeStruct((t, f), BF16)
    return pl.pallas_call(
        body, name="ffn_dh", out_shape=(shp, shp), grid=(t // tm, f // tn),
        in_specs=[dy_spec, w_spec, o_spec, o_spec], out_specs=(o_spec, o_spec),
        compiler_params=_params(("parallel", "parallel")),
    )(dyb, w2, a, b)


def _rmsnorm_fwd(x, g, *, tr=256):
    t, d = x.shape

    def body(x_ref, g_ref, o_ref):
        xv = x_ref[...]
        y = xv * lax.rsqrt(jnp.mean(xv * xv, axis=-1, keepdims=True) + EPS)
        o_ref[...] = (y * g_ref[...]).astype(BF16)

    return pl.pallas_call(
        body, name="rmsnorm_fwd", out_shape=jax.ShapeDtypeStruct((t, d), BF16), grid=(t // tr,),
        in_specs=[pl.BlockSpec((tr, d), lambda i: (i, 0)), pl.BlockSpec((1, d), lambda i: (0, 0))],
        out_specs=pl.BlockSpec((tr, d), lambda i: (i, 0)),
        compiler_params=_params(("parallel",)),
    )(x, g.reshape(1, d))


def _rmsnorm_bwd(dy, x, g, dres, *, tr=256):
    t, d = x.shape

    def body(dy_ref, x_ref, g_ref, dres_ref, dx_ref, dxb_ref, dg_ref):
        xv = x_ref[...]
        dyv = dy_ref[...].astype(F32)
        r = lax.rsqrt(jnp.mean(xv * xv, axis=-1, keepdims=True) + EPS)
        xhat = xv * r
        dxhat = dyv * g_ref[...]
        c = jnp.mean(dxhat * xhat, axis=-1, keepdims=True)
        dx = dres_ref[...] + r * (dxhat - xhat * c)
        dx_ref[...] = dx
        dxb_ref[...] = dx.astype(BF16)
        part = jnp.sum(dyv * xhat, axis=0, keepdims=True)

        @pl.when(pl.program_id(0) == 0)
        def _():
            dg_ref[...] = part

        @pl.when(pl.program_id(0) > 0)
        def _():
            dg_ref[...] += part

    row = pl.BlockSpec((tr, d), lambda i: (i, 0))
    vec = pl.BlockSpec((1, d), lambda i: (0, 0))
    dx, dxb, dg = pl.pallas_call(
        body, name="rmsnorm_bwd",
        out_shape=(jax.ShapeDtypeStruct((t, d), F32), jax.ShapeDtypeStruct((t, d), BF16),
                   jax.ShapeDtypeStruct((1, d), F32)),
        grid=(t // tr,), in_specs=[row, row, vec, row], out_specs=(row, row, vec),
        compiler_params=_params(("arbitrary",)),
    )(dy, x, g.reshape(1, d), dres)
    return dx, dxb, dg.reshape(d)


def _loss_head(y, target, *, tr=256):
    t, d = y.shape

    def body(y_ref, t_ref, dy_ref, dyb_ref, s_ref):
        err = y_ref[...] - t_ref[...]
        dy = err * (1.0 / d)
        dy_ref[...] = dy
        dyb_ref[...] = dy.astype(BF16)
        part = jnp.full((1, 128), jnp.sum(err * err), F32)

        @pl.when(pl.program_id(0) == 0)
        def _():
            s_ref[...] = part

        @pl.when(pl.program_id(0) > 0)
        def _():
            s_ref[...] += part

    row = pl.BlockSpec((tr, d), lambda i: (i, 0))
    return pl.pallas_call(
        body, name="loss_head",
        out_shape=(jax.ShapeDtypeStruct((t, d), F32), jax.ShapeDtypeStruct((t, d), BF16),
                   jax.ShapeDtypeStruct((1, 128), F32)),
        grid=(t // tr,), in_specs=[row, row],
        out_specs=(row, row, pl.BlockSpec((1, 128), lambda i: (0, 0))),
        compiler_params=_params(("arbitrary",)),
    )(y, target)


def _adamw(w, g, m, v, *, tr):
    rows, cols = w.shape

    def body(w_ref, g_ref, m_ref, v_ref, d_ref, nm_ref, nv_ref):
        gv = g_ref[...]
        nm = ADAM_B1 * m_ref[...] + (1.0 - ADAM_B1) * gv
        nv = ADAM_B2 * v_ref[...] + (1.0 - ADAM_B2) * jnp.square(gv)
        m_hat = nm / (1.0 - ADAM_B1 ** ADAM_STEP)
        v_hat = nv / (1.0 - ADAM_B2 ** ADAM_STEP)
        d_ref[...] = -ADAM_LR * (m_hat / (jnp.sqrt(v_hat) + ADAM_EPS) + ADAM_WD * w_ref[...])
        nm_ref[...] = nm
        nv_ref[...] = nv

    spec = pl.BlockSpec((tr, cols), lambda i: (i, 0))
    shp = jax.ShapeDtypeStruct((rows, cols), F32)
    return pl.pallas_call(
        body, name="adamw", out_shape=(shp, shp, shp), grid=(rows // tr,),
        in_specs=[spec] * 4, out_specs=(spec, spec, spec),
        compiler_params=_params(("parallel",)),
    )(w, g, m, v)


def _headnorm(xf, g):
    r = lax.rsqrt(jnp.mean(xf * xf, axis=-1, keepdims=True) + EPS)
    xhat = xf * r
    return xhat * g, xhat, r


def _headnorm_bwd(dn, xhat, r, g):
    dxhat = dn * g
    return r * (dxhat - xhat * jnp.mean(dxhat * xhat, axis=-1, keepdims=True))


_NT = (((1,), (1,)), ((), ()))
_TN = (((0,), (0,)), ((), ()))


def _attn_masks(n, nb):
    qi = lax.broadcasted_iota(jnp.int32, (ATTN_BLOCK, ATTN_BLOCK), 0)
    ci = lax.broadcasted_iota(jnp.int32, (ATTN_BLOCK, ATTN_BLOCK), 1)
    d_prev = qi + ATTN_BLOCK - ci
    d_cur = qi - ci
    return d_prev, d_cur, (ci >= qi), (ci <= qi)


def _head_lane(tile, h):
    lane = lax.broadcasted_iota(jnp.int32, tile.shape, 1)
    return jnp.sum(jnp.where(lane == h, tile, 0.0), axis=-1, keepdims=True)


def _set_head_lane(tile, h, col):
    lane = lax.broadcasted_iota(jnp.int32, tile.shape, 1)
    return jnp.where(lane == h, col, tile)


def _attn_fwd(qv, kv, vv, offs, qg, kg, *, dil):
    l = qv.shape[0]
    nb = l // ATTN_BLOCK
    scale = HEAD_DIM ** -0.5

    def body(q_ref, kp_ref, kc_ref, vp_ref, vc_ref, qg_ref, kg_ref, o_ref, lse_ref):
        n = pl.program_id(1)
        d_prev, d_cur, ok_prev, ok_cur = _attn_masks(n, nb)
        ok_prev = ok_prev & (n > 0)
        b_prev = d_prev.astype(F32) * float(dil)
        b_cur = d_cur.astype(F32) * float(dil)
        lse = jnp.zeros((ATTN_BLOCK, HEAD_DIM), F32)
        for h in range(HEADS):
            sl = slice(h * HEAD_DIM, (h + 1) * HEAD_DIM)
            slope = 2.0 ** (-8.0 * (h + 1) / HEADS)
            q = _headnorm(q_ref[:, sl].astype(F32), qg_ref[...])[0].astype(BF16)
            kp = _headnorm(kp_ref[:, sl].astype(F32), kg_ref[...])[0].astype(BF16)
            kc = _headnorm(kc_ref[:, sl].astype(F32), kg_ref[...])[0].astype(BF16)
            s1 = lax.dot_general(q, kp, _NT, preferred_element_type=F32) * scale
            s2 = lax.dot_general(q, kc, _NT, preferred_element_type=F32) * scale
            s1 = jnp.where(ok_prev, s1 - slope * b_prev, NEG_BIG)
            s2 = jnp.where(ok_cur, s2 - slope * b_cur, NEG_BIG)
            m = jnp.maximum(jnp.max(s1, axis=-1, keepdims=True), jnp.max(s2, axis=-1, keepdims=True))
            p1 = jnp.exp(s1 - m)
            p2 = jnp.exp(s2 - m)
            den = jnp.sum(p1, axis=-1, keepdims=True) + jnp.sum(p2, axis=-1, keepdims=True)
            inv = 1.0 / den
            o = jnp.dot((p1 * inv).astype(BF16), vp_ref[:, sl], preferred_element_type=F32)
            o = o + jnp.dot((p2 * inv).astype(BF16), vc_ref[:, sl], preferred_element_type=F32)
            o_ref[:, sl] = o
            lse = _set_head_lane(lse, h, m + jnp.log(den))
        lse_ref[...] = lse

    cur, prev, _ = _attn_specs(nb)
    vec = pl.BlockSpec((1, HEAD_DIM), lambda r, n: (0, 0))
    return pl.pallas_call(
        body, name="attn_fwd_d%d" % dil,
        out_shape=(jax.ShapeDtypeStruct((l, dil * A_WIDTH), F32), jax.ShapeDtypeStruct((l, dil * HEAD_DIM), F32)),
        grid=(dil, nb),
        in_specs=[cur(offs[0]), prev(offs[1]), cur(offs[1]), prev(offs[2]), cur(offs[2]), vec, vec],
        out_specs=(cur(0), cur(0, HEAD_DIM)),
        compiler_params=_params(("parallel", "parallel")),
    )(qv, kv, kv, vv, vv, qg.reshape(1, HEAD_DIM), kg.reshape(1, HEAD_DIM))


def _attn_specs(nb):
    def cur(off, width=A_WIDTH):
        return pl.BlockSpec((ATTN_BLOCK, width), lambda r, n: (n, off + r))

    def prev(off, width=A_WIDTH):
        return pl.BlockSpec((ATTN_BLOCK, width), lambda r, n: (jnp.maximum(n - 1, 0), off + r))

    def nxt(off, width=A_WIDTH):
        return pl.BlockSpec((ATTN_BLOCK, width), lambda r, n: (jnp.minimum(n + 1, nb - 1), off + r))

    return cur, prev, nxt


def _attn_combine(outs, lses, *, tr=256):
    t, w = outs[0].shape

    def body(o0, o1, o2, l0, l1, l2, y_ref, lse_ref):
        a0, a1, a2 = l0[...], l1[...], l2[...]
        m = jnp.maximum(jnp.maximum(a0, a1), a2)
        e0, e1, e2 = jnp.exp(a0 - m), jnp.exp(a1 - m), jnp.exp(a2 - m)
        s = e0 + e1 + e2
        inv = 1.0 / s
        w0, w1, w2 = e0 * inv, e1 * inv, e2 * inv
        lse_ref[...] = m + jnp.log(s)
        for h in range(HEADS):
            sl = slice(h * HEAD_DIM, (h + 1) * HEAD_DIM)
            y = (_head_lane(w0, h) * o0[:, sl] + _head_lane(w1, h) * o1[:, sl] + _head_lane(w2, h) * o2[:, sl])
            y_ref[:, sl] = y.astype(BF16)

    row = pl.BlockSpec((tr, w), lambda i: (i, 0))
    stat = pl.BlockSpec((tr, HEAD_DIM), lambda i: (i, 0))
    return pl.pallas_call(
        body, name="attn_combine",
        out_shape=(jax.ShapeDtypeStruct((t, w), BF16), jax.ShapeDtypeStruct((t, HEAD_DIM), F32)),
        grid=(t // tr,), in_specs=[row] * 3 + [stat] * 3, out_specs=(row, stat),
        compiler_params=_params(("parallel",)),
    )(*outs, *lses)


def _attn_delta(dy, y, *, tr=256):
    t, w = y.shape

    def body(dy_ref, y_ref, o_ref):
        out = jnp.zeros((tr, HEAD_DIM), F32)
        for h in range(HEADS):
            sl = slice(h * HEAD_DIM, (h + 1) * HEAD_DIM)
            dlt = jnp.sum(dy_ref[:, sl] * y_ref[:, sl].astype(F32), axis=-1, keepdims=True)
            out = _set_head_lane(out, h, dlt)
        o_ref[...] = out

    row = pl.BlockSpec((tr, w), lambda i: (i, 0))
    return pl.pallas_call(
        body, name="attn_delta", out_shape=jax.ShapeDtypeStruct((t, HEAD_DIM), F32), grid=(t // tr,),
        in_specs=[row, row], out_specs=pl.BlockSpec((tr, HEAD_DIM), lambda i: (i, 0)),
        compiler_params=_params(("parallel",)),
    )(dy, y)


def _attn_bwd(qv, kv, vv, dyv, offs, lsev, dltv, qg, kg, *, dil):
    l = qv.shape[0]
    w = dil * A_WIDTH
    nb = l // ATTN_BLOCK
    scale = HEAD_DIM ** -0.5

    def body(qc_ref, qn_ref, kp_ref, kc_ref, vp_ref, vc_ref, dyc_ref, dyn_ref, lc_ref, ln_ref,
             dc_ref, dn_ref, qg_ref, kg_ref, dq_ref, dk_ref, dv_ref, dqg_ref, dkg_ref):
        n = pl.program_id(1)
        first = (pl.program_id(0) == 0) & (n == 0)
        d_prev, d_cur, ok_prev, ok_cur = _attn_masks(n, nb)
        ok_t1 = ok_prev & (n > 0)
        ok_t3 = ok_prev & (n < nb - 1)
        b_prev = d_prev.astype(F32) * float(dil)
        b_cur = d_cur.astype(F32) * float(dil)
        qgv, kgv = qg_ref[...], kg_ref[...]
        dqg = jnp.zeros((1, HEAD_DIM), F32)
        dkg = jnp.zeros((1, HEAD_DIM), F32)
        for h in range(HEADS):
            sl = slice(h * HEAD_DIM, (h + 1) * HEAD_DIM)
            slope = 2.0 ** (-8.0 * (h + 1) / HEADS)
            qc, qc_hat, qc_r = _headnorm(qc_ref[:, sl].astype(F32), qgv)
            qn = _headnorm(qn_ref[:, sl].astype(F32), qgv)[0].astype(BF16)
            kp = _headnorm(kp_ref[:, sl].astype(F32), kgv)[0].astype(BF16)
            kc, kc_hat, kc_r = _headnorm(kc_ref[:, sl].astype(F32), kgv)
            qc = qc.astype(BF16)
            kc = kc.astype(BF16)
            vp, vc = vp_ref[:, sl], vc_ref[:, sl]
            dyc, dyn = dyc_ref[:, sl].astype(BF16), dyn_ref[:, sl].astype(BF16)

            def tile(q, k, v, dy, lse, dlt, ok, bias):
                s = lax.dot_general(q, k, _NT, preferred_element_type=F32) * scale
                p = jnp.where(ok, jnp.exp(jnp.where(ok, s - slope * bias, NEG_BIG) - lse), 0.0)
                dp = lax.dot_general(dy, v, _NT, preferred_element_type=F32)
                return p.astype(BF16), (p * (dp - dlt)).astype(BF16)

            lse_c, dlt_c = _head_lane(lc_ref[...], h), _head_lane(dc_ref[...], h)
            p1, ds1 = tile(qc, kp, vp, dyc, lse_c, dlt_c, ok_t1, b_prev)
            p2, ds2 = tile(qc, kc, vc, dyc, lse_c, dlt_c, ok_cur, b_cur)
            p3, ds3 = tile(qn, kc, vc, dyn, _head_lane(ln_ref[...], h), _head_lane(dn_ref[...], h), ok_t3, b_prev)
            dqn = scale * (jnp.dot(ds1, kp, preferred_element_type=F32) + jnp.dot(ds2, kc, preferred_element_type=F32))
            dkn = scale * (lax.dot_general(ds2, qc, _TN, preferred_element_type=F32)
                           + lax.dot_general(ds3, qn, _TN, preferred_element_type=F32))
            dv = (lax.dot_general(p2, dyc, _TN, preferred_element_type=F32)
                  + lax.dot_general(p3, dyn, _TN, preferred_element_type=F32))
            dqg = dqg + jnp.sum(dqn * qc_hat, axis=0, keepdims=True)
            dkg = dkg + jnp.sum(dkn * kc_hat, axis=0, keepdims=True)
            dq_ref[:, sl] = _headnorm_bwd(dqn, qc_hat, qc_r, qgv).astype(BF16)
            dk_ref[:, sl] = _headnorm_bwd(dkn, kc_hat, kc_r, kgv).astype(BF16)
            dv_ref[:, sl] = dv.astype(BF16)

        @pl.when(first)
        def _():
            dqg_ref[...] = dqg
            dkg_ref[...] = dkg

        @pl.when(jnp.logical_not(first))
        def _():
            dqg_ref[...] += dqg
            dkg_ref[...] += dkg

    cur, prev, nxt = _attn_specs(nb)
    o_q, o_k, o_v, o_dy = offs
    stat_c, stat_n = cur(0, HEAD_DIM), nxt(0, HEAD_DIM)
    vec = pl.BlockSpec((1, HEAD_DIM), lambda r, n: (0, 0))
    shp = jax.ShapeDtypeStruct((l, w), BF16)
    gshp = jax.ShapeDtypeStruct((1, HEAD_DIM), F32)
    return pl.pallas_call(
        body, name="attn_bwd_d%d" % dil, out_shape=(shp, shp, shp, gshp, gshp), grid=(dil, nb),
        in_specs=[cur(o_q), nxt(o_q), prev(o_k), cur(o_k), prev(o_v), cur(o_v), cur(o_dy), nxt(o_dy),
                  stat_c, stat_n, stat_c, stat_n, vec, vec],
        out_specs=(cur(0), cur(0), cur(0), vec, vec),
        compiler_params=_params(("arbitrary", "arbitrary")),
    )(qv, qv, kv, kv, vv, vv, dyv, dyv, lsev, lsev, dltv, dltv,
      qg.reshape(1, HEAD_DIM), kg.reshape(1, HEAD_DIM))


def _prev_halo(tr, tc, col0):
    return pl.BlockSpec((HALO, tc), lambda j, i: (jnp.maximum(i * (tr // HALO) - 1, 0), col0 + j))


def _next_halo(tr, tc, col0, rows):
    last = rows // HALO - 1
    return pl.BlockSpec((HALO, tc), lambda j, i: (jnp.minimum((i + 1) * (tr // HALO), last), col0 + j))


def _cur_block(tr, tc, col0):
    return pl.BlockSpec((tr, tc), lambda j, i: (i, col0 + j))


def _gateconv_fwd(h, conv_w, *, col0, tr=512, tc=256):
    t = h.shape[0]
    width = conv_w.shape[1]
    nc = width // tc
    c0 = col0 // tc

    def body(bg_ref, cg_ref, xt_ref, cgh_ref, xth_ref, w_ref, y_ref, pad_ref):
        i = pl.program_id(1)
        halo = cgh_ref[...].astype(F32) * xth_ref[...].astype(F32)
        pad_ref[0:HALO, :] = jnp.where(i > 0, halo, 0.0)
        pad_ref[HALO:HALO + tr, :] = cg_ref[...].astype(F32) * xt_ref[...].astype(F32)
        conv = None
        for j in range(B_CONV):
            term = w_ref[j:j + 1, :] * pad_ref[HALO - (B_CONV - 1) + j:HALO - (B_CONV - 1) + j + tr, :]
            conv = term if conv is None else conv + term
        y_ref[...] = (bg_ref[...].astype(F32) * conv).astype(BF16)

    return pl.pallas_call(
        body, name="gateconv_fwd", out_shape=jax.ShapeDtypeStruct((t, width), BF16), grid=(nc, t // tr),
        in_specs=[_cur_block(tr, tc, c0), _cur_block(tr, tc, c0 + nc), _cur_block(tr, tc, c0 + 2 * nc),
                  _prev_halo(tr, tc, c0 + nc), _prev_halo(tr, tc, c0 + 2 * nc),
                  pl.BlockSpec((8, tc), lambda j, i: (0, j))],
        out_specs=_cur_block(tr, tc, 0),
        scratch_shapes=[pltpu.VMEM((HALO + tr, tc), F32)],
        compiler_params=_params(("parallel", "arbitrary")),
    )(h, h, h, h, h, _pad_rows(conv_w, 8))


def _pad_rows(w, rows):
    return jnp.pad(w, ((0, rows - w.shape[0]), (0, 0)))


def _gateconv_bwd(h, dy, conv_w, *, col0, dcol0, tr=512, tc=256):
    t = h.shape[0]
    width = conv_w.shape[1]
    nc = width // tc
    c0 = col0 // tc
    dc0 = dcol0 // tc
    nt = t // tr

    def body(bg_ref, cg_ref, xt_ref, cgh_ref, xth_ref, bgn_ref, dy_ref, dyn_ref, w_ref,
             dbg_ref, dcg_ref, dxt_ref, dw_ref, pad_ref, padd_ref):
        i = pl.program_id(1)
        cg = cg_ref[...].astype(F32)
        xt = xt_ref[...].astype(F32)
        bg = bg_ref[...].astype(F32)
        dyv = dy_ref[...]
        halo = cgh_ref[...].astype(F32) * xth_ref[...].astype(F32)
        pad_ref[0:HALO, :] = jnp.where(i > 0, halo, 0.0)
        pad_ref[HALO:HALO + tr, :] = cg * xt
        dconv = dyv * bg
        padd_ref[0:tr, :] = dconv
        padd_ref[tr:tr + HALO, :] = jnp.where(i < nt - 1, dyn_ref[...] * bgn_ref[...].astype(F32), 0.0)
        conv = None
        du = None
        dws = []
        for j in range(B_CONV):
            off = HALO - (B_CONV - 1) + j
            shifted = pad_ref[off:off + tr, :]
            term = w_ref[j:j + 1, :] * shifted
            conv = term if conv is None else conv + term
            dws.append(jnp.sum(dconv * shifted, axis=0, keepdims=True))
            back = w_ref[j:j + 1, :] * padd_ref[B_CONV - 1 - j:B_CONV - 1 - j + tr, :]
            du = back if du is None else du + back
        dbg_ref[...] = (dyv * conv).astype(BF16)
        dcg_ref[...] = (du * xt).astype(BF16)
        dxt_ref[...] = (du * cg).astype(BF16)
        dw = _stack_rows(dws, 8, tc)

        @pl.when(i == 0)
        def _():
            dw_ref[...] = dw

        @pl.when(i > 0)
        def _():
            dw_ref[...] += dw

    oshp = jax.ShapeDtypeStruct((t, width), BF16)
    return pl.pallas_call(
        body, name="gateconv_bwd",
        out_shape=(oshp, oshp, oshp, jax.ShapeDtypeStruct((8, width), F32)), grid=(nc, nt),
        in_specs=[_cur_block(tr, tc, c0), _cur_block(tr, tc, c0 + nc), _cur_block(tr, tc, c0 + 2 * nc),
                  _prev_halo(tr, tc, c0 + nc), _prev_halo(tr, tc, c0 + 2 * nc),
                  _next_halo(tr, tc, c0, t), _cur_block(tr, tc, dc0), _next_halo(tr, tc, dc0, t),
                  pl.BlockSpec((8, tc), lambda j, i: (0, j))],
        out_specs=(_cur_block(tr, tc, 0), _cur_block(tr, tc, 0), _cur_block(tr, tc, 0),
                   pl.BlockSpec((8, tc), lambda j, i: (0, j))),
        scratch_shapes=[pltpu.VMEM((HALO + tr, tc), F32), pltpu.VMEM((tr + HALO, tc), F32)],
        compiler_params=_params(("parallel", "arbitrary")),
    )(h, h, h, h, h, h, dy, dy, _pad_rows(conv_w, 8))


def _stack_rows(rows, n, width):
    idx = lax.broadcasted_iota(jnp.int32, (n, width), 0)
    out = jnp.zeros((n, width), F32)
    for j, r in enumerate(rows):
        out = jnp.where(idx == j, r, out)
    return out


CONV_ROWS = 64


def _glu_conv_fwd(hod, conv_w, conv_b, *, tr=512, tc=256):
    t = hod.shape[0]
    width = conv_w.shape[1]
    nc = width // tc

    def body(val_ref, gate_ref, valh_ref, gateh_ref, w_ref, b_ref, u1_ref, pad_ref):
        i = pl.program_id(1)
        halo = valh_ref[...].astype(F32) * _sigmoid(gateh_ref[...].astype(F32))
        pad_ref[0:HALO, :] = jnp.where(i > 0, halo, 0.0)
        pad_ref[HALO:HALO + tr, :] = val_ref[...].astype(F32) * _sigmoid(gate_ref[...].astype(F32))
        for c in range(tr // CONV_ROWS):
            base = HALO + c * CONV_ROWS - (C_CONV - 1)
            acc = None
            for j in range(C_CONV):
                term = w_ref[j:j + 1, :] * pad_ref[base + j:base + j + CONV_ROWS, :]
                acc = term if acc is None else acc + term
            u1_ref[c * CONV_ROWS:(c + 1) * CONV_ROWS, :] = acc + b_ref[...]

    return pl.pallas_call(
        body, name="glu_conv_fwd", out_shape=jax.ShapeDtypeStruct((t, width), F32), grid=(nc, t // tr),
        in_specs=[_cur_block(tr, tc, 0), _cur_block(tr, tc, nc), _prev_halo(tr, tc, 0), _prev_halo(tr, tc, nc),
                  pl.BlockSpec((32, tc), lambda j, i: (0, j)), pl.BlockSpec((1, tc), lambda j, i: (0, j))],
        out_specs=_cur_block(tr, tc, 0),
        scratch_shapes=[pltpu.VMEM((HALO + tr, tc), F32)],
        compiler_params=_params(("parallel", "arbitrary")),
    )(hod, hod, hod, hod, _pad_rows(conv_w, 32), conv_b.reshape(1, width))


def _ln_silu_fwd(u1, g, b, *, tr=256):
    t, width = u1.shape

    def body(u_ref, g_ref, b_ref, o_ref):
        uv = u_ref[...]
        mu = jnp.mean(uv, axis=-1, keepdims=True)
        var = jnp.mean(jnp.square(uv - mu), axis=-1, keepdims=True)
        u2 = ((uv - mu) * lax.rsqrt(var + EPS)) * g_ref[...] + b_ref[...]
        o_ref[...] = (u2 * _sigmoid(u2)).astype(BF16)

    row = pl.BlockSpec((tr, width), lambda i: (i, 0))
    vec = pl.BlockSpec((1, width), lambda i: (0, 0))
    return pl.pallas_call(
        body, name="ln_silu_fwd", out_shape=jax.ShapeDtypeStruct((t, width), BF16), grid=(t // tr,),
        in_specs=[row, vec, vec], out_specs=row, compiler_params=_params(("parallel",)),
    )(u1, g.reshape(1, width), b.reshape(1, width))


def _ln_silu_bwd(du, u1, g, b, *, col0, tr=256):
    t, width = u1.shape

    def body(du_ref, u_ref, g_ref, b_ref, du1_ref, dg_ref, db_ref, dcb_ref):
        uv = u_ref[...]
        mu = jnp.mean(uv, axis=-1, keepdims=True)
        var = jnp.mean(jnp.square(uv - mu), axis=-1, keepdims=True)
        rstd = lax.rsqrt(var + EPS)
        xh = (uv - mu) * rstd
        u2 = xh * g_ref[...] + b_ref[...]
        sig = _sigmoid(u2)
        du2 = du_ref[...] * (sig * (1.0 + u2 * (1.0 - sig)))
        dxh = du2 * g_ref[...]
        du1 = rstd * (dxh - jnp.mean(dxh, axis=-1, keepdims=True)
                      - xh * jnp.mean(dxh * xh, axis=-1, keepdims=True))
        du1_ref[...] = du1
        parts = (jnp.sum(du2 * xh, axis=0, keepdims=True), jnp.sum(du2, axis=0, keepdims=True),
                 jnp.sum(du1, axis=0, keepdims=True))

        @pl.when(pl.program_id(0) == 0)
        def _():
            dg_ref[...], db_ref[...], dcb_ref[...] = parts

        @pl.when(pl.program_id(0) > 0)
        def _():
            dg_ref[...] += parts[0]
            db_ref[...] += parts[1]
            dcb_ref[...] += parts[2]

    row = pl.BlockSpec((tr, width), lambda i: (i, 0))
    vec = pl.BlockSpec((1, width), lambda i: (0, 0))
    vshp = jax.ShapeDtypeStruct((1, width), F32)
    return pl.pallas_call(
        body, name="ln_silu_bwd", out_shape=(jax.ShapeDtypeStruct((t, width), F32), vshp, vshp, vshp),
        grid=(t // tr,),
        in_specs=[pl.BlockSpec((tr, width), lambda i: (i, col0 // width)), row, vec, vec],
        out_specs=(row, vec, vec, vec), compiler_params=_params(("arbitrary",)),
    )(du, u1, g.reshape(1, width), b.reshape(1, width))


def _glu_conv_bwd(hod, du1, conv_w, *, tr=512, tc=256):
    t = hod.shape[0]
    width = conv_w.shape[1]
    nc = width // tc
    nt = t // tr

    def body(val_ref, gate_ref, valh_ref, gateh_ref, du_ref, dun_ref, w_ref,
             dval_ref, dgate_ref, dw_ref, pad_ref, padd_ref, du0_ref):
        i = pl.program_id(1)
        val = val_ref[...].astype(F32)
        sig = _sigmoid(gate_ref[...].astype(F32))
        halo = valh_ref[...].astype(F32) * _sigmoid(gateh_ref[...].astype(F32))
        pad_ref[0:HALO, :] = jnp.where(i > 0, halo, 0.0)
        pad_ref[HALO:HALO + tr, :] = val * sig
        padd_ref[0:tr, :] = du_ref[...]
        padd_ref[tr:tr + HALO, :] = jnp.where(i < nt - 1, dun_ref[...], 0.0)
        dws = [jnp.zeros((1, tc), F32)] * C_CONV
        for c in range(tr // CONV_ROWS):
            r0 = c * CONV_ROWS
            duc = padd_ref[r0:r0 + CONV_ROWS, :]
            acc = None
            for j in range(C_CONV):
                back = w_ref[j:j + 1, :] * padd_ref[r0 + C_CONV - 1 - j:r0 + C_CONV - 1 - j + CONV_ROWS, :]
                acc = back if acc is None else acc + back
                off = HALO + r0 - (C_CONV - 1) + j
                dws[j] = dws[j] + jnp.sum(duc * pad_ref[off:off + CONV_ROWS, :], axis=0, keepdims=True)
            du0_ref[r0:r0 + CONV_ROWS, :] = acc
        du0 = du0_ref[...]
        dval_ref[...] = (du0 * sig).astype(BF16)
        dgate_ref[...] = (du0 * val * sig * (1.0 - sig)).astype(BF16)
        dw = _stack_rows(dws, 32, tc)

        @pl.when(i == 0)
        def _():
            dw_ref[...] = dw

        @pl.when(i > 0)
        def _():
            dw_ref[...] += dw

    oshp = jax.ShapeDtypeStruct((t, width), BF16)
    wspec = pl.BlockSpec((32, tc), lambda j, i: (0, j))
    return pl.pallas_call(
        body, name="glu_conv_bwd", out_shape=(oshp, oshp, jax.ShapeDtypeStruct((32, width), F32)), grid=(nc, nt),
        in_specs=[_cur_block(tr, tc, 0), _cur_block(tr, tc, nc), _prev_halo(tr, tc, 0), _prev_halo(tr, tc, nc),
                  _cur_block(tr, tc, 0), _next_halo(tr, tc, 0, t), wspec],
        out_specs=(_cur_block(tr, tc, 0), _cur_block(tr, tc, 0), wspec),
        scratch_shapes=[pltpu.VMEM((HALO + tr, tc), F32), pltpu.VMEM((tr + HALO, tc), F32),
                        pltpu.VMEM((tr, tc), F32)],
        compiler_params=_params(("parallel", "arbitrary")),
    )(hod, hod, hod, hod, du1, du1, _pad_rows(conv_w, 32))


def _pooled(pad_ref, g, kw, tr, i):
    gw = pad_ref.shape[1] // len(D_WINDOWS)
    cols = slice(g * gw, (g + 1) * gw)
    tot = None
    for j in range(kw):
        sh = pad_ref[HALO - j:HALO - j + tr, cols]
        tot = sh if tot is None else tot + sh
    return tot / _window_count(tr, gw, kw, i * tr) - pad_ref[HALO:HALO + tr, cols]


def _window_count(rows, width, kw, row0):
    t1 = (lax.broadcasted_iota(jnp.int32, (rows, width), 0) + (row0 + 1)).astype(F32)
    return jnp.minimum(t1, float(kw))


def _pool_fwd(hod, pool_w, pool_scale, *, tr=256):
    t = hod.shape[0]
    width = pool_scale.shape[0]
    ng = len(D_WINDOWS)
    gw = width // ng

    def body(z_ref, zh_ref, w_ref, s_ref, y_ref, pad_ref):
        i = pl.program_id(1)
        pad_ref[0:HALO, :] = jnp.where(i > 0, zh_ref[...].astype(F32), 0.0)
        pad_ref[HALO:HALO + tr, :] = z_ref[...].astype(F32)
        for g, kw in enumerate(D_WINDOWS):
            cols = slice(g * gw, (g + 1) * gw)
            pre = jnp.dot(_pooled(pad_ref, g, kw, tr, i).astype(BF16), w_ref[g], preferred_element_type=F32)
            y_ref[:, cols] = (pre * s_ref[:, cols]).astype(BF16)

    return pl.pallas_call(
        body, name="pool_fwd", out_shape=jax.ShapeDtypeStruct((t, width), BF16), grid=(1, t // tr),
        in_specs=[_cur_block(tr, width, 2), _prev_halo(tr, width, 2),
                  pl.BlockSpec((ng, gw, gw), lambda j, i: (0, 0, 0)), pl.BlockSpec((1, width), lambda j, i: (0, 0))],
        out_specs=_cur_block(tr, width, 0),
        scratch_shapes=[pltpu.VMEM((HALO + tr, width), F32)],
        compiler_params=_params(("parallel", "arbitrary")),
    )(hod, hod, pool_w, pool_scale.reshape(1, width))


def _pool_bwd(hod, dy, pool_w, pool_scale, *, dcol0, tr=256):
    t = hod.shape[0]
    width = pool_scale.shape[0]
    ng = len(D_WINDOWS)
    gw = width // ng
    nt = t // tr

    def body(z_ref, zh_ref, dy_ref, dyn_ref, w_ref, s_ref, dz_ref, dw_ref, ds_ref, pad_ref, pade_ref):
        i = pl.program_id(1)
        pad_ref[0:HALO, :] = jnp.where(i > 0, zh_ref[...].astype(F32), 0.0)
        pad_ref[HALO:HALO + tr, :] = z_ref[...].astype(F32)
        dws = []
        dss = []
        for g, kw in enumerate(D_WINDOWS):
            cols = slice(g * gw, (g + 1) * gw)
            wg = w_ref[g]
            dyc = dy_ref[:, cols]
            dpre = (dyc * s_ref[:, cols]).astype(BF16)
            dpre_n = (dyn_ref[:, cols] * s_ref[:, cols]).astype(BF16)
            dpl = lax.dot_general(dpre, wg, _NT, preferred_element_type=F32)
            dpl_n = lax.dot_general(dpre_n, wg, _NT, preferred_element_type=F32)
            pade_ref[0:tr, cols] = dpl / _window_count(tr, gw, kw, i * tr)
            pade_ref[tr:tr + HALO, cols] = jnp.where(i < nt - 1, dpl_n / _window_count(HALO, gw, kw, (i + 1) * tr), 0.0)
            tot = None
            for j in range(kw):
                sh = pade_ref[j:j + tr, cols]
                tot = sh if tot is None else tot + sh
            dz_ref[:, cols] = (tot - dpl).astype(BF16)
            pooled = _pooled(pad_ref, g, kw, tr, i).astype(BF16)
            pre = jnp.dot(pooled, wg, preferred_element_type=F32)
            dss.append(jnp.sum(dyc * pre, axis=0, keepdims=True))
            dws.append(lax.dot_general(pooled, dpre, _TN, preferred_element_type=F32))

        @pl.when(i == 0)
        def _():
            for g in range(ng):
                dw_ref[g] = dws[g]
                ds_ref[:, g * gw:(g + 1) * gw] = dss[g]

        @pl.when(i > 0)
        def _():
            for g in range(ng):
                dw_ref[g] += dws[g]
                ds_ref[:, g * gw:(g + 1) * gw] += dss[g]

    dc = dcol0 // width
    wspec = pl.BlockSpec((ng, gw, gw), lambda j, i: (0, 0, 0))
    vspec = pl.BlockSpec((1, width), lambda j, i: (0, 0))
    return pl.pallas_call(
        body, name="pool_bwd",
        out_shape=(jax.ShapeDtypeStruct((t, width), BF16), jax.ShapeDtypeStruct((ng, gw, gw), F32),
                   jax.ShapeDtypeStruct((1, width), F32)),
        grid=(1, nt),
        in_specs=[_cur_block(tr, width, 2), _prev_halo(tr, width, 2), _cur_block(tr, width, dc),
                  _next_halo(tr, width, dc, t), wspec, vspec],
        out_specs=(_cur_block(tr, width, 0), wspec, vspec),
        scratch_shapes=[pltpu.VMEM((HALO + tr, width), F32), pltpu.VMEM((tr + HALO, width), F32)],
        compiler_params=_params(("arbitrary", "arbitrary")),
    )(hod, hod, dy, dy, pool_w, pool_scale.reshape(1, width))


TM = 1024
TN = 512
TK_ACC = 512


def _dw_full(a, b, *, m, n, alpha, axis, name):
    t = a.shape[0]
    tm = TM if m % TM == 0 else TN
    return _matmul([(a, b)], ta=True, m=m, n=n, k=t, tm=tm, tn=TN, tk=t, alpha=alpha, out_dtype=BF16, name=name)


def _ffn_fwd(x, g, w1, w3, w2):
    t, d = x.shape
    f = w1.shape[1]
    xn = _rmsnorm_fwd(x, g)
    a, b, h = _ffn_up(xn, w1, w3, tm=TM, tn=TN)
    y = _matmul([(h, w2)], m=t, n=d, k=f, tm=TM, tn=TN, tk=f, res=x, alpha=0.5, name="ffn_down")
    return y, (x, xn, a, b, h)


def _ffn_bwd(dx, dxb, saved, g, w1, w3, w2, push):
    x, xn, a, b, h = saved
    t, d = x.shape
    f = w1.shape[1]
    da, db = _ffn_dh(dxb, w2, a, b, tm=TM, tn=TN)
    tok = push([dict(a=xn, b=da, m=d, n=f, alpha=1.0, axis=1, name="ffn_dw1"),
                dict(a=xn, b=db, m=d, n=f, alpha=1.0, axis=1, name="ffn_dw3"),
                dict(a=h, b=dxb, m=f, n=d, alpha=0.5, axis=0, name="ffn_dw2")])
    dxn = _matmul([(da, w1), (db, w3)], tb=True, m=t, n=d, k=f, tm=TM, tn=d, tk=TK_ACC, tok=tok, name="ffn_dxn")
    return _rmsnorm_bwd(dxn, x, g, dx)


def _mix_out_fwd(x, ycat, w_out):
    t, d = x.shape
    return _matmul([(ycat, w_out)], m=t, n=d, k=d, tm=TM, tn=TN, tk=d, res=x, name="mix_out")


def _mix_out_bwd(dxb, w_out):
    t, d = dxb.shape
    return _matmul([(dxb, w_out)], tb=True, m=t, n=d, k=d, tm=TM, tn=TN, tk=d, name="mix_dy")


def _mix_in_bwd(dh, xn, w_in, x, g, dx, ycat, dxb, push):
    t, d = x.shape
    n_in = w_in.shape[1]
    tok = push([dict(a=xn, b=dh, m=d, n=n_in, alpha=1.0, axis=1, name="mix_dw_in"),
                dict(a=ycat, b=dxb, m=d, n=d, alpha=1.0, axis=0, name="mix_dw_out")])
    dxn = _matmul([(dh, w_in)], tb=True, m=t, n=d, k=n_in, tm=TM, tn=d, tk=TK_ACC, tok=tok, name="mix_dxn")
    return _rmsnorm_bwd(dxn, x, g, dx)


def _group_view(a, col0, dil, width=A_WIDTH):
    if dil == 1:
        return a, col0 // width
    t = a.shape[0]
    return a[:, col0:col0 + width].reshape(t // dil, dil * width), 0


def _even_fwd(x, g, w_in, qg, kg, conv_w, w_out):
    t, d = x.shape
    n_in = w_in.shape[1]
    nq = len(A_DILATIONS) * A_WIDTH
    xn = _rmsnorm_fwd(x, g)
    h = _matmul([(xn, w_in)], m=t, n=n_in, k=d, tm=TM, tn=TN, tk=d, out_dtype=BF16, name="ev_in")
    outs, lses = [], []
    for gi, dil in enumerate(A_DILATIONS):
        (qv, oq), (kv, ok), (vv, ov) = (_group_view(h, part * nq + gi * A_WIDTH, dil) for part in range(3))
        o, l = _attn_fwd(qv, kv, vv, (oq, ok, ov), qg, kg, dil=dil)
        outs.append(o.reshape(t, A_WIDTH))
        lses.append(l.reshape(t, HEAD_DIM))
    ya, lse = _attn_combine(outs, lses)
    yb = _gateconv_fwd(h, conv_w, col0=3 * nq)
    ycat = jnp.concatenate([ya, yb], axis=1)
    return _mix_out_fwd(x, ycat, w_out), (x, xn, h, ya, lse, ycat)


def _even_bwd(dx, dxb, saved, g, w_in, qg, kg, conv_w, w_out, push):
    x, xn, h, ya, lse, ycat = saved
    t, d = x.shape
    nq = len(A_DILATIONS) * A_WIDTH
    dycat = _mix_out_bwd(dxb, w_out)
    dlt = _attn_delta(dycat, ya)
    dqs, dks, dvs = [], [], []
    dqg = jnp.zeros((HEAD_DIM,), F32)
    dkg = jnp.zeros((HEAD_DIM,), F32)
    for gi, dil in enumerate(A_DILATIONS):
        (qv, oq), (kv, ok), (vv, ov) = (_group_view(h, part * nq + gi * A_WIDTH, dil) for part in range(3))
        dyv, ody = _group_view(dycat, 0, dil)
        dq, dk, dv, dqg_i, dkg_i = _attn_bwd(
            qv, kv, vv, dyv, (oq, ok, ov, ody), _group_view(lse, 0, dil, HEAD_DIM)[0],
            _group_view(dlt, 0, dil, HEAD_DIM)[0], qg, kg, dil=dil)
        dqs.append(dq.reshape(t, A_WIDTH))
        dks.append(dk.reshape(t, A_WIDTH))
        dvs.append(dv.reshape(t, A_WIDTH))
        dqg = dqg + dqg_i.reshape(HEAD_DIM)
        dkg = dkg + dkg_i.reshape(HEAD_DIM)
    dbg, dcg, dxt, dcw = _gateconv_bwd(h, dycat, conv_w, col0=3 * nq, dcol0=A_WIDTH)
    dh = jnp.concatenate(dqs + dks + dvs + [dbg, dcg, dxt], axis=1)
    dx, dxb, dg = _mix_in_bwd(dh, xn, w_in, x, g, dx, ycat, dxb, push)
    return dx, dxb, dg, dqg, dkg, dcw[:B_CONV]


def _odd_fwd(x, g, w_in, conv_w, conv_b, ln_g, ln_b, pool_w, pool_scale, w_out):
    t, d = x.shape
    n_in = w_in.shape[1]
    xn = _rmsnorm_fwd(x, g)
    hod = _matmul([(xn, w_in)], m=t, n=n_in, k=d, tm=TM, tn=TN, tk=d, out_dtype=BF16, name="od_in")
    u1 = _glu_conv_fwd(hod, conv_w, conv_b)
    u = _ln_silu_fwd(u1, ln_g, ln_b)
    yd = _pool_fwd(hod, pool_w.astype(BF16), pool_scale)
    ycat = jnp.concatenate([u, yd], axis=1)
    return _mix_out_fwd(x, ycat, w_out), (x, xn, hod, u1, ycat)


def _odd_bwd(dx, dxb, saved, g, w_in, conv_w, conv_b, ln_g, ln_b, pool_w, pool_scale, w_out, push):
    x, xn, hod, u1, ycat = saved
    width = conv_w.shape[1]
    dycat = _mix_out_bwd(dxb, w_out)
    du1, dlg, dlb, dcb = _ln_silu_bwd(dycat, u1, ln_g, ln_b, col0=0)
    dval, dgate, dcw = _glu_conv_bwd(hod, du1, conv_w)
    dz, dpw, dps = _pool_bwd(hod, dycat, pool_w.astype(BF16), pool_scale, dcol0=width)
    dh = jnp.concatenate([dval, dgate, dz], axis=1)
    dx, dxb, dg = _mix_in_bwd(dh, xn, w_in, x, g, dx, ycat, dxb, push)
    return (dx, dxb, dg, dcw[:C_CONV], dcb.reshape(width), dlg.reshape(width), dlb.reshape(width),
            dpw, dps.reshape(width))


def _sublayer_matrices(s):
    layer, slot = divmod(s, 3)
    if slot == 1:
        kind = "ev" if layer % 2 == 0 else "od"
        return [(kind + "_w_in", layer // 2), (kind + "_w_out", layer // 2)]
    j = 2 * layer + slot // 2
    return [("ffn_w1", j), ("ffn_w3", j), ("ffn_w2", j)]


def _local_step(x, target, wts, fetch=None, grads=None):
    depth = wts["norm_g"].shape[0]
    if fetch is None:
        fetch = lambda s, after: ([wts[name][idx] for name, idx in _sublayer_matrices(s)], 0.0)
    gr = {}
    if grads is None:
        def grads(s, specs):
            for (name, idx), spec in zip(_sublayer_matrices(s), specs):
                gr.setdefault(name, {})[idx] = _dw_full(**spec)
            return 0.0

    def gain(layer, slot, tok):
        return wts["norm_g"][layer, slot] + tok

    saved = []
    for layer in range(depth):
        i = layer // 2
        s = 3 * layer
        m0, tok = fetch(s, x)
        x, s0 = _ffn_fwd(x, gain(layer, 0, tok), *m0)
        m1, tok = fetch(s + 1, x)
        if layer % 2 == 0:
            x, s1 = _even_fwd(x, gain(layer, 1, tok), m1[0], wts["ev_q_gain"][i],
                              wts["ev_k_gain"][i], wts["ev_conv_w"][i], m1[1])
        else:
            x, s1 = _odd_fwd(x, gain(layer, 1, tok), m1[0], wts["od_conv_w"][i],
                             wts["od_conv_b"][i], wts["od_ln_g"][i], wts["od_ln_b"][i], wts["od_pool_w"][i],
                             wts["od_pool_scale"][i], m1[1])
        m2, tok = fetch(s + 2, x)
        x, s2 = _ffn_fwd(x, gain(layer, 2, tok), *m2)
        saved.append(((s0, m0), (s1, m1), (s2, m2)))
    dx, dxb, sq = _loss_head(x, target)

    n_even, n_odd = (depth + 1) // 2, depth // 2
    for k in ("ev_q_gain", "ev_k_gain", "ev_conv_w"):
        gr[k] = [None] * n_even
    for k in ("od_conv_w", "od_conv_b", "od_ln_g", "od_ln_b", "od_pool_w", "od_pool_scale"):
        gr[k] = [None] * n_odd
    dnorm = [[None] * 3 for _ in range(depth)]
    norm_g = wts["norm_g"]

    def push_for(s):
        return lambda specs: grads(s, specs)

    for layer in reversed(range(depth)):
        i = layer // 2
        s = 3 * layer
        (s0, m0), (s1, m1), (s2, m2) = saved[layer]
        dx, dxb, dnorm[layer][2] = _ffn_bwd(dx, dxb, s2, norm_g[layer, 2], *m2, push_for(s + 2))
        if layer % 2 == 0:
            (dx, dxb, dnorm[layer][1], gr["ev_q_gain"][i], gr["ev_k_gain"][i], gr["ev_conv_w"][i]) = _even_bwd(
                dx, dxb, s1, norm_g[layer, 1], m1[0], wts["ev_q_gain"][i],
                wts["ev_k_gain"][i], wts["ev_conv_w"][i], m1[1], push_for(s + 1))
        else:
            (dx, dxb, dnorm[layer][1], gr["od_conv_w"][i], gr["od_conv_b"][i], gr["od_ln_g"][i],
             gr["od_ln_b"][i], gr["od_pool_w"][i], gr["od_pool_scale"][i]) = _odd_bwd(
                dx, dxb, s1, norm_g[layer, 1], m1[0], wts["od_conv_w"][i],
                wts["od_conv_b"][i], wts["od_ln_g"][i], wts["od_ln_b"][i], wts["od_pool_w"][i],
                wts["od_pool_scale"][i], m1[1], push_for(s + 1))
        dx, dxb, dnorm[layer][0] = _ffn_bwd(dx, dxb, s0, norm_g[layer, 0], *m0, push_for(s))
    gr["norm_g"] = jnp.stack([jnp.stack(r) for r in dnorm])
    for name in list(gr):
        if isinstance(gr[name], dict):
            gr[name] = [gr[name][idx] for idx in sorted(gr[name])]
    return sq, dx, gr


HBM_SPEC = pl.BlockSpec(memory_space=pltpu.HBM)
SEM_SPEC = pl.BlockSpec(memory_space=pltpu.SEMAPHORE)
ANY_SPEC = pl.BlockSpec(memory_space=pl.ANY)
EFFECT = pltpu.SideEffectType.DATAFLOW_SIDE_EFFECTING


def _place():
    x, y, c = lax.axis_index("x"), lax.axis_index("y"), lax.axis_index("c")
    chips = [(1 - x, y), (x, 1 - y), (1 - x, 1 - y)]
    return x, y, c, chips


def _chip_index(x, y):
    return 2 * x + y


def _ds(start, size, align):
    if isinstance(start, int):
        return pl.ds(start, size)
    return pl.ds(pl.multiple_of(start, align), size)


def _half(ref, axis, h):
    r, c = ref.shape[-2:]
    if axis == 1:
        return ref.at[_ds(h * (r // 2), r // 2, 16), :]
    return ref.at[:, _ds(h * (c // 2), c // 2, 128)]


def _chunk(ref, axis, j, n=N_CHIPS):
    r, c = ref.shape[-2:]
    if axis == 1:
        return ref.at[:, _ds(j * (c // n), c // n, 128)]
    return ref.at[_ds(j * (r // n), r // n, 16), :]


def _remote(src, dst, send_sem, recv_sem, device):
    return pltpu.make_async_remote_copy(src_ref=src, dst_ref=dst, send_sem=send_sem, recv_sem=recv_sem,
                                        device_id=device, device_id_type=MESH)


def _hbm(a):
    return pltpu.with_memory_space_constraint(a, pltpu.HBM)


def _cast_into(stacked, idx, chip, axis, tok):
    _, r, c = stacked.shape
    full = (r, N_CHIPS * c) if axis == 1 else (N_CHIPS * r, c)
    tr = 128
    while tr > 16 and tr * c * 4 > (1 << 20):
        tr //= 2
    if axis == 1:
        o_map = lambda i, s: (i, s[0])
    else:
        o_map = lambda i, s: (s[0] * (r // tr) + i, 0)

    def body(s_ref, w_ref, tok_ref, o_ref):
        o_ref[...] = (w_ref[...] + tok_ref[0:1, 0:1]).astype(BF16)

    return pl.pallas_call(
        body, name="cast_into", out_shape=jax.ShapeDtypeStruct(full, BF16),
        grid_spec=pltpu.PrefetchScalarGridSpec(
            num_scalar_prefetch=1, grid=(r // tr,),
            in_specs=[pl.BlockSpec((None, tr, c), lambda i, s: (idx, i, 0)),
                      pl.BlockSpec((8, 128), lambda i, s: (0, 0))],
            out_specs=pl.BlockSpec((tr, c), o_map)),
        compiler_params=_params(("parallel",)),
    )(chip, stacked, tok)


def _own_piece(ref, axis, me, cc):
    return _half(_chunk(ref, axis, me), axis, cc)


def _ag_start(fulls, axes):
    n = len(fulls)

    def body(*refs):
        ins = refs[:n]
        send, recv = refs[n:4 * n], refs[4 * n:7 * n]
        token = refs[8 * n]
        x, y, cc, chips = _place()
        me = _chip_index(x, y)
        for i in range(n):
            piece = _own_piece(ins[i], axes[i], me, cc)
            for k, chip in enumerate(chips):
                _remote(piece, piece, send[3 * i + k], recv[3 * i + k], (*chip, cc)).start()
        token[...] = jnp.zeros_like(token)

    sem = pltpu.SemaphoreType.DMA(())
    outs = pl.pallas_call(
        body, name="ag_start_%d" % n,
        out_shape=tuple([sem] * (6 * n) + [pltpu.HBM(f.shape, f.dtype) for f in fulls]
                        + [jax.ShapeDtypeStruct((8, 128), F32)]),
        in_specs=[HBM_SPEC] * n,
        out_specs=tuple([SEM_SPEC] * (6 * n) + [HBM_SPEC] * n + [pl.BlockSpec(memory_space=pltpu.VMEM)]),
        input_output_aliases={i: 6 * n + i for i in range(n)},
        compiler_params=pltpu.CompilerParams(has_side_effects=EFFECT),
    )(*[_hbm(f) for f in fulls])
    return outs[:3 * n], outs[3 * n:6 * n], outs[6 * n:7 * n], outs[7 * n]


def _ag_wait(send, recv, fulls, axes, after):
    n = len(fulls)

    def body(*refs):
        ins = refs[:n]
        send_s, recv_s = refs[n:4 * n], refs[4 * n:7 * n]
        x, y, cc, chips = _place()
        me = _chip_index(x, y)
        for i in range(n):
            mine = _own_piece(ins[i], axes[i], me, cc)
            for k, chip in enumerate(chips):
                got = _own_piece(ins[i], axes[i], _chip_index(*chip), cc)
                cp = _remote(mine, got, send_s[3 * i + k], recv_s[3 * i + k], (*chip, cc))
                cp.wait_send()
                cp.wait_recv()

    return pl.pallas_call(
        body, name="ag_wait_%d" % n,
        out_shape=tuple(pltpu.HBM(f.shape, f.dtype) for f in fulls),
        in_specs=[HBM_SPEC] * n + [SEM_SPEC] * (6 * n) + [ANY_SPEC],
        out_specs=tuple([HBM_SPEC] * n),
        input_output_aliases={i: i for i in range(n)},
        compiler_params=pltpu.CompilerParams(has_side_effects=EFFECT),
    )(*fulls, *send, *recv, after)


def _ag_forward(fulls, axes):
    n = len(fulls)

    def body(*refs):
        ins = refs[:n]
        send_sems, recv_sems = refs[2 * n], refs[2 * n + 1]
        x, y, cc, chips = _place()
        cps = []
        for i in range(n):
            for k, chip in enumerate(chips):
                got = _own_piece(ins[i], axes[i], _chip_index(*chip), cc)
                cp = _remote(got, got, send_sems.at[3 * i + k], recv_sems.at[3 * i + k], (x, y, 1 - cc))
                cp.start()
                cps.append(cp)
        for i in range(n):
            for k, chip in enumerate(chips):
                other = _own_piece(ins[i], axes[i], _chip_index(*chip), 1 - cc)
                cps[3 * i + k].wait_send()
                _remote(other, other, send_sems.at[3 * i + k], recv_sems.at[3 * i + k], (x, y, cc)).wait_recv()

    return pl.pallas_call(
        body, name="ag_forward_%d" % n,
        out_shape=tuple(jax.ShapeDtypeStruct(f.shape, f.dtype) for f in fulls),
        in_specs=[HBM_SPEC] * n, out_specs=tuple([HBM_SPEC] * n),
        input_output_aliases={i: i for i in range(n)},
        scratch_shapes=[pltpu.SemaphoreType.DMA((3 * n,)), pltpu.SemaphoreType.DMA((3 * n,))],
    )(*fulls)


def _rs_pair(gs):
    n = len(gs)

    def body(*refs):
        g_refs, o_refs = refs[:n], refs[n:2 * n]
        send_sems, recv_sems = refs[2 * n], refs[2 * n + 1]
        x, y, cc, _ = _place()
        cps = [_remote(g_refs[i], o_refs[i], send_sems.at[i], recv_sems.at[i], (x, y, 1 - cc)) for i in range(n)]
        for cp in cps:
            cp.start()
        for cp in cps:
            cp.wait()

    return pl.pallas_call(
        body, name="rs_pair_%d" % n, out_shape=tuple(jax.ShapeDtypeStruct(g.shape, g.dtype) for g in gs),
        in_specs=[HBM_SPEC] * n, out_specs=tuple([HBM_SPEC] * n),
        scratch_shapes=[pltpu.SemaphoreType.DMA((n,)), pltpu.SemaphoreType.DMA((n,))],
    )(*gs)


def _pair_sums(specs, core):
    calls = []
    for spec in specs:
        t = spec["a"].shape[0]
        m, n = spec["m"], spec["n"]
        if spec["axis"] == 1:
            dims = dict(m=m // 2, n=n, tm=min(TM, m // 2), tn=TN)
            operand = "a"
        else:
            dims = dict(m=m, n=n // 2, tm=TN, tn=n // 2)
            operand = "b"
        calls.append((spec, operand, dict(ta=True, k=t, tk=t, alpha=spec["alpha"], out_dtype=BF16, **dims)))
    sent = [_matmul([(spec["a"], spec["b"])], half=(core, False, operand), name=spec["name"] + "_sib", **kw)
            for spec, operand, kw in calls]
    got = _rs_pair(sent)
    return [_matmul([(spec["a"], spec["b"])], half=(core, True, operand), res=r, name=spec["name"], **kw)
            for (spec, operand, kw), r in zip(calls, got)]


def _piece_shape(p, axis):
    r, c = p.shape
    return (r, c // N_CHIPS) if axis == 1 else (r // N_CHIPS, c)


def _rs_chips_start(ps, axes):
    n = len(ps)
    lands = [lax.empty((3,) + _piece_shape(p, ax), p.dtype) for p, ax in zip(ps, axes)]

    def body(*refs):
        p_refs, land_refs = refs[:n], refs[n:2 * n]
        send, recv = refs[2 * n:5 * n], refs[5 * n:8 * n]
        token = refs[10 * n]
        x, y, cc, chips = _place()
        for i in range(n):
            for k, chip in enumerate(chips):
                _remote(_chunk(p_refs[i], axes[i], _chip_index(*chip)), land_refs[i].at[k],
                        send[3 * i + k], recv[3 * i + k], (*chip, cc)).start()
        token[...] = jnp.zeros_like(token)

    sem = pltpu.SemaphoreType.DMA(())
    outs = pl.pallas_call(
        body, name="rs_start_%d" % n,
        out_shape=tuple([sem] * (6 * n) + [pltpu.HBM(a.shape, a.dtype) for a in list(ps) + lands]
                        + [jax.ShapeDtypeStruct((8, 128), F32)]),
        in_specs=[HBM_SPEC] * (2 * n),
        out_specs=tuple([SEM_SPEC] * (6 * n) + [HBM_SPEC] * (2 * n) + [pl.BlockSpec(memory_space=pltpu.VMEM)]),
        input_output_aliases={i: 6 * n + i for i in range(2 * n)},
        compiler_params=pltpu.CompilerParams(has_side_effects=EFFECT),
    )(*[_hbm(a) for a in list(ps) + lands])
    return outs[:3 * n], outs[3 * n:6 * n], outs[6 * n:7 * n], outs[7 * n:8 * n], outs[8 * n][0, 0]


def _rs_chips_wait(send, recv, ps, lands, axes, after):
    n = len(ps)

    def body(*refs):
        p_refs, land_refs = refs[:n], refs[n:2 * n]
        send_s, recv_s = refs[2 * n:5 * n], refs[5 * n:8 * n]
        x, y, cc, chips = _place()
        for i in range(n):
            for k, chip in enumerate(chips):
                cp = _remote(_chunk(p_refs[i], axes[i], _chip_index(*chip)), land_refs[i].at[k],
                             send_s[3 * i + k], recv_s[3 * i + k], (*chip, cc))
                cp.wait_send()
                cp.wait_recv()

    outs = pl.pallas_call(
        body, name="rs_wait_%d" % n,
        out_shape=tuple(pltpu.HBM(a.shape, a.dtype) for a in list(ps) + list(lands)),
        in_specs=[HBM_SPEC] * (2 * n) + [SEM_SPEC] * (6 * n) + [ANY_SPEC],
        out_specs=tuple([HBM_SPEC] * (2 * n)),
        input_output_aliases={i: i for i in range(2 * n)},
        compiler_params=pltpu.CompilerParams(has_side_effects=EFFECT),
    )(*ps, *lands, *send, *recv, after)
    return outs[:n], outs[n:]


def _add_chips(p, got, chip, core, axis, *, idx, count, into, tr, tc):
    _, pr, pc = got.shape
    shard = (2 * pr, pc) if axis == 1 else (pr, 2 * pc)
    if axis == 1:
        p_map = lambda i, j, sc, so: (i, sc[0] * (pc // tc) + j)
        o_map = lambda i, j, sc, so: (idx, so[0] * (pr // tr) + i, j)
    else:
        p_map = lambda i, j, sc, so: (sc[0] * (pr // tr) + i, j)
        o_map = lambda i, j, sc, so: (idx, i, so[0] * (pc // tc) + j)

    def body(sc_ref, so_ref, p_ref, r_ref, *rest):
        o_ref = rest[-1]
        acc = p_ref[...].astype(F32)
        for k in range(3):
            acc = acc + r_ref[k].astype(F32)
        o_ref[...] = acc

    in_specs = [pl.BlockSpec((tr, tc), p_map), pl.BlockSpec((3, tr, tc), lambda i, j, sc, so: (0, i, j))]
    args = [chip, core, p, got]
    aliases = {}
    if into is not None:
        in_specs.append(ANY_SPEC)
        args.append(into)
        aliases = {4: 0}
    return pl.pallas_call(
        body, name="add_chips", out_shape=jax.ShapeDtypeStruct((count,) + shard, F32),
        grid_spec=pltpu.PrefetchScalarGridSpec(
            num_scalar_prefetch=2, grid=(pr // tr, pc // tc), in_specs=in_specs,
            out_specs=pl.BlockSpec((None, tr, tc), o_map)),
        input_output_aliases=aliases,
        compiler_params=_params(("parallel", "parallel")),
    )(*args)


def _rs_final(stacked, axis):
    n = stacked.shape[0]

    def body(s_ref, o_ref, send_sems, recv_sems):
        x, y, cc, _ = _place()
        cps = []
        for i in range(n):
            mine = _half(s_ref.at[i], axis, cc)
            cp = _remote(mine, mine, send_sems.at[i], recv_sems.at[i], (x, y, 1 - cc))
            cp.start()
            cps.append(cp)
        for i, cp in enumerate(cps):
            other = _half(s_ref.at[i], axis, 1 - cc)
            cp.wait_send()
            _remote(other, other, send_sems.at[i], recv_sems.at[i], (x, y, cc)).wait_recv()

    return pl.pallas_call(
        body, name="rs_final", out_shape=jax.ShapeDtypeStruct(stacked.shape, stacked.dtype),
        in_specs=[HBM_SPEC], out_specs=HBM_SPEC, input_output_aliases={0: 0},
        scratch_shapes=[pltpu.SemaphoreType.DMA((n,)), pltpu.SemaphoreType.DMA((n,))],
    )(stacked)


def _ag_small(packed):
    rows, cols = packed.shape

    def body(s_ref, o_ref, tok_ref, send_sems, recv_sems, local_sem):
        x, y, cc, chips = _place()
        me = _chip_index(x, y)
        own = pltpu.make_async_copy(s_ref, o_ref.at[me], local_sem)
        own.start()
        cps = [_remote(s_ref, o_ref.at[me], send_sems.at[k], recv_sems.at[k], (*chip, cc))
               for k, chip in enumerate(chips)]
        for cp in cps:
            cp.start()
        for k, chip in enumerate(chips):
            cps[k].wait_send()
            got = o_ref.at[_chip_index(*chip)]
            _remote(got, got, send_sems.at[k], recv_sems.at[k], (x, y, cc)).wait_recv()
        own.wait()
        tok_ref[...] = jnp.zeros_like(tok_ref)

    return pl.pallas_call(
        body, name="ag_small",
        out_shape=(jax.ShapeDtypeStruct((N_CHIPS, rows, cols), packed.dtype), jax.ShapeDtypeStruct((8, 128), F32)),
        in_specs=[HBM_SPEC], out_specs=(HBM_SPEC, pl.BlockSpec(memory_space=pltpu.VMEM)),
        scratch_shapes=[pltpu.SemaphoreType.DMA((3,)), pltpu.SemaphoreType.DMA((3,)), pltpu.SemaphoreType.DMA],
    )(packed)


def _rs_small(packed):
    _, rows, cols = packed.shape
    rels = [(bx, by, bc) for bx in (0, 1) for by in (0, 1) for bc in (0, 1)][1:]

    def body(s_ref, o_ref, send_sems, recv_sems, local_sem):
        x, y, cc, _ = _place()
        me = 4 * x + 2 * y + cc
        own = pltpu.make_async_copy(s_ref.at[_chip_index(x, y)], o_ref.at[me], local_sem)
        own.start()
        peers = [(jnp.bitwise_xor(x, bx), jnp.bitwise_xor(y, by), jnp.bitwise_xor(cc, bc)) for bx, by, bc in rels]
        cps = [_remote(s_ref.at[_chip_index(px, py)], o_ref.at[me], send_sems.at[k], recv_sems.at[k], (px, py, pc))
               for k, (px, py, pc) in enumerate(peers)]
        for cp in cps:
            cp.start()
        for k, (px, py, pc) in enumerate(peers):
            cps[k].wait_send()
            got = o_ref.at[4 * px + 2 * py + pc]
            _remote(got, got, send_sems.at[k], recv_sems.at[k], (x, y, cc)).wait_recv()
        own.wait()

    return pl.pallas_call(
        body, name="rs_small", out_shape=jax.ShapeDtypeStruct((2 * N_CHIPS, rows, cols), packed.dtype),
        in_specs=[HBM_SPEC], out_specs=HBM_SPEC,
        scratch_shapes=[pltpu.SemaphoreType.DMA((7,)), pltpu.SemaphoreType.DMA((7,)), pltpu.SemaphoreType.DMA],
    )(packed)


def _sum_slots(slots, *, tr=8):
    n, rows, cols = slots.shape

    def body(s_ref, o_ref):
        acc = s_ref[0]
        for k in range(1, n):
            acc = acc + s_ref[k]
        o_ref[...] = acc

    return pl.pallas_call(
        body, name="sum_slots", out_shape=jax.ShapeDtypeStruct((rows, cols), F32), grid=(rows // tr,),
        in_specs=[pl.BlockSpec((n, tr, cols), lambda i: (0, i, 0))], out_specs=pl.BlockSpec((tr, cols), lambda i: (i, 0)),
        compiler_params=_params(("parallel",)),
    )(slots)


MATRIX_AXIS = {"ffn_w1": 1, "ffn_w3": 1, "ffn_w2": 0, "ev_w_in": 1, "ev_w_out": 0, "od_w_in": 1, "od_w_out": 0}
SMALL_SHARDED = ("norm_g", "ev_conv_w", "od_conv_w", "od_conv_b", "od_ln_g", "od_ln_b", "od_pool_scale")
SMALL_REPLICATED = ("ev_q_gain", "ev_k_gain")
PACK_COLS = 1024
PACK_ROW_ALIGN = 8


def _pack(parts):
    flat = jnp.concatenate([p.reshape(-1).astype(F32) for p in parts])
    per = PACK_COLS * PACK_ROW_ALIGN
    total = -(-flat.shape[0] // per) * per
    return jnp.pad(flat, (0, total - flat.shape[0])).reshape(total // PACK_COLS, PACK_COLS)


def _unpack(packed, shapes):
    flat = packed.reshape(-1)
    out, pos = [], 0
    for shp in shapes:
        size = 1
        for s in shp:
            size *= s
        out.append(flat[pos:pos + size].reshape(shp))
        pos += size
    return out


def _row_tile(rows, cols, itemsize=4, budget=1 << 20):
    tr = 8
    while rows % (2 * tr) == 0 and 2 * tr * cols * itemsize <= budget:
        tr *= 2
    return tr


def _piece_tiles(hr, hc, axis):
    pr, pc = (hr, hc // N_CHIPS) if axis == 1 else (hr // N_CHIPS, hc)
    tr = 128
    while tr > 16 and tr * pc * 4 > (1 << 20):
        tr //= 2
    return tr, pc


AG_LOOKAHEAD = 2


def kernel(x, norm_g, ffn_w1, ffn_w3, ffn_w2, ev_w_in, ev_q_gain, ev_k_gain, ev_conv_w, ev_w_out, od_w_in, od_conv_w, od_conv_b, od_ln_g, od_ln_b, od_pool_w, od_pool_scale, od_w_out, loss_target, m_norm_g, m_ffn_w1, m_ffn_w3, m_ffn_w2, m_ev_w_in, m_ev_q_gain, m_ev_k_gain, m_ev_conv_w, m_ev_w_out, m_od_w_in, m_od_conv_w, m_od_conv_b, m_od_ln_g, m_od_ln_b, m_od_pool_w, m_od_pool_scale, m_od_w_out, v_norm_g, v_ffn_w1, v_ffn_w3, v_ffn_w2, v_ev_w_in, v_ev_q_gain, v_ev_k_gain, v_ev_conv_w, v_ev_w_out, v_od_w_in, v_od_conv_w, v_od_conv_b, v_od_ln_g, v_od_ln_b, v_od_pool_w, v_od_pool_scale, v_od_w_out):
    names = ["norm_g", "ffn_w1", "ffn_w3", "ffn_w2", "ev_w_in", "ev_q_gain", "ev_k_gain", "ev_conv_w", "ev_w_out",
             "od_w_in", "od_conv_w", "od_conv_b", "od_ln_g", "od_ln_b", "od_pool_w", "od_pool_scale", "od_w_out"]
    w = dict(zip(names, (norm_g, ffn_w1, ffn_w3, ffn_w2, ev_w_in, ev_q_gain, ev_k_gain, ev_conv_w, ev_w_out,
                         od_w_in, od_conv_w, od_conv_b, od_ln_g, od_ln_b, od_pool_w, od_pool_scale, od_w_out)))
    m = dict(zip(names, (m_norm_g, m_ffn_w1, m_ffn_w3, m_ffn_w2, m_ev_w_in, m_ev_q_gain, m_ev_k_gain, m_ev_conv_w,
                         m_ev_w_out, m_od_w_in, m_od_conv_w, m_od_conv_b, m_od_ln_g, m_od_ln_b, m_od_pool_w,
                         m_od_pool_scale, m_od_w_out)))
    v = dict(zip(names, (v_norm_g, v_ffn_w1, v_ffn_w3, v_ffn_w2, v_ev_w_in, v_ev_q_gain, v_ev_k_gain, v_ev_conv_w,
                         v_ev_w_out, v_od_w_in, v_od_conv_w, v_od_conv_b, v_od_ln_g, v_od_ln_b, v_od_pool_w,
                         v_od_pool_scale, v_od_w_out)))
    cx, cy, cc = lax.axis_index("x"), lax.axis_index("y"), lax.axis_index("c")
    cc_arr = jnp.reshape(cc, (1,)).astype(jnp.int32)
    chip_arr = jnp.reshape(_chip_index(cx, cy), (1,)).astype(jnp.int32)
    n_sub = 3 * norm_g.shape[0]
    stacked = {name: w[name].reshape((-1,) + w[name].shape[-2:]) for name in MATRIX_AXIS}

    def axes_of(s):
        return [MATRIX_AXIS[name] for name, _ in _sublayer_matrices(s)]

    ag_inflight = {}

    wts = {}
    small_names = SMALL_SHARDED + ("od_pool_w",)
    gathered, small_tok = _ag_small(_pack([w[k] for k in small_names]))
    per_chip = [_unpack(gathered[j], [w[k].shape for k in small_names]) for j in range(N_CHIPS)]
    for idx, k in enumerate(small_names):
        ax = 2 if k == "od_pool_w" else w[k].ndim - 1
        wts[k] = jnp.concatenate([per_chip[j][idx] for j in range(N_CHIPS)], axis=ax)
    for k in SMALL_REPLICATED:
        wts[k] = w[k]

    ag_tokens = [small_tok]

    def ag_start(s):
        fulls = [_cast_into(stacked[name], idx, chip_arr, MATRIX_AXIS[name], ag_tokens[-1])
                 for name, idx in _sublayer_matrices(s)]
        send, recv, fulls, tok = _ag_start(fulls, axes_of(s))
        ag_inflight[s] = (send, recv, fulls)
        ag_tokens.append(tok)
        return tok[0, 0]

    x0 = x[0]
    first_tok = sum(ag_start(s) for s in range(min(AG_LOOKAHEAD, n_sub)))

    def fetch(s, after):
        tok = first_tok if s == 0 else 0.0
        if s + AG_LOOKAHEAD < n_sub:
            tok = tok + ag_start(s + AG_LOOKAHEAD)
        send, recv, fulls = ag_inflight.pop(s)
        fulls = _ag_wait(send, recv, fulls, axes_of(s), after)
        return list(_ag_forward(fulls, axes_of(s))), tok

    rs_inflight = {}

    def grads_hook(s, specs):
        send, recv, ps, lands, tok = _rs_chips_start(_pair_sums(specs, cc_arr), axes_of(s))
        rs_inflight[s] = (send, recv, ps, lands)
        return tok

    sq, dx, gr = _local_step(x0, loss_target[0], wts, fetch, grads_hook)

    grads = {}
    reduced = {name: None for name in MATRIX_AXIS}
    for s in reversed(range(n_sub)):
        send, recv, ps, lands = rs_inflight.pop(s)
        axes = axes_of(s)
        ps, lands = _rs_chips_wait(send, recv, ps, lands, axes, dx)
        for (name, idx), p, land, ax in zip(_sublayer_matrices(s), ps, lands, axes):
            tr, tc = _piece_tiles(*p.shape, ax)
            reduced[name] = _add_chips(p, land, chip_arr, cc_arr, ax, idx=idx, count=stacked[name].shape[0],
                                       into=reduced[name], tr=tr, tc=tc)
    for name, axis in MATRIX_AXIS.items():
        grads[name] = _rs_final(reduced[name], axis).reshape(w[name].shape)
    small_full = {k: (gr[k] if k == "norm_g" else jnp.stack(gr[k])) for k in small_names + SMALL_REPLICATED}
    chunks = []
    for j in range(N_CHIPS):
        parts = []
        for k in small_names:
            ax = 2 if k == "od_pool_w" else w[k].ndim - 1
            size = w[k].shape[ax]
            parts.append(lax.slice_in_dim(small_full[k], j * size, (j + 1) * size, axis=ax))
        parts += [small_full[k] for k in SMALL_REPLICATED] + [sq[0, :1]]
        chunks.append(_pack(parts))
    summed = _sum_slots(_rs_small(jnp.stack(chunks)))
    pack_names = small_names + SMALL_REPLICATED
    unpacked = _unpack(summed, [w[k].shape for k in pack_names] + [(1,)])
    for k, g in zip(pack_names, unpacked):
        grads[k] = g
    loss = (0.5 / x.shape[-1]) * unpacked[-1][0]

    delta, new_m, new_v = {}, {}, {}
    for name in MATRIX_AXIS:
        shp = w[name].shape
        cols = shp[-1]
        rows = w[name].size // cols
        tr = _row_tile(rows, cols, budget=1 << 20)
        d_, m_, v_ = _adamw(w[name].reshape(rows, cols), grads[name].reshape(rows, cols),
                            m[name].reshape(rows, cols), v[name].reshape(rows, cols), tr=tr)
        delta[name], new_m[name], new_v[name] = d_.reshape(shp), m_.reshape(shp), v_.reshape(shp)
    pw, pg, pm, pv = (_pack([t[k] for k in pack_names]) for t in (w, grads, m, v))
    d_, m_, v_ = _adamw(pw, pg, pm, pv, tr=PACK_ROW_ALIGN)
    shapes = [w[k].shape for k in pack_names]
    for store, packed in ((delta, d_), (new_m, m_), (new_v, v_)):
        for k, a in zip(pack_names, _unpack(packed, shapes)):
            store[k] = a

    return (loss, dx[None], *[grads[k] for k in names], *[delta[k] for k in names],
            *[new_m[k] for k in names], *[new_v[k] for k in names])
```

```python
import jax
import jax.numpy as jnp
from jax import lax
from jax.experimental import pallas as pl
from jax.experimental.pallas import tpu as pltpu

F32 = jnp.float32
BF16 = jnp.bfloat16
MESH = pl.DeviceIdType.MESH

EPS = 1e-6
HEADS = 8
HEAD_DIM = 128
A_WIDTH = HEADS * HEAD_DIM
A_WINDOWS = (128, 512, 2048)
A_DILATIONS = (1, 4, 16)
ATTN_BLOCK = 128
B_CONV = 3
C_CONV = 31
D_WINDOWS = (2, 4, 8, 16)
HALO = 32
N_CHIPS = 4
ADAM_LR = 0.001
ADAM_B1 = 0.9
ADAM_B2 = 0.999
ADAM_EPS = 1e-08
ADAM_WD = 0.01
ADAM_STEP = 10
VMEM_LIMIT_BYTES = 56 * 1024 * 1024
NEG_BIG = -1e30


def _params(sem, **kw):
    return pltpu.CompilerParams(dimension_semantics=sem, vmem_limit_bytes=VMEM_LIMIT_BYTES, **kw)


def _sigmoid(x):
    return 1.0 / (1.0 + jnp.exp(-x))


EPILOGUE_COLS = 256


def _matmul(pairs, *, m, n, k, tm, tn, tk, ta=False, tb=False, out_dtype=F32, res=None,
            alpha=1.0, name, j_outer=False, half=None, tok=None):
    nk = k // tk
    npairs = len(pairs)
    dn = (((0 if ta else 1,), (1 if tb else 0,)), ((), ()))

    def ij(p, q):
        return (q, p) if j_outer else (p, q)

    def shift(s, operand, blocks):
        if half is None or half[2] != operand:
            return 0
        h = s[0][0]
        return (h if half[1] else 1 - h) * blocks

    def a_map(p, q, kk, *s):
        i = ij(p, q)[0] + shift(s, "a", m // tm)
        return (kk, i) if ta else (i, kk)

    def b_map(p, q, kk, *s):
        j = ij(p, q)[1] + shift(s, "b", n // tn)
        return (j, kk) if tb else (kk, j)

    def o_map(p, q, kk, *s):
        return ij(p, q)

    def body(*refs):
        if half is not None:
            refs = refs[1:]
        ab = refs[:2 * npairs]
        pos = 2 * npairs
        res_ref = tok_ref = None
        if res is not None:
            res_ref = refs[pos]
            pos += 1
        if tok is not None:
            tok_ref = refs[pos]
            pos += 1
        o_ref = refs[pos]
        acc_ref = refs[pos + 1] if nk > 1 else None

        def dots():
            tot = None
            for p in range(npairs):
                d = lax.dot_general(ab[2 * p][...], ab[2 * p + 1][...], dn, preferred_element_type=F32)
                tot = d if tot is None else tot + d
            return tot

        def finish(acc):
            r = acc * alpha if alpha != 1.0 else acc
            if res_ref is not None:
                r = res_ref[...].astype(F32) + r
            if tok_ref is not None:
                r = r + tok_ref[0:1, 0:1]
            o_ref[...] = r.astype(o_ref.dtype)

        if nk == 1:
            finish(dots())
        else:
            kk = pl.program_id(2)

            @pl.when(kk == 0)
            def _():
                acc_ref[...] = dots()

            @pl.when(kk > 0)
            def _():
                acc_ref[...] += dots()

            @pl.when(kk == nk - 1)
            def _():
                finish(acc_ref[...])

    a_spec = pl.BlockSpec((tk, tm) if ta else (tm, tk), a_map)
    b_spec = pl.BlockSpec((tn, tk) if tb else (tk, tn), b_map)
    o_spec = pl.BlockSpec((tm, tn), o_map)
    in_specs = [a_spec, b_spec] * npairs
    args = [t for p in pairs for t in p]
    if res is not None:
        in_specs.append(o_spec)
        args.append(res)
    if tok is not None:
        in_specs.append(pl.BlockSpec((8, 128), lambda p, q, kk, *s: (0, 0)))
        args.append(jnp.full((8, 128), tok, F32))
    grid = ij(m // tm, n // tn) + (nk,)
    scratch = [pltpu.VMEM((tm, tn), F32)] if nk > 1 else []
    if half is None:
        kwargs = dict(grid=grid, in_specs=in_specs, out_specs=o_spec, scratch_shapes=scratch)
    else:
        args = [half[0]] + args
        kwargs = dict(grid_spec=pltpu.PrefetchScalarGridSpec(
            num_scalar_prefetch=1, grid=grid, in_specs=in_specs, out_specs=o_spec, scratch_shapes=scratch))
    return pl.pallas_call(
        body, name=name, out_shape=jax.ShapeDtypeStruct((m, n), out_dtype),
        compiler_params=_params(("parallel", "parallel", "arbitrary")), **kwargs,
    )(*args)


def _ffn_up(xn, w1, w3, *, tm, tn):
    t, d = xn.shape
    f = w1.shape[1]

    def body(x_ref, w1_ref, w3_ref, a_ref, b_ref, h_ref):
        x = x_ref[...]
        for c in range(tn // EPILOGUE_COLS):
            cols = slice(c * EPILOGUE_COLS, (c + 1) * EPILOGUE_COLS)
            a = jnp.dot(x, w1_ref[:, cols], preferred_element_type=F32)
            b = jnp.dot(x, w3_ref[:, cols], preferred_element_type=F32)
            a_ref[:, cols] = a.astype(BF16)
            b_ref[:, cols] = b.astype(BF16)
            h_ref[:, cols] = (a * _sigmoid(a) * b).astype(BF16)

    x_spec = pl.BlockSpec((tm, d), lambda i, j: (i, 0))
    w_spec = pl.BlockSpec((d, tn), lambda i, j: (0, j))
    o_spec = pl.BlockSpec((tm, tn), lambda i, j: (i, j))
    shp = jax.ShapeDtypeStruct((t, f), BF16)
    return pl.pallas_call(
        body, name="ffn_up", out_shape=(shp, shp, shp), grid=(t // tm, f // tn),
        in_specs=[x_spec, w_spec, w_spec], out_specs=(o_spec, o_spec, o_spec),
        compiler_params=_params(("parallel", "parallel")),
    )(xn, w1, w3)


def _ffn_dh(dyb, w2, a, b, tok, *, tm, tn):
    t, d = dyb.shape
    f = w2.shape[0]
    dn = (((1,), (1,)), ((), ()))

    def body(dy_ref, w2_ref, a_ref, b_ref, tok_ref, da_ref, db_ref):
        dy = dy_ref[...]
        for c in range(tn // EPILOGUE_COLS):
            cols = slice(c * EPILOGUE_COLS, (c + 1) * EPILOGUE_COLS)
            dh = 0.5 * lax.dot_general(dy, w2_ref[cols, :], dn, preferred_element_type=F32) + tok_ref[0:1, 0:1]
            av = a_ref[:, cols].astype(F32)
            bv = b_ref[:, cols].astype(F32)
            sig = _sigmoid(av)
            da_ref[:, cols] = (dh * bv * (sig * (1.0 + av * (1.0 - sig)))).astype(BF16)
            db_ref[:, cols] = (dh * (av * sig)).astype(BF16)

    dy_spec = pl.BlockSpec((tm, d), lambda i, j: (i, 0))
    w_spec = pl.BlockSpec((tn, d), lambda i, j: (j, 0))
    o_spec = pl.BlockSpec((tm, tn), lambda i, j: (i, j))
    shp = jax.ShapeDtypeStruct((t, f), BF16)
    return pl.pallas_call(
        body, name="ffn_dh", out_shape=(shp, shp), grid=(t // tm, f // tn),
        in_specs=[dy_spec, w_spec, o_spec, o_spec, pl.BlockSpec((8, 128), lambda i, j: (0, 0))],
        out_specs=(o_spec, o_spec),
        compiler_params=_params(("parallel", "parallel")),
    )(dyb, w2, a, b, jnp.full((8, 128), tok, F32))


def _rmsnorm_fwd(x, g, *, tr=256):
    t, d = x.shape

    def body(x_ref, g_ref, o_ref):
        xv = x_ref[...]
        y = xv * lax.rsqrt(jnp.mean(xv * xv, axis=-1, keepdims=True) + EPS)
        o_ref[...] = (y * g_ref[...]).astype(BF16)

    return pl.pallas_call(
        body, name="rmsnorm_fwd", out_shape=jax.ShapeDtypeStruct((t, d), BF16), grid=(t // tr,),
        in_specs=[pl.BlockSpec((tr, d), lambda i: (i, 0)), pl.BlockSpec((1, d), lambda i: (0, 0))],
        out_specs=pl.BlockSpec((tr, d), lambda i: (i, 0)),
        compiler_params=_params(("parallel",)),
    )(x, g.reshape(1, d))


def _rmsnorm_bwd(dy, x, g, dres, *, tr=256):
    t, d = x.shape

    def body(dy_ref, x_ref, g_ref, dres_ref, dx_ref, dxb_ref, dg_ref):
        xv = x_ref[...]
        dyv = dy_ref[...].astype(F32)
        r = lax.rsqrt(jnp.mean(xv * xv, axis=-1, keepdims=True) + EPS)
        xhat = xv * r
        dxhat = dyv * g_ref[...]
        c = jnp.mean(dxhat * xhat, axis=-1, keepdims=True)
        dx = dres_ref[...] + r * (dxhat - xhat * c)
        dx_ref[...] = dx
        dxb_ref[...] = dx.astype(BF16)
        part = jnp.sum(dyv * xhat, axis=0, keepdims=True)

        @pl.when(pl.program_id(0) == 0)
        def _():
            dg_ref[...] = part

        @pl.when(pl.program_id(0) > 0)
        def _():
            dg_ref[...] += part

    row = pl.BlockSpec((tr, d), lambda i: (i, 0))
    vec = pl.BlockSpec((1, d), lambda i: (0, 0))
    dx, dxb, dg = pl.pallas_call(
        body, name="rmsnorm_bwd",
        out_shape=(jax.ShapeDtypeStruct((t, d), F32), jax.ShapeDtypeStruct((t, d), BF16),
                   jax.ShapeDtypeStruct((1, d), F32)),
        grid=(t // tr,), in_specs=[row, row, vec, row], out_specs=(row, row, vec),
        compiler_params=_params(("arbitrary",)),
    )(dy, x, g.reshape(1, d), dres)
    return dx, dxb, dg.reshape(d)


def _loss_head(y, target, *, tr=256):
    t, d = y.shape

    def body(y_ref, t_ref, dy_ref, dyb_ref, s_ref):
        err = y_ref[...] - t_ref[...]
        dy = err * (1.0 / d)
        dy_ref[...] = dy
        dyb_ref[...] = dy.astype(BF16)
        part = jnp.full((1, 128), jnp.sum(err * err), F32)

        @pl.when(pl.program_id(0) == 0)
        def _():
            s_ref[...] = part

        @pl.when(pl.program_id(0) > 0)
        def _():
            s_ref[...] += part

    row = pl.BlockSpec((tr, d), lambda i: (i, 0))
    return pl.pallas_call(
        body, name="loss_head",
        out_shape=(jax.ShapeDtypeStruct((t, d), F32), jax.ShapeDtypeStruct((t, d), BF16),
                   jax.ShapeDtypeStruct((1, 128), F32)),
        grid=(t // tr,), in_specs=[row, row],
        out_specs=(row, row, pl.BlockSpec((1, 128), lambda i: (0, 0))),
        compiler_params=_params(("arbitrary",)),
    )(y, target)


def _adamw(w, g, m, v, *, tr):
    rows, cols = w.shape

    def body(w_ref, g_ref, m_ref, v_ref, d_ref, nm_ref, nv_ref, go_ref):
        gv = g_ref[...]
        go_ref[...] = gv
        nm = ADAM_B1 * m_ref[...] + (1.0 - ADAM_B1) * gv
        nv = ADAM_B2 * v_ref[...] + (1.0 - ADAM_B2) * jnp.square(gv)
        m_hat = nm / (1.0 - ADAM_B1 ** ADAM_STEP)
        v_hat = nv / (1.0 - ADAM_B2 ** ADAM_STEP)
        d_ref[...] = -ADAM_LR * (m_hat / (jnp.sqrt(v_hat) + ADAM_EPS) + ADAM_WD * w_ref[...])
        nm_ref[...] = nm
        nv_ref[...] = nv

    spec = pl.BlockSpec((tr, cols), lambda i: (i, 0))
    shp = jax.ShapeDtypeStruct((rows, cols), F32)
    return pl.pallas_call(
        body, name="adamw", out_shape=(shp, shp, shp, shp), grid=(rows // tr,),
        in_specs=[spec] * 4, out_specs=(spec, spec, spec, spec),
        compiler_params=_params(("parallel",)),
    )(w, g, m, v)


def _headnorm(xf, g):
    r = lax.rsqrt(jnp.mean(xf * xf, axis=-1, keepdims=True) + EPS)
    xhat = xf * r
    return xhat * g, xhat, r


def _headnorm_bwd(dn, xhat, r, g):
    dxhat = dn * g
    return r * (dxhat - xhat * jnp.mean(dxhat * xhat, axis=-1, keepdims=True))


_NT = (((1,), (1,)), ((), ()))
_TN = (((0,), (0,)), ((), ()))


def _attn_masks(n, nb):
    qi = lax.broadcasted_iota(jnp.int32, (ATTN_BLOCK, ATTN_BLOCK), 0)
    ci = lax.broadcasted_iota(jnp.int32, (ATTN_BLOCK, ATTN_BLOCK), 1)
    d_prev = qi + ATTN_BLOCK - ci
    d_cur = qi - ci
    return d_prev, d_cur, (ci >= qi), (ci <= qi)


def _head_lane(tile, h):
    lane = lax.broadcasted_iota(jnp.int32, tile.shape, 1)
    return jnp.sum(jnp.where(lane == h, tile, 0.0), axis=-1, keepdims=True)


def _set_head_lane(tile, h, col):
    lane = lax.broadcasted_iota(jnp.int32, tile.shape, 1)
    return jnp.where(lane == h, col, tile)


def _attn_fwd(qv, kv, vv, offs, qg, kg, *, dil):
    l = qv.shape[0]
    nb = l // ATTN_BLOCK
    scale = HEAD_DIM ** -0.5

    def body(q_ref, kp_ref, kc_ref, vp_ref, vc_ref, qg_ref, kg_ref, o_ref, lse_ref):
        n = pl.program_id(1)
        d_prev, d_cur, ok_prev, ok_cur = _attn_masks(n, nb)
        ok_prev = ok_prev & (n > 0)
        b_prev = d_prev.astype(F32) * float(dil)
        b_cur = d_cur.astype(F32) * float(dil)
        lse = jnp.zeros((ATTN_BLOCK, HEAD_DIM), F32)
        for h in range(HEADS):
            sl = slice(h * HEAD_DIM, (h + 1) * HEAD_DIM)
            slope = 2.0 ** (-8.0 * (h + 1) / HEADS)
            q = _headnorm(q_ref[:, sl].astype(F32), qg_ref[...])[0].astype(BF16)
            kp = _headnorm(kp_ref[:, sl].astype(F32), kg_ref[...])[0].astype(BF16)
            kc = _headnorm(kc_ref[:, sl].astype(F32), kg_ref[...])[0].astype(BF16)
            s1 = lax.dot_general(q, kp, _NT, preferred_element_type=F32) * scale
            s2 = lax.dot_general(q, kc, _NT, preferred_element_type=F32) * scale
            s1 = jnp.where(ok_prev, s1 - slope * b_prev, NEG_BIG)
            s2 = jnp.where(ok_cur, s2 - slope * b_cur, NEG_BIG)
            m = jnp.maximum(jnp.max(s1, axis=-1, keepdims=True), jnp.max(s2, axis=-1, keepdims=True))
            p1 = jnp.exp(s1 - m)
            p2 = jnp.exp(s2 - m)
            den = jnp.sum(p1, axis=-1, keepdims=True) + jnp.sum(p2, axis=-1, keepdims=True)
            inv = 1.0 / den
            o = jnp.dot((p1 * inv).astype(BF16), vp_ref[:, sl], preferred_element_type=F32)
            o = o + jnp.dot((p2 * inv).astype(BF16), vc_ref[:, sl], preferred_element_type=F32)
            o_ref[:, sl] = o
            lse = _set_head_lane(lse, h, m + jnp.log(den))
        lse_ref[...] = lse

    cur, prev, _ = _attn_specs(nb)
    vec = pl.BlockSpec((1, HEAD_DIM), lambda r, n: (0, 0))
    return pl.pallas_call(
        body, name="attn_fwd_d%d" % dil,
        out_shape=(jax.ShapeDtypeStruct((l, dil * A_WIDTH), F32), jax.ShapeDtypeStruct((l, dil * HEAD_DIM), F32)),
        grid=(dil, nb),
        in_specs=[cur(offs[0]), prev(offs[1]), cur(offs[1]), prev(offs[2]), cur(offs[2]), vec, vec],
        out_specs=(cur(0), cur(0, HEAD_DIM)),
        compiler_params=_params(("parallel", "parallel")),
    )(qv, kv, kv, vv, vv, qg.reshape(1, HEAD_DIM), kg.reshape(1, HEAD_DIM))


def _attn_specs(nb):
    def cur(off, width=A_WIDTH):
        return pl.BlockSpec((ATTN_BLOCK, width), lambda r, n: (n, off + r))

    def prev(off, width=A_WIDTH):
        return pl.BlockSpec((ATTN_BLOCK, width), lambda r, n: (jnp.maximum(n - 1, 0), off + r))

    def nxt(off, width=A_WIDTH):
        return pl.BlockSpec((ATTN_BLOCK, width), lambda r, n: (jnp.minimum(n + 1, nb - 1), off + r))

    return cur, prev, nxt


def _attn_combine(outs, lses, *, tr=256):
    t, w = outs[0].shape

    def body(o0, o1, o2, l0, l1, l2, y_ref, lse_ref):
        a0, a1, a2 = l0[...], l1[...], l2[...]
        m = jnp.maximum(jnp.maximum(a0, a1), a2)
        e0, e1, e2 = jnp.exp(a0 - m), jnp.exp(a1 - m), jnp.exp(a2 - m)
        s = e0 + e1 + e2
        inv = 1.0 / s
        w0, w1, w2 = e0 * inv, e1 * inv, e2 * inv
        lse_ref[...] = m + jnp.log(s)
        for h in range(HEADS):
            sl = slice(h * HEAD_DIM, (h + 1) * HEAD_DIM)
            y = (_head_lane(w0, h) * o0[:, sl] + _head_lane(w1, h) * o1[:, sl] + _head_lane(w2, h) * o2[:, sl])
            y_ref[:, sl] = y.astype(BF16)

    row = pl.BlockSpec((tr, w), lambda i: (i, 0))
    stat = pl.BlockSpec((tr, HEAD_DIM), lambda i: (i, 0))
    return pl.pallas_call(
        body, name="attn_combine",
        out_shape=(jax.ShapeDtypeStruct((t, w), BF16), jax.ShapeDtypeStruct((t, HEAD_DIM), F32)),
        grid=(t // tr,), in_specs=[row] * 3 + [stat] * 3, out_specs=(row, stat),
        compiler_params=_params(("parallel",)),
    )(*outs, *lses)


def _attn_delta(dy, y, *, tr=256):
    t, w = y.shape

    def body(dy_ref, y_ref, o_ref):
        out = jnp.zeros((tr, HEAD_DIM), F32)
        for h in range(HEADS):
            sl = slice(h * HEAD_DIM, (h + 1) * HEAD_DIM)
            dlt = jnp.sum(dy_ref[:, sl] * y_ref[:, sl].astype(F32), axis=-1, keepdims=True)
            out = _set_head_lane(out, h, dlt)
        o_ref[...] = out

    row = pl.BlockSpec((tr, w), lambda i: (i, 0))
    return pl.pallas_call(
        body, name="attn_delta", out_shape=jax.ShapeDtypeStruct((t, HEAD_DIM), F32), grid=(t // tr,),
        in_specs=[row, row], out_specs=pl.BlockSpec((tr, HEAD_DIM), lambda i: (i, 0)),
        compiler_params=_params(("parallel",)),
    )(dy, y)


def _attn_bwd(qv, kv, vv, dyv, offs, lsev, dltv, qg, kg, *, dil):
    l = qv.shape[0]
    w = dil * A_WIDTH
    nb = l // ATTN_BLOCK
    scale = HEAD_DIM ** -0.5

    def body(qc_ref, qn_ref, kp_ref, kc_ref, vp_ref, vc_ref, dyc_ref, dyn_ref, lc_ref, ln_ref,
             dc_ref, dn_ref, qg_ref, kg_ref, dq_ref, dk_ref, dv_ref, dqg_ref, dkg_ref):
        n = pl.program_id(1)
        first = (pl.program_id(0) == 0) & (n == 0)
        d_prev, d_cur, ok_prev, ok_cur = _attn_masks(n, nb)
        ok_t1 = ok_prev & (n > 0)
        ok_t3 = ok_prev & (n < nb - 1)
        b_prev = d_prev.astype(F32) * float(dil)
        b_cur = d_cur.astype(F32) * float(dil)
        qgv, kgv = qg_ref[...], kg_ref[...]
        dqg = jnp.zeros((1, HEAD_DIM), F32)
        dkg = jnp.zeros((1, HEAD_DIM), F32)
        for h in range(HEADS):
            sl = slice(h * HEAD_DIM, (h + 1) * HEAD_DIM)
            slope = 2.0 ** (-8.0 * (h + 1) / HEADS)
            qc, qc_hat, qc_r = _headnorm(qc_ref[:, sl].astype(F32), qgv)
            qn = _headnorm(qn_ref[:, sl].astype(F32), qgv)[0].astype(BF16)
            kp = _headnorm(kp_ref[:, sl].astype(F32), kgv)[0].astype(BF16)
            kc, kc_hat, kc_r = _headnorm(kc_ref[:, sl].astype(F32), kgv)
            qc = qc.astype(BF16)
            kc = kc.astype(BF16)
            vp, vc = vp_ref[:, sl], vc_ref[:, sl]
            dyc, dyn = dyc_ref[:, sl].astype(BF16), dyn_ref[:, sl].astype(BF16)

            def tile(q, k, v, dy, lse, dlt, ok, bias):
                s = lax.dot_general(q, k, _NT, preferred_element_type=F32) * scale
                p = jnp.where(ok, jnp.exp(jnp.where(ok, s - slope * bias, NEG_BIG) - lse), 0.0)
                dp = lax.dot_general(dy, v, _NT, preferred_element_type=F32)
                return p.astype(BF16), (p * (dp - dlt)).astype(BF16)

            lse_c, dlt_c = _head_lane(lc_ref[...], h), _head_lane(dc_ref[...], h)
            p1, ds1 = tile(qc, kp, vp, dyc, lse_c, dlt_c, ok_t1, b_prev)
            p2, ds2 = tile(qc, kc, vc, dyc, lse_c, dlt_c, ok_cur, b_cur)
            p3, ds3 = tile(qn, kc, vc, dyn, _head_lane(ln_ref[...], h), _head_lane(dn_ref[...], h), ok_t3, b_prev)
            dqn = scale * (jnp.dot(ds1, kp, preferred_element_type=F32) + jnp.dot(ds2, kc, preferred_element_type=F32))
            dkn = scale * (lax.dot_general(ds2, qc, _TN, preferred_element_type=F32)
                           + lax.dot_general(ds3, qn, _TN, preferred_element_type=F32))
            dv = (lax.dot_general(p2, dyc, _TN, preferred_element_type=F32)
                  + lax.dot_general(p3, dyn, _TN, preferred_element_type=F32))
            dqg = dqg + jnp.sum(dqn * qc_hat, axis=0, keepdims=True)
            dkg = dkg + jnp.sum(dkn * kc_hat, axis=0, keepdims=True)
            dq_ref[:, sl] = _headnorm_bwd(dqn, qc_hat, qc_r, qgv).astype(BF16)
            dk_ref[:, sl] = _headnorm_bwd(dkn, kc_hat, kc_r, kgv).astype(BF16)
            dv_ref[:, sl] = dv.astype(BF16)

        @pl.when(first)
        def _():
            dqg_ref[...] = dqg
            dkg_ref[...] = dkg

        @pl.when(jnp.logical_not(first))
        def _():
            dqg_ref[...] += dqg
            dkg_ref[...] += dkg

    cur, prev, nxt = _attn_specs(nb)
    o_q, o_k, o_v, o_dy = offs
    stat_c, stat_n = cur(0, HEAD_DIM), nxt(0, HEAD_DIM)
    vec = pl.BlockSpec((1, HEAD_DIM), lambda r, n: (0, 0))
    shp = jax.ShapeDtypeStruct((l, w), BF16)
    gshp = jax.ShapeDtypeStruct((1, HEAD_DIM), F32)
    return pl.pallas_call(
        body, name="attn_bwd_d%d" % dil, out_shape=(shp, shp, shp, gshp, gshp), grid=(dil, nb),
        in_specs=[cur(o_q), nxt(o_q), prev(o_k), cur(o_k), prev(o_v), cur(o_v), cur(o_dy), nxt(o_dy),
                  stat_c, stat_n, stat_c, stat_n, vec, vec],
        out_specs=(cur(0), cur(0), cur(0), vec, vec),
        compiler_params=_params(("arbitrary", "arbitrary")),
    )(qv, qv, kv, kv, vv, vv, dyv, dyv, lsev, lsev, dltv, dltv,
      qg.reshape(1, HEAD_DIM), kg.reshape(1, HEAD_DIM))


def _prev_halo(tr, tc, col0):
    return pl.BlockSpec((HALO, tc), lambda j, i: (jnp.maximum(i * (tr // HALO) - 1, 0), col0 + j))


def _next_halo(tr, tc, col0, rows):
    last = rows // HALO - 1
    return pl.BlockSpec((HALO, tc), lambda j, i: (jnp.minimum((i + 1) * (tr // HALO), last), col0 + j))


def _cur_block(tr, tc, col0):
    return pl.BlockSpec((tr, tc), lambda j, i: (i, col0 + j))


def _gateconv_fwd(h, conv_w, *, col0, tr=512, tc=256):
    t = h.shape[0]
    width = conv_w.shape[1]
    nc = width // tc
    c0 = col0 // tc

    def body(bg_ref, cg_ref, xt_ref, cgh_ref, xth_ref, w_ref, y_ref, pad_ref):
        i = pl.program_id(1)
        halo = cgh_ref[...].astype(F32) * xth_ref[...].astype(F32)
        pad_ref[0:HALO, :] = jnp.where(i > 0, halo, 0.0)
        pad_ref[HALO:HALO + tr, :] = cg_ref[...].astype(F32) * xt_ref[...].astype(F32)
        conv = None
        for j in range(B_CONV):
            term = w_ref[j:j + 1, :] * pad_ref[HALO - (B_CONV - 1) + j:HALO - (B_CONV - 1) + j + tr, :]
            conv = term if conv is None else conv + term
        y_ref[...] = (bg_ref[...].astype(F32) * conv).astype(BF16)

    return pl.pallas_call(
        body, name="gateconv_fwd", out_shape=jax.ShapeDtypeStruct((t, width), BF16), grid=(nc, t // tr),
        in_specs=[_cur_block(tr, tc, c0), _cur_block(tr, tc, c0 + nc), _cur_block(tr, tc, c0 + 2 * nc),
                  _prev_halo(tr, tc, c0 + nc), _prev_halo(tr, tc, c0 + 2 * nc),
                  pl.BlockSpec((8, tc), lambda j, i: (0, j))],
        out_specs=_cur_block(tr, tc, 0),
        scratch_shapes=[pltpu.VMEM((HALO + tr, tc), F32)],
        compiler_params=_params(("parallel", "arbitrary")),
    )(h, h, h, h, h, _pad_rows(conv_w, 8))


def _pad_rows(w, rows):
    return jnp.pad(w, ((0, rows - w.shape[0]), (0, 0)))


def _gateconv_bwd(h, dy, conv_w, *, col0, dcol0, tr=512, tc=256):
    t = h.shape[0]
    width = conv_w.shape[1]
    nc = width // tc
    c0 = col0 // tc
    dc0 = dcol0 // tc
    nt = t // tr

    def body(bg_ref, cg_ref, xt_ref, cgh_ref, xth_ref, bgn_ref, dy_ref, dyn_ref, w_ref,
             dbg_ref, dcg_ref, dxt_ref, dw_ref, pad_ref, padd_ref):
        i = pl.program_id(1)
        cg = cg_ref[...].astype(F32)
        xt = xt_ref[...].astype(F32)
        bg = bg_ref[...].astype(F32)
        dyv = dy_ref[...]
        halo = cgh_ref[...].astype(F32) * xth_ref[...].astype(F32)
        pad_ref[0:HALO, :] = jnp.where(i > 0, halo, 0.0)
        pad_ref[HALO:HALO + tr, :] = cg * xt
        dconv = dyv * bg
        padd_ref[0:tr, :] = dconv
        padd_ref[tr:tr + HALO, :] = jnp.where(i < nt - 1, dyn_ref[...] * bgn_ref[...].astype(F32), 0.0)
        conv = None
        du = None
        dws = []
        for j in range(B_CONV):
            off = HALO - (B_CONV - 1) + j
            shifted = pad_ref[off:off + tr, :]
            term = w_ref[j:j + 1, :] * shifted
            conv = term if conv is None else conv + term
            dws.append(jnp.sum(dconv * shifted, axis=0, keepdims=True))
            back = w_ref[j:j + 1, :] * padd_ref[B_CONV - 1 - j:B_CONV - 1 - j + tr, :]
            du = back if du is None else du + back
        dbg_ref[...] = (dyv * conv).astype(BF16)
        dcg_ref[...] = (du * xt).astype(BF16)
        dxt_ref[...] = (du * cg).astype(BF16)
        dw = _stack_rows(dws, 8, tc)

        @pl.when(i == 0)
        def _():
            dw_ref[...] = dw

        @pl.when(i > 0)
        def _():
            dw_ref[...] += dw

    oshp = jax.ShapeDtypeStruct((t, width), BF16)
    return pl.pallas_call(
        body, name="gateconv_bwd",
        out_shape=(oshp, oshp, oshp, jax.ShapeDtypeStruct((8, width), F32)), grid=(nc, nt),
        in_specs=[_cur_block(tr, tc, c0), _cur_block(tr, tc, c0 + nc), _cur_block(tr, tc, c0 + 2 * nc),
                  _prev_halo(tr, tc, c0 + nc), _prev_halo(tr, tc, c0 + 2 * nc),
                  _next_halo(tr, tc, c0, t), _cur_block(tr, tc, dc0), _next_halo(tr, tc, dc0, t),
                  pl.BlockSpec((8, tc), lambda j, i: (0, j))],
        out_specs=(_cur_block(tr, tc, 0), _cur_block(tr, tc, 0), _cur_block(tr, tc, 0),
                   pl.BlockSpec((8, tc), lambda j, i: (0, j))),
        scratch_shapes=[pltpu.VMEM((HALO + tr, tc), F32), pltpu.VMEM((tr + HALO, tc), F32)],
        compiler_params=_params(("parallel", "arbitrary")),
    )(h, h, h, h, h, h, dy, dy, _pad_rows(conv_w, 8))


def _stack_rows(rows, n, width):
    idx = lax.broadcasted_iota(jnp.int32, (n, width), 0)
    out = jnp.zeros((n, width), F32)
    for j, r in enumerate(rows):
        out = jnp.where(idx == j, r, out)
    return out


CONV_ROWS = 64


def _glu_conv_fwd(hod, conv_w, conv_b, *, tr=512, tc=256):
    t = hod.shape[0]
    width = conv_w.shape[1]
    nc = width // tc

    def body(val_ref, gate_ref, valh_ref, gateh_ref, w_ref, b_ref, u1_ref, pad_ref):
        i = pl.program_id(1)
        halo = valh_ref[...].astype(F32) * _sigmoid(gateh_ref[...].astype(F32))
        pad_ref[0:HALO, :] = jnp.where(i > 0, halo, 0.0)
        pad_ref[HALO:HALO + tr, :] = val_ref[...].astype(F32) * _sigmoid(gate_ref[...].astype(F32))
        for c in range(tr // CONV_ROWS):
            base = HALO + c * CONV_ROWS - (C_CONV - 1)
            acc = None
            for j in range(C_CONV):
                term = w_ref[j:j + 1, :] * pad_ref[base + j:base + j + CONV_ROWS, :]
                acc = term if acc is None else acc + term
            u1_ref[c * CONV_ROWS:(c + 1) * CONV_ROWS, :] = acc + b_ref[...]

    return pl.pallas_call(
        body, name="glu_conv_fwd", out_shape=jax.ShapeDtypeStruct((t, width), F32), grid=(nc, t // tr),
        in_specs=[_cur_block(tr, tc, 0), _cur_block(tr, tc, nc), _prev_halo(tr, tc, 0), _prev_halo(tr, tc, nc),
                  pl.BlockSpec((32, tc), lambda j, i: (0, j)), pl.BlockSpec((1, tc), lambda j, i: (0, j))],
        out_specs=_cur_block(tr, tc, 0),
        scratch_shapes=[pltpu.VMEM((HALO + tr, tc), F32)],
        compiler_params=_params(("parallel", "arbitrary")),
    )(hod, hod, hod, hod, _pad_rows(conv_w, 32), conv_b.reshape(1, width))


def _ln_silu_fwd(u1, g, b, *, tr=256):
    t, width = u1.shape

    def body(u_ref, g_ref, b_ref, o_ref):
        uv = u_ref[...]
        mu = jnp.mean(uv, axis=-1, keepdims=True)
        var = jnp.mean(jnp.square(uv - mu), axis=-1, keepdims=True)
        u2 = ((uv - mu) * lax.rsqrt(var + EPS)) * g_ref[...] + b_ref[...]
        o_ref[...] = (u2 * _sigmoid(u2)).astype(BF16)

    row = pl.BlockSpec((tr, width), lambda i: (i, 0))
    vec = pl.BlockSpec((1, width), lambda i: (0, 0))
    return pl.pallas_call(
        body, name="ln_silu_fwd", out_shape=jax.ShapeDtypeStruct((t, width), BF16), grid=(t // tr,),
        in_specs=[row, vec, vec], out_specs=row, compiler_params=_params(("parallel",)),
    )(u1, g.reshape(1, width), b.reshape(1, width))


def _ln_silu_bwd(du, u1, g, b, *, col0, tr=256):
    t, width = u1.shape

    def body(du_ref, u_ref, g_ref, b_ref, du1_ref, dg_ref, db_ref, dcb_ref):
        uv = u_ref[...]
        mu = jnp.mean(uv, axis=-1, keepdims=True)
        var = jnp.mean(jnp.square(uv - mu), axis=-1, keepdims=True)
        rstd = lax.rsqrt(var + EPS)
        xh = (uv - mu) * rstd
        u2 = xh * g_ref[...] + b_ref[...]
        sig = _sigmoid(u2)
        du2 = du_ref[...] * (sig * (1.0 + u2 * (1.0 - sig)))
        dxh = du2 * g_ref[...]
        du1 = rstd * (dxh - jnp.mean(dxh, axis=-1, keepdims=True)
                      - xh * jnp.mean(dxh * xh, axis=-1, keepdims=True))
        du1_ref[...] = du1
        parts = (jnp.sum(du2 * xh, axis=0, keepdims=True), jnp.sum(du2, axis=0, keepdims=True),
                 jnp.sum(du1, axis=0, keepdims=True))

        @pl.when(pl.program_id(0) == 0)
        def _():
            dg_ref[...], db_ref[...], dcb_ref[...] = parts

        @pl.when(pl.program_id(0) > 0)
        def _():
            dg_ref[...] += parts[0]
            db_ref[...] += parts[1]
            dcb_ref[...] += parts[2]

    row = pl.BlockSpec((tr, width), lambda i: (i, 0))
    vec = pl.BlockSpec((1, width), lambda i: (0, 0))
    vshp = jax.ShapeDtypeStruct((1, width), F32)
    return pl.pallas_call(
        body, name="ln_silu_bwd", out_shape=(jax.ShapeDtypeStruct((t, width), F32), vshp, vshp, vshp),
        grid=(t // tr,),
        in_specs=[pl.BlockSpec((tr, width), lambda i: (i, col0 // width)), row, vec, vec],
        out_specs=(row, vec, vec, vec), compiler_params=_params(("arbitrary",)),
    )(du, u1, g.reshape(1, width), b.reshape(1, width))


def _glu_conv_bwd(hod, du1, conv_w, *, tr=512, tc=256):
    t = hod.shape[0]
    width = conv_w.shape[1]
    nc = width // tc
    nt = t // tr

    def body(val_ref, gate_ref, valh_ref, gateh_ref, du_ref, dun_ref, w_ref,
             dval_ref, dgate_ref, dw_ref, pad_ref, padd_ref, du0_ref):
        i = pl.program_id(1)
        val = val_ref[...].astype(F32)
        sig = _sigmoid(gate_ref[...].astype(F32))
        halo = valh_ref[...].astype(F32) * _sigmoid(gateh_ref[...].astype(F32))
        pad_ref[0:HALO, :] = jnp.where(i > 0, halo, 0.0)
        pad_ref[HALO:HALO + tr, :] = val * sig
        padd_ref[0:tr, :] = du_ref[...]
        padd_ref[tr:tr + HALO, :] = jnp.where(i < nt - 1, dun_ref[...], 0.0)
        dws = [jnp.zeros((1, tc), F32)] * C_CONV
        for c in range(tr // CONV_ROWS):
            r0 = c * CONV_ROWS
            duc = padd_ref[r0:r0 + CONV_ROWS, :]
            acc = None
            for j in range(C_CONV):
                back = w_ref[j:j + 1, :] * padd_ref[r0 + C_CONV - 1 - j:r0 + C_CONV - 1 - j + CONV_ROWS, :]
                acc = back if acc is None else acc + back
                off = HALO + r0 - (C_CONV - 1) + j
                dws[j] = dws[j] + jnp.sum(duc * pad_ref[off:off + CONV_ROWS, :], axis=0, keepdims=True)
            du0_ref[r0:r0 + CONV_ROWS, :] = acc
        du0 = du0_ref[...]
        dval_ref[...] = (du0 * sig).astype(BF16)
        dgate_ref[...] = (du0 * val * sig * (1.0 - sig)).astype(BF16)
        dw = _stack_rows(dws, 32, tc)

        @pl.when(i == 0)
        def _():
            dw_ref[...] = dw

        @pl.when(i > 0)
        def _():
            dw_ref[...] += dw

    oshp = jax.ShapeDtypeStruct((t, width), BF16)
    wspec = pl.BlockSpec((32, tc), lambda j, i: (0, j))
    return pl.pallas_call(
        body, name="glu_conv_bwd", out_shape=(oshp, oshp, jax.ShapeDtypeStruct((32, width), F32)), grid=(nc, nt),
        in_specs=[_cur_block(tr, tc, 0), _cur_block(tr, tc, nc), _prev_halo(tr, tc, 0), _prev_halo(tr, tc, nc),
                  _cur_block(tr, tc, 0), _next_halo(tr, tc, 0, t), wspec],
        out_specs=(_cur_block(tr, tc, 0), _cur_block(tr, tc, 0), wspec),
        scratch_shapes=[pltpu.VMEM((HALO + tr, tc), F32), pltpu.VMEM((tr + HALO, tc), F32),
                        pltpu.VMEM((tr, tc), F32)],
        compiler_params=_params(("parallel", "arbitrary")),
    )(hod, hod, hod, hod, du1, du1, _pad_rows(conv_w, 32))


def _pooled(pad_ref, g, kw, tr, i):
    gw = pad_ref.shape[1] // len(D_WINDOWS)
    cols = slice(g * gw, (g + 1) * gw)
    tot = None
    for j in range(kw):
        sh = pad_ref[HALO - j:HALO - j + tr, cols]
        tot = sh if tot is None else tot + sh
    return tot / _window_count(tr, gw, kw, i * tr) - pad_ref[HALO:HALO + tr, cols]


def _window_count(rows, width, kw, row0):
    t1 = (lax.broadcasted_iota(jnp.int32, (rows, width), 0) + (row0 + 1)).astype(F32)
    return jnp.minimum(t1, float(kw))


def _pool_fwd(hod, pool_w, pool_scale, *, tr=256):
    t = hod.shape[0]
    width = pool_scale.shape[0]
    ng = len(D_WINDOWS)
    gw = width // ng

    def body(z_ref, zh_ref, w_ref, s_ref, y_ref, pad_ref):
        i = pl.program_id(1)
        pad_ref[0:HALO, :] = jnp.where(i > 0, zh_ref[...].astype(F32), 0.0)
        pad_ref[HALO:HALO + tr, :] = z_ref[...].astype(F32)
        for g, kw in enumerate(D_WINDOWS):
            cols = slice(g * gw, (g + 1) * gw)
            pre = jnp.dot(_pooled(pad_ref, g, kw, tr, i).astype(BF16), w_ref[g], preferred_element_type=F32)
            y_ref[:, cols] = (pre * s_ref[:, cols]).astype(BF16)

    return pl.pallas_call(
        body, name="pool_fwd", out_shape=jax.ShapeDtypeStruct((t, width), BF16), grid=(1, t // tr),
        in_specs=[_cur_block(tr, width, 2), _prev_halo(tr, width, 2),
                  pl.BlockSpec((ng, gw, gw), lambda j, i: (0, 0, 0)), pl.BlockSpec((1, width), lambda j, i: (0, 0))],
        out_specs=_cur_block(tr, width, 0),
        scratch_shapes=[pltpu.VMEM((HALO + tr, width), F32)],
        compiler_params=_params(("parallel", "arbitrary")),
    )(hod, hod, pool_w, pool_scale.reshape(1, width))


def _pool_bwd(hod, dy, pool_w, pool_scale, *, dcol0, tr=256):
    t = hod.shape[0]
    width = pool_scale.shape[0]
    ng = len(D_WINDOWS)
    gw = width // ng
    nt = t // tr

    def body(z_ref, zh_ref, dy_ref, dyn_ref, w_ref, s_ref, dz_ref, dw_ref, ds_ref, pad_ref, pade_ref):
        i = pl.program_id(1)
        pad_ref[0:HALO, :] = jnp.where(i > 0, zh_ref[...].astype(F32), 0.0)
        pad_ref[HALO:HALO + tr, :] = z_ref[...].astype(F32)
        dws = []
        dss = []
        for g, kw in enumerate(D_WINDOWS):
            cols = slice(g * gw, (g + 1) * gw)
            wg = w_ref[g]
            dyc = dy_ref[:, cols]
            dpre = (dyc * s_ref[:, cols]).astype(BF16)
            dpre_n = (dyn_ref[:, cols] * s_ref[:, cols]).astype(BF16)
            dpl = lax.dot_general(dpre, wg, _NT, preferred_element_type=F32)
            dpl_n = lax.dot_general(dpre_n, wg, _NT, preferred_element_type=F32)
            pade_ref[0:tr, cols] = dpl / _window_count(tr, gw, kw, i * tr)
            pade_ref[tr:tr + HALO, cols] = jnp.where(i < nt - 1, dpl_n / _window_count(HALO, gw, kw, (i + 1) * tr), 0.0)
            tot = None
            for j in range(kw):
                sh = pade_ref[j:j + tr, cols]
                tot = sh if tot is None else tot + sh
            dz_ref[:, cols] = (tot - dpl).astype(BF16)
            pooled = _pooled(pad_ref, g, kw, tr, i).astype(BF16)
            pre = jnp.dot(pooled, wg, preferred_element_type=F32)
            dss.append(jnp.sum(dyc * pre, axis=0, keepdims=True))
            dws.append(lax.dot_general(pooled, dpre, _TN, preferred_element_type=F32))

        @pl.when(i == 0)
        def _():
            for g in range(ng):
                dw_ref[g] = dws[g]
                ds_ref[:, g * gw:(g + 1) * gw] = dss[g]

        @pl.when(i > 0)
        def _():
            for g in range(ng):
                dw_ref[g] += dws[g]
                ds_ref[:, g * gw:(g + 1) * gw] += dss[g]

    dc = dcol0 // width
    wspec = pl.BlockSpec((ng, gw, gw), lambda j, i: (0, 0, 0))
    vspec = pl.BlockSpec((1, width), lambda j, i: (0, 0))
    return pl.pallas_call(
        body, name="pool_bwd",
        out_shape=(jax.ShapeDtypeStruct((t, width), BF16), jax.ShapeDtypeStruct((ng, gw, gw), F32),
                   jax.ShapeDtypeStruct((1, width), F32)),
        grid=(1, nt),
        in_specs=[_cur_block(tr, width, 2), _prev_halo(tr, width, 2), _cur_block(tr, width, dc),
                  _next_halo(tr, width, dc, t), wspec, vspec],
        out_specs=(_cur_block(tr, width, 0), wspec, vspec),
        scratch_shapes=[pltpu.VMEM((HALO + tr, width), F32), pltpu.VMEM((tr + HALO, width), F32)],
        compiler_params=_params(("arbitrary", "arbitrary")),
    )(hod, hod, dy, dy, pool_w, pool_scale.reshape(1, width))


TM = 1024
TN = 512
TK_ACC = 512


def _dw_full(a, b, *, m, n, alpha, axis, name):
    t = a.shape[0]
    tm = TM if m % TM == 0 else TN
    return _matmul([(a, b)], ta=True, m=m, n=n, k=t, tm=tm, tn=TN, tk=t, alpha=alpha, out_dtype=BF16, name=name)


def _ffn_fwd(x, g, w1, w3, w2):
    t, d = x.shape
    f = w1.shape[1]
    xn = _rmsnorm_fwd(x, g)
    a, b, h = _ffn_up(xn, w1, w3, tm=TM, tn=TN)
    y = _matmul([(h, w2)], m=t, n=d, k=f, tm=TM, tn=TN, tk=f, res=x, alpha=0.5, name="ffn_down")
    return y, (x, xn, a, b, h)


def _ffn_bwd(dx, dxb, saved, g, w1, w3, w2, push, tok, last=False):
    begin, finish = push
    x, xn, a, b, h = saved
    t, d = x.shape
    f = w1.shape[1]
    da, db = _ffn_dh(dxb, w2, a, b, tok, tm=TM, tn=TN)
    tok = begin([dict(a=xn, b=da, m=d, n=f, alpha=1.0, axis=1, name="ffn_dw1"),
                 dict(a=xn, b=db, m=d, n=f, alpha=1.0, axis=1, name="ffn_dw3"),
                 dict(a=h, b=dxb, m=f, n=d, alpha=0.5, axis=0, name="ffn_dw2")])
    if last:
        tok = finish(da)
    dxn = _matmul([(da, w1), (db, w3)], tb=True, m=t, n=d, k=f, tm=TM, tn=d, tk=TK_ACC, tok=tok, name="ffn_dxn")
    dx, dxb, dg = _rmsnorm_bwd(dxn, x, g, dx)
    return dx, dxb, dg, (tok if last else finish(dx))


def _mix_out_fwd(x, ycat, w_out):
    t, d = x.shape
    return _matmul([(ycat, w_out)], m=t, n=d, k=d, tm=TM, tn=TN, tk=d, res=x, name="mix_out")


def _mix_out_bwd(dxb, w_out, tok):
    t, d = dxb.shape
    return _matmul([(dxb, w_out)], tb=True, m=t, n=d, k=d, tm=TM, tn=TN, tk=d, tok=tok, name="mix_dy")


def _mix_in_bwd(dh, xn, w_in, x, g, dx, ycat, dxb, push):
    begin, finish = push
    t, d = x.shape
    n_in = w_in.shape[1]
    tok = begin([dict(a=xn, b=dh, m=d, n=n_in, alpha=1.0, axis=1, name="mix_dw_in"),
                 dict(a=ycat, b=dxb, m=d, n=d, alpha=1.0, axis=0, name="mix_dw_out")])
    dxn = _matmul([(dh, w_in)], tb=True, m=t, n=d, k=n_in, tm=TM, tn=d, tk=TK_ACC, tok=tok, name="mix_dxn")
    dx, dxb, dg = _rmsnorm_bwd(dxn, x, g, dx)
    return dx, dxb, dg, finish(dx)


def _group_view(a, col0, dil, width=A_WIDTH):
    if dil == 1:
        return a, col0 // width
    t = a.shape[0]
    return a[:, col0:col0 + width].reshape(t // dil, dil * width), 0


def _even_fwd(x, g, w_in, qg, kg, conv_w, w_out):
    t, d = x.shape
    n_in = w_in.shape[1]
    nq = len(A_DILATIONS) * A_WIDTH
    xn = _rmsnorm_fwd(x, g)
    h = _matmul([(xn, w_in)], m=t, n=n_in, k=d, tm=TM, tn=TN, tk=d, out_dtype=BF16, name="ev_in")
    outs, lses = [], []
    for gi, dil in enumerate(A_DILATIONS):
        (qv, oq), (kv, ok), (vv, ov) = (_group_view(h, part * nq + gi * A_WIDTH, dil) for part in range(3))
        o, l = _attn_fwd(qv, kv, vv, (oq, ok, ov), qg, kg, dil=dil)
        outs.append(o.reshape(t, A_WIDTH))
        lses.append(l.reshape(t, HEAD_DIM))
    ya, lse = _attn_combine(outs, lses)
    yb = _gateconv_fwd(h, conv_w, col0=3 * nq)
    ycat = jnp.concatenate([ya, yb], axis=1)
    return _mix_out_fwd(x, ycat, w_out), (x, xn, h, ya, lse, ycat)


def _even_bwd(dx, dxb, saved, g, w_in, qg, kg, conv_w, w_out, push, tok):
    x, xn, h, ya, lse, ycat = saved
    t, d = x.shape
    nq = len(A_DILATIONS) * A_WIDTH
    dycat = _mix_out_bwd(dxb, w_out, tok)
    dlt = _attn_delta(dycat, ya)
    dqs, dks, dvs = [], [], []
    dqg = jnp.zeros((HEAD_DIM,), F32)
    dkg = jnp.zeros((HEAD_DIM,), F32)
    for gi, dil in enumerate(A_DILATIONS):
        (qv, oq), (kv, ok), (vv, ov) = (_group_view(h, part * nq + gi * A_WIDTH, dil) for part in range(3))
        dyv, ody = _group_view(dycat, 0, dil)
        dq, dk, dv, dqg_i, dkg_i = _attn_bwd(
            qv, kv, vv, dyv, (oq, ok, ov, ody), _group_view(lse, 0, dil, HEAD_DIM)[0],
            _group_view(dlt, 0, dil, HEAD_DIM)[0], qg, kg, dil=dil)
        dqs.append(dq.reshape(t, A_WIDTH))
        dks.append(dk.reshape(t, A_WIDTH))
        dvs.append(dv.reshape(t, A_WIDTH))
        dqg = dqg + dqg_i.reshape(HEAD_DIM)
        dkg = dkg + dkg_i.reshape(HEAD_DIM)
    dbg, dcg, dxt, dcw = _gateconv_bwd(h, dycat, conv_w, col0=3 * nq, dcol0=A_WIDTH)
    dh = jnp.concatenate(dqs + dks + dvs + [dbg, dcg, dxt], axis=1)
    dx, dxb, dg, tok = _mix_in_bwd(dh, xn, w_in, x, g, dx, ycat, dxb, push)
    return dx, dxb, dg, dqg, dkg, dcw[:B_CONV], tok


def _odd_fwd(x, g, w_in, conv_w, conv_b, ln_g, ln_b, pool_w, pool_scale, w_out):
    t, d = x.shape
    n_in = w_in.shape[1]
    xn = _rmsnorm_fwd(x, g)
    hod = _matmul([(xn, w_in)], m=t, n=n_in, k=d, tm=TM, tn=TN, tk=d, out_dtype=BF16, name="od_in")
    u1 = _glu_conv_fwd(hod, conv_w, conv_b)
    u = _ln_silu_fwd(u1, ln_g, ln_b)
    yd = _pool_fwd(hod, pool_w.astype(BF16), pool_scale)
    ycat = jnp.concatenate([u, yd], axis=1)
    return _mix_out_fwd(x, ycat, w_out), (x, xn, hod, u1, ycat)


def _odd_bwd(dx, dxb, saved, g, w_in, conv_w, conv_b, ln_g, ln_b, pool_w, pool_scale, w_out, push, tok):
    x, xn, hod, u1, ycat = saved
    width = conv_w.shape[1]
    dycat = _mix_out_bwd(dxb, w_out, tok)
    du1, dlg, dlb, dcb = _ln_silu_bwd(dycat, u1, ln_g, ln_b, col0=0)
    dval, dgate, dcw = _glu_conv_bwd(hod, du1, conv_w)
    dz, dpw, dps = _pool_bwd(hod, dycat, pool_w.astype(BF16), pool_scale, dcol0=width)
    dh = jnp.concatenate([dval, dgate, dz], axis=1)
    dx, dxb, dg, tok = _mix_in_bwd(dh, xn, w_in, x, g, dx, ycat, dxb, push)
    return (dx, dxb, dg, dcw[:C_CONV], dcb.reshape(width), dlg.reshape(width), dlb.reshape(width),
            dpw, dps.reshape(width), tok)


def _sublayer_matrices(s):
    layer, slot = divmod(s, 3)
    if slot == 1:
        kind = "ev" if layer % 2 == 0 else "od"
        return [(kind + "_w_in", layer // 2), (kind + "_w_out", layer // 2)]
    j = 2 * layer + slot // 2
    return [("ffn_w1", j), ("ffn_w3", j), ("ffn_w2", j)]


def _local_step(x, target, wts, fetch=None, grads=None):
    depth = wts["norm_g"].shape[0]
    if fetch is None:
        fetch = lambda s, after: ([wts[name][idx] for name, idx in _sublayer_matrices(s)], 0.0)
    gr = {}
    if grads is None:
        def begin(s, specs):
            for (name, idx), spec in zip(_sublayer_matrices(s), specs):
                gr.setdefault(name, {})[idx] = _dw_full(**spec)
            return 0.0

        grads = (begin, lambda s, after: 0.0)

    def gain(layer, slot, tok):
        return wts["norm_g"][layer, slot] + tok

    saved = []
    for layer in range(depth):
        i = layer // 2
        s = 3 * layer
        m0, tok = fetch(s, x)
        x, s0 = _ffn_fwd(x, gain(layer, 0, tok), *m0)
        m1, tok = fetch(s + 1, x)
        if layer % 2 == 0:
            x, s1 = _even_fwd(x, gain(layer, 1, tok), m1[0], wts["ev_q_gain"][i],
                              wts["ev_k_gain"][i], wts["ev_conv_w"][i], m1[1])
        else:
            x, s1 = _odd_fwd(x, gain(layer, 1, tok), m1[0], wts["od_conv_w"][i],
                             wts["od_conv_b"][i], wts["od_ln_g"][i], wts["od_ln_b"][i], wts["od_pool_w"][i],
                             wts["od_pool_scale"][i], m1[1])
        m2, tok = fetch(s + 2, x)
        x, s2 = _ffn_fwd(x, gain(layer, 2, tok), *m2)
        saved.append(((s0, m0), (s1, m1), (s2, m2)))
    dx, dxb, sq = _loss_head(x, target)

    n_even, n_odd = (depth + 1) // 2, depth // 2
    for k in ("ev_q_gain", "ev_k_gain", "ev_conv_w"):
        gr[k] = [None] * n_even
    for k in ("od_conv_w", "od_conv_b", "od_ln_g", "od_ln_b", "od_pool_w", "od_pool_scale"):
        gr[k] = [None] * n_odd
    dnorm = [[None] * 3 for _ in range(depth)]
    norm_g = wts["norm_g"]

    def push_for(s):
        return (lambda specs: grads[0](s, specs)), (lambda after: grads[1](s, after))

    tok = 0.0
    for layer in reversed(range(depth)):
        i = layer // 2
        s = 3 * layer
        (s0, m0), (s1, m1), (s2, m2) = saved[layer]
        dx, dxb, dnorm[layer][2], tok = _ffn_bwd(dx, dxb, s2, norm_g[layer, 2], *m2, push_for(s + 2), tok)
        if layer % 2 == 0:
            (dx, dxb, dnorm[layer][1], gr["ev_q_gain"][i], gr["ev_k_gain"][i], gr["ev_conv_w"][i], tok) = _even_bwd(
                dx, dxb, s1, norm_g[layer, 1], m1[0], wts["ev_q_gain"][i],
                wts["ev_k_gain"][i], wts["ev_conv_w"][i], m1[1], push_for(s + 1), tok)
        else:
            (dx, dxb, dnorm[layer][1], gr["od_conv_w"][i], gr["od_conv_b"][i], gr["od_ln_g"][i],
             gr["od_ln_b"][i], gr["od_pool_w"][i], gr["od_pool_scale"][i], tok) = _odd_bwd(
                dx, dxb, s1, norm_g[layer, 1], m1[0], wts["od_conv_w"][i],
                wts["od_conv_b"][i], wts["od_ln_g"][i], wts["od_ln_b"][i], wts["od_pool_w"][i],
                wts["od_pool_scale"][i], m1[1], push_for(s + 1), tok)
        dx, dxb, dnorm[layer][0], tok = _ffn_bwd(dx, dxb, s0, norm_g[layer, 0], *m0, push_for(s), tok,
                                                 last=(layer == 0))
    gr["norm_g"] = jnp.stack([jnp.stack(r) for r in dnorm])
    for name in list(gr):
        if isinstance(gr[name], dict):
            gr[name] = [gr[name][idx] for idx in sorted(gr[name])]
    return sq, dx, gr


HBM_SPEC = pl.BlockSpec(memory_space=pltpu.HBM)
SEM_SPEC = pl.BlockSpec(memory_space=pltpu.SEMAPHORE)
ANY_SPEC = pl.BlockSpec(memory_space=pl.ANY)
EFFECT = pltpu.SideEffectType.DATAFLOW_SIDE_EFFECTING


def _place():
    x, y, c = lax.axis_index("x"), lax.axis_index("y"), lax.axis_index("c")
    chips = [(1 - x, y), (x, 1 - y), (1 - x, 1 - y)]
    return x, y, c, chips


def _chip_index(x, y):
    return 2 * x + y


def _ds(start, size, align):
    if isinstance(start, int):
        return pl.ds(start, size)
    return pl.ds(pl.multiple_of(start, align), size)


def _half(ref, axis, h):
    r, c = ref.shape[-2:]
    if axis == 1:
        return ref.at[_ds(h * (r // 2), r // 2, 16), :]
    return ref.at[:, _ds(h * (c // 2), c // 2, 128)]


def _chunk(ref, axis, j, n=N_CHIPS):
    r, c = ref.shape[-2:]
    if axis == 1:
        return ref.at[:, _ds(j * (c // n), c // n, 128)]
    return ref.at[_ds(j * (r // n), r // n, 16), :]


def _remote(src, dst, send_sem, recv_sem, device):
    return pltpu.make_async_remote_copy(src_ref=src, dst_ref=dst, send_sem=send_sem, recv_sem=recv_sem,
                                        device_id=device, device_id_type=MESH)


def _hbm(a):
    return pltpu.with_memory_space_constraint(a, pltpu.HBM)


def _cast_into(stacked, idx, chip, axis, tok):
    _, r, c = stacked.shape
    full = (r, N_CHIPS * c) if axis == 1 else (N_CHIPS * r, c)
    tr = 128
    while tr > 16 and tr * c * 4 > (1 << 20):
        tr //= 2
    if axis == 1:
        o_map = lambda i, s: (i, s[0])
    else:
        o_map = lambda i, s: (s[0] * (r // tr) + i, 0)

    def body(s_ref, w_ref, tok_ref, o_ref):
        o_ref[...] = (w_ref[...] + tok_ref[0:1, 0:1]).astype(BF16)

    return pl.pallas_call(
        body, name="cast_into", out_shape=jax.ShapeDtypeStruct(full, BF16),
        grid_spec=pltpu.PrefetchScalarGridSpec(
            num_scalar_prefetch=1, grid=(r // tr,),
            in_specs=[pl.BlockSpec((None, tr, c), lambda i, s: (idx, i, 0)),
                      pl.BlockSpec((8, 128), lambda i, s: (0, 0))],
            out_specs=pl.BlockSpec((tr, c), o_map)),
        compiler_params=_params(("parallel",)),
    )(chip, stacked, tok)


def _own_piece(ref, axis, me, cc):
    return _half(_chunk(ref, axis, me), axis, cc)


def _ag_start(fulls, axes):
    n = len(fulls)

    def body(*refs):
        ins = refs[:n]
        send, recv = refs[n:4 * n], refs[4 * n:7 * n]
        token = refs[8 * n]
        x, y, cc, chips = _place()
        me = _chip_index(x, y)
        for i in range(n):
            piece = _own_piece(ins[i], axes[i], me, cc)
            for k, chip in enumerate(chips):
                _remote(piece, piece, send[3 * i + k], recv[3 * i + k], (*chip, cc)).start()
        token[...] = jnp.zeros_like(token)

    sem = pltpu.SemaphoreType.DMA(())
    outs = pl.pallas_call(
        body, name="ag_start_%d" % n,
        out_shape=tuple([sem] * (6 * n) + [pltpu.HBM(f.shape, f.dtype) for f in fulls]
                        + [jax.ShapeDtypeStruct((8, 128), F32)]),
        in_specs=[HBM_SPEC] * n,
        out_specs=tuple([SEM_SPEC] * (6 * n) + [HBM_SPEC] * n + [pl.BlockSpec(memory_space=pltpu.VMEM)]),
        input_output_aliases={i: 6 * n + i for i in range(n)},
        compiler_params=pltpu.CompilerParams(has_side_effects=EFFECT),
    )(*[_hbm(f) for f in fulls])
    return outs[:3 * n], outs[3 * n:6 * n], outs[6 * n:7 * n], outs[7 * n]


def _ag_wait(send, recv, fulls, axes, after):
    n = len(fulls)

    def body(*refs):
        ins = refs[:n]
        send_s, recv_s = refs[n:4 * n], refs[4 * n:7 * n]
        x, y, cc, chips = _place()
        me = _chip_index(x, y)
        for i in range(n):
            mine = _own_piece(ins[i], axes[i], me, cc)
            for k, chip in enumerate(chips):
                got = _own_piece(ins[i], axes[i], _chip_index(*chip), cc)
                cp = _remote(mine, got, send_s[3 * i + k], recv_s[3 * i + k], (*chip, cc))
                cp.wait_send()
                cp.wait_recv()

    return pl.pallas_call(
        body, name="ag_wait_%d" % n,
        out_shape=tuple(pltpu.HBM(f.shape, f.dtype) for f in fulls),
        in_specs=[HBM_SPEC] * n + [SEM_SPEC] * (6 * n) + [ANY_SPEC],
        out_specs=tuple([HBM_SPEC] * n),
        input_output_aliases={i: i for i in range(n)},
        compiler_params=pltpu.CompilerParams(has_side_effects=EFFECT),
    )(*fulls, *send, *recv, after)


def _ag_forward(fulls, axes):
    n = len(fulls)

    def body(*refs):
        ins = refs[:n]
        send_sems, recv_sems = refs[2 * n], refs[2 * n + 1]
        x, y, cc, chips = _place()
        cps = []
        for i in range(n):
            for k, chip in enumerate(chips):
                got = _own_piece(ins[i], axes[i], _chip_index(*chip), cc)
                cp = _remote(got, got, send_sems.at[3 * i + k], recv_sems.at[3 * i + k], (x, y, 1 - cc))
                cp.start()
                cps.append(cp)
        for i in range(n):
            for k, chip in enumerate(chips):
                other = _own_piece(ins[i], axes[i], _chip_index(*chip), 1 - cc)
                cps[3 * i + k].wait_send()
                _remote(other, other, send_sems.at[3 * i + k], recv_sems.at[3 * i + k], (x, y, cc)).wait_recv()

    return pl.pallas_call(
        body, name="ag_forward_%d" % n,
        out_shape=tuple(jax.ShapeDtypeStruct(f.shape, f.dtype) for f in fulls),
        in_specs=[HBM_SPEC] * n, out_specs=tuple([HBM_SPEC] * n),
        input_output_aliases={i: i for i in range(n)},
        scratch_shapes=[pltpu.SemaphoreType.DMA((3 * n,)), pltpu.SemaphoreType.DMA((3 * n,))],
    )(*fulls)


def _pair_start(gs):
    n = len(gs)
    lands = [lax.empty(g.shape, g.dtype) for g in gs]

    def body(*refs):
        g_refs, land_refs = refs[:n], refs[n:2 * n]
        send, recv = refs[2 * n:3 * n], refs[3 * n:4 * n]
        token = refs[6 * n]
        x, y, cc, _ = _place()
        for i in range(n):
            _remote(g_refs[i], land_refs[i], send[i], recv[i], (x, y, 1 - cc)).start()
        token[...] = jnp.zeros_like(token)

    sem = pltpu.SemaphoreType.DMA(())
    outs = pl.pallas_call(
        body, name="pair_start_%d" % n,
        out_shape=tuple([sem] * (2 * n) + [pltpu.HBM(a.shape, a.dtype) for a in list(gs) + lands]
                        + [jax.ShapeDtypeStruct((8, 128), F32)]),
        in_specs=[HBM_SPEC] * (2 * n),
        out_specs=tuple([SEM_SPEC] * (2 * n) + [HBM_SPEC] * (2 * n) + [pl.BlockSpec(memory_space=pltpu.VMEM)]),
        input_output_aliases={i: 2 * n + i for i in range(2 * n)},
        compiler_params=pltpu.CompilerParams(has_side_effects=EFFECT),
    )(*[_hbm(a) for a in list(gs) + lands])
    return outs[:n], outs[n:2 * n], outs[2 * n:3 * n], outs[3 * n:4 * n], outs[4 * n][0, 0]


def _pair_wait(send, recv, gs, lands, after):
    n = len(gs)

    def body(*refs):
        g_refs, land_refs = refs[:n], refs[n:2 * n]
        send_s, recv_s = refs[2 * n:3 * n], refs[3 * n:4 * n]
        x, y, cc, _ = _place()
        for i in range(n):
            cp = _remote(g_refs[i], land_refs[i], send_s[i], recv_s[i], (x, y, 1 - cc))
            cp.wait_send()
            cp.wait_recv()

    outs = pl.pallas_call(
        body, name="pair_wait_%d" % n,
        out_shape=tuple(pltpu.HBM(a.shape, a.dtype) for a in list(gs) + list(lands)),
        in_specs=[HBM_SPEC] * (2 * n) + [SEM_SPEC] * (2 * n) + [ANY_SPEC],
        out_specs=tuple([HBM_SPEC] * (2 * n)),
        input_output_aliases={i: i for i in range(2 * n)},
        compiler_params=pltpu.CompilerParams(has_side_effects=EFFECT),
    )(*gs, *lands, *send, *recv, after)
    return outs[n:]


def _pair_begin(specs, core):
    calls = []
    for spec in specs:
        t = spec["a"].shape[0]
        m, n = spec["m"], spec["n"]
        if spec["axis"] == 1:
            dims = dict(m=m // 2, n=n, tm=min(TM, m // 2), tn=TN)
            operand = "a"
        else:
            dims = dict(m=m, n=n // 2, tm=TN, tn=n // 2)
            operand = "b"
        calls.append((spec, operand, dict(ta=True, k=t, tk=t, alpha=spec["alpha"], out_dtype=BF16, **dims)))
    sent = [_matmul([(spec["a"], spec["b"])], half=(core, False, operand), name=spec["name"] + "_sib", **kw)
            for spec, operand, kw in calls]
    send, recv, sent, lands, tok = _pair_start(sent)
    return (calls, core, send, recv, sent, lands), tok


def _pair_finish(state, after):
    calls, core, send, recv, sent, lands = state
    got = _pair_wait(send, recv, sent, lands, after)
    return [_matmul([(spec["a"], spec["b"])], half=(core, True, operand), res=r, name=spec["name"], **kw)
            for (spec, operand, kw), r in zip(calls, got)]


def _piece_shape(p, axis):
    r, c = p.shape
    return (r, c // N_CHIPS) if axis == 1 else (r // N_CHIPS, c)


def _rs_chips_start(ps, axes):
    n = len(ps)
    lands = [lax.empty((3,) + _piece_shape(p, ax), p.dtype) for p, ax in zip(ps, axes)]

    def body(*refs):
        p_refs, land_refs = refs[:n], refs[n:2 * n]
        send, recv = refs[2 * n:5 * n], refs[5 * n:8 * n]
        token = refs[10 * n]
        x, y, cc, chips = _place()
        for i in range(n):
            for k, chip in enumerate(chips):
                _remote(_chunk(p_refs[i], axes[i], _chip_index(*chip)), land_refs[i].at[k],
                        send[3 * i + k], recv[3 * i + k], (*chip, cc)).start()
        token[...] = jnp.zeros_like(token)

    sem = pltpu.SemaphoreType.DMA(())
    outs = pl.pallas_call(
        body, name="rs_start_%d" % n,
        out_shape=tuple([sem] * (6 * n) + [pltpu.HBM(a.shape, a.dtype) for a in list(ps) + lands]
                        + [jax.ShapeDtypeStruct((8, 128), F32)]),
        in_specs=[HBM_SPEC] * (2 * n),
        out_specs=tuple([SEM_SPEC] * (6 * n) + [HBM_SPEC] * (2 * n) + [pl.BlockSpec(memory_space=pltpu.VMEM)]),
        input_output_aliases={i: 6 * n + i for i in range(2 * n)},
        compiler_params=pltpu.CompilerParams(has_side_effects=EFFECT),
    )(*[_hbm(a) for a in list(ps) + lands])
    return outs[:3 * n], outs[3 * n:6 * n], outs[6 * n:7 * n], outs[7 * n:8 * n], outs[8 * n][0, 0]


def _rs_chips_wait(send, recv, ps, lands, axes, after):
    n = len(ps)

    def body(*refs):
        p_refs, land_refs = refs[:n], refs[n:2 * n]
        send_s, recv_s = refs[2 * n:5 * n], refs[5 * n:8 * n]
        x, y, cc, chips = _place()
        for i in range(n):
            for k, chip in enumerate(chips):
                cp = _remote(_chunk(p_refs[i], axes[i], _chip_index(*chip)), land_refs[i].at[k],
                             send_s[3 * i + k], recv_s[3 * i + k], (*chip, cc))
                cp.wait_send()
                cp.wait_recv()

    outs = pl.pallas_call(
        body, name="rs_wait_%d" % n,
        out_shape=tuple(pltpu.HBM(a.shape, a.dtype) for a in list(ps) + list(lands)),
        in_specs=[HBM_SPEC] * (2 * n) + [SEM_SPEC] * (6 * n) + [ANY_SPEC],
        out_specs=tuple([HBM_SPEC] * (2 * n)),
        input_output_aliases={i: i for i in range(2 * n)},
        compiler_params=pltpu.CompilerParams(has_side_effects=EFFECT),
    )(*ps, *lands, *send, *recv, after)
    return outs[:n], outs[n:]


def _add_chips(p, got, chip, core, axis, *, idx, count, into, tr, tc):
    _, pr, pc = got.shape
    shard = (2 * pr, pc) if axis == 1 else (pr, 2 * pc)
    if axis == 1:
        p_map = lambda i, j, sc, so: (i, sc[0] * (pc // tc) + j)
        o_map = lambda i, j, sc, so: (idx, so[0] * (pr // tr) + i, j)
    else:
        p_map = lambda i, j, sc, so: (sc[0] * (pr // tr) + i, j)
        o_map = lambda i, j, sc, so: (idx, i, so[0] * (pc // tc) + j)

    def body(sc_ref, so_ref, p_ref, r_ref, *rest):
        o_ref = rest[-1]
        acc = p_ref[...].astype(F32)
        for k in range(3):
            acc = acc + r_ref[k].astype(F32)
        o_ref[...] = acc

    in_specs = [pl.BlockSpec((tr, tc), p_map), pl.BlockSpec((3, tr, tc), lambda i, j, sc, so: (0, i, j))]
    args = [chip, core, p, got]
    aliases = {}
    if into is not None:
        in_specs.append(ANY_SPEC)
        args.append(into)
        aliases = {4: 0}
    return pl.pallas_call(
        body, name="add_chips", out_shape=jax.ShapeDtypeStruct((count,) + shard, F32),
        grid_spec=pltpu.PrefetchScalarGridSpec(
            num_scalar_prefetch=2, grid=(pr // tr, pc // tc), in_specs=in_specs,
            out_specs=pl.BlockSpec((None, tr, tc), o_map)),
        input_output_aliases=aliases,
        compiler_params=_params(("parallel", "parallel")),
    )(*args)


def _rs_final(stacked, axis):
    n = stacked.shape[0]

    def body(s_ref, o_ref, send_sems, recv_sems):
        x, y, cc, _ = _place()
        cps = []
        for i in range(n):
            mine = _half(s_ref.at[i], axis, cc)
            cp = _remote(mine, mine, send_sems.at[i], recv_sems.at[i], (x, y, 1 - cc))
            cp.start()
            cps.append(cp)
        for i, cp in enumerate(cps):
            other = _half(s_ref.at[i], axis, 1 - cc)
            cp.wait_send()
            _remote(other, other, send_sems.at[i], recv_sems.at[i], (x, y, cc)).wait_recv()

    return pl.pallas_call(
        body, name="rs_final", out_shape=jax.ShapeDtypeStruct(stacked.shape, stacked.dtype),
        in_specs=[HBM_SPEC], out_specs=HBM_SPEC, input_output_aliases={0: 0},
        scratch_shapes=[pltpu.SemaphoreType.DMA((n,)), pltpu.SemaphoreType.DMA((n,))],
    )(stacked)


def _ag_small(packed):
    rows, cols = packed.shape

    def body(s_ref, o_ref, tok_ref, send_sems, recv_sems, local_sem):
        x, y, cc, chips = _place()
        me = _chip_index(x, y)
        own = pltpu.make_async_copy(s_ref, o_ref.at[me], local_sem)
        own.start()
        cps = [_remote(s_ref, o_ref.at[me], send_sems.at[k], recv_sems.at[k], (*chip, cc))
               for k, chip in enumerate(chips)]
        for cp in cps:
            cp.start()
        for k, chip in enumerate(chips):
            cps[k].wait_send()
            got = o_ref.at[_chip_index(*chip)]
            _remote(got, got, send_sems.at[k], recv_sems.at[k], (x, y, cc)).wait_recv()
        own.wait()
        tok_ref[...] = jnp.zeros_like(tok_ref)

    return pl.pallas_call(
        body, name="ag_small",
        out_shape=(jax.ShapeDtypeStruct((N_CHIPS, rows, cols), packed.dtype), jax.ShapeDtypeStruct((8, 128), F32)),
        in_specs=[HBM_SPEC], out_specs=(HBM_SPEC, pl.BlockSpec(memory_space=pltpu.VMEM)),
        scratch_shapes=[pltpu.SemaphoreType.DMA((3,)), pltpu.SemaphoreType.DMA((3,)), pltpu.SemaphoreType.DMA],
    )(packed)


def _rs_small(packed):
    _, rows, cols = packed.shape
    rels = [(bx, by, bc) for bx in (0, 1) for by in (0, 1) for bc in (0, 1)][1:]

    def body(s_ref, o_ref, send_sems, recv_sems, local_sem):
        x, y, cc, _ = _place()
        me = 4 * x + 2 * y + cc
        own = pltpu.make_async_copy(s_ref.at[_chip_index(x, y)], o_ref.at[me], local_sem)
        own.start()
        peers = [(jnp.bitwise_xor(x, bx), jnp.bitwise_xor(y, by), jnp.bitwise_xor(cc, bc)) for bx, by, bc in rels]
        cps = [_remote(s_ref.at[_chip_index(px, py)], o_ref.at[me], send_sems.at[k], recv_sems.at[k], (px, py, pc))
               for k, (px, py, pc) in enumerate(peers)]
        for cp in cps:
            cp.start()
        for k, (px, py, pc) in enumerate(peers):
            cps[k].wait_send()
            got = o_ref.at[4 * px + 2 * py + pc]
            _remote(got, got, send_sems.at[k], recv_sems.at[k], (x, y, cc)).wait_recv()
        own.wait()

    return pl.pallas_call(
        body, name="rs_small", out_shape=jax.ShapeDtypeStruct((2 * N_CHIPS, rows, cols), packed.dtype),
        in_specs=[HBM_SPEC], out_specs=HBM_SPEC,
        scratch_shapes=[pltpu.SemaphoreType.DMA((7,)), pltpu.SemaphoreType.DMA((7,)), pltpu.SemaphoreType.DMA],
    )(packed)


def _sum_slots(slots, *, tr=8):
    n, rows, cols = slots.shape

    def body(s_ref, o_ref):
        acc = s_ref[0]
        for k in range(1, n):
            acc = acc + s_ref[k]
        o_ref[...] = acc

    return pl.pallas_call(
        body, name="sum_slots", out_shape=jax.ShapeDtypeStruct((rows, cols), F32), grid=(rows // tr,),
        in_specs=[pl.BlockSpec((n, tr, cols), lambda i: (0, i, 0))], out_specs=pl.BlockSpec((tr, cols), lambda i: (i, 0)),
        compiler_params=_params(("parallel",)),
    )(slots)


MATRIX_AXIS = {"ffn_w1": 1, "ffn_w3": 1, "ffn_w2": 0, "ev_w_in": 1, "ev_w_out": 0, "od_w_in": 1, "od_w_out": 0}
SMALL_SHARDED = ("norm_g", "ev_conv_w", "od_conv_w", "od_conv_b", "od_ln_g", "od_ln_b", "od_pool_scale")
SMALL_REPLICATED = ("ev_q_gain", "ev_k_gain")
PACK_COLS = 1024
PACK_ROW_ALIGN = 8


def _pack(parts):
    flat = jnp.concatenate([p.reshape(-1).astype(F32) for p in parts])
    per = PACK_COLS * PACK_ROW_ALIGN
    total = -(-flat.shape[0] // per) * per
    return jnp.pad(flat, (0, total - flat.shape[0])).reshape(total // PACK_COLS, PACK_COLS)


def _unpack(packed, shapes):
    flat = packed.reshape(-1)
    out, pos = [], 0
    for shp in shapes:
        size = 1
        for s in shp:
            size *= s
        out.append(flat[pos:pos + size].reshape(shp))
        pos += size
    return out


def _row_tile(rows, cols, itemsize=4, budget=1 << 20):
    tr = 8
    while rows % (2 * tr) == 0 and 2 * tr * cols * itemsize <= budget:
        tr *= 2
    return tr


def _piece_tiles(hr, hc, axis):
    pr, pc = (hr, hc // N_CHIPS) if axis == 1 else (hr // N_CHIPS, hc)
    tr = 128
    while tr > 16 and tr * pc * 4 > (1 << 20):
        tr //= 2
    return tr, pc


AG_LOOKAHEAD = 2


def kernel(x, norm_g, ffn_w1, ffn_w3, ffn_w2, ev_w_in, ev_q_gain, ev_k_gain, ev_conv_w, ev_w_out, od_w_in, od_conv_w, od_conv_b, od_ln_g, od_ln_b, od_pool_w, od_pool_scale, od_w_out, loss_target, m_norm_g, m_ffn_w1, m_ffn_w3, m_ffn_w2, m_ev_w_in, m_ev_q_gain, m_ev_k_gain, m_ev_conv_w, m_ev_w_out, m_od_w_in, m_od_conv_w, m_od_conv_b, m_od_ln_g, m_od_ln_b, m_od_pool_w, m_od_pool_scale, m_od_w_out, v_norm_g, v_ffn_w1, v_ffn_w3, v_ffn_w2, v_ev_w_in, v_ev_q_gain, v_ev_k_gain, v_ev_conv_w, v_ev_w_out, v_od_w_in, v_od_conv_w, v_od_conv_b, v_od_ln_g, v_od_ln_b, v_od_pool_w, v_od_pool_scale, v_od_w_out):
    names = ["norm_g", "ffn_w1", "ffn_w3", "ffn_w2", "ev_w_in", "ev_q_gain", "ev_k_gain", "ev_conv_w", "ev_w_out",
             "od_w_in", "od_conv_w", "od_conv_b", "od_ln_g", "od_ln_b", "od_pool_w", "od_pool_scale", "od_w_out"]
    w = dict(zip(names, (norm_g, ffn_w1, ffn_w3, ffn_w2, ev_w_in, ev_q_gain, ev_k_gain, ev_conv_w, ev_w_out,
                         od_w_in, od_conv_w, od_conv_b, od_ln_g, od_ln_b, od_pool_w, od_pool_scale, od_w_out)))
    m = dict(zip(names, (m_norm_g, m_ffn_w1, m_ffn_w3, m_ffn_w2, m_ev_w_in, m_ev_q_gain, m_ev_k_gain, m_ev_conv_w,
                         m_ev_w_out, m_od_w_in, m_od_conv_w, m_od_conv_b, m_od_ln_g, m_od_ln_b, m_od_pool_w,
                         m_od_pool_scale, m_od_w_out)))
    v = dict(zip(names, (v_norm_g, v_ffn_w1, v_ffn_w3, v_ffn_w2, v_ev_w_in, v_ev_q_gain, v_ev_k_gain, v_ev_conv_w,
                         v_ev_w_out, v_od_w_in, v_od_conv_w, v_od_conv_b, v_od_ln_g, v_od_ln_b, v_od_pool_w,
                         v_od_pool_scale, v_od_w_out)))
    cx, cy, cc = lax.axis_index("x"), lax.axis_index("y"), lax.axis_index("c")
    cc_arr = jnp.reshape(cc, (1,)).astype(jnp.int32)
    chip_arr = jnp.reshape(_chip_index(cx, cy), (1,)).astype(jnp.int32)
    n_sub = 3 * norm_g.shape[0]
    stacked = {name: w[name].reshape((-1,) + w[name].shape[-2:]) for name in MATRIX_AXIS}

    def axes_of(s):
        return [MATRIX_AXIS[name] for name, _ in _sublayer_matrices(s)]

    ag_inflight = {}

    wts = {}
    small_names = SMALL_SHARDED + ("od_pool_w",)
    gathered, small_tok = _ag_small(_pack([w[k] for k in small_names]))
    per_chip = [_unpack(gathered[j], [w[k].shape for k in small_names]) for j in range(N_CHIPS)]
    for idx, k in enumerate(small_names):
        ax = 2 if k == "od_pool_w" else w[k].ndim - 1
        wts[k] = jnp.concatenate([per_chip[j][idx] for j in range(N_CHIPS)], axis=ax)
    for k in SMALL_REPLICATED:
        wts[k] = w[k]

    ag_tokens = [small_tok]

    def ag_start(s):
        fulls = [_cast_into(stacked[name], idx, chip_arr, MATRIX_AXIS[name], ag_tokens[-1])
                 for name, idx in _sublayer_matrices(s)]
        send, recv, fulls, tok = _ag_start(fulls, axes_of(s))
        ag_inflight[s] = (send, recv, fulls)
        ag_tokens.append(tok)
        return tok[0, 0]

    x0 = x[0]
    first_tok = sum(ag_start(s) for s in range(min(AG_LOOKAHEAD, n_sub)))

    def fetch(s, after):
        tok = first_tok if s == 0 else 0.0
        if s + AG_LOOKAHEAD < n_sub:
            tok = tok + ag_start(s + AG_LOOKAHEAD)
        send, recv, fulls = ag_inflight.pop(s)
        fulls = _ag_wait(send, recv, fulls, axes_of(s), after)
        return list(_ag_forward(fulls, axes_of(s))), tok

    rs_inflight = {}

    pair_inflight = {}

    def grads_begin(s, specs):
        pair_inflight[s], tok = _pair_begin(specs, cc_arr)
        return tok

    def grads_finish(s, after):
        send, recv, ps, lands, tok = _rs_chips_start(_pair_finish(pair_inflight.pop(s), after), axes_of(s))
        rs_inflight[s] = (send, recv, ps, lands)
        return tok

    sq, dx, gr = _local_step(x0, loss_target[0], wts, fetch, (grads_begin, grads_finish))

    grads = {}
    reduced = {name: None for name in MATRIX_AXIS}
    for s in reversed(range(n_sub)):
        send, recv, ps, lands = rs_inflight.pop(s)
        axes = axes_of(s)
        ps, lands = _rs_chips_wait(send, recv, ps, lands, axes, dx)
        for (name, idx), p, land, ax in zip(_sublayer_matrices(s), ps, lands, axes):
            tr, tc = _piece_tiles(*p.shape, ax)
            reduced[name] = _add_chips(p, land, chip_arr, cc_arr, ax, idx=idx, count=stacked[name].shape[0],
                                       into=reduced[name], tr=tr, tc=tc)
    for name, axis in MATRIX_AXIS.items():
        grads[name] = _rs_final(reduced[name], axis).reshape(w[name].shape)
    small_full = {k: (gr[k] if k == "norm_g" else jnp.stack(gr[k])) for k in small_names + SMALL_REPLICATED}
    chunks = []
    for j in range(N_CHIPS):
        parts = []
        for k in small_names:
            ax = 2 if k == "od_pool_w" else w[k].ndim - 1
            size = w[k].shape[ax]
            parts.append(lax.slice_in_dim(small_full[k], j * size, (j + 1) * size, axis=ax))
        parts += [small_full[k] for k in SMALL_REPLICATED] + [sq[0, :1]]
        chunks.append(_pack(parts))
    summed = _sum_slots(_rs_small(jnp.stack(chunks)))
    pack_names = small_names + SMALL_REPLICATED
    unpacked = _unpack(summed, [w[k].shape for k in pack_names] + [(1,)])
    for k, g in zip(pack_names, unpacked):
        grads[k] = g
    loss = (0.5 / x.shape[-1]) * unpacked[-1][0]

    delta, new_m, new_v = {}, {}, {}
    for name in MATRIX_AXIS:
        shp = w[name].shape
        cols = shp[-1]
        rows = w[name].size // cols
        tr = _row_tile(rows, cols, budget=1 << 20)
        d_, m_, v_, g_ = _adamw(w[name].reshape(rows, cols), grads[name].reshape(rows, cols),
                                m[name].reshape(rows, cols), v[name].reshape(rows, cols), tr=tr)
        delta[name], new_m[name], new_v[name] = d_.reshape(shp), m_.reshape(shp), v_.reshape(shp)
        grads[name] = g_.reshape(shp)
    pw, pg, pm, pv = (_pack([t[k] for k in pack_names]) for t in (w, grads, m, v))
    d_, m_, v_, _ = _adamw(pw, pg, pm, pv, tr=PACK_ROW_ALIGN)
    shapes = [w[k].shape for k in pack_names]
    for store, packed in ((delta, d_), (new_m, m_), (new_v, v_)):
        for k, a in zip(pack_names, _unpack(packed, shapes)):
            store[k] = a

    return (loss, dx[None], *[grads[k] for k in names], *[delta[k] for k in names],
            *[new_m[k] for k in names], *[new_v[k] for k in names])
```

```python
import jax
import jax.numpy as jnp
from jax import lax
from jax.experimental import pallas as pl
from jax.experimental.pallas import tpu as pltpu

F32 = jnp.float32
BF16 = jnp.bfloat16
MESH = pl.DeviceIdType.MESH

EPS = 1e-6
HEADS = 8
HEAD_DIM = 128
A_WIDTH = HEADS * HEAD_DIM
A_WINDOWS = (128, 512, 2048)
A_DILATIONS = (1, 4, 16)
ATTN_BLOCK = 128
B_CONV = 3
C_CONV = 31
D_WINDOWS = (2, 4, 8, 16)
HALO = 32
N_CHIPS = 4
ADAM_LR = 0.001
ADAM_B1 = 0.9
ADAM_B2 = 0.999
ADAM_EPS = 1e-08
ADAM_WD = 0.01
ADAM_STEP = 10
VMEM_LIMIT_BYTES = 56 * 1024 * 1024
NEG_BIG = -1e30


def _params(sem, **kw):
    return pltpu.CompilerParams(dimension_semantics=sem, vmem_limit_bytes=VMEM_LIMIT_BYTES, **kw)


def _sigmoid(x):
    return 1.0 / (1.0 + jnp.exp(-x))


EPILOGUE_COLS = 256


def _matmul(pairs, *, m, n, k, tm, tn, tk, ta=False, tb=False, out_dtype=F32, res=None,
            alpha=1.0, name, j_outer=False, half=None, tok=None):
    nk = k // tk
    npairs = len(pairs)
    dn = (((0 if ta else 1,), (1 if tb else 0,)), ((), ()))

    def ij(p, q):
        return (q, p) if j_outer else (p, q)

    def shift(s, operand, blocks):
        if half is None or half[2] != operand:
            return 0
        h = s[0][0]
        return (h if half[1] else 1 - h) * blocks

    def a_map(p, q, kk, *s):
        i = ij(p, q)[0] + shift(s, "a", m // tm)
        return (kk, i) if ta else (i, kk)

    def b_map(p, q, kk, *s):
        j = ij(p, q)[1] + shift(s, "b", n // tn)
        return (j, kk) if tb else (kk, j)

    def o_map(p, q, kk, *s):
        return ij(p, q)

    def body(*refs):
        if half is not None:
            refs = refs[1:]
        ab = refs[:2 * npairs]
        pos = 2 * npairs
        res_ref = tok_ref = None
        if res is not None:
            res_ref = refs[pos]
            pos += 1
        if tok is not None:
            tok_ref = refs[pos]
            pos += 1
        o_ref = refs[pos]
        acc_ref = refs[pos + 1] if nk > 1 else None

        def dots():
            tot = None
            for p in range(npairs):
                d = lax.dot_general(ab[2 * p][...], ab[2 * p + 1][...], dn, preferred_element_type=F32)
                tot = d if tot is None else tot + d
            return tot

        def finish(acc):
            r = acc * alpha if alpha != 1.0 else acc
            if res_ref is not None:
                r = res_ref[...].astype(F32) + r
            if tok_ref is not None:
                r = r + tok_ref[0:1, 0:1]
            o_ref[...] = r.astype(o_ref.dtype)

        if nk == 1:
            finish(dots())
        else:
            kk = pl.program_id(2)

            @pl.when(kk == 0)
            def _():
                acc_ref[...] = dots()

            @pl.when(kk > 0)
            def _():
                acc_ref[...] += dots()

            @pl.when(kk == nk - 1)
            def _():
                finish(acc_ref[...])

    a_spec = pl.BlockSpec((tk, tm) if ta else (tm, tk), a_map)
    b_spec = pl.BlockSpec((tn, tk) if tb else (tk, tn), b_map)
    o_spec = pl.BlockSpec((tm, tn), o_map)
    in_specs = [a_spec, b_spec] * npairs
    args = [t for p in pairs for t in p]
    if res is not None:
        in_specs.append(o_spec)
        args.append(res)
    if tok is not None:
        in_specs.append(pl.BlockSpec((8, 128), lambda p, q, kk, *s: (0, 0)))
        args.append(jnp.full((8, 128), tok, F32))
    grid = ij(m // tm, n // tn) + (nk,)
    scratch = [pltpu.VMEM((tm, tn), F32)] if nk > 1 else []
    if half is None:
        kwargs = dict(grid=grid, in_specs=in_specs, out_specs=o_spec, scratch_shapes=scratch)
    else:
        args = [half[0]] + args
        kwargs = dict(grid_spec=pltpu.PrefetchScalarGridSpec(
            num_scalar_prefetch=1, grid=grid, in_specs=in_specs, out_specs=o_spec, scratch_shapes=scratch))
    return pl.pallas_call(
        body, name=name, out_shape=jax.ShapeDtypeStruct((m, n), out_dtype),
        compiler_params=_params(("parallel", "parallel", "arbitrary")), **kwargs,
    )(*args)


def _ffn_up(xn, w1, w3, *, tm, tn):
    t, d = xn.shape
    f = w1.shape[1]

    def body(x_ref, w1_ref, w3_ref, a_ref, b_ref, h_ref):
        x = x_ref[...]
        for c in range(tn // EPILOGUE_COLS):
            cols = slice(c * EPILOGUE_COLS, (c + 1) * EPILOGUE_COLS)
            a = jnp.dot(x, w1_ref[:, cols], preferred_element_type=F32)
            b = jnp.dot(x, w3_ref[:, cols], preferred_element_type=F32)
            a_ref[:, cols] = a.astype(BF16)
            b_ref[:, cols] = b.astype(BF16)
            h_ref[:, cols] = (a * _sigmoid(a) * b).astype(BF16)

    x_spec = pl.BlockSpec((tm, d), lambda i, j: (i, 0))
    w_spec = pl.BlockSpec((d, tn), lambda i, j: (0, j))
    o_spec = pl.BlockSpec((tm, tn), lambda i, j: (i, j))
    shp = jax.ShapeDtypeStruct((t, f), BF16)
    return pl.pallas_call(
        body, name="ffn_up", out_shape=(shp, shp, shp), grid=(t // tm, f // tn),
        in_specs=[x_spec, w_spec, w_spec], out_specs=(o_spec, o_spec, o_spec),
        compiler_params=_params(("parallel", "parallel")),
    )(xn, w1, w3)


def _ffn_dh(dyb, w2, a, b, tok, *, tm, tn):
    t, d = dyb.shape
    f = w2.shape[0]
    dn = (((1,), (1,)), ((), ()))

    def body(dy_ref, w2_ref, a_ref, b_ref, tok_ref, da_ref, db_ref):
        dy = dy_ref[...]
        for c in range(tn // EPILOGUE_COLS):
            cols = slice(c * EPILOGUE_COLS, (c + 1) * EPILOGUE_COLS)
            dh = 0.5 * lax.dot_general(dy, w2_ref[cols, :], dn, preferred_element_type=F32) + tok_ref[0:1, 0:1]
            av = a_ref[:, cols].astype(F32)
            bv = b_ref[:, cols].astype(F32)
            sig = _sigmoid(av)
            da_ref[:, cols] = (dh * bv * (sig * (1.0 + av * (1.0 - sig)))).astype(BF16)
            db_ref[:, cols] = (dh * (av * sig)).astype(BF16)

    dy_spec = pl.BlockSpec((tm, d), lambda i, j: (i, 0))
    w_spec = pl.BlockSpec((tn, d), lambda i, j: (j, 0))
    o_spec = pl.BlockSpec((tm, tn), lambda i, j: (i, j))
    shp = jax.ShapeDtypeStruct((t, f), BF16)
    return pl.pallas_call(
        body, name="ffn_dh", out_shape=(shp, shp), grid=(t // tm, f // tn),
        in_specs=[dy_spec, w_spec, o_spec, o_spec, pl.BlockSpec((8, 128), lambda i, j: (0, 0))],
        out_specs=(o_spec, o_spec),
        compiler_params=_params(("parallel", "parallel")),
    )(dyb, w2, a, b, jnp.full((8, 128), tok, F32))


def _rmsnorm_fwd(x, g, *, tr=256):
    t, d = x.shape

    def body(x_ref, g_ref, o_ref):
        xv = x_ref[...]
        y = xv * lax.rsqrt(jnp.mean(xv * xv, axis=-1, keepdims=True) + EPS)
        o_ref[...] = (y * g_ref[...]).astype(BF16)

    return pl.pallas_call(
        body, name="rmsnorm_fwd", out_shape=jax.ShapeDtypeStruct((t, d), BF16), grid=(t // tr,),
        in_specs=[pl.BlockSpec((tr, d), lambda i: (i, 0)), pl.BlockSpec((1, d), lambda i: (0, 0))],
        out_specs=pl.BlockSpec((tr, d), lambda i: (i, 0)),
        compiler_params=_params(("parallel",)),
    )(x, g.reshape(1, d))


def _rmsnorm_bwd(dy, x, g, dres, *, tr=256):
    t, d = x.shape

    def body(dy_ref, x_ref, g_ref, dres_ref, dx_ref, dxb_ref, dg_ref):
        xv = x_ref[...]
        dyv = dy_ref[...].astype(F32)
        r = lax.rsqrt(jnp.mean(xv * xv, axis=-1, keepdims=True) + EPS)
        xhat = xv * r
        dxhat = dyv * g_ref[...]
        c = jnp.mean(dxhat * xhat, axis=-1, keepdims=True)
        dx = dres_ref[...] + r * (dxhat - xhat * c)
        dx_ref[...] = dx
        dxb_ref[...] = dx.astype(BF16)
        part = jnp.sum(dyv * xhat, axis=0, keepdims=True)

        @pl.when(pl.program_id(0) == 0)
        def _():
            dg_ref[...] = part

        @pl.when(pl.program_id(0) > 0)
        def _():
            dg_ref[...] += part

    row = pl.BlockSpec((tr, d), lambda i: (i, 0))
    vec = pl.BlockSpec((1, d), lambda i: (0, 0))
    dx, dxb, dg = pl.pallas_call(
        body, name="rmsnorm_bwd",
        out_shape=(jax.ShapeDtypeStruct((t, d), F32), jax.ShapeDtypeStruct((t, d), BF16),
                   jax.ShapeDtypeStruct((1, d), F32)),
        grid=(t // tr,), in_specs=[row, row, vec, row], out_specs=(row, row, vec),
        compiler_params=_params(("arbitrary",)),
    )(dy, x, g.reshape(1, d), dres)
    return dx, dxb, dg.reshape(d)


def _loss_head(y, target, *, tr=256):
    t, d = y.shape

    def body(y_ref, t_ref, dy_ref, dyb_ref, s_ref):
        err = y_ref[...] - t_ref[...]
        dy = err * (1.0 / d)
        dy_ref[...] = dy
        dyb_ref[...] = dy.astype(BF16)
        part = jnp.full((1, 128), jnp.sum(err * err), F32)

        @pl.when(pl.program_id(0) == 0)
        def _():
            s_ref[...] = part

        @pl.when(pl.program_id(0) > 0)
        def _():
            s_ref[...] += part

    row = pl.BlockSpec((tr, d), lambda i: (i, 0))
    return pl.pallas_call(
        body, name="loss_head",
        out_shape=(jax.ShapeDtypeStruct((t, d), F32), jax.ShapeDtypeStruct((t, d), BF16),
                   jax.ShapeDtypeStruct((1, 128), F32)),
        grid=(t // tr,), in_specs=[row, row],
        out_specs=(row, row, pl.BlockSpec((1, 128), lambda i: (0, 0))),
        compiler_params=_params(("arbitrary",)),
    )(y, target)


def _adamw(w, g, m, v, *, tr):
    rows, cols = w.shape

    def body(w_ref, g_ref, m_ref, v_ref, d_ref, nm_ref, nv_ref, go_ref):
        gv = g_ref[...]
        go_ref[...] = gv
        nm = ADAM_B1 * m_ref[...] + (1.0 - ADAM_B1) * gv
        nv = ADAM_B2 * v_ref[...] + (1.0 - ADAM_B2) * jnp.square(gv)
        m_hat = nm / (1.0 - ADAM_B1 ** ADAM_STEP)
        v_hat = nv / (1.0 - ADAM_B2 ** ADAM_STEP)
        d_ref[...] = -ADAM_LR * (m_hat / (jnp.sqrt(v_hat) + ADAM_EPS) + ADAM_WD * w_ref[...])
        nm_ref[...] = nm
        nv_ref[...] = nv

    spec = pl.BlockSpec((tr, cols), lambda i: (i, 0))
    shp = jax.ShapeDtypeStruct((rows, cols), F32)
    return pl.pallas_call(
        body, name="adamw", out_shape=(shp, shp, shp, shp), grid=(rows // tr,),
        in_specs=[spec] * 4, out_specs=(spec, spec, spec, spec),
        compiler_params=_params(("parallel",)),
    )(w, g, m, v)


def _headnorm(xf, g):
    r = lax.rsqrt(jnp.mean(xf * xf, axis=-1, keepdims=True) + EPS)
    xhat = xf * r
    return xhat * g, xhat, r


def _headnorm_bwd(dn, xhat, r, g):
    dxhat = dn * g
    return r * (dxhat - xhat * jnp.mean(dxhat * xhat, axis=-1, keepdims=True))


_NT = (((1,), (1,)), ((), ()))
_TN = (((0,), (0,)), ((), ()))


def _attn_masks(n, nb):
    qi = lax.broadcasted_iota(jnp.int32, (ATTN_BLOCK, ATTN_BLOCK), 0)
    ci = lax.broadcasted_iota(jnp.int32, (ATTN_BLOCK, ATTN_BLOCK), 1)
    d_prev = qi + ATTN_BLOCK - ci
    d_cur = qi - ci
    return d_prev, d_cur, (ci >= qi), (ci <= qi)


def _head_lane(tile, h):
    lane = lax.broadcasted_iota(jnp.int32, tile.shape, 1)
    return jnp.sum(jnp.where(lane == h, tile, 0.0), axis=-1, keepdims=True)


def _set_head_lane(tile, h, col):
    lane = lax.broadcasted_iota(jnp.int32, tile.shape, 1)
    return jnp.where(lane == h, col, tile)


def _attn_fwd(qv, kv, vv, offs, qg, kg, *, dil):
    l = qv.shape[0]
    nb = l // ATTN_BLOCK
    scale = HEAD_DIM ** -0.5

    def body(q_ref, kp_ref, kc_ref, vp_ref, vc_ref, qg_ref, kg_ref, o_ref, lse_ref):
        n = pl.program_id(1)
        d_prev, d_cur, ok_prev, ok_cur = _attn_masks(n, nb)
        ok_prev = ok_prev & (n > 0)
        b_prev = d_prev.astype(F32) * float(dil)
        b_cur = d_cur.astype(F32) * float(dil)
        lse = jnp.zeros((ATTN_BLOCK, HEAD_DIM), F32)
        for h in range(HEADS):
            sl = slice(h * HEAD_DIM, (h + 1) * HEAD_DIM)
            slope = 2.0 ** (-8.0 * (h + 1) / HEADS)
            q = _headnorm(q_ref[:, sl].astype(F32), qg_ref[...])[0].astype(BF16)
            kp = _headnorm(kp_ref[:, sl].astype(F32), kg_ref[...])[0].astype(BF16)
            kc = _headnorm(kc_ref[:, sl].astype(F32), kg_ref[...])[0].astype(BF16)
            s1 = lax.dot_general(q, kp, _NT, preferred_element_type=F32) * scale
            s2 = lax.dot_general(q, kc, _NT, preferred_element_type=F32) * scale
            s1 = jnp.where(ok_prev, s1 - slope * b_prev, NEG_BIG)
            s2 = jnp.where(ok_cur, s2 - slope * b_cur, NEG_BIG)
            m = jnp.maximum(jnp.max(s1, axis=-1, keepdims=True), jnp.max(s2, axis=-1, keepdims=True))
            p1 = jnp.exp(s1 - m)
            p2 = jnp.exp(s2 - m)
            den = jnp.sum(p1, axis=-1, keepdims=True) + jnp.sum(p2, axis=-1, keepdims=True)
            inv = 1.0 / den
            o = jnp.dot((p1 * inv).astype(BF16), vp_ref[:, sl], preferred_element_type=F32)
            o = o + jnp.dot((p2 * inv).astype(BF16), vc_ref[:, sl], preferred_element_type=F32)
            o_ref[:, sl] = o
            lse = _set_head_lane(lse, h, m + jnp.log(den))
        lse_ref[...] = lse

    cur, prev, _ = _attn_specs(nb)
    vec = pl.BlockSpec((1, HEAD_DIM), lambda r, n: (0, 0))
    return pl.pallas_call(
        body, name="attn_fwd_d%d" % dil,
        out_shape=(jax.ShapeDtypeStruct((l, dil * A_WIDTH), F32), jax.ShapeDtypeStruct((l, dil * HEAD_DIM), F32)),
        grid=(dil, nb),
        in_specs=[cur(offs[0]), prev(offs[1]), cur(offs[1]), prev(offs[2]), cur(offs[2]), vec, vec],
        out_specs=(cur(0), cur(0, HEAD_DIM)),
        compiler_params=_params(("parallel", "parallel")),
    )(qv, kv, kv, vv, vv, qg.reshape(1, HEAD_DIM), kg.reshape(1, HEAD_DIM))


def _attn_specs(nb):
    def cur(off, width=A_WIDTH):
        return pl.BlockSpec((ATTN_BLOCK, width), lambda r, n: (n, off + r))

    def prev(off, width=A_WIDTH):
        return pl.BlockSpec((ATTN_BLOCK, width), lambda r, n: (jnp.maximum(n - 1, 0), off + r))

    def nxt(off, width=A_WIDTH):
        return pl.BlockSpec((ATTN_BLOCK, width), lambda r, n: (jnp.minimum(n + 1, nb - 1), off + r))

    return cur, prev, nxt


def _attn_combine(outs, lses, *, tr=256):
    t, w = outs[0].shape

    def body(o0, o1, o2, l0, l1, l2, y_ref, lse_ref):
        a0, a1, a2 = l0[...], l1[...], l2[...]
        m = jnp.maximum(jnp.maximum(a0, a1), a2)
        e0, e1, e2 = jnp.exp(a0 - m), jnp.exp(a1 - m), jnp.exp(a2 - m)
        s = e0 + e1 + e2
        inv = 1.0 / s
        w0, w1, w2 = e0 * inv, e1 * inv, e2 * inv
        lse_ref[...] = m + jnp.log(s)
        for h in range(HEADS):
            sl = slice(h * HEAD_DIM, (h + 1) * HEAD_DIM)
            y = (_head_lane(w0, h) * o0[:, sl] + _head_lane(w1, h) * o1[:, sl] + _head_lane(w2, h) * o2[:, sl])
            y_ref[:, sl] = y.astype(BF16)

    row = pl.BlockSpec((tr, w), lambda i: (i, 0))
    stat = pl.BlockSpec((tr, HEAD_DIM), lambda i: (i, 0))
    return pl.pallas_call(
        body, name="attn_combine",
        out_shape=(jax.ShapeDtypeStruct((t, w), BF16), jax.ShapeDtypeStruct((t, HEAD_DIM), F32)),
        grid=(t // tr,), in_specs=[row] * 3 + [stat] * 3, out_specs=(row, stat),
        compiler_params=_params(("parallel",)),
    )(*outs, *lses)


def _attn_delta(dy, y, *, tr=256):
    t, w = y.shape

    def body(dy_ref, y_ref, o_ref, dyb_ref):
        out = jnp.zeros((tr, HEAD_DIM), F32)
        for h in range(HEADS):
            sl = slice(h * HEAD_DIM, (h + 1) * HEAD_DIM)
            dlt = jnp.sum(dy_ref[:, sl] * y_ref[:, sl].astype(F32), axis=-1, keepdims=True)
            out = _set_head_lane(out, h, dlt)
        o_ref[...] = out
        dyb_ref[...] = dy_ref[...].astype(BF16)

    row = pl.BlockSpec((tr, w), lambda i: (i, 0))
    return pl.pallas_call(
        body, name="attn_delta",
        out_shape=(jax.ShapeDtypeStruct((t, HEAD_DIM), F32), jax.ShapeDtypeStruct((t, w), BF16)), grid=(t // tr,),
        in_specs=[row, row], out_specs=(pl.BlockSpec((tr, HEAD_DIM), lambda i: (i, 0)), row),
        compiler_params=_params(("parallel",)),
    )(dy, y)


def _attn_bwd(qv, kv, vv, dyv, offs, lsev, dltv, qg, kg, *, dil):
    l = qv.shape[0]
    w = dil * A_WIDTH
    nb = l // ATTN_BLOCK
    scale = HEAD_DIM ** -0.5

    def body(qc_ref, qn_ref, kp_ref, kc_ref, vp_ref, vc_ref, dyc_ref, dyn_ref, lc_ref, ln_ref,
             dc_ref, dn_ref, qg_ref, kg_ref, dq_ref, dk_ref, dv_ref, dqg_ref, dkg_ref):
        n = pl.program_id(1)
        first = (pl.program_id(0) == 0) & (n == 0)
        d_prev, d_cur, ok_prev, ok_cur = _attn_masks(n, nb)
        ok_t1 = ok_prev & (n > 0)
        ok_t3 = ok_prev & (n < nb - 1)
        b_prev = d_prev.astype(F32) * float(dil)
        b_cur = d_cur.astype(F32) * float(dil)
        qgv, kgv = qg_ref[...], kg_ref[...]
        dqg = jnp.zeros((1, HEAD_DIM), F32)
        dkg = jnp.zeros((1, HEAD_DIM), F32)
        for h in range(HEADS):
            sl = slice(h * HEAD_DIM, (h + 1) * HEAD_DIM)
            slope = 2.0 ** (-8.0 * (h + 1) / HEADS)
            qc, qc_hat, qc_r = _headnorm(qc_ref[:, sl].astype(F32), qgv)
            qn = _headnorm(qn_ref[:, sl].astype(F32), qgv)[0].astype(BF16)
            kp = _headnorm(kp_ref[:, sl].astype(F32), kgv)[0].astype(BF16)
            kc, kc_hat, kc_r = _headnorm(kc_ref[:, sl].astype(F32), kgv)
            qc = qc.astype(BF16)
            kc = kc.astype(BF16)
            vp, vc = vp_ref[:, sl], vc_ref[:, sl]
            dyc, dyn = dyc_ref[:, sl].astype(BF16), dyn_ref[:, sl].astype(BF16)

            def tile(q, k, v, dy, lse, dlt, ok, bias):
                s = lax.dot_general(q, k, _NT, preferred_element_type=F32) * scale
                p = jnp.where(ok, jnp.exp(jnp.where(ok, s - slope * bias, NEG_BIG) - lse), 0.0)
                dp = lax.dot_general(dy, v, _NT, preferred_element_type=F32)
                return p.astype(BF16), (p * (dp - dlt)).astype(BF16)

            lse_c, dlt_c = _head_lane(lc_ref[...], h), _head_lane(dc_ref[...], h)
            p1, ds1 = tile(qc, kp, vp, dyc, lse_c, dlt_c, ok_t1, b_prev)
            p2, ds2 = tile(qc, kc, vc, dyc, lse_c, dlt_c, ok_cur, b_cur)
            p3, ds3 = tile(qn, kc, vc, dyn, _head_lane(ln_ref[...], h), _head_lane(dn_ref[...], h), ok_t3, b_prev)
            dqn = scale * (jnp.dot(ds1, kp, preferred_element_type=F32) + jnp.dot(ds2, kc, preferred_element_type=F32))
            dkn = scale * (lax.dot_general(ds2, qc, _TN, preferred_element_type=F32)
                           + lax.dot_general(ds3, qn, _TN, preferred_element_type=F32))
            dv = (lax.dot_general(p2, dyc, _TN, preferred_element_type=F32)
                  + lax.dot_general(p3, dyn, _TN, preferred_element_type=F32))
            dqg = dqg + jnp.sum(dqn * qc_hat, axis=0, keepdims=True)
            dkg = dkg + jnp.sum(dkn * kc_hat, axis=0, keepdims=True)
            dq_ref[:, sl] = _headnorm_bwd(dqn, qc_hat, qc_r, qgv).astype(BF16)
            dk_ref[:, sl] = _headnorm_bwd(dkn, kc_hat, kc_r, kgv).astype(BF16)
            dv_ref[:, sl] = dv.astype(BF16)

        @pl.when(first)
        def _():
            dqg_ref[...] = dqg
            dkg_ref[...] = dkg

        @pl.when(jnp.logical_not(first))
        def _():
            dqg_ref[...] += dqg
            dkg_ref[...] += dkg

    cur, prev, nxt = _attn_specs(nb)
    o_q, o_k, o_v, o_dy = offs
    stat_c, stat_n = cur(0, HEAD_DIM), nxt(0, HEAD_DIM)
    vec = pl.BlockSpec((1, HEAD_DIM), lambda r, n: (0, 0))
    shp = jax.ShapeDtypeStruct((l, w), BF16)
    gshp = jax.ShapeDtypeStruct((1, HEAD_DIM), F32)
    return pl.pallas_call(
        body, name="attn_bwd_d%d" % dil, out_shape=(shp, shp, shp, gshp, gshp), grid=(dil, nb),
        in_specs=[cur(o_q), nxt(o_q), prev(o_k), cur(o_k), prev(o_v), cur(o_v), cur(o_dy), nxt(o_dy),
                  stat_c, stat_n, stat_c, stat_n, vec, vec],
        out_specs=(cur(0), cur(0), cur(0), vec, vec),
        compiler_params=_params(("arbitrary", "arbitrary")),
    )(qv, qv, kv, kv, vv, vv, dyv, dyv, lsev, lsev, dltv, dltv,
      qg.reshape(1, HEAD_DIM), kg.reshape(1, HEAD_DIM))


def _prev_halo(tr, tc, col0):
    return pl.BlockSpec((HALO, tc), lambda j, i: (jnp.maximum(i * (tr // HALO) - 1, 0), col0 + j))


def _next_halo(tr, tc, col0, rows):
    last = rows // HALO - 1
    return pl.BlockSpec((HALO, tc), lambda j, i: (jnp.minimum((i + 1) * (tr // HALO), last), col0 + j))


def _cur_block(tr, tc, col0):
    return pl.BlockSpec((tr, tc), lambda j, i: (i, col0 + j))


def _gateconv_fwd(h, conv_w, *, col0, tr=512, tc=256):
    t = h.shape[0]
    width = conv_w.shape[1]
    nc = width // tc
    c0 = col0 // tc

    def body(bg_ref, cg_ref, xt_ref, cgh_ref, xth_ref, w_ref, y_ref, pad_ref):
        i = pl.program_id(1)
        halo = cgh_ref[...].astype(F32) * xth_ref[...].astype(F32)
        pad_ref[0:HALO, :] = jnp.where(i > 0, halo, 0.0)
        pad_ref[HALO:HALO + tr, :] = cg_ref[...].astype(F32) * xt_ref[...].astype(F32)
        conv = None
        for j in range(B_CONV):
            term = w_ref[j:j + 1, :] * pad_ref[HALO - (B_CONV - 1) + j:HALO - (B_CONV - 1) + j + tr, :]
            conv = term if conv is None else conv + term
        y_ref[...] = (bg_ref[...].astype(F32) * conv).astype(BF16)

    return pl.pallas_call(
        body, name="gateconv_fwd", out_shape=jax.ShapeDtypeStruct((t, width), BF16), grid=(nc, t // tr),
        in_specs=[_cur_block(tr, tc, c0), _cur_block(tr, tc, c0 + nc), _cur_block(tr, tc, c0 + 2 * nc),
                  _prev_halo(tr, tc, c0 + nc), _prev_halo(tr, tc, c0 + 2 * nc),
                  pl.BlockSpec((8, tc), lambda j, i: (0, j))],
        out_specs=_cur_block(tr, tc, 0),
        scratch_shapes=[pltpu.VMEM((HALO + tr, tc), F32)],
        compiler_params=_params(("parallel", "arbitrary")),
    )(h, h, h, h, h, _pad_rows(conv_w, 8))


def _pad_rows(w, rows):
    return jnp.pad(w, ((0, rows - w.shape[0]), (0, 0)))


def _gateconv_bwd(h, dy, conv_w, *, col0, dcol0, tr=512, tc=256):
    t = h.shape[0]
    width = conv_w.shape[1]
    nc = width // tc
    c0 = col0 // tc
    dc0 = dcol0 // tc
    nt = t // tr

    def body(bg_ref, cg_ref, xt_ref, cgh_ref, xth_ref, bgn_ref, dy_ref, dyn_ref, w_ref,
             dbg_ref, dcg_ref, dxt_ref, dw_ref, pad_ref, padd_ref):
        i = pl.program_id(1)
        cg = cg_ref[...].astype(F32)
        xt = xt_ref[...].astype(F32)
        bg = bg_ref[...].astype(F32)
        dyv = dy_ref[...]
        halo = cgh_ref[...].astype(F32) * xth_ref[...].astype(F32)
        pad_ref[0:HALO, :] = jnp.where(i > 0, halo, 0.0)
        pad_ref[HALO:HALO + tr, :] = cg * xt
        dconv = dyv * bg
        padd_ref[0:tr, :] = dconv
        padd_ref[tr:tr + HALO, :] = jnp.where(i < nt - 1, dyn_ref[...] * bgn_ref[...].astype(F32), 0.0)
        conv = None
        du = None
        dws = []
        for j in range(B_CONV):
            off = HALO - (B_CONV - 1) + j
            shifted = pad_ref[off:off + tr, :]
            term = w_ref[j:j + 1, :] * shifted
            conv = term if conv is None else conv + term
            dws.append(jnp.sum(dconv * shifted, axis=0, keepdims=True))
            back = w_ref[j:j + 1, :] * padd_ref[B_CONV - 1 - j:B_CONV - 1 - j + tr, :]
            du = back if du is None else du + back
        dbg_ref[...] = (dyv * conv).astype(BF16)
        dcg_ref[...] = (du * xt).astype(BF16)
        dxt_ref[...] = (du * cg).astype(BF16)
        dw = _stack_rows(dws, 8, tc)

        @pl.when(i == 0)
        def _():
            dw_ref[...] = dw

        @pl.when(i > 0)
        def _():
            dw_ref[...] += dw

    oshp = jax.ShapeDtypeStruct((t, width), BF16)
    return pl.pallas_call(
        body, name="gateconv_bwd",
        out_shape=(oshp, oshp, oshp, jax.ShapeDtypeStruct((8, width), F32)), grid=(nc, nt),
        in_specs=[_cur_block(tr, tc, c0), _cur_block(tr, tc, c0 + nc), _cur_block(tr, tc, c0 + 2 * nc),
                  _prev_halo(tr, tc, c0 + nc), _prev_halo(tr, tc, c0 + 2 * nc),
                  _next_halo(tr, tc, c0, t), _cur_block(tr, tc, dc0), _next_halo(tr, tc, dc0, t),
                  pl.BlockSpec((8, tc), lambda j, i: (0, j))],
        out_specs=(_cur_block(tr, tc, 0), _cur_block(tr, tc, 0), _cur_block(tr, tc, 0),
                   pl.BlockSpec((8, tc), lambda j, i: (0, j))),
        scratch_shapes=[pltpu.VMEM((HALO + tr, tc), F32), pltpu.VMEM((tr + HALO, tc), F32)],
        compiler_params=_params(("parallel", "arbitrary")),
    )(h, h, h, h, h, h, dy, dy, _pad_rows(conv_w, 8))


def _stack_rows(rows, n, width):
    idx = lax.broadcasted_iota(jnp.int32, (n, width), 0)
    out = jnp.zeros((n, width), F32)
    for j, r in enumerate(rows):
        out = jnp.where(idx == j, r, out)
    return out


CONV_ROWS = 64


def _glu_conv_fwd(hod, conv_w, conv_b, *, tr=512, tc=256):
    t = hod.shape[0]
    width = conv_w.shape[1]
    nc = width // tc

    def body(val_ref, gate_ref, valh_ref, gateh_ref, w_ref, b_ref, u1_ref, pad_ref):
        i = pl.program_id(1)
        halo = valh_ref[...].astype(F32) * _sigmoid(gateh_ref[...].astype(F32))
        pad_ref[0:HALO, :] = jnp.where(i > 0, halo, 0.0)
        pad_ref[HALO:HALO + tr, :] = val_ref[...].astype(F32) * _sigmoid(gate_ref[...].astype(F32))
        for c in range(tr // CONV_ROWS):
            base = HALO + c * CONV_ROWS - (C_CONV - 1)
            acc = None
            for j in range(C_CONV):
                term = w_ref[j:j + 1, :] * pad_ref[base + j:base + j + CONV_ROWS, :]
                acc = term if acc is None else acc + term
            u1_ref[c * CONV_ROWS:(c + 1) * CONV_ROWS, :] = acc + b_ref[...]

    return pl.pallas_call(
        body, name="glu_conv_fwd", out_shape=jax.ShapeDtypeStruct((t, width), F32), grid=(nc, t // tr),
        in_specs=[_cur_block(tr, tc, 0), _cur_block(tr, tc, nc), _prev_halo(tr, tc, 0), _prev_halo(tr, tc, nc),
                  pl.BlockSpec((32, tc), lambda j, i: (0, j)), pl.BlockSpec((1, tc), lambda j, i: (0, j))],
        out_specs=_cur_block(tr, tc, 0),
        scratch_shapes=[pltpu.VMEM((HALO + tr, tc), F32)],
        compiler_params=_params(("parallel", "arbitrary")),
    )(hod, hod, hod, hod, _pad_rows(conv_w, 32), conv_b.reshape(1, width))


def _ln_silu_fwd(u1, g, b, *, tr=256):
    t, width = u1.shape

    def body(u_ref, g_ref, b_ref, o_ref):
        uv = u_ref[...]
        mu = jnp.mean(uv, axis=-1, keepdims=True)
        var = jnp.mean(jnp.square(uv - mu), axis=-1, keepdims=True)
        u2 = ((uv - mu) * lax.rsqrt(var + EPS)) * g_ref[...] + b_ref[...]
        o_ref[...] = (u2 * _sigmoid(u2)).astype(BF16)

    row = pl.BlockSpec((tr, width), lambda i: (i, 0))
    vec = pl.BlockSpec((1, width), lambda i: (0, 0))
    return pl.pallas_call(
        body, name="ln_silu_fwd", out_shape=jax.ShapeDtypeStruct((t, width), BF16), grid=(t // tr,),
        in_specs=[row, vec, vec], out_specs=row, compiler_params=_params(("parallel",)),
    )(u1, g.reshape(1, width), b.reshape(1, width))


def _ln_silu_bwd(du, u1, g, b, *, col0, tr=256):
    t, width = u1.shape

    def body(du_ref, u_ref, g_ref, b_ref, du1_ref, dg_ref, db_ref, dcb_ref):
        uv = u_ref[...]
        mu = jnp.mean(uv, axis=-1, keepdims=True)
        var = jnp.mean(jnp.square(uv - mu), axis=-1, keepdims=True)
        rstd = lax.rsqrt(var + EPS)
        xh = (uv - mu) * rstd
        u2 = xh * g_ref[...] + b_ref[...]
        sig = _sigmoid(u2)
        du2 = du_ref[...] * (sig * (1.0 + u2 * (1.0 - sig)))
        dxh = du2 * g_ref[...]
        du1 = rstd * (dxh - jnp.mean(dxh, axis=-1, keepdims=True)
                      - xh * jnp.mean(dxh * xh, axis=-1, keepdims=True))
        du1_ref[...] = du1
        parts = (jnp.sum(du2 * xh, axis=0, keepdims=True), jnp.sum(du2, axis=0, keepdims=True),
                 jnp.sum(du1, axis=0, keepdims=True))

        @pl.when(pl.program_id(0) == 0)
        def _():
            dg_ref[...], db_ref[...], dcb_ref[...] = parts

        @pl.when(pl.program_id(0) > 0)
        def _():
            dg_ref[...] += parts[0]
            db_ref[...] += parts[1]
            dcb_ref[...] += parts[2]

    row = pl.BlockSpec((tr, width), lambda i: (i, 0))
    vec = pl.BlockSpec((1, width), lambda i: (0, 0))
    vshp = jax.ShapeDtypeStruct((1, width), F32)
    return pl.pallas_call(
        body, name="ln_silu_bwd", out_shape=(jax.ShapeDtypeStruct((t, width), F32), vshp, vshp, vshp),
        grid=(t // tr,),
        in_specs=[pl.BlockSpec((tr, width), lambda i: (i, col0 // width)), row, vec, vec],
        out_specs=(row, vec, vec, vec), compiler_params=_params(("arbitrary",)),
    )(du, u1, g.reshape(1, width), b.reshape(1, width))


def _glu_conv_bwd(hod, du1, conv_w, *, tr=512, tc=256):
    t = hod.shape[0]
    width = conv_w.shape[1]
    nc = width // tc
    nt = t // tr

    def body(val_ref, gate_ref, valh_ref, gateh_ref, du_ref, dun_ref, w_ref,
             dval_ref, dgate_ref, dw_ref, pad_ref, padd_ref, du0_ref):
        i = pl.program_id(1)
        val = val_ref[...].astype(F32)
        sig = _sigmoid(gate_ref[...].astype(F32))
        halo = valh_ref[...].astype(F32) * _sigmoid(gateh_ref[...].astype(F32))
        pad_ref[0:HALO, :] = jnp.where(i > 0, halo, 0.0)
        pad_ref[HALO:HALO + tr, :] = val * sig
        padd_ref[0:tr, :] = du_ref[...]
        padd_ref[tr:tr + HALO, :] = jnp.where(i < nt - 1, dun_ref[...], 0.0)
        dws = [jnp.zeros((1, tc), F32)] * C_CONV
        for c in range(tr // CONV_ROWS):
            r0 = c * CONV_ROWS
            duc = padd_ref[r0:r0 + CONV_ROWS, :]
            acc = None
            for j in range(C_CONV):
                back = w_ref[j:j + 1, :] * padd_ref[r0 + C_CONV - 1 - j:r0 + C_CONV - 1 - j + CONV_ROWS, :]
                acc = back if acc is None else acc + back
                off = HALO + r0 - (C_CONV - 1) + j
                dws[j] = dws[j] + jnp.sum(duc * pad_ref[off:off + CONV_ROWS, :], axis=0, keepdims=True)
            du0_ref[r0:r0 + CONV_ROWS, :] = acc
        du0 = du0_ref[...]
        dval_ref[...] = (du0 * sig).astype(BF16)
        dgate_ref[...] = (du0 * val * sig * (1.0 - sig)).astype(BF16)
        dw = _stack_rows(dws, 32, tc)

        @pl.when(i == 0)
        def _():
            dw_ref[...] = dw

        @pl.when(i > 0)
        def _():
            dw_ref[...] += dw

    oshp = jax.ShapeDtypeStruct((t, width), BF16)
    wspec = pl.BlockSpec((32, tc), lambda j, i: (0, j))
    return pl.pallas_call(
        body, name="glu_conv_bwd", out_shape=(oshp, oshp, jax.ShapeDtypeStruct((32, width), F32)), grid=(nc, nt),
        in_specs=[_cur_block(tr, tc, 0), _cur_block(tr, tc, nc), _prev_halo(tr, tc, 0), _prev_halo(tr, tc, nc),
                  _cur_block(tr, tc, 0), _next_halo(tr, tc, 0, t), wspec],
        out_specs=(_cur_block(tr, tc, 0), _cur_block(tr, tc, 0), wspec),
        scratch_shapes=[pltpu.VMEM((HALO + tr, tc), F32), pltpu.VMEM((tr + HALO, tc), F32),
                        pltpu.VMEM((tr, tc), F32)],
        compiler_params=_params(("parallel", "arbitrary")),
    )(hod, hod, hod, hod, du1, du1, _pad_rows(conv_w, 32))


def _pooled(pad_ref, g, kw, tr, i):
    gw = pad_ref.shape[1] // len(D_WINDOWS)
    cols = slice(g * gw, (g + 1) * gw)
    tot = None
    for j in range(kw):
        sh = pad_ref[HALO - j:HALO - j + tr, cols]
        tot = sh if tot is None else tot + sh
    return tot / _window_count(tr, gw, kw, i * tr) - pad_ref[HALO:HALO + tr, cols]


def _window_count(rows, width, kw, row0):
    t1 = (lax.broadcasted_iota(jnp.int32, (rows, width), 0) + (row0 + 1)).astype(F32)
    return jnp.minimum(t1, float(kw))


def _pool_fwd(hod, pool_w, pool_scale, *, tr=256):
    t = hod.shape[0]
    width = pool_scale.shape[0]
    ng = len(D_WINDOWS)
    gw = width // ng

    def body(z_ref, zh_ref, w_ref, s_ref, y_ref, pad_ref):
        i = pl.program_id(1)
        pad_ref[0:HALO, :] = jnp.where(i > 0, zh_ref[...].astype(F32), 0.0)
        pad_ref[HALO:HALO + tr, :] = z_ref[...].astype(F32)
        for g, kw in enumerate(D_WINDOWS):
            cols = slice(g * gw, (g + 1) * gw)
            pre = jnp.dot(_pooled(pad_ref, g, kw, tr, i).astype(BF16), w_ref[g], preferred_element_type=F32)
            y_ref[:, cols] = (pre * s_ref[:, cols]).astype(BF16)

    return pl.pallas_call(
        body, name="pool_fwd", out_shape=jax.ShapeDtypeStruct((t, width), BF16), grid=(1, t // tr),
        in_specs=[_cur_block(tr, width, 2), _prev_halo(tr, width, 2),
                  pl.BlockSpec((ng, gw, gw), lambda j, i: (0, 0, 0)), pl.BlockSpec((1, width), lambda j, i: (0, 0))],
        out_specs=_cur_block(tr, width, 0),
        scratch_shapes=[pltpu.VMEM((HALO + tr, width), F32)],
        compiler_params=_params(("parallel", "arbitrary")),
    )(hod, hod, pool_w, pool_scale.reshape(1, width))


def _pool_bwd(hod, dy, pool_w, pool_scale, *, dcol0, tr=256):
    t = hod.shape[0]
    width = pool_scale.shape[0]
    ng = len(D_WINDOWS)
    gw = width // ng
    nt = t // tr

    def body(z_ref, zh_ref, dy_ref, dyn_ref, w_ref, s_ref, dz_ref, dw_ref, ds_ref, pad_ref, pade_ref):
        i = pl.program_id(1)
        pad_ref[0:HALO, :] = jnp.where(i > 0, zh_ref[...].astype(F32), 0.0)
        pad_ref[HALO:HALO + tr, :] = z_ref[...].astype(F32)
        dws = []
        dss = []
        for g, kw in enumerate(D_WINDOWS):
            cols = slice(g * gw, (g + 1) * gw)
            wg = w_ref[g]
            dyc = dy_ref[:, cols]
            dpre = (dyc * s_ref[:, cols]).astype(BF16)
            dpre_n = (dyn_ref[:, cols] * s_ref[:, cols]).astype(BF16)
            dpl = lax.dot_general(dpre, wg, _NT, preferred_element_type=F32)
            dpl_n = lax.dot_general(dpre_n, wg, _NT, preferred_element_type=F32)
            pade_ref[0:tr, cols] = dpl / _window_count(tr, gw, kw, i * tr)
            pade_ref[tr:tr + HALO, cols] = jnp.where(i < nt - 1, dpl_n / _window_count(HALO, gw, kw, (i + 1) * tr), 0.0)
            tot = None
            for j in range(kw):
                sh = pade_ref[j:j + tr, cols]
                tot = sh if tot is None else tot + sh
            dz_ref[:, cols] = (tot - dpl).astype(BF16)
            pooled = _pooled(pad_ref, g, kw, tr, i).astype(BF16)
            pre = jnp.dot(pooled, wg, preferred_element_type=F32)
            dss.append(jnp.sum(dyc * pre, axis=0, keepdims=True))
            dws.append(lax.dot_general(pooled, dpre, _TN, preferred_element_type=F32))

        @pl.when(i == 0)
        def _():
            for g in range(ng):
                dw_ref[g] = dws[g]
                ds_ref[:, g * gw:(g + 1) * gw] = dss[g]

        @pl.when(i > 0)
        def _():
            for g in range(ng):
                dw_ref[g] += dws[g]
                ds_ref[:, g * gw:(g + 1) * gw] += dss[g]

    dc = dcol0 // width
    wspec = pl.BlockSpec((ng, gw, gw), lambda j, i: (0, 0, 0))
    vspec = pl.BlockSpec((1, width), lambda j, i: (0, 0))
    return pl.pallas_call(
        body, name="pool_bwd",
        out_shape=(jax.ShapeDtypeStruct((t, width), BF16), jax.ShapeDtypeStruct((ng, gw, gw), F32),
                   jax.ShapeDtypeStruct((1, width), F32)),
        grid=(1, nt),
        in_specs=[_cur_block(tr, width, 2), _prev_halo(tr, width, 2), _cur_block(tr, width, dc),
                  _next_halo(tr, width, dc, t), wspec, vspec],
        out_specs=(_cur_block(tr, width, 0), wspec, vspec),
        scratch_shapes=[pltpu.VMEM((HALO + tr, width), F32), pltpu.VMEM((tr + HALO, width), F32)],
        compiler_params=_params(("arbitrary", "arbitrary")),
    )(hod, hod, dy, dy, pool_w, pool_scale.reshape(1, width))


TM = 1024
TN = 512
TK_ACC = 512


def _dw_full(a, b, *, m, n, alpha, axis, name):
    t = a.shape[0]
    tm = TM if m % TM == 0 else TN
    return _matmul([(a, b)], ta=True, m=m, n=n, k=t, tm=tm, tn=TN, tk=t, alpha=alpha, out_dtype=BF16, name=name)


def _ffn_fwd(x, g, w1, w3, w2):
    t, d = x.shape
    f = w1.shape[1]
    xn = _rmsnorm_fwd(x, g)
    a, b, h = _ffn_up(xn, w1, w3, tm=TM, tn=TN)
    if callable(w2):
        w2 = w2(h)
    y = _matmul([(h, w2)], m=t, n=d, k=f, tm=TM, tn=TN, tk=f, res=x, alpha=0.5, name="ffn_down")
    return y, (x, xn, a, b, h), [w1, w3, w2]


def _ffn_bwd(dx, dxb, saved, g, w1, w3, w2, push, tok, last=False):
    begin, finish = push
    x, xn, a, b, h = saved
    t, d = x.shape
    f = w1.shape[1]
    da, db = _ffn_dh(dxb, w2, a, b, tok, tm=TM, tn=TN)
    tok = begin([dict(a=xn, b=da, m=d, n=f, alpha=1.0, axis=1, name="ffn_dw1"),
                 dict(a=xn, b=db, m=d, n=f, alpha=1.0, axis=1, name="ffn_dw3"),
                 dict(a=h, b=dxb, m=f, n=d, alpha=0.5, axis=0, name="ffn_dw2")])
    if last:
        tok = finish(da)
    dxn = _matmul([(da, w1), (db, w3)], tb=True, m=t, n=d, k=f, tm=TM, tn=d, tk=TK_ACC, tok=tok, name="ffn_dxn")
    dx, dxb, dg = _rmsnorm_bwd(dxn, x, g, dx)
    return dx, dxb, dg, (tok if last else finish(dx))


def _mix_out_fwd(x, ycat, w_out):
    t, d = x.shape
    return _matmul([(ycat, w_out)], m=t, n=d, k=d, tm=TM, tn=TN, tk=d, res=x, name="mix_out")


def _mix_out_bwd(dxb, w_out, tok):
    t, d = dxb.shape
    return _matmul([(dxb, w_out)], tb=True, m=t, n=d, k=d, tm=TM, tn=TN, tk=d, tok=tok, name="mix_dy")


def _mix_in_bwd(dh, xn, w_in, x, g, dx, ycat, dxb, push):
    begin, finish = push
    t, d = x.shape
    n_in = w_in.shape[1]
    tok = begin([dict(a=xn, b=dh, m=d, n=n_in, alpha=1.0, axis=1, name="mix_dw_in"),
                 dict(a=ycat, b=dxb, m=d, n=d, alpha=1.0, axis=0, name="mix_dw_out")])
    dxn = _matmul([(dh, w_in)], tb=True, m=t, n=d, k=n_in, tm=TM, tn=d, tk=TK_ACC, tok=tok, name="mix_dxn")
    dx, dxb, dg = _rmsnorm_bwd(dxn, x, g, dx)
    return dx, dxb, dg, finish(dx)


def _group_view(a, col0, dil, width=A_WIDTH):
    if dil == 1:
        return a, col0 // width
    t = a.shape[0]
    return a[:, col0:col0 + width].reshape(t // dil, dil * width), 0


def _even_fwd(x, g, w_in, qg, kg, conv_w, w_out):
    t, d = x.shape
    n_in = w_in.shape[1]
    nq = len(A_DILATIONS) * A_WIDTH
    xn = _rmsnorm_fwd(x, g)
    h = _matmul([(xn, w_in)], m=t, n=n_in, k=d, tm=TM, tn=TN, tk=d, out_dtype=BF16, name="ev_in")
    outs, lses, views = [], [], []
    for gi, dil in enumerate(A_DILATIONS):
        qkv = [_group_view(h, part * nq + gi * A_WIDTH, dil) for part in range(3)]
        (qv, oq), (kv, ok), (vv, ov) = qkv
        o, l = _attn_fwd(qv, kv, vv, (oq, ok, ov), qg, kg, dil=dil)
        outs.append(o.reshape(t, A_WIDTH))
        lses.append(l.reshape(t, HEAD_DIM))
        views.append(qkv)
    ya, lse = _attn_combine(outs, lses)
    yb = _gateconv_fwd(h, conv_w, col0=3 * nq)
    ycat = jnp.concatenate([ya, yb], axis=1)
    return _mix_out_fwd(x, ycat, w_out), (x, xn, h, ya, lse, ycat, views)


def _even_bwd(dx, dxb, saved, g, w_in, qg, kg, conv_w, w_out, push, tok):
    x, xn, h, ya, lse, ycat, views = saved
    t, d = x.shape
    nq = len(A_DILATIONS) * A_WIDTH
    dycat = _mix_out_bwd(dxb, w_out, tok)
    dlt, dya = _attn_delta(dycat, ya)
    dqs, dks, dvs = [], [], []
    dqg = jnp.zeros((HEAD_DIM,), F32)
    dkg = jnp.zeros((HEAD_DIM,), F32)
    for gi, dil in enumerate(A_DILATIONS):
        (qv, oq), (kv, ok), (vv, ov) = views[gi]
        dyv, ody = _group_view(dya, 0, dil)
        dq, dk, dv, dqg_i, dkg_i = _attn_bwd(
            qv, kv, vv, dyv, (oq, ok, ov, ody), _group_view(lse, 0, dil, HEAD_DIM)[0],
            _group_view(dlt, 0, dil, HEAD_DIM)[0], qg, kg, dil=dil)
        dqs.append(dq.reshape(t, A_WIDTH))
        dks.append(dk.reshape(t, A_WIDTH))
        dvs.append(dv.reshape(t, A_WIDTH))
        dqg = dqg + dqg_i.reshape(HEAD_DIM)
        dkg = dkg + dkg_i.reshape(HEAD_DIM)
    dbg, dcg, dxt, dcw = _gateconv_bwd(h, dycat, conv_w, col0=3 * nq, dcol0=A_WIDTH)
    dh = jnp.concatenate(dqs + dks + dvs + [dbg, dcg, dxt], axis=1)
    dx, dxb, dg, tok = _mix_in_bwd(dh, xn, w_in, x, g, dx, ycat, dxb, push)
    return dx, dxb, dg, dqg, dkg, dcw[:B_CONV], tok


def _odd_fwd(x, g, w_in, conv_w, conv_b, ln_g, ln_b, pool_w, pool_scale, w_out):
    t, d = x.shape
    n_in = w_in.shape[1]
    xn = _rmsnorm_fwd(x, g)
    hod = _matmul([(xn, w_in)], m=t, n=n_in, k=d, tm=TM, tn=TN, tk=d, out_dtype=BF16, name="od_in")
    u1 = _glu_conv_fwd(hod, conv_w, conv_b)
    u = _ln_silu_fwd(u1, ln_g, ln_b)
    yd = _pool_fwd(hod, pool_w.astype(BF16), pool_scale)
    ycat = jnp.concatenate([u, yd], axis=1)
    return _mix_out_fwd(x, ycat, w_out), (x, xn, hod, u1, ycat)


def _odd_bwd(dx, dxb, saved, g, w_in, conv_w, conv_b, ln_g, ln_b, pool_w, pool_scale, w_out, push, tok):
    x, xn, hod, u1, ycat = saved
    width = conv_w.shape[1]
    dycat = _mix_out_bwd(dxb, w_out, tok)
    du1, dlg, dlb, dcb = _ln_silu_bwd(dycat, u1, ln_g, ln_b, col0=0)
    dval, dgate, dcw = _glu_conv_bwd(hod, du1, conv_w)
    dz, dpw, dps = _pool_bwd(hod, dycat, pool_w.astype(BF16), pool_scale, dcol0=width)
    dh = jnp.concatenate([dval, dgate, dz], axis=1)
    dx, dxb, dg, tok = _mix_in_bwd(dh, xn, w_in, x, g, dx, ycat, dxb, push)
    return (dx, dxb, dg, dcw[:C_CONV], dcb.reshape(width), dlg.reshape(width), dlb.reshape(width),
            dpw, dps.reshape(width), tok)


def _sublayer_matrices(s):
    layer, slot = divmod(s, 3)
    if slot == 1:
        kind = "ev" if layer % 2 == 0 else "od"
        return [(kind + "_w_in", layer // 2), (kind + "_w_out", layer // 2)]
    j = 2 * layer + slot // 2
    return [("ffn_w1", j), ("ffn_w3", j), ("ffn_w2", j)]


def _local_step(x, target, wts, fetch=None, grads=None):
    depth = wts["norm_g"].shape[0]
    if fetch is None:
        fetch = lambda s, after: ([wts[name][idx] for name, idx in _sublayer_matrices(s)], 0.0)
    gr = {}
    if grads is None:
        def begin(s, specs):
            for (name, idx), spec in zip(_sublayer_matrices(s), specs):
                gr.setdefault(name, {})[idx] = _dw_full(**spec)
            return 0.0

        grads = (begin, lambda s, after: 0.0)

    def gain(layer, slot, tok):
        return wts["norm_g"][layer, slot] + tok

    saved = []
    for layer in range(depth):
        i = layer // 2
        s = 3 * layer
        m0, tok = fetch(s, x)
        x, s0, m0 = _ffn_fwd(x, gain(layer, 0, tok), *m0)
        m1, tok = fetch(s + 1, x)
        if layer % 2 == 0:
            x, s1 = _even_fwd(x, gain(layer, 1, tok), m1[0], wts["ev_q_gain"][i],
                              wts["ev_k_gain"][i], wts["ev_conv_w"][i], m1[1])
        else:
            x, s1 = _odd_fwd(x, gain(layer, 1, tok), m1[0], wts["od_conv_w"][i],
                             wts["od_conv_b"][i], wts["od_ln_g"][i], wts["od_ln_b"][i], wts["od_pool_w"][i],
                             wts["od_pool_scale"][i], m1[1])
        m2, tok = fetch(s + 2, x)
        x, s2, m2 = _ffn_fwd(x, gain(layer, 2, tok), *m2)
        saved.append(((s0, m0), (s1, m1), (s2, m2)))
    dx, dxb, sq = _loss_head(x, target)

    n_even, n_odd = (depth + 1) // 2, depth // 2
    for k in ("ev_q_gain", "ev_k_gain", "ev_conv_w"):
        gr[k] = [None] * n_even
    for k in ("od_conv_w", "od_conv_b", "od_ln_g", "od_ln_b", "od_pool_w", "od_pool_scale"):
        gr[k] = [None] * n_odd
    dnorm = [[None] * 3 for _ in range(depth)]
    norm_g = wts["norm_g"]

    def push_for(s):
        return (lambda specs: grads[0](s, specs)), (lambda after: grads[1](s, after))

    tok = 0.0
    for layer in reversed(range(depth)):
        i = layer // 2
        s = 3 * layer
        (s0, m0), (s1, m1), (s2, m2) = saved[layer]
        dx, dxb, dnorm[layer][2], tok = _ffn_bwd(dx, dxb, s2, norm_g[layer, 2], *m2, push_for(s + 2), tok)
        if layer % 2 == 0:
            (dx, dxb, dnorm[layer][1], gr["ev_q_gain"][i], gr["ev_k_gain"][i], gr["ev_conv_w"][i], tok) = _even_bwd(
                dx, dxb, s1, norm_g[layer, 1], m1[0], wts["ev_q_gain"][i],
                wts["ev_k_gain"][i], wts["ev_conv_w"][i], m1[1], push_for(s + 1), tok)
        else:
            (dx, dxb, dnorm[layer][1], gr["od_conv_w"][i], gr["od_conv_b"][i], gr["od_ln_g"][i],
             gr["od_ln_b"][i], gr["od_pool_w"][i], gr["od_pool_scale"][i], tok) = _odd_bwd(
                dx, dxb, s1, norm_g[layer, 1], m1[0], wts["od_conv_w"][i],
                wts["od_conv_b"][i], wts["od_ln_g"][i], wts["od_ln_b"][i], wts["od_pool_w"][i],
                wts["od_pool_scale"][i], m1[1], push_for(s + 1), tok)
        dx, dxb, dnorm[layer][0], tok = _ffn_bwd(dx, dxb, s0, norm_g[layer, 0], *m0, push_for(s), tok,
                                                 last=(layer == 0))
    gr["norm_g"] = jnp.stack([jnp.stack(r) for r in dnorm])
    for name in list(gr):
        if isinstance(gr[name], dict):
            gr[name] = [gr[name][idx] for idx in sorted(gr[name])]
    return sq, dx, gr


HBM_SPEC = pl.BlockSpec(memory_space=pltpu.HBM)
SEM_SPEC = pl.BlockSpec(memory_space=pltpu.SEMAPHORE)
ANY_SPEC = pl.BlockSpec(memory_space=pl.ANY)
EFFECT = pltpu.SideEffectType.DATAFLOW_SIDE_EFFECTING


def _place():
    x, y, c = lax.axis_index("x"), lax.axis_index("y"), lax.axis_index("c")
    chips = [(1 - x, y), (x, 1 - y), (1 - x, 1 - y)]
    return x, y, c, chips


def _chip_index(x, y):
    return 2 * x + y


def _ds(start, size, align):
    if isinstance(start, int):
        return pl.ds(start, size)
    return pl.ds(pl.multiple_of(start, align), size)


def _half(ref, axis, h):
    r, c = ref.shape[-2:]
    if axis == 1:
        return ref.at[_ds(h * (r // 2), r // 2, 16), :]
    return ref.at[:, _ds(h * (c // 2), c // 2, 128)]


def _chunk(ref, axis, j, n=N_CHIPS):
    r, c = ref.shape[-2:]
    if axis == 1:
        return ref.at[:, _ds(j * (c // n), c // n, 128)]
    return ref.at[_ds(j * (r // n), r // n, 16), :]


def _remote(src, dst, send_sem, recv_sem, device):
    return pltpu.make_async_remote_copy(src_ref=src, dst_ref=dst, send_sem=send_sem, recv_sem=recv_sem,
                                        device_id=device, device_id_type=MESH)


def _hbm(a):
    return pltpu.with_memory_space_constraint(a, pltpu.HBM)


def _cast_into(stacked, idx, chip, axis, tok):
    _, r, c = stacked.shape
    full = (r, N_CHIPS * c) if axis == 1 else (N_CHIPS * r, c)
    tr = 128
    while tr > 16 and tr * c * 4 > (1 << 20):
        tr //= 2
    if axis == 1:
        o_map = lambda i, s: (i, s[0])
    else:
        o_map = lambda i, s: (s[0] * (r // tr) + i, 0)

    def body(s_ref, w_ref, tok_ref, o_ref):
        o_ref[...] = (w_ref[...] + tok_ref[0:1, 0:1]).astype(BF16)

    return pl.pallas_call(
        body, name="cast_into", out_shape=jax.ShapeDtypeStruct(full, BF16),
        grid_spec=pltpu.PrefetchScalarGridSpec(
            num_scalar_prefetch=1, grid=(r // tr,),
            in_specs=[pl.BlockSpec((None, tr, c), lambda i, s: (idx, i, 0)),
                      pl.BlockSpec((8, 128), lambda i, s: (0, 0))],
            out_specs=pl.BlockSpec((tr, c), o_map)),
        compiler_params=_params(("parallel",)),
    )(chip, stacked, tok)


def _own_piece(ref, axis, me, cc):
    return _half(_chunk(ref, axis, me), axis, cc)


def _ag_start(fulls, axes, tag=""):
    n = len(fulls)

    def body(*refs):
        ins = refs[:n]
        send, recv = refs[n:4 * n], refs[4 * n:7 * n]
        token = refs[8 * n]
        x, y, cc, chips = _place()
        me = _chip_index(x, y)
        for i in range(n):
            piece = _own_piece(ins[i], axes[i], me, cc)
            for k, chip in enumerate(chips):
                _remote(piece, piece, send[3 * i + k], recv[3 * i + k], (*chip, cc)).start()
        token[...] = jnp.zeros_like(token)

    sem = pltpu.SemaphoreType.DMA(())
    outs = pl.pallas_call(
        body, name="ag_start_%d%s" % (n, tag),
        out_shape=tuple([sem] * (6 * n) + [pltpu.HBM(f.shape, f.dtype) for f in fulls]
                        + [jax.ShapeDtypeStruct((8, 128), F32)]),
        in_specs=[HBM_SPEC] * n,
        out_specs=tuple([SEM_SPEC] * (6 * n) + [HBM_SPEC] * n + [pl.BlockSpec(memory_space=pltpu.VMEM)]),
        input_output_aliases={i: 6 * n + i for i in range(n)},
        compiler_params=pltpu.CompilerParams(has_side_effects=EFFECT),
    )(*[_hbm(f) for f in fulls])
    return outs[:3 * n], outs[3 * n:6 * n], outs[6 * n:7 * n], outs[7 * n]


def _ag_wait(send, recv, fulls, axes, after, tag=""):
    n = len(fulls)

    def body(*refs):
        ins = refs[:n]
        send_s, recv_s = refs[n:4 * n], refs[4 * n:7 * n]
        x, y, cc, chips = _place()
        me = _chip_index(x, y)
        for i in range(n):
            mine = _own_piece(ins[i], axes[i], me, cc)
            for k, chip in enumerate(chips):
                got = _own_piece(ins[i], axes[i], _chip_index(*chip), cc)
                cp = _remote(mine, got, send_s[3 * i + k], recv_s[3 * i + k], (*chip, cc))
                cp.wait_send()
                cp.wait_recv()

    return pl.pallas_call(
        body, name="ag_wait_%d%s" % (n, tag),
        out_shape=tuple(pltpu.HBM(f.shape, f.dtype) for f in fulls),
        in_specs=[HBM_SPEC] * n + [SEM_SPEC] * (6 * n) + [ANY_SPEC],
        out_specs=tuple([HBM_SPEC] * n),
        input_output_aliases={i: i for i in range(n)},
        compiler_params=pltpu.CompilerParams(has_side_effects=EFFECT),
    )(*fulls, *send, *recv, after)


def _ag_forward(fulls, axes, tag=""):
    n = len(fulls)

    def body(*refs):
        ins = refs[:n]
        send_sems, recv_sems = refs[2 * n], refs[2 * n + 1]
        x, y, cc, chips = _place()
        cps = []
        for i in range(n):
            for k, chip in enumerate(chips):
                got = _own_piece(ins[i], axes[i], _chip_index(*chip), cc)
                cp = _remote(got, got, send_sems.at[3 * i + k], recv_sems.at[3 * i + k], (x, y, 1 - cc))
                cp.start()
                cps.append(cp)
        for i in range(n):
            for k, chip in enumerate(chips):
                other = _own_piece(ins[i], axes[i], _chip_index(*chip), 1 - cc)
                cps[3 * i + k].wait_send()
                _remote(other, other, send_sems.at[3 * i + k], recv_sems.at[3 * i + k], (x, y, cc)).wait_recv()

    return pl.pallas_call(
        body, name="ag_forward_%d%s" % (n, tag),
        out_shape=tuple(jax.ShapeDtypeStruct(f.shape, f.dtype) for f in fulls),
        in_specs=[HBM_SPEC] * n, out_specs=tuple([HBM_SPEC] * n),
        input_output_aliases={i: i for i in range(n)},
        scratch_shapes=[pltpu.SemaphoreType.DMA((3 * n,)), pltpu.SemaphoreType.DMA((3 * n,))],
    )(*fulls)


def _pair_start(gs):
    n = len(gs)
    lands = [lax.empty(g.shape, g.dtype) for g in gs]

    def body(*refs):
        g_refs, land_refs = refs[:n], refs[n:2 * n]
        send, recv = refs[2 * n:3 * n], refs[3 * n:4 * n]
        token = refs[6 * n]
        x, y, cc, _ = _place()
        for i in range(n):
            _remote(g_refs[i], land_refs[i], send[i], recv[i], (x, y, 1 - cc)).start()
        token[...] = jnp.zeros_like(token)

    sem = pltpu.SemaphoreType.DMA(())
    outs = pl.pallas_call(
        body, name="pair_start_%d" % n,
        out_shape=tuple([sem] * (2 * n) + [pltpu.HBM(a.shape, a.dtype) for a in list(gs) + lands]
                        + [jax.ShapeDtypeStruct((8, 128), F32)]),
        in_specs=[HBM_SPEC] * (2 * n),
        out_specs=tuple([SEM_SPEC] * (2 * n) + [HBM_SPEC] * (2 * n) + [pl.BlockSpec(memory_space=pltpu.VMEM)]),
        input_output_aliases={i: 2 * n + i for i in range(2 * n)},
        compiler_params=pltpu.CompilerParams(has_side_effects=EFFECT),
    )(*[_hbm(a) for a in list(gs) + lands])
    return outs[:n], outs[n:2 * n], outs[2 * n:3 * n], outs[3 * n:4 * n], outs[4 * n][0, 0]


def _pair_wait(send, recv, gs, lands, after):
    n = len(gs)

    def body(*refs):
        g_refs, land_refs = refs[:n], refs[n:2 * n]
        send_s, recv_s = refs[2 * n:3 * n], refs[3 * n:4 * n]
        x, y, cc, _ = _place()
        for i in range(n):
            cp = _remote(g_refs[i], land_refs[i], send_s[i], recv_s[i], (x, y, 1 - cc))
            cp.wait_send()
            cp.wait_recv()

    outs = pl.pallas_call(
        body, name="pair_wait_%d" % n,
        out_shape=tuple(pltpu.HBM(a.shape, a.dtype) for a in list(gs) + list(lands)),
        in_specs=[HBM_SPEC] * (2 * n) + [SEM_SPEC] * (2 * n) + [ANY_SPEC],
        out_specs=tuple([HBM_SPEC] * (2 * n)),
        input_output_aliases={i: i for i in range(2 * n)},
        compiler_params=pltpu.CompilerParams(has_side_effects=EFFECT),
    )(*gs, *lands, *send, *recv, after)
    return outs[n:]


def _pair_begin(specs, core):
    calls = []
    for spec in specs:
        t = spec["a"].shape[0]
        m, n = spec["m"], spec["n"]
        if spec["axis"] == 1:
            dims = dict(m=m // 2, n=n, tm=min(TM, m // 2), tn=TN)
            operand = "a"
        else:
            dims = dict(m=m, n=n // 2, tm=TN, tn=n // 2)
            operand = "b"
        calls.append((spec, operand, dict(ta=True, k=t, tk=t, alpha=spec["alpha"], out_dtype=BF16, **dims)))
    sent = [_matmul([(spec["a"], spec["b"])], half=(core, False, operand), name=spec["name"] + "_sib", **kw)
            for spec, operand, kw in calls]
    send, recv, sent, lands, tok = _pair_start(sent)
    return (calls, core, send, recv, sent, lands), tok


def _pair_finish(state, after):
    calls, core, send, recv, sent, lands = state
    got = _pair_wait(send, recv, sent, lands, after)
    return [_matmul([(spec["a"], spec["b"])], half=(core, True, operand), res=r, name=spec["name"], **kw)
            for (spec, operand, kw), r in zip(calls, got)]


def _piece_shape(p, axis):
    r, c = p.shape
    return (r, c // N_CHIPS) if axis == 1 else (r // N_CHIPS, c)


def _rs_chips_start(ps, axes):
    n = len(ps)
    lands = [lax.empty((3,) + _piece_shape(p, ax), p.dtype) for p, ax in zip(ps, axes)]

    def body(*refs):
        p_refs, land_refs = refs[:n], refs[n:2 * n]
        send, recv = refs[2 * n:5 * n], refs[5 * n:8 * n]
        token = refs[10 * n]
        x, y, cc, chips = _place()
        for i in range(n):
            for k, chip in enumerate(chips):
                _remote(_chunk(p_refs[i], axes[i], _chip_index(*chip)), land_refs[i].at[k],
                        send[3 * i + k], recv[3 * i + k], (*chip, cc)).start()
        token[...] = jnp.zeros_like(token)

    sem = pltpu.SemaphoreType.DMA(())
    outs = pl.pallas_call(
        body, name="rs_start_%d" % n,
        out_shape=tuple([sem] * (6 * n) + [pltpu.HBM(a.shape, a.dtype) for a in list(ps) + lands]
                        + [jax.ShapeDtypeStruct((8, 128), F32)]),
        in_specs=[HBM_SPEC] * (2 * n),
        out_specs=tuple([SEM_SPEC] * (6 * n) + [HBM_SPEC] * (2 * n) + [pl.BlockSpec(memory_space=pltpu.VMEM)]),
        input_output_aliases={i: 6 * n + i for i in range(2 * n)},
        compiler_params=pltpu.CompilerParams(has_side_effects=EFFECT),
    )(*[_hbm(a) for a in list(ps) + lands])
    return outs[:3 * n], outs[3 * n:6 * n], outs[6 * n:7 * n], outs[7 * n:8 * n], outs[8 * n][0, 0]


def _rs_chips_wait(send, recv, ps, lands, axes, after):
    n = len(ps)

    def body(*refs):
        p_refs, land_refs = refs[:n], refs[n:2 * n]
        send_s, recv_s = refs[2 * n:5 * n], refs[5 * n:8 * n]
        x, y, cc, chips = _place()
        for i in range(n):
            for k, chip in enumerate(chips):
                cp = _remote(_chunk(p_refs[i], axes[i], _chip_index(*chip)), land_refs[i].at[k],
                             send_s[3 * i + k], recv_s[3 * i + k], (*chip, cc))
                cp.wait_send()
                cp.wait_recv()

    outs = pl.pallas_call(
        body, name="rs_wait_%d" % n,
        out_shape=tuple(pltpu.HBM(a.shape, a.dtype) for a in list(ps) + list(lands)),
        in_specs=[HBM_SPEC] * (2 * n) + [SEM_SPEC] * (6 * n) + [ANY_SPEC],
        out_specs=tuple([HBM_SPEC] * (2 * n)),
        input_output_aliases={i: i for i in range(2 * n)},
        compiler_params=pltpu.CompilerParams(has_side_effects=EFFECT),
    )(*ps, *lands, *send, *recv, after)
    return outs[:n], outs[n:]


def _add_chips(p, got, chip, core, axis, *, idx, count, into, tr, tc):
    _, pr, pc = got.shape
    shard = (2 * pr, pc) if axis == 1 else (pr, 2 * pc)
    if axis == 1:
        p_map = lambda i, j, sc, so: (i, sc[0] * (pc // tc) + j)
        o_map = lambda i, j, sc, so: (idx, so[0] * (pr // tr) + i, j)
    else:
        p_map = lambda i, j, sc, so: (sc[0] * (pr // tr) + i, j)
        o_map = lambda i, j, sc, so: (idx, i, so[0] * (pc // tc) + j)

    def body(sc_ref, so_ref, p_ref, r_ref, *rest):
        o_ref = rest[-1]
        acc = p_ref[...].astype(F32)
        for k in range(3):
            acc = acc + r_ref[k].astype(F32)
        o_ref[...] = acc

    in_specs = [pl.BlockSpec((tr, tc), p_map), pl.BlockSpec((3, tr, tc), lambda i, j, sc, so: (0, i, j))]
    args = [chip, core, p, got]
    aliases = {}
    if into is not None:
        in_specs.append(ANY_SPEC)
        args.append(into)
        aliases = {4: 0}
    return pl.pallas_call(
        body, name="add_chips", out_shape=jax.ShapeDtypeStruct((count,) + shard, F32),
        grid_spec=pltpu.PrefetchScalarGridSpec(
            num_scalar_prefetch=2, grid=(pr // tr, pc // tc), in_specs=in_specs,
            out_specs=pl.BlockSpec((None, tr, tc), o_map)),
        input_output_aliases=aliases,
        compiler_params=_params(("parallel", "parallel")),
    )(*args)


def _rs_final(stacked, axis):
    n = stacked.shape[0]

    def body(s_ref, o_ref, send_sems, recv_sems):
        x, y, cc, _ = _place()
        cps = []
        for i in range(n):
            mine = _half(s_ref.at[i], axis, cc)
            cp = _remote(mine, mine, send_sems.at[i], recv_sems.at[i], (x, y, 1 - cc))
            cp.start()
            cps.append(cp)
        for i, cp in enumerate(cps):
            other = _half(s_ref.at[i], axis, 1 - cc)
            cp.wait_send()
            _remote(other, other, send_sems.at[i], recv_sems.at[i], (x, y, cc)).wait_recv()

    return pl.pallas_call(
        body, name="rs_final", out_shape=jax.ShapeDtypeStruct(stacked.shape, stacked.dtype),
        in_specs=[HBM_SPEC], out_specs=HBM_SPEC, input_output_aliases={0: 0},
        scratch_shapes=[pltpu.SemaphoreType.DMA((n,)), pltpu.SemaphoreType.DMA((n,))],
    )(stacked)


def _ag_small(packed):
    rows, cols = packed.shape

    def body(s_ref, o_ref, tok_ref, send_sems, recv_sems, local_sem):
        x, y, cc, chips = _place()
        me = _chip_index(x, y)
        own = pltpu.make_async_copy(s_ref, o_ref.at[me], local_sem)
        own.start()
        cps = [_remote(s_ref, o_ref.at[me], send_sems.at[k], recv_sems.at[k], (*chip, cc))
               for k, chip in enumerate(chips)]
        for cp in cps:
            cp.start()
        for k, chip in enumerate(chips):
            cps[k].wait_send()
            got = o_ref.at[_chip_index(*chip)]
            _remote(got, got, send_sems.at[k], recv_sems.at[k], (x, y, cc)).wait_recv()
        own.wait()
        tok_ref[...] = jnp.zeros_like(tok_ref)

    return pl.pallas_call(
        body, name="ag_small",
        out_shape=(jax.ShapeDtypeStruct((N_CHIPS, rows, cols), packed.dtype), jax.ShapeDtypeStruct((8, 128), F32)),
        in_specs=[HBM_SPEC], out_specs=(HBM_SPEC, pl.BlockSpec(memory_space=pltpu.VMEM)),
        scratch_shapes=[pltpu.SemaphoreType.DMA((3,)), pltpu.SemaphoreType.DMA((3,)), pltpu.SemaphoreType.DMA],
    )(packed)


def _rs_small(packed):
    _, rows, cols = packed.shape
    rels = [(bx, by, bc) for bx in (0, 1) for by in (0, 1) for bc in (0, 1)][1:]

    def body(s_ref, o_ref, send_sems, recv_sems, local_sem):
        x, y, cc, _ = _place()
        me = 4 * x + 2 * y + cc
        own = pltpu.make_async_copy(s_ref.at[_chip_index(x, y)], o_ref.at[me], local_sem)
        own.start()
        peers = [(jnp.bitwise_xor(x, bx), jnp.bitwise_xor(y, by), jnp.bitwise_xor(cc, bc)) for bx, by, bc in rels]
        cps = [_remote(s_ref.at[_chip_index(px, py)], o_ref.at[me], send_sems.at[k], recv_sems.at[k], (px, py, pc))
               for k, (px, py, pc) in enumerate(peers)]
        for cp in cps:
            cp.start()
        for k, (px, py, pc) in enumerate(peers):
            cps[k].wait_send()
            got = o_ref.at[4 * px + 2 * py + pc]
            _remote(got, got, send_sems.at[k], recv_sems.at[k], (x, y, cc)).wait_recv()
        own.wait()

    return pl.pallas_call(
        body, name="rs_small", out_shape=jax.ShapeDtypeStruct((2 * N_CHIPS, rows, cols), packed.dtype),
        in_specs=[HBM_SPEC], out_specs=HBM_SPEC,
        scratch_shapes=[pltpu.SemaphoreType.DMA((7,)), pltpu.SemaphoreType.DMA((7,)), pltpu.SemaphoreType.DMA],
    )(packed)


def _sum_slots(slots, *, tr=8):
    n, rows, cols = slots.shape

    def body(s_ref, o_ref):
        acc = s_ref[0]
        for k in range(1, n):
            acc = acc + s_ref[k]
        o_ref[...] = acc

    return pl.pallas_call(
        body, name="sum_slots", out_shape=jax.ShapeDtypeStruct((rows, cols), F32), grid=(rows // tr,),
        in_specs=[pl.BlockSpec((n, tr, cols), lambda i: (0, i, 0))], out_specs=pl.BlockSpec((tr, cols), lambda i: (i, 0)),
        compiler_params=_params(("parallel",)),
    )(slots)


MATRIX_AXIS = {"ffn_w1": 1, "ffn_w3": 1, "ffn_w2": 0, "ev_w_in": 1, "ev_w_out": 0, "od_w_in": 1, "od_w_out": 0}
SMALL_SHARDED = ("norm_g", "ev_conv_w", "od_conv_w", "od_conv_b", "od_ln_g", "od_ln_b", "od_pool_scale")
SMALL_REPLICATED = ("ev_q_gain", "ev_k_gain")
PACK_COLS = 1024
PACK_ROW_ALIGN = 8


def _pack(parts):
    flat = jnp.concatenate([p.reshape(-1).astype(F32) for p in parts])
    per = PACK_COLS * PACK_ROW_ALIGN
    total = -(-flat.shape[0] // per) * per
    return jnp.pad(flat, (0, total - flat.shape[0])).reshape(total // PACK_COLS, PACK_COLS)


def _unpack(packed, shapes):
    flat = packed.reshape(-1)
    out, pos = [], 0
    for shp in shapes:
        size = 1
        for s in shp:
            size *= s
        out.append(flat[pos:pos + size].reshape(shp))
        pos += size
    return out


def _row_tile(rows, cols, itemsize=4, budget=1 << 20):
    tr = 8
    while rows % (2 * tr) == 0 and 2 * tr * cols * itemsize <= budget:
        tr *= 2
    return tr


def _piece_tiles(hr, hc, axis):
    pr, pc = (hr, hc // N_CHIPS) if axis == 1 else (hr // N_CHIPS, hc)
    tr = 128
    while tr > 16 and tr * pc * 4 > (1 << 20):
        tr //= 2
    return tr, pc


AG_LOOKAHEAD = 2


def kernel(x, norm_g, ffn_w1, ffn_w3, ffn_w2, ev_w_in, ev_q_gain, ev_k_gain, ev_conv_w, ev_w_out, od_w_in, od_conv_w, od_conv_b, od_ln_g, od_ln_b, od_pool_w, od_pool_scale, od_w_out, loss_target, m_norm_g, m_ffn_w1, m_ffn_w3, m_ffn_w2, m_ev_w_in, m_ev_q_gain, m_ev_k_gain, m_ev_conv_w, m_ev_w_out, m_od_w_in, m_od_conv_w, m_od_conv_b, m_od_ln_g, m_od_ln_b, m_od_pool_w, m_od_pool_scale, m_od_w_out, v_norm_g, v_ffn_w1, v_ffn_w3, v_ffn_w2, v_ev_w_in, v_ev_q_gain, v_ev_k_gain, v_ev_conv_w, v_ev_w_out, v_od_w_in, v_od_conv_w, v_od_conv_b, v_od_ln_g, v_od_ln_b, v_od_pool_w, v_od_pool_scale, v_od_w_out):
    names = ["norm_g", "ffn_w1", "ffn_w3", "ffn_w2", "ev_w_in", "ev_q_gain", "ev_k_gain", "ev_conv_w", "ev_w_out",
             "od_w_in", "od_conv_w", "od_conv_b", "od_ln_g", "od_ln_b", "od_pool_w", "od_pool_scale", "od_w_out"]
    w = dict(zip(names, (norm_g, ffn_w1, ffn_w3, ffn_w2, ev_w_in, ev_q_gain, ev_k_gain, ev_conv_w, ev_w_out,
                         od_w_in, od_conv_w, od_conv_b, od_ln_g, od_ln_b, od_pool_w, od_pool_scale, od_w_out)))
    m = dict(zip(names, (m_norm_g, m_ffn_w1, m_ffn_w3, m_ffn_w2, m_ev_w_in, m_ev_q_gain, m_ev_k_gain, m_ev_conv_w,
                         m_ev_w_out, m_od_w_in, m_od_conv_w, m_od_conv_b, m_od_ln_g, m_od_ln_b, m_od_pool_w,
                         m_od_pool_scale, m_od_w_out)))
    v = dict(zip(names, (v_norm_g, v_ffn_w1, v_ffn_w3, v_ffn_w2, v_ev_w_in, v_ev_q_gain, v_ev_k_gain, v_ev_conv_w,
                         v_ev_w_out, v_od_w_in, v_od_conv_w, v_od_conv_b, v_od_ln_g, v_od_ln_b, v_od_pool_w,
                         v_od_pool_scale, v_od_w_out)))
    cx, cy, cc = lax.axis_index("x"), lax.axis_index("y"), lax.axis_index("c")
    cc_arr = jnp.reshape(cc, (1,)).astype(jnp.int32)
    chip_arr = jnp.reshape(_chip_index(cx, cy), (1,)).astype(jnp.int32)
    n_sub = 3 * norm_g.shape[0]
    stacked = {name: w[name].reshape((-1,) + w[name].shape[-2:]) for name in MATRIX_AXIS}

    def axes_of(s):
        return [MATRIX_AXIS[name] for name, _ in _sublayer_matrices(s)]

    ag_inflight = {}

    wts = {}
    small_names = SMALL_SHARDED + ("od_pool_w",)
    gathered, small_tok = _ag_small(_pack([w[k] for k in small_names]))
    per_chip = [_unpack(gathered[j], [w[k].shape for k in small_names]) for j in range(N_CHIPS)]
    for idx, k in enumerate(small_names):
        ax = 2 if k == "od_pool_w" else w[k].ndim - 1
        wts[k] = jnp.concatenate([per_chip[j][idx] for j in range(N_CHIPS)], axis=ax)
    for k in SMALL_REPLICATED:
        wts[k] = w[k]

    ag_tokens = [small_tok]

    def ag_start_group(key, mats, tag=""):
        axes = [MATRIX_AXIS[name] for name, _ in mats]
        fulls = [_cast_into(stacked[name], idx, chip_arr, MATRIX_AXIS[name], ag_tokens[-1]) for name, idx in mats]
        send, recv, fulls, tok = _ag_start(fulls, axes, tag)
        ag_inflight[key] = (send, recv, fulls, axes, tag)
        ag_tokens.append(tok)
        return tok[0, 0]

    def ag_finish_group(key, after):
        send, recv, fulls, axes, tag = ag_inflight.pop(key)
        return list(_ag_forward(_ag_wait(send, recv, fulls, axes, after, tag), axes, tag))

    def ag_start(s):
        mats = _sublayer_matrices(s)
        if s == 0:
            return ag_start_group((s, 0), mats[:2], "_up") + ag_start_group((s, 1), mats[2:], "_down")
        return ag_start_group((s, 0), mats)

    x0 = x[0]
    first_tok = sum(ag_start(s) for s in range(min(AG_LOOKAHEAD, n_sub)))

    def fetch(s, after):
        tok = first_tok if s == 0 else 0.0
        if s + AG_LOOKAHEAD < n_sub:
            tok = tok + ag_start(s + AG_LOOKAHEAD)
        mats = ag_finish_group((s, 0), after)
        if s == 0:
            mats.append(lambda behind: ag_finish_group((s, 1), behind)[0])
        return mats, tok

    rs_inflight = {}

    pair_inflight = {}

    def grads_begin(s, specs):
        pair_inflight[s], tok = _pair_begin(specs, cc_arr)
        return tok

    def grads_finish(s, after):
        send, recv, ps, lands, tok = _rs_chips_start(_pair_finish(pair_inflight.pop(s), after), axes_of(s))
        rs_inflight[s] = (send, recv, ps, lands)
        return tok

    sq, dx, gr = _local_step(x0, loss_target[0], wts, fetch, (grads_begin, grads_finish))

    grads = {}
    reduced = {name: None for name in MATRIX_AXIS}
    for s in reversed(range(n_sub)):
        send, recv, ps, lands = rs_inflight.pop(s)
        axes = axes_of(s)
        ps, lands = _rs_chips_wait(send, recv, ps, lands, axes, dx)
        for (name, idx), p, land, ax in zip(_sublayer_matrices(s), ps, lands, axes):
            tr, tc = _piece_tiles(*p.shape, ax)
            reduced[name] = _add_chips(p, land, chip_arr, cc_arr, ax, idx=idx, count=stacked[name].shape[0],
                                       into=reduced[name], tr=tr, tc=tc)
    for name, axis in MATRIX_AXIS.items():
        grads[name] = _rs_final(reduced[name], axis).reshape(w[name].shape)
    small_full = {k: (gr[k] if k == "norm_g" else jnp.stack(gr[k])) for k in small_names + SMALL_REPLICATED}
    chunks = []
    for j in range(N_CHIPS):
        parts = []
        for k in small_names:
            ax = 2 if k == "od_pool_w" else w[k].ndim - 1
            size = w[k].shape[ax]
            parts.append(lax.slice_in_dim(small_full[k], j * size, (j + 1) * size, axis=ax))
        parts += [small_full[k] for k in SMALL_REPLICATED] + [sq[0, :1]]
        chunks.append(_pack(parts))
    summed = _sum_slots(_rs_small(jnp.stack(chunks)))
    pack_names = small_names + SMALL_REPLICATED
    unpacked = _unpack(summed, [w[k].shape for k in pack_names] + [(1,)])
    for k, g in zip(pack_names, unpacked):
        grads[k] = g
    loss = (0.5 / x.shape[-1]) * unpacked[-1][0]

    delta, new_m, new_v = {}, {}, {}
    for name in MATRIX_AXIS:
        shp = w[name].shape
        cols = shp[-1]
        rows = w[name].size // cols
        tr = _row_tile(rows, cols, budget=1 << 20)
        d_, m_, v_, g_ = _adamw(w[name].reshape(rows, cols), grads[name].reshape(rows, cols),
                                m[name].reshape(rows, cols), v[name].reshape(rows, cols), tr=tr)
        delta[name], new_m[name], new_v[name] = d_.reshape(shp), m_.reshape(shp), v_.reshape(shp)
        grads[name] = g_.reshape(shp)
    pw, pg, pm, pv = (_pack([t[k] for k in pack_names]) for t in (w, grads, m, v))
    d_, m_, v_, _ = _adamw(pw, pg, pm, pv, tr=PACK_ROW_ALIGN)
    shapes = [w[k].shape for k in pack_names]
    for store, packed in ((delta, d_), (new_m, m_), (new_v, v_)):
        for k, a in zip(pack_names, _unpack(packed, shapes)):
            store[k] = a

    return (loss, dx[None], *[grads[k] for k in names], *[delta[k] for k in names],
            *[new_m[k] for k in names], *[new_v[k] for k in names])
```

```python
import jax
import jax.numpy as jnp
from jax import lax
from jax.experimental import pallas as pl
from jax.experimental.pallas import tpu as pltpu

F32 = jnp.float32
BF16 = jnp.bfloat16
MESH = pl.DeviceIdType.MESH

EPS = 1e-6
HEADS = 8
HEAD_DIM = 128
A_WIDTH = HEADS * HEAD_DIM
A_WINDOWS = (128, 512, 2048)
A_DILATIONS = (1, 4, 16)
ATTN_BLOCK = 128
B_CONV = 3
C_CONV = 31
D_WINDOWS = (2, 4, 8, 16)
HALO = 32
N_CHIPS = 4
ADAM_LR = 0.001
ADAM_B1 = 0.9
ADAM_B2 = 0.999
ADAM_EPS = 1e-08
ADAM_WD = 0.01
ADAM_STEP = 10
VMEM_LIMIT_BYTES = 56 * 1024 * 1024
NEG_BIG = -1e30


def _params(sem, **kw):
    return pltpu.CompilerParams(dimension_semantics=sem, vmem_limit_bytes=VMEM_LIMIT_BYTES, **kw)


def _sigmoid(x):
    return 1.0 / (1.0 + jnp.exp(-x))


EPILOGUE_COLS = 256


def _matmul(pairs, *, m, n, k, tm, tn, tk, ta=False, tb=False, out_dtype=F32, res=None,
            alpha=1.0, name, j_outer=False, half=None, tok=None):
    nk = k // tk
    npairs = len(pairs)
    dn = (((0 if ta else 1,), (1 if tb else 0,)), ((), ()))

    def ij(p, q):
        return (q, p) if j_outer else (p, q)

    def shift(s, operand, blocks):
        if half is None or half[2] != operand:
            return 0
        h = s[0][0]
        return (h if half[1] else 1 - h) * blocks

    def a_map(p, q, kk, *s):
        i = ij(p, q)[0] + shift(s, "a", m // tm)
        return (kk, i) if ta else (i, kk)

    def b_map(p, q, kk, *s):
        j = ij(p, q)[1] + shift(s, "b", n // tn)
        return (j, kk) if tb else (kk, j)

    def o_map(p, q, kk, *s):
        return ij(p, q)

    def body(*refs):
        if half is not None:
            refs = refs[1:]
        ab = refs[:2 * npairs]
        pos = 2 * npairs
        res_ref = tok_ref = None
        if res is not None:
            res_ref = refs[pos]
            pos += 1
        if tok is not None:
            tok_ref = refs[pos]
            pos += 1
        o_ref = refs[pos]
        acc_ref = refs[pos + 1] if nk > 1 else None

        def dots():
            tot = None
            for p in range(npairs):
                d = lax.dot_general(ab[2 * p][...], ab[2 * p + 1][...], dn, preferred_element_type=F32)
                tot = d if tot is None else tot + d
            return tot

        def finish(acc):
            r = acc * alpha if alpha != 1.0 else acc
            if res_ref is not None:
                r = res_ref[...].astype(F32) + r
            if tok_ref is not None:
                r = r + tok_ref[0:1, 0:1]
            o_ref[...] = r.astype(o_ref.dtype)

        if nk == 1:
            finish(dots())
        else:
            kk = pl.program_id(2)

            @pl.when(kk == 0)
            def _():
                acc_ref[...] = dots()

            @pl.when(kk > 0)
            def _():
                acc_ref[...] += dots()

            @pl.when(kk == nk - 1)
            def _():
                finish(acc_ref[...])

    a_spec = pl.BlockSpec((tk, tm) if ta else (tm, tk), a_map)
    b_spec = pl.BlockSpec((tn, tk) if tb else (tk, tn), b_map)
    o_spec = pl.BlockSpec((tm, tn), o_map)
    in_specs = [a_spec, b_spec] * npairs
    args = [t for p in pairs for t in p]
    if res is not None:
        in_specs.append(o_spec)
        args.append(res)
    if tok is not None:
        in_specs.append(pl.BlockSpec((8, 128), lambda p, q, kk, *s: (0, 0)))
        args.append(jnp.full((8, 128), tok, F32))
    grid = ij(m // tm, n // tn) + (nk,)
    scratch = [pltpu.VMEM((tm, tn), F32)] if nk > 1 else []
    if half is None:
        kwargs = dict(grid=grid, in_specs=in_specs, out_specs=o_spec, scratch_shapes=scratch)
    else:
        args = [half[0]] + args
        kwargs = dict(grid_spec=pltpu.PrefetchScalarGridSpec(
            num_scalar_prefetch=1, grid=grid, in_specs=in_specs, out_specs=o_spec, scratch_shapes=scratch))
    return pl.pallas_call(
        body, name=name, out_shape=jax.ShapeDtypeStruct((m, n), out_dtype),
        compiler_params=_params(("parallel", "parallel", "arbitrary")), **kwargs,
    )(*args)


def _ffn_up(xn, w1, w3, *, tm, tn):
    t, d = xn.shape
    f = w1.shape[1]

    def body(x_ref, w1_ref, w3_ref, a_ref, b_ref, h_ref):
        x = x_ref[...]
        for c in range(tn // EPILOGUE_COLS):
            cols = slice(c * EPILOGUE_COLS, (c + 1) * EPILOGUE_COLS)
            a = jnp.dot(x, w1_ref[:, cols], preferred_element_type=F32)
            b = jnp.dot(x, w3_ref[:, cols], preferred_element_type=F32)
            a_ref[:, cols] = a.astype(BF16)
            b_ref[:, cols] = b.astype(BF16)
            h_ref[:, cols] = (a * _sigmoid(a) * b).astype(BF16)

    x_spec = pl.BlockSpec((tm, d), lambda i, j: (i, 0))
    w_spec = pl.BlockSpec((d, tn), lambda i, j: (0, j))
    o_spec = pl.BlockSpec((tm, tn), lambda i, j: (i, j))
    shp = jax.ShapeDtypeStruct((t, f), BF16)
    return pl.pallas_call(
        body, name="ffn_up", out_shape=(shp, shp, shp), grid=(t // tm, f // tn),
        in_specs=[x_spec, w_spec, w_spec], out_specs=(o_spec, o_spec, o_spec),
        compiler_params=_params(("parallel", "parallel")),
    )(xn, w1, w3)


def _ffn_dh(dyb, w2, a, b, tok, *, tm, tn):
    t, d = dyb.shape
    f = w2.shape[0]
    dn = (((1,), (1,)), ((), ()))

    def body(dy_ref, w2_ref, a_ref, b_ref, tok_ref, da_ref, db_ref):
        dy = dy_ref[...]
        for c in range(tn // EPILOGUE_COLS):
            cols = slice(c * EPILOGUE_COLS, (c + 1) * EPILOGUE_COLS)
            dh = 0.5 * lax.dot_general(dy, w2_ref[cols, :], dn, preferred_element_type=F32) + tok_ref[0:1, 0:1]
            av = a_ref[:, cols].astype(F32)
            bv = b_ref[:, cols].astype(F32)
            sig = _sigmoid(av)
            da_ref[:, cols] = (dh * bv * (sig * (1.0 + av * (1.0 - sig)))).astype(BF16)
            db_ref[:, cols] = (dh * (av * sig)).astype(BF16)

    dy_spec = pl.BlockSpec((tm, d), lambda i, j: (i, 0))
    w_spec = pl.BlockSpec((tn, d), lambda i, j: (j, 0))
    o_spec = pl.BlockSpec((tm, tn), lambda i, j: (i, j))
    shp = jax.ShapeDtypeStruct((t, f), BF16)
    return pl.pallas_call(
        body, name="ffn_dh", out_shape=(shp, shp), grid=(t // tm, f // tn),
        in_specs=[dy_spec, w_spec, o_spec, o_spec, pl.BlockSpec((8, 128), lambda i, j: (0, 0))],
        out_specs=(o_spec, o_spec),
        compiler_params=_params(("parallel", "parallel")),
    )(dyb, w2, a, b, jnp.full((8, 128), tok, F32))


def _rmsnorm_fwd(x, g, *, tr=256):
    t, d = x.shape

    def body(x_ref, g_ref, o_ref):
        xv = x_ref[...]
        y = xv * lax.rsqrt(jnp.mean(xv * xv, axis=-1, keepdims=True) + EPS)
        o_ref[...] = (y * g_ref[...]).astype(BF16)

    return pl.pallas_call(
        body, name="rmsnorm_fwd", out_shape=jax.ShapeDtypeStruct((t, d), BF16), grid=(t // tr,),
        in_specs=[pl.BlockSpec((tr, d), lambda i: (i, 0)), pl.BlockSpec((1, d), lambda i: (0, 0))],
        out_specs=pl.BlockSpec((tr, d), lambda i: (i, 0)),
        compiler_params=_params(("parallel",)),
    )(x, g.reshape(1, d))


def _rmsnorm_bwd(dy, x, g, dres, *, tr=256):
    t, d = x.shape

    def body(dy_ref, x_ref, g_ref, dres_ref, dx_ref, dxb_ref, dg_ref):
        xv = x_ref[...]
        dyv = dy_ref[...].astype(F32)
        r = lax.rsqrt(jnp.mean(xv * xv, axis=-1, keepdims=True) + EPS)
        xhat = xv * r
        dxhat = dyv * g_ref[...]
        c = jnp.mean(dxhat * xhat, axis=-1, keepdims=True)
        dx = dres_ref[...] + r * (dxhat - xhat * c)
        dx_ref[...] = dx
        dxb_ref[...] = dx.astype(BF16)
        part = jnp.sum(dyv * xhat, axis=0, keepdims=True)

        @pl.when(pl.program_id(0) == 0)
        def _():
            dg_ref[...] = part

        @pl.when(pl.program_id(0) > 0)
        def _():
            dg_ref[...] += part

    row = pl.BlockSpec((tr, d), lambda i: (i, 0))
    vec = pl.BlockSpec((1, d), lambda i: (0, 0))
    dx, dxb, dg = pl.pallas_call(
        body, name="rmsnorm_bwd",
        out_shape=(jax.ShapeDtypeStruct((t, d), F32), jax.ShapeDtypeStruct((t, d), BF16),
                   jax.ShapeDtypeStruct((1, d), F32)),
        grid=(t // tr,), in_specs=[row, row, vec, row], out_specs=(row, row, vec),
        compiler_params=_params(("arbitrary",)),
    )(dy, x, g.reshape(1, d), dres)
    return dx, dxb, dg.reshape(d)


def _loss_head(y, target, *, tr=256):
    t, d = y.shape

    def body(y_ref, t_ref, dy_ref, dyb_ref, s_ref):
        err = y_ref[...] - t_ref[...]
        dy = err * (1.0 / d)
        dy_ref[...] = dy
        dyb_ref[...] = dy.astype(BF16)
        part = jnp.full((1, 128), jnp.sum(err * err), F32)

        @pl.when(pl.program_id(0) == 0)
        def _():
            s_ref[...] = part

        @pl.when(pl.program_id(0) > 0)
        def _():
            s_ref[...] += part

    row = pl.BlockSpec((tr, d), lambda i: (i, 0))
    return pl.pallas_call(
        body, name="loss_head",
        out_shape=(jax.ShapeDtypeStruct((t, d), F32), jax.ShapeDtypeStruct((t, d), BF16),
                   jax.ShapeDtypeStruct((1, 128), F32)),
        grid=(t // tr,), in_specs=[row, row],
        out_specs=(row, row, pl.BlockSpec((1, 128), lambda i: (0, 0))),
        compiler_params=_params(("arbitrary",)),
    )(y, target)


def _adamw(w, g, m, v, *, tr):
    rows, cols = w.shape

    def body(w_ref, g_ref, m_ref, v_ref, d_ref, nm_ref, nv_ref, go_ref):
        gv = g_ref[...]
        go_ref[...] = gv
        nm = ADAM_B1 * m_ref[...] + (1.0 - ADAM_B1) * gv
        nv = ADAM_B2 * v_ref[...] + (1.0 - ADAM_B2) * jnp.square(gv)
        m_hat = nm / (1.0 - ADAM_B1 ** ADAM_STEP)
        v_hat = nv / (1.0 - ADAM_B2 ** ADAM_STEP)
        d_ref[...] = -ADAM_LR * (m_hat / (jnp.sqrt(v_hat) + ADAM_EPS) + ADAM_WD * w_ref[...])
        nm_ref[...] = nm
        nv_ref[...] = nv

    spec = pl.BlockSpec((tr, cols), lambda i: (i, 0))
    shp = jax.ShapeDtypeStruct((rows, cols), F32)
    return pl.pallas_call(
        body, name="adamw", out_shape=(shp, shp, shp, shp), grid=(rows // tr,),
        in_specs=[spec] * 4, out_specs=(spec, spec, spec, spec),
        compiler_params=_params(("parallel",)),
    )(w, g, m, v)


def _headnorm(xf, g):
    r = lax.rsqrt(jnp.mean(xf * xf, axis=-1, keepdims=True) + EPS)
    xhat = xf * r
    return xhat * g, xhat, r


def _headnorm_bwd(dn, xhat, r, g):
    dxhat = dn * g
    return r * (dxhat - xhat * jnp.mean(dxhat * xhat, axis=-1, keepdims=True))


_NT = (((1,), (1,)), ((), ()))
_TN = (((0,), (0,)), ((), ()))


def _attn_masks(n, nb):
    qi = lax.broadcasted_iota(jnp.int32, (ATTN_BLOCK, ATTN_BLOCK), 0)
    ci = lax.broadcasted_iota(jnp.int32, (ATTN_BLOCK, ATTN_BLOCK), 1)
    d_prev = qi + ATTN_BLOCK - ci
    d_cur = qi - ci
    return d_prev, d_cur, (ci >= qi), (ci <= qi)


def _head_lane(tile, h):
    lane = lax.broadcasted_iota(jnp.int32, tile.shape, 1)
    return jnp.sum(jnp.where(lane == h, tile, 0.0), axis=-1, keepdims=True)


def _set_head_lane(tile, h, col):
    lane = lax.broadcasted_iota(jnp.int32, tile.shape, 1)
    return jnp.where(lane == h, col, tile)


def _attn_fwd(qv, kv, vv, offs, qg, kg, *, dil):
    l = qv.shape[0]
    nb = l // ATTN_BLOCK
    scale = HEAD_DIM ** -0.5

    def body(q_ref, kp_ref, kc_ref, vp_ref, vc_ref, qg_ref, kg_ref, o_ref, lse_ref):
        n = pl.program_id(1)
        d_prev, d_cur, ok_prev, ok_cur = _attn_masks(n, nb)
        ok_prev = ok_prev & (n > 0)
        b_prev = d_prev.astype(F32) * float(dil)
        b_cur = d_cur.astype(F32) * float(dil)
        lse = jnp.zeros((ATTN_BLOCK, HEAD_DIM), F32)
        for h in range(HEADS):
            sl = slice(h * HEAD_DIM, (h + 1) * HEAD_DIM)
            slope = 2.0 ** (-8.0 * (h + 1) / HEADS)
            q = _headnorm(q_ref[:, sl].astype(F32), qg_ref[...])[0].astype(BF16)
            kp = _headnorm(kp_ref[:, sl].astype(F32), kg_ref[...])[0].astype(BF16)
            kc = _headnorm(kc_ref[:, sl].astype(F32), kg_ref[...])[0].astype(BF16)
            s1 = lax.dot_general(q, kp, _NT, preferred_element_type=F32) * scale
            s2 = lax.dot_general(q, kc, _NT, preferred_element_type=F32) * scale
            s1 = jnp.where(ok_prev, s1 - slope * b_prev, NEG_BIG)
            s2 = jnp.where(ok_cur, s2 - slope * b_cur, NEG_BIG)
            m = jnp.maximum(jnp.max(s1, axis=-1, keepdims=True), jnp.max(s2, axis=-1, keepdims=True))
            p1 = jnp.exp(s1 - m)
            p2 = jnp.exp(s2 - m)
            den = jnp.sum(p1, axis=-1, keepdims=True) + jnp.sum(p2, axis=-1, keepdims=True)
            inv = 1.0 / den
            o = jnp.dot((p1 * inv).astype(BF16), vp_ref[:, sl], preferred_element_type=F32)
            o = o + jnp.dot((p2 * inv).astype(BF16), vc_ref[:, sl], preferred_element_type=F32)
            o_ref[:, sl] = o
            lse = _set_head_lane(lse, h, m + jnp.log(den))
        lse_ref[...] = lse

    cur, prev, _ = _attn_specs(nb)
    vec = pl.BlockSpec((1, HEAD_DIM), lambda r, n: (0, 0))
    return pl.pallas_call(
        body, name="attn_fwd_d%d" % dil,
        out_shape=(jax.ShapeDtypeStruct((l, dil * A_WIDTH), F32), jax.ShapeDtypeStruct((l, dil * HEAD_DIM), F32)),
        grid=(dil, nb),
        in_specs=[cur(offs[0]), prev(offs[1]), cur(offs[1]), prev(offs[2]), cur(offs[2]), vec, vec],
        out_specs=(cur(0), cur(0, HEAD_DIM)),
        compiler_params=_params(("parallel", "parallel")),
    )(qv, kv, kv, vv, vv, qg.reshape(1, HEAD_DIM), kg.reshape(1, HEAD_DIM))


def _attn_specs(nb):
    def cur(off, width=A_WIDTH):
        return pl.BlockSpec((ATTN_BLOCK, width), lambda r, n: (n, off + r))

    def prev(off, width=A_WIDTH):
        return pl.BlockSpec((ATTN_BLOCK, width), lambda r, n: (jnp.maximum(n - 1, 0), off + r))

    def nxt(off, width=A_WIDTH):
        return pl.BlockSpec((ATTN_BLOCK, width), lambda r, n: (jnp.minimum(n + 1, nb - 1), off + r))

    return cur, prev, nxt


def _attn_combine(outs, lses, *, tr=256):
    t, w = outs[0].shape

    def body(o0, o1, o2, l0, l1, l2, y_ref, lse_ref):
        a0, a1, a2 = l0[...], l1[...], l2[...]
        m = jnp.maximum(jnp.maximum(a0, a1), a2)
        e0, e1, e2 = jnp.exp(a0 - m), jnp.exp(a1 - m), jnp.exp(a2 - m)
        s = e0 + e1 + e2
        inv = 1.0 / s
        w0, w1, w2 = e0 * inv, e1 * inv, e2 * inv
        lse_ref[...] = m + jnp.log(s)
        for h in range(HEADS):
            sl = slice(h * HEAD_DIM, (h + 1) * HEAD_DIM)
            y = (_head_lane(w0, h) * o0[:, sl] + _head_lane(w1, h) * o1[:, sl] + _head_lane(w2, h) * o2[:, sl])
            y_ref[:, sl] = y.astype(BF16)

    row = pl.BlockSpec((tr, w), lambda i: (i, 0))
    stat = pl.BlockSpec((tr, HEAD_DIM), lambda i: (i, 0))
    return pl.pallas_call(
        body, name="attn_combine",
        out_shape=(jax.ShapeDtypeStruct((t, w), BF16), jax.ShapeDtypeStruct((t, HEAD_DIM), F32)),
        grid=(t // tr,), in_specs=[row] * 3 + [stat] * 3, out_specs=(row, stat),
        compiler_params=_params(("parallel",)),
    )(*outs, *lses)


def _attn_delta(dy, y, *, tr=256):
    t, w = y.shape

    def body(dy_ref, y_ref, o_ref, dyb_ref):
        out = jnp.zeros((tr, HEAD_DIM), F32)
        for h in range(HEADS):
            sl = slice(h * HEAD_DIM, (h + 1) * HEAD_DIM)
            dlt = jnp.sum(dy_ref[:, sl] * y_ref[:, sl].astype(F32), axis=-1, keepdims=True)
            out = _set_head_lane(out, h, dlt)
        o_ref[...] = out
        dyb_ref[...] = dy_ref[...].astype(BF16)

    row = pl.BlockSpec((tr, w), lambda i: (i, 0))
    return pl.pallas_call(
        body, name="attn_delta",
        out_shape=(jax.ShapeDtypeStruct((t, HEAD_DIM), F32), jax.ShapeDtypeStruct((t, w), BF16)), grid=(t // tr,),
        in_specs=[row, row], out_specs=(pl.BlockSpec((tr, HEAD_DIM), lambda i: (i, 0)), row),
        compiler_params=_params(("parallel",)),
    )(dy, y)


def _attn_bwd(qv, kv, vv, dyv, offs, lsev, dltv, qg, kg, *, dil):
    l = qv.shape[0]
    w = dil * A_WIDTH
    nb = l // ATTN_BLOCK
    scale = HEAD_DIM ** -0.5

    def body(qc_ref, qn_ref, kp_ref, kc_ref, vp_ref, vc_ref, dyc_ref, dyn_ref, lc_ref, ln_ref,
             dc_ref, dn_ref, qg_ref, kg_ref, dq_ref, dk_ref, dv_ref, dqg_ref, dkg_ref):
        n = pl.program_id(1)
        first = (pl.program_id(0) == 0) & (n == 0)
        d_prev, d_cur, ok_prev, ok_cur = _attn_masks(n, nb)
        ok_t1 = ok_prev & (n > 0)
        ok_t3 = ok_prev & (n < nb - 1)
        b_prev = d_prev.astype(F32) * float(dil)
        b_cur = d_cur.astype(F32) * float(dil)
        qgv, kgv = qg_ref[...], kg_ref[...]
        dqg = jnp.zeros((1, HEAD_DIM), F32)
        dkg = jnp.zeros((1, HEAD_DIM), F32)
        for h in range(HEADS):
            sl = slice(h * HEAD_DIM, (h + 1) * HEAD_DIM)
            slope = 2.0 ** (-8.0 * (h + 1) / HEADS)
            qc, qc_hat, qc_r = _headnorm(qc_ref[:, sl].astype(F32), qgv)
            qn = _headnorm(qn_ref[:, sl].astype(F32), qgv)[0].astype(BF16)
            kp = _headnorm(kp_ref[:, sl].astype(F32), kgv)[0].astype(BF16)
            kc, kc_hat, kc_r = _headnorm(kc_ref[:, sl].astype(F32), kgv)
            qc = qc.astype(BF16)
            kc = kc.astype(BF16)
            vp, vc = vp_ref[:, sl], vc_ref[:, sl]
            dyc, dyn = dyc_ref[:, sl].astype(BF16), dyn_ref[:, sl].astype(BF16)

            def tile(q, k, v, dy, lse, dlt, ok, bias):
                s = lax.dot_general(q, k, _NT, preferred_element_type=F32) * scale
                p = jnp.where(ok, jnp.exp(jnp.where(ok, s - slope * bias, NEG_BIG) - lse), 0.0)
                dp = lax.dot_general(dy, v, _NT, preferred_element_type=F32)
                return p.astype(BF16), (p * (dp - dlt)).astype(BF16)

            lse_c, dlt_c = _head_lane(lc_ref[...], h), _head_lane(dc_ref[...], h)
            p1, ds1 = tile(qc, kp, vp, dyc, lse_c, dlt_c, ok_t1, b_prev)
            p2, ds2 = tile(qc, kc, vc, dyc, lse_c, dlt_c, ok_cur, b_cur)
            p3, ds3 = tile(qn, kc, vc, dyn, _head_lane(ln_ref[...], h), _head_lane(dn_ref[...], h), ok_t3, b_prev)
            dqn = scale * (jnp.dot(ds1, kp, preferred_element_type=F32) + jnp.dot(ds2, kc, preferred_element_type=F32))
            dkn = scale * (lax.dot_general(ds2, qc, _TN, preferred_element_type=F32)
                           + lax.dot_general(ds3, qn, _TN, preferred_element_type=F32))
            dv = (lax.dot_general(p2, dyc, _TN, preferred_element_type=F32)
                  + lax.dot_general(p3, dyn, _TN, preferred_element_type=F32))
            dqg = dqg + jnp.sum(dqn * qc_hat, axis=0, keepdims=True)
            dkg = dkg + jnp.sum(dkn * kc_hat, axis=0, keepdims=True)
            dq_ref[:, sl] = _headnorm_bwd(dqn, qc_hat, qc_r, qgv).astype(BF16)
            dk_ref[:, sl] = _headnorm_bwd(dkn, kc_hat, kc_r, kgv).astype(BF16)
            dv_ref[:, sl] = dv.astype(BF16)

        @pl.when(first)
        def _():
            dqg_ref[...] = dqg
            dkg_ref[...] = dkg

        @pl.when(jnp.logical_not(first))
        def _():
            dqg_ref[...] += dqg
            dkg_ref[...] += dkg

    cur, prev, nxt = _attn_specs(nb)
    o_q, o_k, o_v, o_dy = offs
    stat_c, stat_n = cur(0, HEAD_DIM), nxt(0, HEAD_DIM)
    vec = pl.BlockSpec((1, HEAD_DIM), lambda r, n: (0, 0))
    shp = jax.ShapeDtypeStruct((l, w), BF16)
    gshp = jax.ShapeDtypeStruct((1, HEAD_DIM), F32)
    return pl.pallas_call(
        body, name="attn_bwd_d%d" % dil, out_shape=(shp, shp, shp, gshp, gshp), grid=(dil, nb),
        in_specs=[cur(o_q), nxt(o_q), prev(o_k), cur(o_k), prev(o_v), cur(o_v), cur(o_dy), nxt(o_dy),
                  stat_c, stat_n, stat_c, stat_n, vec, vec],
        out_specs=(cur(0), cur(0), cur(0), vec, vec),
        compiler_params=_params(("arbitrary", "arbitrary")),
    )(qv, qv, kv, kv, vv, vv, dyv, dyv, lsev, lsev, dltv, dltv,
      qg.reshape(1, HEAD_DIM), kg.reshape(1, HEAD_DIM))


def _prev_halo(tr, tc, col0):
    return pl.BlockSpec((HALO, tc), lambda j, i: (jnp.maximum(i * (tr // HALO) - 1, 0), col0 + j))


def _next_halo(tr, tc, col0, rows):
    last = rows // HALO - 1
    return pl.BlockSpec((HALO, tc), lambda j, i: (jnp.minimum((i + 1) * (tr // HALO), last), col0 + j))


def _cur_block(tr, tc, col0):
    return pl.BlockSpec((tr, tc), lambda j, i: (i, col0 + j))


def _gateconv_fwd(h, conv_w, *, col0, tr=512, tc=256):
    t = h.shape[0]
    width = conv_w.shape[1]
    nc = width // tc
    c0 = col0 // tc

    def body(bg_ref, cg_ref, xt_ref, cgh_ref, xth_ref, w_ref, y_ref, pad_ref):
        i = pl.program_id(1)
        halo = cgh_ref[...].astype(F32) * xth_ref[...].astype(F32)
        pad_ref[0:HALO, :] = jnp.where(i > 0, halo, 0.0)
        pad_ref[HALO:HALO + tr, :] = cg_ref[...].astype(F32) * xt_ref[...].astype(F32)
        conv = None
        for j in range(B_CONV):
            term = w_ref[j:j + 1, :] * pad_ref[HALO - (B_CONV - 1) + j:HALO - (B_CONV - 1) + j + tr, :]
            conv = term if conv is None else conv + term
        y_ref[...] = (bg_ref[...].astype(F32) * conv).astype(BF16)

    return pl.pallas_call(
        body, name="gateconv_fwd", out_shape=jax.ShapeDtypeStruct((t, width), BF16), grid=(nc, t // tr),
        in_specs=[_cur_block(tr, tc, c0), _cur_block(tr, tc, c0 + nc), _cur_block(tr, tc, c0 + 2 * nc),
                  _prev_halo(tr, tc, c0 + nc), _prev_halo(tr, tc, c0 + 2 * nc),
                  pl.BlockSpec((8, tc), lambda j, i: (0, j))],
        out_specs=_cur_block(tr, tc, 0),
        scratch_shapes=[pltpu.VMEM((HALO + tr, tc), F32)],
        compiler_params=_params(("parallel", "arbitrary")),
    )(h, h, h, h, h, _pad_rows(conv_w, 8))


def _pad_rows(w, rows):
    return jnp.pad(w, ((0, rows - w.shape[0]), (0, 0)))


def _gateconv_bwd(h, dy, conv_w, *, col0, dcol0, tr=512, tc=256):
    t = h.shape[0]
    width = conv_w.shape[1]
    nc = width // tc
    c0 = col0 // tc
    dc0 = dcol0 // tc
    nt = t // tr

    def body(bg_ref, cg_ref, xt_ref, cgh_ref, xth_ref, bgn_ref, dy_ref, dyn_ref, w_ref,
             dbg_ref, dcg_ref, dxt_ref, dw_ref, pad_ref, padd_ref):
        i = pl.program_id(1)
        cg = cg_ref[...].astype(F32)
        xt = xt_ref[...].astype(F32)
        bg = bg_ref[...].astype(F32)
        dyv = dy_ref[...]
        halo = cgh_ref[...].astype(F32) * xth_ref[...].astype(F32)
        pad_ref[0:HALO, :] = jnp.where(i > 0, halo, 0.0)
        pad_ref[HALO:HALO + tr, :] = cg * xt
        dconv = dyv * bg
        padd_ref[0:tr, :] = dconv
        padd_ref[tr:tr + HALO, :] = jnp.where(i < nt - 1, dyn_ref[...] * bgn_ref[...].astype(F32), 0.0)
        conv = None
        du = None
        dws = []
        for j in range(B_CONV):
            off = HALO - (B_CONV - 1) + j
            shifted = pad_ref[off:off + tr, :]
            term = w_ref[j:j + 1, :] * shifted
            conv = term if conv is None else conv + term
            dws.append(jnp.sum(dconv * shifted, axis=0, keepdims=True))
            back = w_ref[j:j + 1, :] * padd_ref[B_CONV - 1 - j:B_CONV - 1 - j + tr, :]
            du = back if du is None else du + back
        dbg_ref[...] = (dyv * conv).astype(BF16)
        dcg_ref[...] = (du * xt).astype(BF16)
        dxt_ref[...] = (du * cg).astype(BF16)
        dw = _stack_rows(dws, 8, tc)

        @pl.when(i == 0)
        def _():
            dw_ref[...] = dw

        @pl.when(i > 0)
        def _():
            dw_ref[...] += dw

    oshp = jax.ShapeDtypeStruct((t, width), BF16)
    return pl.pallas_call(
        body, name="gateconv_bwd",
        out_shape=(oshp, oshp, oshp, jax.ShapeDtypeStruct((8, width), F32)), grid=(nc, nt),
        in_specs=[_cur_block(tr, tc, c0), _cur_block(tr, tc, c0 + nc), _cur_block(tr, tc, c0 + 2 * nc),
                  _prev_halo(tr, tc, c0 + nc), _prev_halo(tr, tc, c0 + 2 * nc),
                  _next_halo(tr, tc, c0, t), _cur_block(tr, tc, dc0), _next_halo(tr, tc, dc0, t),
                  pl.BlockSpec((8, tc), lambda j, i: (0, j))],
        out_specs=(_cur_block(tr, tc, 0), _cur_block(tr, tc, 0), _cur_block(tr, tc, 0),
                   pl.BlockSpec((8, tc), lambda j, i: (0, j))),
        scratch_shapes=[pltpu.VMEM((HALO + tr, tc), F32), pltpu.VMEM((tr + HALO, tc), F32)],
        compiler_params=_params(("parallel", "arbitrary")),
    )(h, h, h, h, h, h, dy, dy, _pad_rows(conv_w, 8))


def _stack_rows(rows, n, width):
    idx = lax.broadcasted_iota(jnp.int32, (n, width), 0)
    out = jnp.zeros((n, width), F32)
    for j, r in enumerate(rows):
        out = jnp.where(idx == j, r, out)
    return out


CONV_ROWS = 64


def _glu_conv_fwd(hod, conv_w, conv_b, *, tr=512, tc=256):
    t = hod.shape[0]
    width = conv_w.shape[1]
    nc = width // tc

    def body(val_ref, gate_ref, valh_ref, gateh_ref, w_ref, b_ref, u1_ref, pad_ref):
        i = pl.program_id(1)
        halo = valh_ref[...].astype(F32) * _sigmoid(gateh_ref[...].astype(F32))
        pad_ref[0:HALO, :] = jnp.where(i > 0, halo, 0.0)
        pad_ref[HALO:HALO + tr, :] = val_ref[...].astype(F32) * _sigmoid(gate_ref[...].astype(F32))
        for c in range(tr // CONV_ROWS):
            base = HALO + c * CONV_ROWS - (C_CONV - 1)
            acc = None
            for j in range(C_CONV):
                term = w_ref[j:j + 1, :] * pad_ref[base + j:base + j + CONV_ROWS, :]
                acc = term if acc is None else acc + term
            u1_ref[c * CONV_ROWS:(c + 1) * CONV_ROWS, :] = acc + b_ref[...]

    return pl.pallas_call(
        body, name="glu_conv_fwd", out_shape=jax.ShapeDtypeStruct((t, width), F32), grid=(nc, t // tr),
        in_specs=[_cur_block(tr, tc, 0), _cur_block(tr, tc, nc), _prev_halo(tr, tc, 0), _prev_halo(tr, tc, nc),
                  pl.BlockSpec((32, tc), lambda j, i: (0, j)), pl.BlockSpec((1, tc), lambda j, i: (0, j))],
        out_specs=_cur_block(tr, tc, 0),
        scratch_shapes=[pltpu.VMEM((HALO + tr, tc), F32)],
        compiler_params=_params(("parallel", "arbitrary")),
    )(hod, hod, hod, hod, _pad_rows(conv_w, 32), conv_b.reshape(1, width))


def _ln_silu_fwd(u1, g, b, *, tr=256):
    t, width = u1.shape

    def body(u_ref, g_ref, b_ref, o_ref):
        uv = u_ref[...]
        mu = jnp.mean(uv, axis=-1, keepdims=True)
        var = jnp.mean(jnp.square(uv - mu), axis=-1, keepdims=True)
        u2 = ((uv - mu) * lax.rsqrt(var + EPS)) * g_ref[...] + b_ref[...]
        o_ref[...] = (u2 * _sigmoid(u2)).astype(BF16)

    row = pl.BlockSpec((tr, width), lambda i: (i, 0))
    vec = pl.BlockSpec((1, width), lambda i: (0, 0))
    return pl.pallas_call(
        body, name="ln_silu_fwd", out_shape=jax.ShapeDtypeStruct((t, width), BF16), grid=(t // tr,),
        in_specs=[row, vec, vec], out_specs=row, compiler_params=_params(("parallel",)),
    )(u1, g.reshape(1, width), b.reshape(1, width))


def _ln_silu_bwd(du, u1, g, b, *, col0, tr=256):
    t, width = u1.shape

    def body(du_ref, u_ref, g_ref, b_ref, du1_ref, dg_ref, db_ref, dcb_ref):
        uv = u_ref[...]
        mu = jnp.mean(uv, axis=-1, keepdims=True)
        var = jnp.mean(jnp.square(uv - mu), axis=-1, keepdims=True)
        rstd = lax.rsqrt(var + EPS)
        xh = (uv - mu) * rstd
        u2 = xh * g_ref[...] + b_ref[...]
        sig = _sigmoid(u2)
        du2 = du_ref[...] * (sig * (1.0 + u2 * (1.0 - sig)))
        dxh = du2 * g_ref[...]
        du1 = rstd * (dxh - jnp.mean(dxh, axis=-1, keepdims=True)
                      - xh * jnp.mean(dxh * xh, axis=-1, keepdims=True))
        du1_ref[...] = du1
        parts = (jnp.sum(du2 * xh, axis=0, keepdims=True), jnp.sum(du2, axis=0, keepdims=True),
                 jnp.sum(du1, axis=0, keepdims=True))

        @pl.when(pl.program_id(0) == 0)
        def _():
            dg_ref[...], db_ref[...], dcb_ref[...] = parts

        @pl.when(pl.program_id(0) > 0)
        def _():
            dg_ref[...] += parts[0]
            db_ref[...] += parts[1]
            dcb_ref[...] += parts[2]

    row = pl.BlockSpec((tr, width), lambda i: (i, 0))
    vec = pl.BlockSpec((1, width), lambda i: (0, 0))
    vshp = jax.ShapeDtypeStruct((1, width), F32)
    return pl.pallas_call(
        body, name="ln_silu_bwd", out_shape=(jax.ShapeDtypeStruct((t, width), F32), vshp, vshp, vshp),
        grid=(t // tr,),
        in_specs=[pl.BlockSpec((tr, width), lambda i: (i, col0 // width)), row, vec, vec],
        out_specs=(row, vec, vec, vec), compiler_params=_params(("arbitrary",)),
    )(du, u1, g.reshape(1, width), b.reshape(1, width))


def _glu_conv_bwd(hod, du1, conv_w, *, tr=512, tc=256):
    t = hod.shape[0]
    width = conv_w.shape[1]
    nc = width // tc
    nt = t // tr

    def body(val_ref, gate_ref, valh_ref, gateh_ref, du_ref, dun_ref, w_ref,
             dval_ref, dgate_ref, dw_ref, pad_ref, padd_ref, du0_ref):
        i = pl.program_id(1)
        val = val_ref[...].astype(F32)
        sig = _sigmoid(gate_ref[...].astype(F32))
        halo = valh_ref[...].astype(F32) * _sigmoid(gateh_ref[...].astype(F32))
        pad_ref[0:HALO, :] = jnp.where(i > 0, halo, 0.0)
        pad_ref[HALO:HALO + tr, :] = val * sig
        padd_ref[0:tr, :] = du_ref[...]
        padd_ref[tr:tr + HALO, :] = jnp.where(i < nt - 1, dun_ref[...], 0.0)
        dws = [jnp.zeros((1, tc), F32)] * C_CONV
        for c in range(tr // CONV_ROWS):
            r0 = c * CONV_ROWS
            duc = padd_ref[r0:r0 + CONV_ROWS, :]
            acc = None
            for j in range(C_CONV):
                back = w_ref[j:j + 1, :] * padd_ref[r0 + C_CONV - 1 - j:r0 + C_CONV - 1 - j + CONV_ROWS, :]
                acc = back if acc is None else acc + back
                off = HALO + r0 - (C_CONV - 1) + j
                dws[j] = dws[j] + jnp.sum(duc * pad_ref[off:off + CONV_ROWS, :], axis=0, keepdims=True)
            du0_ref[r0:r0 + CONV_ROWS, :] = acc
        du0 = du0_ref[...]
        dval_ref[...] = (du0 * sig).astype(BF16)
        dgate_ref[...] = (du0 * val * sig * (1.0 - sig)).astype(BF16)
        dw = _stack_rows(dws, 32, tc)

        @pl.when(i == 0)
        def _():
            dw_ref[...] = dw

        @pl.when(i > 0)
        def _():
            dw_ref[...] += dw

    oshp = jax.ShapeDtypeStruct((t, width), BF16)
    wspec = pl.BlockSpec((32, tc), lambda j, i: (0, j))
    return pl.pallas_call(
        body, name="glu_conv_bwd", out_shape=(oshp, oshp, jax.ShapeDtypeStruct((32, width), F32)), grid=(nc, nt),
        in_specs=[_cur_block(tr, tc, 0), _cur_block(tr, tc, nc), _prev_halo(tr, tc, 0), _prev_halo(tr, tc, nc),
                  _cur_block(tr, tc, 0), _next_halo(tr, tc, 0, t), wspec],
        out_specs=(_cur_block(tr, tc, 0), _cur_block(tr, tc, 0), wspec),
        scratch_shapes=[pltpu.VMEM((HALO + tr, tc), F32), pltpu.VMEM((tr + HALO, tc), F32),
                        pltpu.VMEM((tr, tc), F32)],
        compiler_params=_params(("parallel", "arbitrary")),
    )(hod, hod, hod, hod, du1, du1, _pad_rows(conv_w, 32))


def _pooled(pad_ref, g, kw, tr, i):
    gw = pad_ref.shape[1] // len(D_WINDOWS)
    cols = slice(g * gw, (g + 1) * gw)
    tot = None
    for j in range(kw):
        sh = pad_ref[HALO - j:HALO - j + tr, cols]
        tot = sh if tot is None else tot + sh
    return tot / _window_count(tr, gw, kw, i * tr) - pad_ref[HALO:HALO + tr, cols]


def _window_count(rows, width, kw, row0):
    t1 = (lax.broadcasted_iota(jnp.int32, (rows, width), 0) + (row0 + 1)).astype(F32)
    return jnp.minimum(t1, float(kw))


def _pool_fwd(hod, pool_w, pool_scale, *, tr=256):
    t = hod.shape[0]
    width = pool_scale.shape[0]
    ng = len(D_WINDOWS)
    gw = width // ng

    def body(z_ref, zh_ref, w_ref, s_ref, y_ref, pad_ref):
        i = pl.program_id(1)
        pad_ref[0:HALO, :] = jnp.where(i > 0, zh_ref[...].astype(F32), 0.0)
        pad_ref[HALO:HALO + tr, :] = z_ref[...].astype(F32)
        for g, kw in enumerate(D_WINDOWS):
            cols = slice(g * gw, (g + 1) * gw)
            pre = jnp.dot(_pooled(pad_ref, g, kw, tr, i).astype(BF16), w_ref[g], preferred_element_type=F32)
            y_ref[:, cols] = (pre * s_ref[:, cols]).astype(BF16)

    return pl.pallas_call(
        body, name="pool_fwd", out_shape=jax.ShapeDtypeStruct((t, width), BF16), grid=(1, t // tr),
        in_specs=[_cur_block(tr, width, 2), _prev_halo(tr, width, 2),
                  pl.BlockSpec((ng, gw, gw), lambda j, i: (0, 0, 0)), pl.BlockSpec((1, width), lambda j, i: (0, 0))],
        out_specs=_cur_block(tr, width, 0),
        scratch_shapes=[pltpu.VMEM((HALO + tr, width), F32)],
        compiler_params=_params(("parallel", "arbitrary")),
    )(hod, hod, pool_w, pool_scale.reshape(1, width))


def _pool_bwd(hod, dy, pool_w, pool_scale, *, dcol0, tr=256):
    t = hod.shape[0]
    width = pool_scale.shape[0]
    ng = len(D_WINDOWS)
    gw = width // ng
    nt = t // tr

    def body(z_ref, zh_ref, dy_ref, dyn_ref, w_ref, s_ref, dz_ref, dw_ref, ds_ref, pad_ref, pade_ref):
        i = pl.program_id(1)
        pad_ref[0:HALO, :] = jnp.where(i > 0, zh_ref[...].astype(F32), 0.0)
        pad_ref[HALO:HALO + tr, :] = z_ref[...].astype(F32)
        dws = []
        dss = []
        for g, kw in enumerate(D_WINDOWS):
            cols = slice(g * gw, (g + 1) * gw)
            wg = w_ref[g]
            dyc = dy_ref[:, cols]
            dpre = (dyc * s_ref[:, cols]).astype(BF16)
            dpre_n = (dyn_ref[:, cols] * s_ref[:, cols]).astype(BF16)
            dpl = lax.dot_general(dpre, wg, _NT, preferred_element_type=F32)
            dpl_n = lax.dot_general(dpre_n, wg, _NT, preferred_element_type=F32)
            pade_ref[0:tr, cols] = dpl / _window_count(tr, gw, kw, i * tr)
            pade_ref[tr:tr + HALO, cols] = jnp.where(i < nt - 1, dpl_n / _window_count(HALO, gw, kw, (i + 1) * tr), 0.0)
            tot = None
            for j in range(kw):
                sh = pade_ref[j:j + tr, cols]
                tot = sh if tot is None else tot + sh
            dz_ref[:, cols] = (tot - dpl).astype(BF16)
            pooled = _pooled(pad_ref, g, kw, tr, i).astype(BF16)
            pre = jnp.dot(pooled, wg, preferred_element_type=F32)
            dss.append(jnp.sum(dyc * pre, axis=0, keepdims=True))
            dws.append(lax.dot_general(pooled, dpre, _TN, preferred_element_type=F32))

        @pl.when(i == 0)
        def _():
            for g in range(ng):
                dw_ref[g] = dws[g]
                ds_ref[:, g * gw:(g + 1) * gw] = dss[g]

        @pl.when(i > 0)
        def _():
            for g in range(ng):
                dw_ref[g] += dws[g]
                ds_ref[:, g * gw:(g + 1) * gw] += dss[g]

    dc = dcol0 // width
    wspec = pl.BlockSpec((ng, gw, gw), lambda j, i: (0, 0, 0))
    vspec = pl.BlockSpec((1, width), lambda j, i: (0, 0))
    return pl.pallas_call(
        body, name="pool_bwd",
        out_shape=(jax.ShapeDtypeStruct((t, width), BF16), jax.ShapeDtypeStruct((ng, gw, gw), F32),
                   jax.ShapeDtypeStruct((1, width), F32)),
        grid=(1, nt),
        in_specs=[_cur_block(tr, width, 2), _prev_halo(tr, width, 2), _cur_block(tr, width, dc),
                  _next_halo(tr, width, dc, t), wspec, vspec],
        out_specs=(_cur_block(tr, width, 0), wspec, vspec),
        scratch_shapes=[pltpu.VMEM((HALO + tr, width), F32), pltpu.VMEM((tr + HALO, width), F32)],
        compiler_params=_params(("arbitrary", "arbitrary")),
    )(hod, hod, dy, dy, pool_w, pool_scale.reshape(1, width))


TM = 1024
TN = 512
TK_ACC = 512


def _dw_full(a, b, *, m, n, alpha, axis, name):
    t = a.shape[0]
    tm = TM if m % TM == 0 else TN
    return _matmul([(a, b)], ta=True, m=m, n=n, k=t, tm=tm, tn=TN, tk=t, alpha=alpha, out_dtype=BF16, name=name)


def _ffn_fwd(x, g, w1, w3, w2):
    t, d = x.shape
    f = w1.shape[1]
    xn = _rmsnorm_fwd(x, g)
    a, b, h = _ffn_up(xn, w1, w3, tm=TM, tn=TN)
    if callable(w2):
        w2 = w2(h)
    y = _matmul([(h, w2)], m=t, n=d, k=f, tm=TM, tn=TN, tk=f, res=x, alpha=0.5, name="ffn_down")
    return y, (x, xn, a, b, h), [w1, w3, w2]


def _ffn_bwd(dx, dxb, saved, g, w1, w3, w2, push, tok, last=False):
    begin, finish = push
    x, xn, a, b, h = saved
    t, d = x.shape
    f = w1.shape[1]
    da, db = _ffn_dh(dxb, w2, a, b, tok, tm=TM, tn=TN)
    tok = begin([dict(a=xn, b=da, m=d, n=f, alpha=1.0, axis=1, name="ffn_dw1"),
                 dict(a=xn, b=db, m=d, n=f, alpha=1.0, axis=1, name="ffn_dw3"),
                 dict(a=h, b=dxb, m=f, n=d, alpha=0.5, axis=0, name="ffn_dw2")])
    if last:
        tok = finish(da)
    dxn = _matmul([(da, w1), (db, w3)], tb=True, m=t, n=d, k=f, tm=TM, tn=d, tk=TK_ACC, tok=tok, name="ffn_dxn")
    dx, dxb, dg = _rmsnorm_bwd(dxn, x, g, dx)
    return dx, dxb, dg, (tok if last else finish(dx))


def _mix_out_fwd(x, ycat, w_out):
    t, d = x.shape
    return _matmul([(ycat, w_out)], m=t, n=d, k=d, tm=TM, tn=TN, tk=d, res=x, name="mix_out")


def _mix_out_bwd(dxb, w_out, tok):
    t, d = dxb.shape
    return _matmul([(dxb, w_out)], tb=True, m=t, n=d, k=d, tm=TM, tn=TN, tk=d, tok=tok, name="mix_dy")


def _mix_in_bwd(dh, xn, w_in, x, g, dx, ycat, dxb, push):
    begin, finish = push
    t, d = x.shape
    n_in = w_in.shape[1]
    tok = begin([dict(a=xn, b=dh, m=d, n=n_in, alpha=1.0, axis=1, name="mix_dw_in"),
                 dict(a=ycat, b=dxb, m=d, n=d, alpha=1.0, axis=0, name="mix_dw_out")])
    dxn = _matmul([(dh, w_in)], tb=True, m=t, n=d, k=n_in, tm=TM, tn=d, tk=TK_ACC, tok=tok, name="mix_dxn")
    dx, dxb, dg = _rmsnorm_bwd(dxn, x, g, dx)
    return dx, dxb, dg, finish(dx)


def _group_view(a, col0, dil, width=A_WIDTH):
    if dil == 1:
        return a, col0 // width
    t = a.shape[0]
    return a[:, col0:col0 + width].reshape(t // dil, dil * width), 0


def _even_fwd(x, g, w_in, qg, kg, conv_w, w_out):
    t, d = x.shape
    n_in = w_in.shape[1]
    nq = len(A_DILATIONS) * A_WIDTH
    xn = _rmsnorm_fwd(x, g)
    h = _matmul([(xn, w_in)], m=t, n=n_in, k=d, tm=TM, tn=TN, tk=d, out_dtype=BF16, name="ev_in")
    outs, lses, views = [], [], []
    for gi, dil in enumerate(A_DILATIONS):
        qkv = [_group_view(h, part * nq + gi * A_WIDTH, dil) for part in range(3)]
        (qv, oq), (kv, ok), (vv, ov) = qkv
        o, l = _attn_fwd(qv, kv, vv, (oq, ok, ov), qg, kg, dil=dil)
        outs.append(o.reshape(t, A_WIDTH))
        lses.append(l.reshape(t, HEAD_DIM))
        views.append(qkv)
    ya, lse = _attn_combine(outs, lses)
    yb = _gateconv_fwd(h, conv_w, col0=3 * nq)
    ycat = jnp.concatenate([ya, yb], axis=1)
    return _mix_out_fwd(x, ycat, w_out), (x, xn, h, ya, lse, ycat, views)


def _even_bwd(dx, dxb, saved, g, w_in, qg, kg, conv_w, w_out, push, tok):
    x, xn, h, ya, lse, ycat, views = saved
    t, d = x.shape
    nq = len(A_DILATIONS) * A_WIDTH
    dycat = _mix_out_bwd(dxb, w_out, tok)
    dlt, dya = _attn_delta(dycat, ya)
    dqs, dks, dvs = [], [], []
    dqg = jnp.zeros((HEAD_DIM,), F32)
    dkg = jnp.zeros((HEAD_DIM,), F32)
    for gi, dil in enumerate(A_DILATIONS):
        (qv, oq), (kv, ok), (vv, ov) = views[gi]
        dyv, ody = _group_view(dya, 0, dil)
        dq, dk, dv, dqg_i, dkg_i = _attn_bwd(
            qv, kv, vv, dyv, (oq, ok, ov, ody), _group_view(lse, 0, dil, HEAD_DIM)[0],
            _group_view(dlt, 0, dil, HEAD_DIM)[0], qg, kg, dil=dil)
        dqs.append(dq.reshape(t, A_WIDTH))
        dks.append(dk.reshape(t, A_WIDTH))
        dvs.append(dv.reshape(t, A_WIDTH))
        dqg = dqg + dqg_i.reshape(HEAD_DIM)
        dkg = dkg + dkg_i.reshape(HEAD_DIM)
    dbg, dcg, dxt, dcw = _gateconv_bwd(h, dycat, conv_w, col0=3 * nq, dcol0=A_WIDTH)
    dh = jnp.concatenate(dqs + dks + dvs + [dbg, dcg, dxt], axis=1)
    dx, dxb, dg, tok = _mix_in_bwd(dh, xn, w_in, x, g, dx, ycat, dxb, push)
    return dx, dxb, dg, dqg, dkg, dcw[:B_CONV], tok


def _odd_fwd(x, g, w_in, conv_w, conv_b, ln_g, ln_b, pool_w, pool_scale, w_out):
    t, d = x.shape
    n_in = w_in.shape[1]
    xn = _rmsnorm_fwd(x, g)
    hod = _matmul([(xn, w_in)], m=t, n=n_in, k=d, tm=TM, tn=TN, tk=d, out_dtype=BF16, name="od_in")
    u1 = _glu_conv_fwd(hod, conv_w, conv_b)
    u = _ln_silu_fwd(u1, ln_g, ln_b)
    yd = _pool_fwd(hod, pool_w.astype(BF16), pool_scale)
    ycat = jnp.concatenate([u, yd], axis=1)
    return _mix_out_fwd(x, ycat, w_out), (x, xn, hod, u1, ycat)


def _odd_bwd(dx, dxb, saved, g, w_in, conv_w, conv_b, ln_g, ln_b, pool_w, pool_scale, w_out, push, tok):
    x, xn, hod, u1, ycat = saved
    width = conv_w.shape[1]
    dycat = _mix_out_bwd(dxb, w_out, tok)
    du1, dlg, dlb, dcb = _ln_silu_bwd(dycat, u1, ln_g, ln_b, col0=0)
    dval, dgate, dcw = _glu_conv_bwd(hod, du1, conv_w)
    dz, dpw, dps = _pool_bwd(hod, dycat, pool_w.astype(BF16), pool_scale, dcol0=width)
    dh = jnp.concatenate([dval, dgate, dz], axis=1)
    dx, dxb, dg, tok = _mix_in_bwd(dh, xn, w_in, x, g, dx, ycat, dxb, push)
    return (dx, dxb, dg, dcw[:C_CONV], dcb.reshape(width), dlg.reshape(width), dlb.reshape(width),
            dpw, dps.reshape(width), tok)


def _sublayer_matrices(s):
    layer, slot = divmod(s, 3)
    if slot == 1:
        kind = "ev" if layer % 2 == 0 else "od"
        return [(kind + "_w_in", layer // 2), (kind + "_w_out", layer // 2)]
    j = 2 * layer + slot // 2
    return [("ffn_w1", j), ("ffn_w3", j), ("ffn_w2", j)]


def _local_step(x, target, wts, fetch=None, grads=None):
    depth = wts["norm_g"].shape[0]
    if fetch is None:
        fetch = lambda s, after: ([wts[name][idx] for name, idx in _sublayer_matrices(s)], 0.0)
    gr = {}
    if grads is None:
        def begin(s, specs):
            for (name, idx), spec in zip(_sublayer_matrices(s), specs):
                gr.setdefault(name, {})[idx] = _dw_full(**spec)
            return 0.0

        grads = (begin, lambda s, after: 0.0)

    def gain(layer, slot, tok):
        return wts["norm_g"][layer, slot] + tok

    saved = []
    for layer in range(depth):
        i = layer // 2
        s = 3 * layer
        m0, tok = fetch(s, x)
        x, s0, m0 = _ffn_fwd(x, gain(layer, 0, tok), *m0)
        m1, tok = fetch(s + 1, x)
        if layer % 2 == 0:
            x, s1 = _even_fwd(x, gain(layer, 1, tok), m1[0], wts["ev_q_gain"][i],
                              wts["ev_k_gain"][i], wts["ev_conv_w"][i], m1[1])
        else:
            x, s1 = _odd_fwd(x, gain(layer, 1, tok), m1[0], wts["od_conv_w"][i],
                             wts["od_conv_b"][i], wts["od_ln_g"][i], wts["od_ln_b"][i], wts["od_pool_w"][i],
                             wts["od_pool_scale"][i], m1[1])
        m2, tok = fetch(s + 2, x)
        x, s2, m2 = _ffn_fwd(x, gain(layer, 2, tok), *m2)
        saved.append(((s0, m0), (s1, m1), (s2, m2)))
    dx, dxb, sq = _loss_head(x, target)

    n_even, n_odd = (depth + 1) // 2, depth // 2
    for k in ("ev_q_gain", "ev_k_gain", "ev_conv_w"):
        gr[k] = [None] * n_even
    for k in ("od_conv_w", "od_conv_b", "od_ln_g", "od_ln_b", "od_pool_w", "od_pool_scale"):
        gr[k] = [None] * n_odd
    dnorm = [[None] * 3 for _ in range(depth)]
    norm_g = wts["norm_g"]

    def push_for(s):
        return (lambda specs: grads[0](s, specs)), (lambda after: grads[1](s, after))

    tok = 0.0
    for layer in reversed(range(depth)):
        i = layer // 2
        s = 3 * layer
        (s0, m0), (s1, m1), (s2, m2) = saved[layer]
        dx, dxb, dnorm[layer][2], tok = _ffn_bwd(dx, dxb, s2, norm_g[layer, 2], *m2, push_for(s + 2), tok)
        if layer % 2 == 0:
            (dx, dxb, dnorm[layer][1], gr["ev_q_gain"][i], gr["ev_k_gain"][i], gr["ev_conv_w"][i], tok) = _even_bwd(
                dx, dxb, s1, norm_g[layer, 1], m1[0], wts["ev_q_gain"][i],
                wts["ev_k_gain"][i], wts["ev_conv_w"][i], m1[1], push_for(s + 1), tok)
        else:
            (dx, dxb, dnorm[layer][1], gr["od_conv_w"][i], gr["od_conv_b"][i], gr["od_ln_g"][i],
             gr["od_ln_b"][i], gr["od_pool_w"][i], gr["od_pool_scale"][i], tok) = _odd_bwd(
                dx, dxb, s1, norm_g[layer, 1], m1[0], wts["od_conv_w"][i],
                wts["od_conv_b"][i], wts["od_ln_g"][i], wts["od_ln_b"][i], wts["od_pool_w"][i],
                wts["od_pool_scale"][i], m1[1], push_for(s + 1), tok)
        dx, dxb, dnorm[layer][0], tok = _ffn_bwd(dx, dxb, s0, norm_g[layer, 0], *m0, push_for(s), tok,
                                                 last=(layer == 0))
    gr["norm_g"] = jnp.stack([jnp.stack(r) for r in dnorm])
    for name in list(gr):
        if isinstance(gr[name], dict):
            gr[name] = [gr[name][idx] for idx in sorted(gr[name])]
    return sq, dx, gr


HBM_SPEC = pl.BlockSpec(memory_space=pltpu.HBM)
SEM_SPEC = pl.BlockSpec(memory_space=pltpu.SEMAPHORE)
ANY_SPEC = pl.BlockSpec(memory_space=pl.ANY)
EFFECT = pltpu.SideEffectType.DATAFLOW_SIDE_EFFECTING


def _place():
    x, y, c = lax.axis_index("x"), lax.axis_index("y"), lax.axis_index("c")
    chips = [(1 - x, y), (x, 1 - y), (1 - x, 1 - y)]
    return x, y, c, chips


def _chip_index(x, y):
    return 2 * x + y


def _ds(start, size, align):
    if isinstance(start, int):
        return pl.ds(start, size)
    return pl.ds(pl.multiple_of(start, align), size)


def _half(ref, axis, h):
    r, c = ref.shape[-2:]
    if axis == 1:
        return ref.at[_ds(h * (r // 2), r // 2, 16), :]
    return ref.at[:, _ds(h * (c // 2), c // 2, 128)]


def _chunk(ref, axis, j, n=N_CHIPS):
    r, c = ref.shape[-2:]
    if axis == 1:
        return ref.at[:, _ds(j * (c // n), c // n, 128)]
    return ref.at[_ds(j * (r // n), r // n, 16), :]


def _remote(src, dst, send_sem, recv_sem, device):
    return pltpu.make_async_remote_copy(src_ref=src, dst_ref=dst, send_sem=send_sem, recv_sem=recv_sem,
                                        device_id=device, device_id_type=MESH)


def _hbm(a):
    return pltpu.with_memory_space_constraint(a, pltpu.HBM)


def _cast_into(stacked, idx, chip, axis, tok):
    _, r, c = stacked.shape
    full = (r, N_CHIPS * c) if axis == 1 else (N_CHIPS * r, c)
    tr = 128
    while tr > 16 and tr * c * 4 > (1 << 20):
        tr //= 2
    if axis == 1:
        o_map = lambda i, s: (i, s[0])
    else:
        o_map = lambda i, s: (s[0] * (r // tr) + i, 0)

    def body(s_ref, w_ref, tok_ref, o_ref):
        o_ref[...] = (w_ref[...] + tok_ref[0:1, 0:1]).astype(BF16)

    return pl.pallas_call(
        body, name="cast_into", out_shape=jax.ShapeDtypeStruct(full, BF16),
        grid_spec=pltpu.PrefetchScalarGridSpec(
            num_scalar_prefetch=1, grid=(r // tr,),
            in_specs=[pl.BlockSpec((None, tr, c), lambda i, s: (idx, i, 0)),
                      pl.BlockSpec((8, 128), lambda i, s: (0, 0))],
            out_specs=pl.BlockSpec((tr, c), o_map)),
        compiler_params=_params(("parallel",)),
    )(chip, stacked, tok)


def _own_piece(ref, axis, me, cc):
    return _half(_chunk(ref, axis, me), axis, cc)


def _ag_start(fulls, axes, tag=""):
    n = len(fulls)

    def body(*refs):
        ins = refs[:n]
        send, recv = refs[n:4 * n], refs[4 * n:7 * n]
        token = refs[8 * n]
        x, y, cc, chips = _place()
        me = _chip_index(x, y)
        for i in range(n):
            piece = _own_piece(ins[i], axes[i], me, cc)
            for k, chip in enumerate(chips):
                _remote(piece, piece, send[3 * i + k], recv[3 * i + k], (*chip, cc)).start()
        token[...] = jnp.zeros_like(token)

    sem = pltpu.SemaphoreType.DMA(())
    outs = pl.pallas_call(
        body, name="ag_start_%d%s" % (n, tag),
        out_shape=tuple([sem] * (6 * n) + [pltpu.HBM(f.shape, f.dtype) for f in fulls]
                        + [jax.ShapeDtypeStruct((8, 128), F32)]),
        in_specs=[HBM_SPEC] * n,
        out_specs=tuple([SEM_SPEC] * (6 * n) + [HBM_SPEC] * n + [pl.BlockSpec(memory_space=pltpu.VMEM)]),
        input_output_aliases={i: 6 * n + i for i in range(n)},
        compiler_params=pltpu.CompilerParams(has_side_effects=EFFECT),
    )(*[_hbm(f) for f in fulls])
    return outs[:3 * n], outs[3 * n:6 * n], outs[6 * n:7 * n], outs[7 * n]


def _ag_wait(send, recv, fulls, axes, after, tag=""):
    n = len(fulls)

    def body(*refs):
        ins = refs[:n]
        send_s, recv_s = refs[n:4 * n], refs[4 * n:7 * n]
        x, y, cc, chips = _place()
        me = _chip_index(x, y)
        for i in range(n):
            mine = _own_piece(ins[i], axes[i], me, cc)
            for k, chip in enumerate(chips):
                got = _own_piece(ins[i], axes[i], _chip_index(*chip), cc)
                cp = _remote(mine, got, send_s[3 * i + k], recv_s[3 * i + k], (*chip, cc))
                cp.wait_send()
                cp.wait_recv()

    return pl.pallas_call(
        body, name="ag_wait_%d%s" % (n, tag),
        out_shape=tuple(pltpu.HBM(f.shape, f.dtype) for f in fulls),
        in_specs=[HBM_SPEC] * n + [SEM_SPEC] * (6 * n) + [ANY_SPEC],
        out_specs=tuple([HBM_SPEC] * n),
        input_output_aliases={i: i for i in range(n)},
        compiler_params=pltpu.CompilerParams(has_side_effects=EFFECT),
    )(*fulls, *send, *recv, after)


def _ag_forward(fulls, axes, tag=""):
    n = len(fulls)

    def body(*refs):
        ins = refs[:n]
        send_sems, recv_sems = refs[2 * n], refs[2 * n + 1]
        x, y, cc, chips = _place()
        cps = []
        for i in range(n):
            for k, chip in enumerate(chips):
                got = _own_piece(ins[i], axes[i], _chip_index(*chip), cc)
                cp = _remote(got, got, send_sems.at[3 * i + k], recv_sems.at[3 * i + k], (x, y, 1 - cc))
                cp.start()
                cps.append(cp)
        for i in range(n):
            for k, chip in enumerate(chips):
                other = _own_piece(ins[i], axes[i], _chip_index(*chip), 1 - cc)
                cps[3 * i + k].wait_send()
                _remote(other, other, send_sems.at[3 * i + k], recv_sems.at[3 * i + k], (x, y, cc)).wait_recv()

    return pl.pallas_call(
        body, name="ag_forward_%d%s" % (n, tag),
        out_shape=tuple(jax.ShapeDtypeStruct(f.shape, f.dtype) for f in fulls),
        in_specs=[HBM_SPEC] * n, out_specs=tuple([HBM_SPEC] * n),
        input_output_aliases={i: i for i in range(n)},
        scratch_shapes=[pltpu.SemaphoreType.DMA((3 * n,)), pltpu.SemaphoreType.DMA((3 * n,))],
    )(*fulls)


def _pair_start(gs):
    n = len(gs)
    lands = [lax.empty(g.shape, g.dtype) for g in gs]

    def body(*refs):
        g_refs, land_refs = refs[:n], refs[n:2 * n]
        send, recv = refs[2 * n:3 * n], refs[3 * n:4 * n]
        token = refs[6 * n]
        x, y, cc, _ = _place()
        for i in range(n):
            _remote(g_refs[i], land_refs[i], send[i], recv[i], (x, y, 1 - cc)).start()
        token[...] = jnp.zeros_like(token)

    sem = pltpu.SemaphoreType.DMA(())
    outs = pl.pallas_call(
        body, name="pair_start_%d" % n,
        out_shape=tuple([sem] * (2 * n) + [pltpu.HBM(a.shape, a.dtype) for a in list(gs) + lands]
                        + [jax.ShapeDtypeStruct((8, 128), F32)]),
        in_specs=[HBM_SPEC] * (2 * n),
        out_specs=tuple([SEM_SPEC] * (2 * n) + [HBM_SPEC] * (2 * n) + [pl.BlockSpec(memory_space=pltpu.VMEM)]),
        input_output_aliases={i: 2 * n + i for i in range(2 * n)},
        compiler_params=pltpu.CompilerParams(has_side_effects=EFFECT),
    )(*[_hbm(a) for a in list(gs) + lands])
    return outs[:n], outs[n:2 * n], outs[2 * n:3 * n], outs[3 * n:4 * n], outs[4 * n][0, 0]


def _pair_wait(send, recv, gs, lands, after):
    n = len(gs)

    def body(*refs):
        g_refs, land_refs = refs[:n], refs[n:2 * n]
        send_s, recv_s = refs[2 * n:3 * n], refs[3 * n:4 * n]
        x, y, cc, _ = _place()
        for i in range(n):
            cp = _remote(g_refs[i], land_refs[i], send_s[i], recv_s[i], (x, y, 1 - cc))
            cp.wait_send()
            cp.wait_recv()

    outs = pl.pallas_call(
        body, name="pair_wait_%d" % n,
        out_shape=tuple(pltpu.HBM(a.shape, a.dtype) for a in list(gs) + list(lands)),
        in_specs=[HBM_SPEC] * (2 * n) + [SEM_SPEC] * (2 * n) + [ANY_SPEC],
        out_specs=tuple([HBM_SPEC] * (2 * n)),
        input_output_aliases={i: i for i in range(2 * n)},
        compiler_params=pltpu.CompilerParams(has_side_effects=EFFECT),
    )(*gs, *lands, *send, *recv, after)
    return outs[n:]


def _pair_begin(specs, core):
    calls = []
    for spec in specs:
        t = spec["a"].shape[0]
        m, n = spec["m"], spec["n"]
        if spec["axis"] == 1:
            dims = dict(m=m // 2, n=n, tm=min(TM, m // 2), tn=TN)
            operand = "a"
        else:
            dims = dict(m=m, n=n // 2, tm=TN, tn=n // 2)
            operand = "b"
        calls.append((spec, operand, dict(ta=True, k=t, tk=t, alpha=spec["alpha"], out_dtype=BF16, **dims)))
    sent = [_matmul([(spec["a"], spec["b"])], half=(core, False, operand), name=spec["name"] + "_sib", **kw)
            for spec, operand, kw in calls]
    send, recv, sent, lands, tok = _pair_start(sent)
    return (calls, core, send, recv, sent, lands), tok


def _pair_finish(state, after):
    calls, core, send, recv, sent, lands = state
    got = _pair_wait(send, recv, sent, lands, after)
    return [_matmul([(spec["a"], spec["b"])], half=(core, True, operand), res=r, name=spec["name"], **kw)
            for (spec, operand, kw), r in zip(calls, got)]


def _piece_shape(p, axis):
    r, c = p.shape
    return (r, c // N_CHIPS) if axis == 1 else (r // N_CHIPS, c)


def _rs_chips_start(ps, axes):
    n = len(ps)
    lands = [lax.empty((3,) + _piece_shape(p, ax), p.dtype) for p, ax in zip(ps, axes)]

    def body(*refs):
        p_refs, land_refs = refs[:n], refs[n:2 * n]
        send, recv = refs[2 * n:5 * n], refs[5 * n:8 * n]
        token = refs[10 * n]
        x, y, cc, chips = _place()
        for i in range(n):
            for k, chip in enumerate(chips):
                _remote(_chunk(p_refs[i], axes[i], _chip_index(*chip)), land_refs[i].at[k],
                        send[3 * i + k], recv[3 * i + k], (*chip, cc)).start()
        token[...] = jnp.zeros_like(token)

    sem = pltpu.SemaphoreType.DMA(())
    outs = pl.pallas_call(
        body, name="rs_start_%d" % n,
        out_shape=tuple([sem] * (6 * n) + [pltpu.HBM(a.shape, a.dtype) for a in list(ps) + lands]
                        + [jax.ShapeDtypeStruct((8, 128), F32)]),
        in_specs=[HBM_SPEC] * (2 * n),
        out_specs=tuple([SEM_SPEC] * (6 * n) + [HBM_SPEC] * (2 * n) + [pl.BlockSpec(memory_space=pltpu.VMEM)]),
        input_output_aliases={i: 6 * n + i for i in range(2 * n)},
        compiler_params=pltpu.CompilerParams(has_side_effects=EFFECT),
    )(*[_hbm(a) for a in list(ps) + lands])
    return outs[:3 * n], outs[3 * n:6 * n], outs[6 * n:7 * n], outs[7 * n:8 * n], outs[8 * n][0, 0]


def _rs_chips_wait(send, recv, ps, lands, axes, after):
    n = len(ps)

    def body(*refs):
        p_refs, land_refs = refs[:n], refs[n:2 * n]
        send_s, recv_s = refs[2 * n:5 * n], refs[5 * n:8 * n]
        x, y, cc, chips = _place()
        for i in range(n):
            for k, chip in enumerate(chips):
                cp = _remote(_chunk(p_refs[i], axes[i], _chip_index(*chip)), land_refs[i].at[k],
                             send_s[3 * i + k], recv_s[3 * i + k], (*chip, cc))
                cp.wait_send()
                cp.wait_recv()

    outs = pl.pallas_call(
        body, name="rs_wait_%d" % n,
        out_shape=tuple(pltpu.HBM(a.shape, a.dtype) for a in list(ps) + list(lands)),
        in_specs=[HBM_SPEC] * (2 * n) + [SEM_SPEC] * (6 * n) + [ANY_SPEC],
        out_specs=tuple([HBM_SPEC] * (2 * n)),
        input_output_aliases={i: i for i in range(2 * n)},
        compiler_params=pltpu.CompilerParams(has_side_effects=EFFECT),
    )(*ps, *lands, *send, *recv, after)
    return outs[:n], outs[n:]


def _add_chips(p, got, chip, core, axis, *, idx, count, into, tr, tc):
    _, pr, pc = got.shape
    shard = (2 * pr, pc) if axis == 1 else (pr, 2 * pc)
    if axis == 1:
        p_map = lambda i, j, sc, so: (i, sc[0] * (pc // tc) + j)
        o_map = lambda i, j, sc, so: (idx, so[0] * (pr // tr) + i, j)
    else:
        p_map = lambda i, j, sc, so: (sc[0] * (pr // tr) + i, j)
        o_map = lambda i, j, sc, so: (idx, i, so[0] * (pc // tc) + j)

    def body(sc_ref, so_ref, p_ref, r_ref, *rest):
        o_ref = rest[-1]
        acc = p_ref[...].astype(F32)
        for k in range(3):
            acc = acc + r_ref[k].astype(F32)
        o_ref[...] = acc

    in_specs = [pl.BlockSpec((tr, tc), p_map), pl.BlockSpec((3, tr, tc), lambda i, j, sc, so: (0, i, j))]
    args = [chip, core, p, got]
    aliases = {}
    if into is not None:
        in_specs.append(ANY_SPEC)
        args.append(into)
        aliases = {4: 0}
    return pl.pallas_call(
        body, name="add_chips", out_shape=jax.ShapeDtypeStruct((count,) + shard, F32),
        grid_spec=pltpu.PrefetchScalarGridSpec(
            num_scalar_prefetch=2, grid=(pr // tr, pc // tc), in_specs=in_specs,
            out_specs=pl.BlockSpec((None, tr, tc), o_map)),
        input_output_aliases=aliases,
        compiler_params=_params(("parallel", "parallel")),
    )(*args)


def _final_start(stacked, axis, tag):
    n = stacked.shape[0]

    def body(s_ref, *refs):
        send, recv = refs[:n], refs[n:2 * n]
        x, y, cc, _ = _place()
        for i in range(n):
            mine = _half(s_ref.at[i], axis, cc)
            _remote(mine, mine, send[i], recv[i], (x, y, 1 - cc)).start()

    sem = pltpu.SemaphoreType.DMA(())
    outs = pl.pallas_call(
        body, name="final_start_" + tag,
        out_shape=tuple([sem] * (2 * n) + [pltpu.HBM(stacked.shape, stacked.dtype)]),
        in_specs=[HBM_SPEC], out_specs=tuple([SEM_SPEC] * (2 * n) + [HBM_SPEC]),
        input_output_aliases={0: 2 * n},
        compiler_params=pltpu.CompilerParams(has_side_effects=EFFECT),
    )(_hbm(stacked))
    return outs[:n], outs[n:2 * n], outs[2 * n]


def _final_wait(send, recv, stacked, axis, tag):
    n = stacked.shape[0]

    def body(s_ref, *refs):
        send_s, recv_s = refs[:n], refs[n:2 * n]
        x, y, cc, _ = _place()
        for i in range(n):
            cp = _remote(_half(s_ref.at[i], axis, cc), _half(s_ref.at[i], axis, 1 - cc), send_s[i], recv_s[i],
                         (x, y, 1 - cc))
            cp.wait_send()
            cp.wait_recv()

    return pl.pallas_call(
        body, name="final_wait_" + tag, out_shape=pltpu.HBM(stacked.shape, stacked.dtype),
        in_specs=[HBM_SPEC] + [SEM_SPEC] * (2 * n), out_specs=HBM_SPEC,
        input_output_aliases={0: 0},
        compiler_params=pltpu.CompilerParams(has_side_effects=EFFECT),
    )(stacked, *send, *recv)


def _rs_final(stacked, axis):
    n = stacked.shape[0]

    def body(s_ref, o_ref, send_sems, recv_sems):
        x, y, cc, _ = _place()
        cps = []
        for i in range(n):
            mine = _half(s_ref.at[i], axis, cc)
            cp = _remote(mine, mine, send_sems.at[i], recv_sems.at[i], (x, y, 1 - cc))
            cp.start()
            cps.append(cp)
        for i, cp in enumerate(cps):
            other = _half(s_ref.at[i], axis, 1 - cc)
            cp.wait_send()
            _remote(other, other, send_sems.at[i], recv_sems.at[i], (x, y, cc)).wait_recv()

    return pl.pallas_call(
        body, name="rs_final", out_shape=jax.ShapeDtypeStruct(stacked.shape, stacked.dtype),
        in_specs=[HBM_SPEC], out_specs=HBM_SPEC, input_output_aliases={0: 0},
        scratch_shapes=[pltpu.SemaphoreType.DMA((n,)), pltpu.SemaphoreType.DMA((n,))],
    )(stacked)


def _ag_small(packed):
    rows, cols = packed.shape

    def body(s_ref, o_ref, tok_ref, send_sems, recv_sems, local_sem):
        x, y, cc, chips = _place()
        me = _chip_index(x, y)
        own = pltpu.make_async_copy(s_ref, o_ref.at[me], local_sem)
        own.start()
        cps = [_remote(s_ref, o_ref.at[me], send_sems.at[k], recv_sems.at[k], (*chip, cc))
               for k, chip in enumerate(chips)]
        for cp in cps:
            cp.start()
        for k, chip in enumerate(chips):
            cps[k].wait_send()
            got = o_ref.at[_chip_index(*chip)]
            _remote(got, got, send_sems.at[k], recv_sems.at[k], (x, y, cc)).wait_recv()
        own.wait()
        tok_ref[...] = jnp.zeros_like(tok_ref)

    return pl.pallas_call(
        body, name="ag_small",
        out_shape=(jax.ShapeDtypeStruct((N_CHIPS, rows, cols), packed.dtype), jax.ShapeDtypeStruct((8, 128), F32)),
        in_specs=[HBM_SPEC], out_specs=(HBM_SPEC, pl.BlockSpec(memory_space=pltpu.VMEM)),
        scratch_shapes=[pltpu.SemaphoreType.DMA((3,)), pltpu.SemaphoreType.DMA((3,)), pltpu.SemaphoreType.DMA],
    )(packed)


def _rs_small(packed):
    _, rows, cols = packed.shape
    rels = [(bx, by, bc) for bx in (0, 1) for by in (0, 1) for bc in (0, 1)][1:]

    def body(s_ref, o_ref, send_sems, recv_sems, local_sem):
        x, y, cc, _ = _place()
        me = 4 * x + 2 * y + cc
        own = pltpu.make_async_copy(s_ref.at[_chip_index(x, y)], o_ref.at[me], local_sem)
        own.start()
        peers = [(jnp.bitwise_xor(x, bx), jnp.bitwise_xor(y, by), jnp.bitwise_xor(cc, bc)) for bx, by, bc in rels]
        cps = [_remote(s_ref.at[_chip_index(px, py)], o_ref.at[me], send_sems.at[k], recv_sems.at[k], (px, py, pc))
               for k, (px, py, pc) in enumerate(peers)]
        for cp in cps:
            cp.start()
        for k, (px, py, pc) in enumerate(peers):
            cps[k].wait_send()
            got = o_ref.at[4 * px + 2 * py + pc]
            _remote(got, got, send_sems.at[k], recv_sems.at[k], (x, y, cc)).wait_recv()
        own.wait()

    return pl.pallas_call(
        body, name="rs_small", out_shape=jax.ShapeDtypeStruct((2 * N_CHIPS, rows, cols), packed.dtype),
        in_specs=[HBM_SPEC], out_specs=HBM_SPEC,
        scratch_shapes=[pltpu.SemaphoreType.DMA((7,)), pltpu.SemaphoreType.DMA((7,)), pltpu.SemaphoreType.DMA],
    )(packed)


def _sum_slots(slots, *, tr=8):
    n, rows, cols = slots.shape

    def body(s_ref, o_ref):
        acc = s_ref[0]
        for k in range(1, n):
            acc = acc + s_ref[k]
        o_ref[...] = acc

    return pl.pallas_call(
        body, name="sum_slots", out_shape=jax.ShapeDtypeStruct((rows, cols), F32), grid=(rows // tr,),
        in_specs=[pl.BlockSpec((n, tr, cols), lambda i: (0, i, 0))], out_specs=pl.BlockSpec((tr, cols), lambda i: (i, 0)),
        compiler_params=_params(("parallel",)),
    )(slots)


MATRIX_AXIS = {"ffn_w1": 1, "ffn_w3": 1, "ffn_w2": 0, "ev_w_in": 1, "ev_w_out": 0, "od_w_in": 1, "od_w_out": 0}
SMALL_SHARDED = ("norm_g", "ev_conv_w", "od_conv_w", "od_conv_b", "od_ln_g", "od_ln_b", "od_pool_scale")
SMALL_REPLICATED = ("ev_q_gain", "ev_k_gain")
PACK_COLS = 1024
PACK_ROW_ALIGN = 8


def _pack(parts):
    flat = jnp.concatenate([p.reshape(-1).astype(F32) for p in parts])
    per = PACK_COLS * PACK_ROW_ALIGN
    total = -(-flat.shape[0] // per) * per
    return jnp.pad(flat, (0, total - flat.shape[0])).reshape(total // PACK_COLS, PACK_COLS)


def _unpack(packed, shapes):
    flat = packed.reshape(-1)
    out, pos = [], 0
    for shp in shapes:
        size = 1
        for s in shp:
            size *= s
        out.append(flat[pos:pos + size].reshape(shp))
        pos += size
    return out


def _row_tile(rows, cols, itemsize=4, budget=1 << 20):
    tr = 8
    while rows % (2 * tr) == 0 and 2 * tr * cols * itemsize <= budget:
        tr *= 2
    return tr


def _piece_tiles(hr, hc, axis):
    pr, pc = (hr, hc // N_CHIPS) if axis == 1 else (hr // N_CHIPS, hc)
    tr = 128
    while tr > 16 and tr * pc * 4 > (1 << 20):
        tr //= 2
    return tr, pc


AG_LOOKAHEAD = 2


def kernel(x, norm_g, ffn_w1, ffn_w3, ffn_w2, ev_w_in, ev_q_gain, ev_k_gain, ev_conv_w, ev_w_out, od_w_in, od_conv_w, od_conv_b, od_ln_g, od_ln_b, od_pool_w, od_pool_scale, od_w_out, loss_target, m_norm_g, m_ffn_w1, m_ffn_w3, m_ffn_w2, m_ev_w_in, m_ev_q_gain, m_ev_k_gain, m_ev_conv_w, m_ev_w_out, m_od_w_in, m_od_conv_w, m_od_conv_b, m_od_ln_g, m_od_ln_b, m_od_pool_w, m_od_pool_scale, m_od_w_out, v_norm_g, v_ffn_w1, v_ffn_w3, v_ffn_w2, v_ev_w_in, v_ev_q_gain, v_ev_k_gain, v_ev_conv_w, v_ev_w_out, v_od_w_in, v_od_conv_w, v_od_conv_b, v_od_ln_g, v_od_ln_b, v_od_pool_w, v_od_pool_scale, v_od_w_out):
    names = ["norm_g", "ffn_w1", "ffn_w3", "ffn_w2", "ev_w_in", "ev_q_gain", "ev_k_gain", "ev_conv_w", "ev_w_out",
             "od_w_in", "od_conv_w", "od_conv_b", "od_ln_g", "od_ln_b", "od_pool_w", "od_pool_scale", "od_w_out"]
    w = dict(zip(names, (norm_g, ffn_w1, ffn_w3, ffn_w2, ev_w_in, ev_q_gain, ev_k_gain, ev_conv_w, ev_w_out,
                         od_w_in, od_conv_w, od_conv_b, od_ln_g, od_ln_b, od_pool_w, od_pool_scale, od_w_out)))
    m = dict(zip(names, (m_norm_g, m_ffn_w1, m_ffn_w3, m_ffn_w2, m_ev_w_in, m_ev_q_gain, m_ev_k_gain, m_ev_conv_w,
                         m_ev_w_out, m_od_w_in, m_od_conv_w, m_od_conv_b, m_od_ln_g, m_od_ln_b, m_od_pool_w,
                         m_od_pool_scale, m_od_w_out)))
    v = dict(zip(names, (v_norm_g, v_ffn_w1, v_ffn_w3, v_ffn_w2, v_ev_w_in, v_ev_q_gain, v_ev_k_gain, v_ev_conv_w,
                         v_ev_w_out, v_od_w_in, v_od_conv_w, v_od_conv_b, v_od_ln_g, v_od_ln_b, v_od_pool_w,
                         v_od_pool_scale, v_od_w_out)))
    cx, cy, cc = lax.axis_index("x"), lax.axis_index("y"), lax.axis_index("c")
    cc_arr = jnp.reshape(cc, (1,)).astype(jnp.int32)
    chip_arr = jnp.reshape(_chip_index(cx, cy), (1,)).astype(jnp.int32)
    n_sub = 3 * norm_g.shape[0]
    stacked = {name: w[name].reshape((-1,) + w[name].shape[-2:]) for name in MATRIX_AXIS}

    def axes_of(s):
        return [MATRIX_AXIS[name] for name, _ in _sublayer_matrices(s)]

    ag_inflight = {}

    wts = {}
    small_names = SMALL_SHARDED + ("od_pool_w",)
    gathered, small_tok = _ag_small(_pack([w[k] for k in small_names]))
    per_chip = [_unpack(gathered[j], [w[k].shape for k in small_names]) for j in range(N_CHIPS)]
    for idx, k in enumerate(small_names):
        ax = 2 if k == "od_pool_w" else w[k].ndim - 1
        wts[k] = jnp.concatenate([per_chip[j][idx] for j in range(N_CHIPS)], axis=ax)
    for k in SMALL_REPLICATED:
        wts[k] = w[k]

    ag_tokens = [small_tok]

    def ag_start_group(key, mats, tag=""):
        axes = [MATRIX_AXIS[name] for name, _ in mats]
        fulls = [_cast_into(stacked[name], idx, chip_arr, MATRIX_AXIS[name], ag_tokens[-1]) for name, idx in mats]
        send, recv, fulls, tok = _ag_start(fulls, axes, tag)
        ag_inflight[key] = (send, recv, fulls, axes, tag)
        ag_tokens.append(tok)
        return tok[0, 0]

    def ag_finish_group(key, after):
        send, recv, fulls, axes, tag = ag_inflight.pop(key)
        return list(_ag_forward(_ag_wait(send, recv, fulls, axes, after, tag), axes, tag))

    def ag_start(s):
        mats = _sublayer_matrices(s)
        if s == 0:
            return ag_start_group((s, 0), mats[:2], "_up") + ag_start_group((s, 1), mats[2:], "_down")
        return ag_start_group((s, 0), mats)

    x0 = x[0]
    first_tok = sum(ag_start(s) for s in range(min(AG_LOOKAHEAD, n_sub)))

    def fetch(s, after):
        tok = first_tok if s == 0 else 0.0
        if s + AG_LOOKAHEAD < n_sub:
            tok = tok + ag_start(s + AG_LOOKAHEAD)
        mats = ag_finish_group((s, 0), after)
        if s == 0:
            mats.append(lambda behind: ag_finish_group((s, 1), behind)[0])
        return mats, tok

    rs_inflight = {}

    pair_inflight = {}

    def grads_begin(s, specs):
        pair_inflight[s], tok = _pair_begin(specs, cc_arr)
        return tok

    def grads_finish(s, after):
        send, recv, ps, lands, tok = _rs_chips_start(_pair_finish(pair_inflight.pop(s), after), axes_of(s))
        rs_inflight[s] = (send, recv, ps, lands)
        return tok

    sq, dx, gr = _local_step(x0, loss_target[0], wts, fetch, (grads_begin, grads_finish))

    grads = {}
    reduced = {name: None for name in MATRIX_AXIS}
    for s in reversed(range(n_sub)):
        send, recv, ps, lands = rs_inflight.pop(s)
        axes = axes_of(s)
        ps, lands = _rs_chips_wait(send, recv, ps, lands, axes, dx)
        for (name, idx), p, land, ax in zip(_sublayer_matrices(s), ps, lands, axes):
            tr, tc = _piece_tiles(*p.shape, ax)
            reduced[name] = _add_chips(p, land, chip_arr, cc_arr, ax, idx=idx, count=stacked[name].shape[0],
                                       into=reduced[name], tr=tr, tc=tc)
    swaps = {name: _final_start(reduced[name], axis, name) for name, axis in MATRIX_AXIS.items()}
    for name, axis in MATRIX_AXIS.items():
        send, recv, buf = swaps[name]
        grads[name] = _final_wait(send, recv, buf, axis, name).reshape(w[name].shape)
    small_full = {k: (gr[k] if k == "norm_g" else jnp.stack(gr[k])) for k in small_names + SMALL_REPLICATED}
    chunks = []
    for j in range(N_CHIPS):
        parts = []
        for k in small_names:
            ax = 2 if k == "od_pool_w" else w[k].ndim - 1
            size = w[k].shape[ax]
            parts.append(lax.slice_in_dim(small_full[k], j * size, (j + 1) * size, axis=ax))
        parts += [small_full[k] for k in SMALL_REPLICATED] + [sq[0, :1]]
        chunks.append(_pack(parts))
    summed = _sum_slots(_rs_small(jnp.stack(chunks)))
    pack_names = small_names + SMALL_REPLICATED
    unpacked = _unpack(summed, [w[k].shape for k in pack_names] + [(1,)])
    for k, g in zip(pack_names, unpacked):
        grads[k] = g
    loss = (0.5 / x.shape[-1]) * unpacked[-1][0]

    delta, new_m, new_v = {}, {}, {}
    for name in MATRIX_AXIS:
        shp = w[name].shape
        cols = shp[-1]
        rows = w[name].size // cols
        tr = _row_tile(rows, cols, budget=1 << 20)
        d_, m_, v_, g_ = _adamw(w[name].reshape(rows, cols), grads[name].reshape(rows, cols),
                                m[name].reshape(rows, cols), v[name].reshape(rows, cols), tr=tr)
        delta[name], new_m[name], new_v[name] = d_.reshape(shp), m_.reshape(shp), v_.reshape(shp)
        grads[name] = g_.reshape(shp)
    pw, pg, pm, pv = (_pack([t[k] for k in pack_names]) for t in (w, grads, m, v))
    d_, m_, v_, _ = _adamw(pw, pg, pm, pv, tr=PACK_ROW_ALIGN)
    shapes = [w[k].shape for k in pack_names]
    for store, packed in ((delta, d_), (new_m, m_), (new_v, v_)):
        for k, a in zip(pack_names, _unpack(packed, shapes)):
            store[k] = a

    return (loss, dx[None], *[grads[k] for k in names], *[delta[k] for k in names],
            *[new_m[k] for k in names], *[new_v[k] for k in names])
```
